```python
import jax, jax.numpy as jnp
from jax import lax
import numpy as np

D_MODEL = 2048
BATCH = 2
SEQ = 8192
DEPTH = 1

HEAD_DIM = 64
ATTN_WIDTH = D_MODEL // 2
ATTN_HEADS = ATTN_WIDTH // HEAD_DIM
ATTN_KV_HEADS = ATTN_HEADS // 4
KV_WIDTH = ATTN_KV_HEADS * HEAD_DIM
WINDOW = 128
BLOCK = 128

POOL_WINDOWS = (2, 4, 8, 16)
POOL_GROUPS = len(POOL_WINDOWS)
POOL_WIDTH = D_MODEL // 2
POOL_GROUP_DIM = POOL_WIDTH // POOL_GROUPS

N_BRANCHES = 2
Q_END = ATTN_WIDTH
K_END = Q_END + KV_WIDTH
V_END = K_END + KV_WIDTH
U_END = V_END + POOL_WIDTH
IN_WIDTH = U_END + N_BRANCHES * D_MODEL

N_GROUPS = 4
EXPERTS_PER_GROUP = 4
N_EXPERTS = N_GROUPS * EXPERTS_PER_GROUP
TOP_K = 2
EXPERT_FF = D_MODEL // 4

N_ADA = 6
NORM_EPS = 1e-6
MASK_VALUE = -1e30

kernel_name = "hybrid_swa_sink_pool_hmoe_adaln"


def rms_norm(x, g):
    xf = x.astype(jnp.float32)
    xf = xf * lax.rsqrt(jnp.mean(xf * xf, axis=-1, keepdims=True) + NORM_EPS)
    return (xf * g.astype(jnp.float32)).astype(x.dtype)


def sliding_window_sink_attention(q, k, v, sinks):
    b, s = q.shape[0], q.shape[1]
    nb = s // BLOCK
    grp = ATTN_HEADS // ATTN_KV_HEADS
    qb = q.reshape(b, nb, BLOCK, ATTN_KV_HEADS, grp, HEAD_DIM)
    pad = jnp.zeros((b, BLOCK, ATTN_KV_HEADS, HEAD_DIM), k.dtype)
    kp = jnp.concatenate([pad, k], axis=1).reshape(b, nb + 1, BLOCK, ATTN_KV_HEADS, HEAD_DIM)
    vp = jnp.concatenate([pad, v], axis=1).reshape(b, nb + 1, BLOCK, ATTN_KV_HEADS, HEAD_DIM)
    k_band = jnp.concatenate([kp[:, :-1], kp[:, 1:]], axis=2)
    v_band = jnp.concatenate([vp[:, :-1], vp[:, 1:]], axis=2)
    logits = jnp.einsum('bnqkgd,bnjkd->bnkgqj', qb, k_band,
                        preferred_element_type=jnp.float32) * (HEAD_DIM ** -0.5)
    r = jnp.arange(BLOCK)[:, None]
    j = jnp.arange(2 * BLOCK)[None, :]
    rel = r + BLOCK - j
    band = (rel >= 0) & (rel < WINDOW)
    blk = jnp.arange(nb)[:, None, None]
    valid = band[None] & ((blk > 0) | (j[None] >= BLOCK))
    logits = jnp.where(valid[None, :, None, None], logits, MASK_VALUE)
    sink = jnp.broadcast_to(
        sinks.astype(jnp.float32).reshape(ATTN_KV_HEADS, grp)[None, None, :, :, None, None],
        logits.shape[:-1] + (1,))
    probs = jax.nn.softmax(jnp.concatenate([logits, sink], axis=-1), axis=-1)[..., :-1]
    out = jnp.einsum('bnkgqj,bnjkd->bnqkgd', probs.astype(v.dtype), v_band)
    return out.reshape(b, s, ATTN_WIDTH)


def multiscale_pool(u, w_pool, pool_scale):
    b, s = u.shape[0], u.shape[1]
    ug = u.reshape(b, s, POOL_GROUPS, POOL_GROUP_DIM)
    pos = jnp.arange(s)
    outs = []
    for gi, w in enumerate(POOL_WINDOWS):
        xg = ug[:, :, gi].astype(jnp.float32)
        cs = jnp.cumsum(xg, axis=1)
        cs_prev = jnp.pad(cs, ((0, 0), (w, 0), (0, 0)))[:, :s]
        count = jnp.minimum(pos + 1, w).astype(jnp.float32)[None, :, None]
        outs.append((cs - cs_prev) / count - xg)
    pooled = jnp.stack(outs, axis=2).astype(u.dtype)
    mixed = jnp.einsum('bsgc,gcd->bsgd', pooled, w_pool).reshape(b, s, POOL_WIDTH)
    return mixed * pool_scale


def hierarchical_moe(h, w_router_group, b_router_group, w_router_expert, b_router_expert,
                     w_e_gate, w_e_up, w_e_down):
    t = h.shape[0]
    g_logits = jnp.matmul(h, w_router_group).astype(jnp.float32) + b_router_group.astype(jnp.float32)
    g_prob = jax.nn.softmax(g_logits, axis=-1)
    g_top_p, g_idx = lax.top_k(g_prob, 1)
    e_logits = (jnp.matmul(h, w_router_expert).astype(jnp.float32)
                + b_router_expert.astype(jnp.float32)).reshape(t, N_GROUPS, EXPERTS_PER_GROUP)
    e_sel = jnp.take_along_axis(e_logits, g_idx[:, :, None], axis=1)[:, 0]
    e_prob = jax.nn.softmax(e_sel, axis=-1)
    e_top_p, e_idx = lax.top_k(e_prob, TOP_K)
    e_top_p = e_top_p / jnp.sum(e_top_p, axis=-1, keepdims=True)
    weights = g_top_p * e_top_p
    global_idx = g_idx * EXPERTS_PER_GROUP + e_idx
    combine = jnp.sum(jax.nn.one_hot(global_idx, N_EXPERTS, dtype=jnp.float32)
                      * weights[..., None], axis=1)
    gate = jnp.einsum('td,edf->tef', h, w_e_gate)
    up = jnp.einsum('td,edf->tef', h, w_e_up)
    act = jax.nn.silu(gate) * up * combine[:, :, None].astype(h.dtype)
    return jnp.einsum('tef,efd->td', act, w_e_down)


def setup_inputs(seed: int = 0) -> dict:
    key = jax.random.key(seed)
    ks = jax.random.split(key, 22)
    n = jax.random.normal
    f32 = jnp.float32
    L, D = DEPTH, D_MODEL
    return {
        "x": n(ks[0], (BATCH, SEQ, D), f32),
        "c": n(ks[1], (BATCH, D), f32),
        "w_ada": n(ks[2], (L, D, N_ADA * D), f32) * (0.5 * D ** -0.5),
        "b_ada": n(ks[3], (L, N_ADA * D), f32) * 0.02,
        "norm1_g": 1.0 + 0.02 * n(ks[4], (L, D), f32),
        "w_in": n(ks[5], (L, D, IN_WIDTH), f32) * D ** -0.5,
        "sinks": n(ks[6], (L, ATTN_HEADS), f32),
        "w_pool": n(ks[7], (L, POOL_GROUPS, POOL_GROUP_DIM, POOL_GROUP_DIM), f32) * POOL_GROUP_DIM ** -0.5,
        "pool_scale": 1.0 + 0.1 * n(ks[8], (L, POOL_WIDTH), f32),
        "w_attn_branch": n(ks[9], (L, ATTN_WIDTH, D), f32) * ATTN_WIDTH ** -0.5,
        "w_pool_branch": n(ks[10], (L, POOL_WIDTH, D), f32) * POOL_WIDTH ** -0.5,
        "w_out": n(ks[11], (L, D, D), f32) * D ** -0.5,
        "norm2_g": 1.0 + 0.02 * n(ks[12], (L, D), f32),
        "w_router_group": n(ks[13], (L, D, N_GROUPS), f32) * D ** -0.5,
        "b_router_group": n(ks[14], (L, N_GROUPS), f32) * 0.01,
        "w_router_expert": n(ks[15], (L, D, N_EXPERTS), f32) * D ** -0.5,
        "b_router_expert": n(ks[16], (L, N_EXPERTS), f32) * 0.01,
        "w_e_gate": n(ks[17], (L, N_EXPERTS, D, EXPERT_FF), f32) * D ** -0.5,
        "w_e_up": n(ks[18], (L, N_EXPERTS, D, EXPERT_FF), f32) * D ** -0.5,
        "w_e_down": n(ks[19], (L, N_EXPERTS, EXPERT_FF, D), f32) * EXPERT_FF ** -0.5,
        "final_g": 1.0 + 0.02 * n(ks[20], (D,), f32),
    }


def reference(x, c, w_ada, b_ada, norm1_g, w_in, sinks, w_pool, pool_scale, w_attn_branch,
              w_pool_branch, w_out, norm2_g, w_router_group, b_router_group, w_router_expert,
              b_router_expert, w_e_gate, w_e_up, w_e_down, final_g):
    b, s, d = x.shape
    for l in range(DEPTH):
        ada = (jnp.matmul(c, w_ada[l]) + b_ada[l])[:, None, :]
        shift1, scale1, gate1, shift2, scale2, gate2 = jnp.split(ada, N_ADA, axis=-1)

        h = rms_norm(x, norm1_g[l]) * (1.0 + scale1) + shift1
        proj = jnp.matmul(h, w_in[l])
        q, k, v, u, gate_logits = jnp.split(proj, [Q_END, K_END, V_END, U_END], axis=-1)
        attn = sliding_window_sink_attention(
            q.reshape(b, s, ATTN_HEADS, HEAD_DIM),
            k.reshape(b, s, ATTN_KV_HEADS, HEAD_DIM),
            v.reshape(b, s, ATTN_KV_HEADS, HEAD_DIM),
            sinks[l])
        pool = multiscale_pool(u, w_pool[l], pool_scale[l])
        gate_a, gate_p = jnp.split(jax.nn.sigmoid(gate_logits), N_BRANCHES, axis=-1)
        merged = (gate_a * jnp.matmul(attn, w_attn_branch[l])
                  + gate_p * jnp.matmul(pool, w_pool_branch[l]))
        x = x + gate1 * jnp.matmul(merged, w_out[l])

        h2 = rms_norm(x, norm2_g[l]) * (1.0 + scale2) + shift2
        y = hierarchical_moe(h2.reshape(b * s, d), w_router_group[l], b_router_group[l],
                             w_router_expert[l], b_router_expert[l],
                             w_e_gate[l], w_e_up[l], w_e_down[l]).reshape(b, s, d)
        x = x + gate2 * y
    return rms_norm(x, final_g)
```

```python
import functools

import jax
import jax.numpy as jnp
from jax import lax
from jax.experimental import pallas as pl
from jax.experimental.pallas import tpu as pltpu

F32 = jnp.float32
BF16 = jnp.bfloat16

D_MODEL = 2048
HEAD_DIM = 64
ATTN_HEADS = 16
KV_HEADS = 4
Q_PER_KV = ATTN_HEADS // KV_HEADS
ATTN_WIDTH = ATTN_HEADS * HEAD_DIM
KV_WIDTH = KV_HEADS * HEAD_DIM
BLOCK = 128
POOL_WINDOWS = (2, 4, 8, 16)
POOL_WIDTH = 1024
POOL_GROUP_DIM = 256
QKVU_WIDTH = ATTN_WIDTH + 2 * KV_WIDTH + POOL_WIDTH
GATE_WIDTH = 2 * D_MODEL
N_GROUPS = 4
EXPERTS_PER_GROUP = 4
N_EXPERTS = 16
EXPERT_FF = 512
N_ADA = 6
NORM_EPS = 1e-6
MASK_VALUE = -1e30

PAIRS_PER_GROUP = 6
N_CLASSES = N_GROUPS * PAIRS_PER_GROUP
CLASS_SLOT_A = (0, 0, 0, 1, 1, 3)
CLASS_SLOT_B = (1, 2, 3, 3, 2, 2)
ROUTER_ROWS = 32
ROUTER_LANES = 128

TM_PROJ = 1024
TN_PROJ = 512
TM_MIX = 512
TM_MERGE = 256
TM_DISPATCH = 512
TM_EXPERT = 256
TM_COMBINE = 256
TN_ADA = 1024

VMEM_LIMIT = 52 * 1024 * 1024


def _rms_modulate(x, g, scale, shift):
    ms = jnp.mean(x * x, axis=-1, keepdims=True)
    return (x * lax.rsqrt(ms + NORM_EPS) * g) * (1.0 + scale) + shift


def _ada_kernel(cb_ref, w_ref, b_ref, o_ref):
    nb = cb_ref.shape[0]
    d = w_ref.shape[0]
    nchunk = w_ref.shape[1] // 128

    def body(kb, accs):
        k0 = pl.multiple_of(kb * 8, 8)
        new = list(accs)
        cbs = [cb_ref[b, pl.ds(k0, 8), :] for b in range(nb)]
        for j in range(nchunk):
            w = w_ref[pl.ds(k0, 8), j * 128:(j + 1) * 128]
            for b in range(nb):
                new[b * nchunk + j] = new[b * nchunk + j] + w * cbs[b]
        return tuple(new)

    init = tuple(jnp.zeros((8, 128), F32) for _ in range(nb * nchunk))
    accs = lax.fori_loop(0, d // 8, body, init, unroll=2)
    for b in range(nb):
        for j in range(nchunk):
            o_ref[b:b + 1, j * 128:(j + 1) * 128] = (
                jnp.sum(accs[b * nchunk + j], axis=0, keepdims=True)
                + b_ref[:, j * 128:(j + 1) * 128])


def _ada(c, w_ada, b_ada):
    nb, d = c.shape
    n = w_ada.shape[1]
    cb = jnp.broadcast_to(c[:, :, None], (nb, d, 128))
    return pl.pallas_call(
        _ada_kernel,
        out_shape=jax.ShapeDtypeStruct((nb, n), F32),
        grid=(n // TN_ADA,),
        in_specs=[
            pl.BlockSpec((nb, d, 128), lambda j: (0, 0, 0)),
            pl.BlockSpec((d, TN_ADA), lambda j: (0, j)),
            pl.BlockSpec((1, TN_ADA), lambda j: (0, j)),
        ],
        out_specs=pl.BlockSpec((nb, TN_ADA), lambda j: (0, j)),
        compiler_params=pltpu.CompilerParams(
            dimension_semantics=("arbitrary",), vmem_limit_bytes=VMEM_LIMIT),
        name="ada",
    )(cb, w_ada, b_ada.reshape(1, n))


def _in_proj_kernel(x_ref, ada_ref, g_ref, w_ref, qkvu_ref, gates_ref, h_ref):
    j = pl.program_id(1)
    n_qkvu = QKVU_WIDTH // TN_PROJ

    @pl.when(j == 0)
    def _():
        chunk = 256
        for r in range(0, TM_PROJ, chunk):
            h = _rms_modulate(x_ref[r:r + chunk, :], g_ref[...],
                              ada_ref[0, 1:2, :], ada_ref[0, 0:1, :])
            h_ref[r:r + chunk, :] = h.astype(BF16)

    acc = jnp.dot(h_ref[...], w_ref[...], preferred_element_type=F32)

    @pl.when(j < n_qkvu)
    def _():
        qkvu_ref[...] = acc.astype(BF16)

    @pl.when(j >= n_qkvu)
    def _():
        gates_ref[...] = jax.nn.sigmoid(acc).astype(BF16)


def _in_proj(x2d, ada3, norm_g, w_in_b, seq):
    t, d = x2d.shape
    n_qkvu = QKVU_WIDTH // TN_PROJ
    n_all = (QKVU_WIDTH + GATE_WIDTH) // TN_PROJ
    tiles_per_seq = seq // TM_PROJ
    return pl.pallas_call(
        _in_proj_kernel,
        out_shape=(jax.ShapeDtypeStruct((t, QKVU_WIDTH), BF16),
                   jax.ShapeDtypeStruct((t, GATE_WIDTH), BF16)),
        grid=(t // TM_PROJ, n_all),
        in_specs=[
            pl.BlockSpec((TM_PROJ, d), lambda i, j: (i, 0)),
            pl.BlockSpec((1, N_ADA, d), lambda i, j: (i // tiles_per_seq, 0, 0)),
            pl.BlockSpec((1, d), lambda i, j: (0, 0)),
            pl.BlockSpec((d, TN_PROJ), lambda i, j: (0, j)),
        ],
        out_specs=(
            pl.BlockSpec((TM_PROJ, TN_PROJ),
                         lambda i, j: (i, jnp.minimum(j, n_qkvu - 1))),
            pl.BlockSpec((TM_PROJ, TN_PROJ),
                         lambda i, j: (i, jnp.maximum(j - n_qkvu, 0))),
        ),
        scratch_shapes=[pltpu.VMEM((TM_PROJ, d), BF16)],
        compiler_params=pltpu.CompilerParams(
            dimension_semantics=("arbitrary", "arbitrary"),
            vmem_limit_bytes=VMEM_LIMIT),
        name="in_proj",
    )(x2d, ada3, norm_g, w_in_b)


def _attn_kernel(sinks_ref, q_ref, kc_ref, vc_ref, kp_ref, vp_ref, o_ref,
                 kfull, vfull, *, steps_per_seq):
    i = pl.program_id(0)
    kfull[0:BLOCK, :] = kp_ref[...]
    kfull[BLOCK:BLOCK + TM_MIX, :] = kc_ref[...]
    vfull[0:BLOCK, :] = vp_ref[...]
    vfull[BLOCK:BLOCK + TM_MIX, :] = vc_ref[...]
    first_step = (i % steps_per_seq) == 0
    nq = Q_PER_KV * BLOCK

    def sb_body(sb, carry):
        r0 = pl.multiple_of(sb * BLOCK, BLOCK)
        row = lax.broadcasted_iota(jnp.int32, (nq, 2 * BLOCK), 0) & (BLOCK - 1)
        col = lax.broadcasted_iota(jnp.int32, (nq, 2 * BLOCK), 1)
        dist = col - row
        lo = jnp.where(jnp.logical_and(first_step, sb == 0), BLOCK, 0)
        valid = (dist > 0) & (dist <= BLOCK) & (col >= lo)
        rg = lax.broadcasted_iota(jnp.int32, (nq, 1), 0) // BLOCK
        for h in range(KV_HEADS):
            kb = kfull[pl.ds(r0, 2 * BLOCK), h * HEAD_DIM:(h + 1) * HEAD_DIM]
            vb = vfull[pl.ds(r0, 2 * BLOCK), h * HEAD_DIM:(h + 1) * HEAD_DIM]
            qg = jnp.concatenate(
                [q_ref[pl.ds(r0, BLOCK),
                       (Q_PER_KV * h + g) * HEAD_DIM:(Q_PER_KV * h + g + 1) * HEAD_DIM]
                 for g in range(Q_PER_KV)], axis=0)
            qg = qg * jnp.asarray(HEAD_DIM ** -0.5, BF16)
            logits = lax.dot_general(qg, kb, (((1,), (1,)), ((), ())),
                                     preferred_element_type=F32)
            logits = jnp.where(valid, logits, MASK_VALUE)
            sink = jnp.full((nq, 1), sinks_ref[Q_PER_KV * h + Q_PER_KV - 1], F32)
            for g in range(Q_PER_KV - 2, -1, -1):
                sink = jnp.where(rg == g, sinks_ref[Q_PER_KV * h + g], sink)
            m = jnp.maximum(jnp.max(logits, axis=1, keepdims=True), sink)
            p = jnp.exp(logits - m)
            denom = jnp.sum(p, axis=1, keepdims=True) + jnp.exp(sink - m)
            o = jnp.dot(p.astype(BF16), vb, preferred_element_type=F32)
            o = (o * (1.0 / denom)).astype(BF16)
            o_ref[pl.ds(r0, BLOCK),
                  h * Q_PER_KV * HEAD_DIM:(h + 1) * Q_PER_KV * HEAD_DIM] = (
                jnp.concatenate([o[g * BLOCK:(g + 1) * BLOCK, :]
                                 for g in range(Q_PER_KV)], axis=1))
        return carry

    lax.fori_loop(0, TM_MIX // BLOCK, sb_body, 0)


def _attn(qkvu, sinks, seq):
    t = qkvu.shape[0]
    steps_per_seq = seq // TM_MIX
    sub = TM_MIX // BLOCK
    kcol = ATTN_WIDTH // KV_WIDTH
    prev = lambda i: jnp.maximum(i * sub - 1, 0)
    return pl.pallas_call(
        functools.partial(_attn_kernel, steps_per_seq=steps_per_seq),
        out_shape=jax.ShapeDtypeStruct((t, ATTN_WIDTH), BF16),
        grid=(t // TM_MIX,),
        in_specs=[
            pl.BlockSpec(memory_space=pltpu.SMEM),
            pl.BlockSpec((TM_MIX, ATTN_WIDTH), lambda i: (i, 0)),
            pl.BlockSpec((TM_MIX, KV_WIDTH), lambda i: (i, kcol)),
            pl.BlockSpec((TM_MIX, KV_WIDTH), lambda i: (i, kcol + 1)),
            pl.BlockSpec((BLOCK, KV_WIDTH), lambda i: (prev(i), kcol)),
            pl.BlockSpec((BLOCK, KV_WIDTH), lambda i: (prev(i), kcol + 1)),
        ],
        out_specs=pl.BlockSpec((TM_MIX, ATTN_WIDTH), lambda i: (i, 0)),
        scratch_shapes=[pltpu.VMEM((BLOCK + TM_MIX, KV_WIDTH), BF16),
                        pltpu.VMEM((BLOCK + TM_MIX, KV_WIDTH), BF16)],
        compiler_params=pltpu.CompilerParams(
            dimension_semantics=("arbitrary",), vmem_limit_bytes=VMEM_LIMIT),
        name="attn",
    )(sinks, qkvu, qkvu, qkvu, qkvu, qkvu)


def _pool_kernel(ulo_ref, uhi_ref, plo_ref, phi_ref, wp_ref, ps_ref, o_ref,
                 ufull, *, steps_per_seq):
    i = pl.program_id(0)
    first_step = (i % steps_per_seq) == 0
    half = POOL_WIDTH // 2
    zeros = jnp.zeros((BLOCK, half), BF16)
    ufull[0:BLOCK, 0:half] = jnp.where(first_step, zeros, plo_ref[...])
    ufull[0:BLOCK, half:POOL_WIDTH] = jnp.where(first_step, zeros, phi_ref[...])
    ufull[BLOCK:BLOCK + TM_MIX, 0:half] = ulo_ref[...]
    ufull[BLOCK:BLOCK + TM_MIX, half:POOL_WIDTH] = uhi_ref[...]
    pos0 = (i % steps_per_seq) * TM_MIX
    row = lax.broadcasted_iota(jnp.int32, (BLOCK, 2 * BLOCK), 0)
    col = lax.broadcasted_iota(jnp.int32, (BLOCK, 2 * BLOCK), 1)
    rel = row + BLOCK - col
    bands = [((rel >= 0) & (rel < w)).astype(F32).astype(BF16)
             for w in POOL_WINDOWS]

    def sb_body(sb, carry):
        r0 = pl.multiple_of(sb * BLOCK, BLOCK)
        pos = pos0 + sb * BLOCK + lax.broadcasted_iota(jnp.int32, (BLOCK, 1), 0)
        for g, w in enumerate(POOL_WINDOWS):
            cols = slice(g * POOL_GROUP_DIM, (g + 1) * POOL_GROUP_DIM)
            band = ufull[pl.ds(r0, 2 * BLOCK), cols]
            ssum = jnp.dot(bands[g], band, preferred_element_type=F32)
            xg = ufull[pl.ds(r0 + BLOCK, BLOCK), cols].astype(F32)
            count = jnp.minimum(pos + 1, w).astype(F32)
            pooled = ssum / count - xg
            mixed = jnp.dot(pooled.astype(BF16), wp_ref[g],
                            preferred_element_type=F32)
            o_ref[pl.ds(r0, BLOCK), cols] = (mixed * ps_ref[:, cols]).astype(BF16)
        return carry

    lax.fori_loop(0, TM_MIX // BLOCK, sb_body, 0)


def _pool(qkvu, w_pool_b, pool_scale, seq):
    t = qkvu.shape[0]
    steps_per_seq = seq // TM_MIX
    sub = TM_MIX // BLOCK
    half = POOL_WIDTH // 2
    ucol = (ATTN_WIDTH + 2 * KV_WIDTH) // half
    prev = lambda i: jnp.maximum(i * sub - 1, 0)
    return pl.pallas_call(
        functools.partial(_pool_kernel, steps_per_seq=steps_per_seq),
        out_shape=jax.ShapeDtypeStruct((t, POOL_WIDTH), BF16),
        grid=(t // TM_MIX,),
        in_specs=[
            pl.BlockSpec((TM_MIX, half), lambda i: (i, ucol)),
            pl.BlockSpec((TM_MIX, half), lambda i: (i, ucol + 1)),
            pl.BlockSpec((BLOCK, half), lambda i: (prev(i), ucol)),
            pl.BlockSpec((BLOCK, half), lambda i: (prev(i), ucol + 1)),
            pl.BlockSpec((len(POOL_WINDOWS), POOL_GROUP_DIM, POOL_GROUP_DIM),
                         lambda i: (0, 0, 0)),
            pl.BlockSpec((1, POOL_WIDTH), lambda i: (0, 0)),
        ],
        out_specs=pl.BlockSpec((TM_MIX, POOL_WIDTH), lambda i: (i, 0)),
        scratch_shapes=[pltpu.VMEM((BLOCK + TM_MIX, POOL_WIDTH), BF16)],
        compiler_params=pltpu.CompilerParams(
            dimension_semantics=("arbitrary",), vmem_limit_bytes=VMEM_LIMIT),
        name="pool",
    )(qkvu, qkvu, qkvu, qkvu, w_pool_b, pool_scale)


def _first_argmax4(v):
    m = jnp.maximum(jnp.maximum(v[0], v[1]), jnp.maximum(v[2], v[3]))
    idx = jnp.where(v[0] == m, 0, jnp.where(v[1] == m, 1,
                                            jnp.where(v[2] == m, 2, 3)))
    return m, idx


def _merge_kernel(attn_ref, pool_ref, ga_ref, gp_ref, x_ref, ada_ref,
                  wa_ref, wp_ref, wo_ref, g2_ref, wr_ref, br_ref,
                  x2_ref, h2_ref, route_ref, counts_ref):
    i = pl.program_id(0)
    a = jnp.dot(attn_ref[...], wa_ref[...], preferred_element_type=F32)
    p = jnp.dot(pool_ref[...], wp_ref[...], preferred_element_type=F32)
    merged = ga_ref[...].astype(F32) * a + gp_ref[...].astype(F32) * p
    out = jnp.dot(merged.astype(BF16), wo_ref[...], preferred_element_type=F32)
    x2 = x_ref[...] + ada_ref[0, 2:3, :] * out
    x2_ref[...] = x2
    h2 = _rms_modulate(x2, g2_ref[...], ada_ref[0, 4:5, :], ada_ref[0, 3:4, :])
    h2_ref[...] = h2

    logits = lax.dot_general(wr_ref[...], h2.astype(BF16),
                             (((1,), (1,)), ((), ())),
                             preferred_element_type=F32) + br_ref[:, 0:1]
    tm = logits.shape[1]
    rows = [logits[r:r + 1, :] for r in range(N_GROUPS + N_EXPERTS)]
    _, gi = _first_argmax4(rows[0:N_GROUPS])
    sel = []
    for e in range(EXPERTS_PER_GROUP):
        v = rows[N_GROUPS + 3 * EXPERTS_PER_GROUP + e]
        for g in range(N_GROUPS - 2, -1, -1):
            v = jnp.where(gi == g, rows[N_GROUPS + g * EXPERTS_PER_GROUP + e], v)
        sel.append(v)
    _, i1 = _first_argmax4(sel)
    rest = [jnp.where(i1 == e, -jnp.inf, sel[e]) for e in range(EXPERTS_PER_GROUP)]
    _, i2 = _first_argmax4(rest)
    code = jnp.minimum(i1, i2) * EXPERTS_PER_GROUP + jnp.maximum(i1, i2)
    cig = jnp.where(code == 1, 0, jnp.where(code == 2, 1, jnp.where(
        code == 3, 2, jnp.where(code == 7, 3, jnp.where(code == 6, 4, 5)))))
    cls = gi * PAIRS_PER_GROUP + cig

    onehot = (lax.broadcasted_iota(jnp.int32, (ROUTER_ROWS, tm), 0) == cls
              ).astype(F32)
    src = lax.broadcasted_iota(jnp.int32, (tm, tm), 0)
    dst = lax.broadcasted_iota(jnp.int32, (tm, tm), 1)
    before = (src < dst).astype(F32).astype(BF16)
    prefix = jnp.dot(onehot.astype(BF16), before, preferred_element_type=F32)

    @pl.when(i == 0)
    def _():
        counts_ref[...] = jnp.zeros_like(counts_ref)

    carry = counts_ref[:, 0:1]
    rank = jnp.sum(onehot * (prefix + carry), axis=0, keepdims=True)
    counts_ref[...] = counts_ref[...] + jnp.sum(onehot, axis=1, keepdims=True)
    route_ref[0:1, :] = cls
    route_ref[1:2, :] = rank.astype(jnp.int32)
    route_ref[2:8, :] = jnp.zeros((6, tm), jnp.int32)


def _merge(attn, pool, gates, x2d, ada3, wa_b, wp_b, wo_b, norm2_g, wr_t, br_col,
           seq):
    t, d = x2d.shape
    tiles_per_seq = seq // TM_MERGE
    const = lambda shape: pl.BlockSpec(shape, lambda i: (0,) * len(shape),
                                       pipeline_mode=pl.Buffered(1))
    return pl.pallas_call(
        _merge_kernel,
        out_shape=(jax.ShapeDtypeStruct((t, d), F32),
                   jax.ShapeDtypeStruct((t, d), F32),
                   jax.ShapeDtypeStruct((8, t), jnp.int32),
                   jax.ShapeDtypeStruct((ROUTER_ROWS, 128), F32)),
        grid=(t // TM_MERGE,),
        in_specs=[
            pl.BlockSpec((TM_MERGE, ATTN_WIDTH), lambda i: (i, 0)),
            pl.BlockSpec((TM_MERGE, POOL_WIDTH), lambda i: (i, 0)),
            pl.BlockSpec((TM_MERGE, d), lambda i: (i, 0)),
            pl.BlockSpec((TM_MERGE, d), lambda i: (i, 1)),
            pl.BlockSpec((TM_MERGE, d), lambda i: (i, 0)),
            pl.BlockSpec((1, N_ADA, d), lambda i: (i // tiles_per_seq, 0, 0)),
            const((ATTN_WIDTH, d)),
            const((POOL_WIDTH, d)),
            const((d, d)),
            const((1, d)),
            const((ROUTER_ROWS, d)),
            const((ROUTER_ROWS, 128)),
        ],
        out_specs=(
            pl.BlockSpec((TM_MERGE, d), lambda i: (i, 0)),
            pl.BlockSpec((TM_MERGE, d), lambda i: (i, 0)),
            pl.BlockSpec((8, TM_MERGE), lambda i: (0, i)),
            pl.BlockSpec((ROUTER_ROWS, 128), lambda i: (0, 0)),
        ),
        compiler_params=pltpu.CompilerParams(
            dimension_semantics=("arbitrary",), vmem_limit_bytes=VMEM_LIMIT),
        name="merge",
    )(attn, pool, gates, gates, x2d, ada3, wa_b, wp_b, wo_b, norm2_g, wr_t, br_col)


def _row_copy(src_ref, src_row, dst_ref, dst_row, sem):
    return pltpu.make_async_copy(src_ref.at[pl.ds(src_row, 1)],
                                 dst_ref.at[pl.ds(dst_row, 1)], sem)


def _dispatch_kernel(pos_ref, h_ref, hs_ref, sem):
    i = pl.program_id(0)
    base = i * TM_DISPATCH

    def issue(r, carry):
        _row_copy(h_ref, r, hs_ref, pos_ref[base + r], sem).start()
        return carry

    lax.fori_loop(0, TM_DISPATCH, issue, 0)

    def drain(r, carry):
        _row_copy(h_ref, r, hs_ref, pos_ref[base + r], sem).wait()
        return carry

    lax.fori_loop(0, TM_DISPATCH, drain, 0)


def _dispatch(pos, h2, n_rows):
    t, d = h2.shape
    return pl.pallas_call(
        _dispatch_kernel,
        out_shape=jax.ShapeDtypeStruct((n_rows, d), F32),
        grid_spec=pltpu.PrefetchScalarGridSpec(
            num_scalar_prefetch=1,
            grid=(t // TM_DISPATCH,),
            in_specs=[pl.BlockSpec((TM_DISPATCH, d), lambda i, pos: (i, 0))],
            out_specs=pl.BlockSpec(memory_space=pl.ANY),
            scratch_shapes=[pltpu.SemaphoreType.DMA(())],
        ),
        compiler_params=pltpu.CompilerParams(
            dimension_semantics=("arbitrary",), vmem_limit_bytes=VMEM_LIMIT,
            has_side_effects=True),
        name="dispatch",
    )(pos, h2)


def _expert_kernel(blk_ref, ea_ref, eb_ref, lane_ref, nvalid_ref, nused_ref,
                   hs_ref, wga_ref, wua_ref, wda_ref, wgb_ref, wub_ref, wdb_ref,
                   wr_ref, br_ref, ys_ref):
    i = pl.program_id(0)

    @pl.when(i < nused_ref[0])
    def _():
        tm = hs_ref.shape[0]
        rows = lax.broadcasted_iota(jnp.int32, (tm, 1), 0)
        h = jnp.where(rows < nvalid_ref[i], hs_ref[...], 0.0).astype(BF16)
        logits = jnp.dot(h, wr_ref[...], preferred_element_type=F32) + br_ref[...]
        lane = lax.broadcasted_iota(jnp.int32, logits.shape, 1)
        is_group = lane < N_GROUPS
        gmax = jnp.max(jnp.where(is_group, logits, -jnp.inf), axis=1, keepdims=True)
        gexp = jnp.where(is_group, jnp.exp(logits - gmax), 0.0)
        pick = lambda v, l: jnp.sum(jnp.where(lane == l, v, 0.0), axis=1,
                                    keepdims=True)
        p_group = pick(gexp, lane_ref[0, i]) / jnp.sum(gexp, axis=1, keepdims=True)
        la = pick(logits, lane_ref[1, i])
        lb = pick(logits, lane_ref[2, i])
        mx = jnp.maximum(la, lb)
        ea = jnp.exp(la - mx)
        eb = jnp.exp(lb - mx)
        inv = p_group / (ea + eb)

        def expert(wg_ref, wu_ref, wd_ref, weight):
            gate = jnp.dot(h, wg_ref[0], preferred_element_type=F32)
            up = jnp.dot(h, wu_ref[0], preferred_element_type=F32)
            act = (jax.nn.silu(gate) * up * weight).astype(BF16)
            return jnp.dot(act, wd_ref[0], preferred_element_type=F32)

        ys_ref[...] = (expert(wga_ref, wua_ref, wda_ref, ea * inv)
                       + expert(wgb_ref, wub_ref, wdb_ref, eb * inv))


def _experts(tile_blk, tile_ea, tile_eb, tile_lanes, tile_nvalid, n_used,
             hs, wg_b, wu_b, wd_b, wr_lanes, br_lanes):
    p_rows, d = hs.shape
    n_tiles = p_rows // TM_EXPERT
    row_map = lambda i, blk, ea, eb, ln, nv, nu: (blk[i], 0)
    wa_map = lambda i, blk, ea, eb, ln, nv, nu: (ea[i], 0, 0)
    wb_map = lambda i, blk, ea, eb, ln, nv, nu: (eb[i], 0, 0)
    const = lambda i, blk, ea, eb, ln, nv, nu: (0, 0)
    return pl.pallas_call(
        _expert_kernel,
        out_shape=jax.ShapeDtypeStruct((p_rows, d), F32),
        grid_spec=pltpu.PrefetchScalarGridSpec(
            num_scalar_prefetch=6,
            grid=(n_tiles,),
            in_specs=[
                pl.BlockSpec((TM_EXPERT, d), row_map),
                pl.BlockSpec((1, d, EXPERT_FF), wa_map),
                pl.BlockSpec((1, d, EXPERT_FF), wa_map),
                pl.BlockSpec((1, EXPERT_FF, d), wa_map),
                pl.BlockSpec((1, d, EXPERT_FF), wb_map),
                pl.BlockSpec((1, d, EXPERT_FF), wb_map),
                pl.BlockSpec((1, EXPERT_FF, d), wb_map),
                pl.BlockSpec((d, ROUTER_LANES), const),
                pl.BlockSpec((1, ROUTER_LANES), const),
            ],
            out_specs=pl.BlockSpec((TM_EXPERT, d), row_map),
        ),
        compiler_params=pltpu.CompilerParams(
            dimension_semantics=("arbitrary",), vmem_limit_bytes=VMEM_LIMIT),
        name="experts",
    )(tile_blk, tile_ea, tile_eb, tile_lanes, tile_nvalid, n_used,
      hs, wg_b, wu_b, wd_b, wg_b, wu_b, wd_b, wr_lanes, br_lanes)


def _combine_kernel(pos_ref, ys_ref, x2_ref, ada_ref, g_ref, o_ref, ybuf, sems):
    i = pl.program_id(0)
    n = pl.num_programs(0)

    def gather(tile, slot):
        def issue(r, carry):
            _row_copy(ys_ref, pos_ref[tile * TM_COMBINE + r],
                      ybuf.at[slot], r, sems.at[slot]).start()
            return carry
        lax.fori_loop(0, TM_COMBINE, issue, 0)

    @pl.when(i == 0)
    def _():
        gather(0, 0)

    @pl.when(i + 1 < n)
    def _():
        gather(i + 1, (i + 1) % 2)

    slot = i % 2

    def drain(r, carry):
        _row_copy(ys_ref, pos_ref[i * TM_COMBINE + r],
                  ybuf.at[slot], r, sems.at[slot]).wait()
        return carry

    lax.fori_loop(0, TM_COMBINE, drain, 0)

    x = x2_ref[...] + ada_ref[0, 5:6, :] * ybuf[slot]
    ms = jnp.mean(x * x, axis=-1, keepdims=True)
    o_ref[...] = x * lax.rsqrt(ms + NORM_EPS) * g_ref[...]


def _combine(pos, ys, x2, ada3, final_g, seq):
    t, d = x2.shape
    tiles_per_seq = seq // TM_COMBINE
    return pl.pallas_call(
        _combine_kernel,
        out_shape=jax.ShapeDtypeStruct((t, d), F32),
        grid_spec=pltpu.PrefetchScalarGridSpec(
            num_scalar_prefetch=1,
            grid=(t // TM_COMBINE,),
            in_specs=[
                pl.BlockSpec(memory_space=pl.ANY),
                pl.BlockSpec((TM_COMBINE, d), lambda i, pos: (i, 0)),
                pl.BlockSpec((1, N_ADA, d),
                             lambda i, pos: (i // tiles_per_seq, 0, 0)),
                pl.BlockSpec((1, d), lambda i, pos: (0, 0)),
            ],
            out_specs=pl.BlockSpec((TM_COMBINE, d), lambda i, pos: (i, 0)),
            scratch_shapes=[pltpu.VMEM((2, TM_COMBINE, d), F32),
                            pltpu.SemaphoreType.DMA((2,))],
        ),
        compiler_params=pltpu.CompilerParams(
            dimension_semantics=("arbitrary",), vmem_limit_bytes=VMEM_LIMIT),
        name="combine",
    )(pos, ys, x2, ada3, final_g)


def _routing_tables(route, counts, n_tiles):
    cls = route[0]
    rank = route[1]
    cnt = counts[:N_CLASSES, 0].astype(jnp.int32)
    tiles_c = (cnt + TM_EXPERT - 1) // TM_EXPERT
    tile_end = jnp.cumsum(tiles_c)
    tile_start = tile_end - tiles_c
    n_used = tile_end[-1]
    pos = tile_start[cls] * TM_EXPERT + rank

    k = jnp.minimum(jnp.arange(n_tiles, dtype=jnp.int32), n_used - 1)
    c = jnp.sum((k[:, None] >= tile_end[None, :]).astype(jnp.int32), axis=1)
    grp = c // PAIRS_PER_GROUP
    cig = c % PAIRS_PER_GROUP
    slot_a = jnp.asarray(CLASS_SLOT_A, jnp.int32)[cig]
    slot_b = jnp.asarray(CLASS_SLOT_B, jnp.int32)[cig]
    ea = grp * EXPERTS_PER_GROUP + slot_a
    eb = grp * EXPERTS_PER_GROUP + slot_b
    lanes = jnp.stack([grp, N_GROUPS + ea, N_GROUPS + eb]).astype(jnp.int32)
    nvalid = jnp.clip(cnt[c] - (k - tile_start[c]) * TM_EXPERT, 0, TM_EXPERT)
    return (pos.astype(jnp.int32), k, ea.astype(jnp.int32), eb.astype(jnp.int32),
            lanes, nvalid.astype(jnp.int32), n_used.reshape(1).astype(jnp.int32))


def kernel(x, c, w_ada, b_ada, norm1_g, w_in, sinks, w_pool, pool_scale,
           w_attn_branch, w_pool_branch, w_out, norm2_g, w_router_group,
           b_router_group, w_router_expert, b_router_expert, w_e_gate, w_e_up,
           w_e_down, final_g):
    b, s, d = x.shape
    t = b * s
    assert w_ada.shape[0] == 1, "single-layer block"
    assert d == D_MODEL and s % TM_PROJ == 0
    l = 0
    n_tiles = t // TM_EXPERT + N_CLASSES
    x2d = x.reshape(t, d)
    ada3 = _ada(c, w_ada[l], b_ada[l]).reshape(b, N_ADA, d)

    qkvu, gates = _in_proj(x2d, ada3, norm1_g[l].reshape(1, d),
                           w_in[l].astype(BF16), s)
    attn = _attn(qkvu, sinks[l], s)
    pool = _pool(qkvu, w_pool[l].astype(BF16), pool_scale[l].reshape(1, -1), s)

    w_r = jnp.concatenate([w_router_group[l], w_router_expert[l]], axis=1)
    b_r = jnp.concatenate([b_router_group[l], b_router_expert[l]])
    n_r = N_GROUPS + N_EXPERTS
    wr_t = jnp.pad(w_r.T, ((0, ROUTER_ROWS - n_r), (0, 0))).astype(BF16)
    br_col = jnp.broadcast_to(
        jnp.pad(b_r, (0, ROUTER_ROWS - n_r))[:, None], (ROUTER_ROWS, 128))
    wr_lanes = jnp.pad(w_r, ((0, 0), (0, ROUTER_LANES - n_r))).astype(BF16)
    br_lanes = jnp.pad(b_r, (0, ROUTER_LANES - n_r)).reshape(1, ROUTER_LANES)

    x2, h2, route, counts = _merge(
        attn, pool, gates, x2d, ada3, w_attn_branch[l].astype(BF16),
        w_pool_branch[l].astype(BF16), w_out[l].astype(BF16),
        norm2_g[l].reshape(1, d), wr_t, br_col, s)

    pos, tile_blk, tile_ea, tile_eb, tile_lanes, tile_nvalid, n_used = (
        _routing_tables(route, counts, n_tiles))
    hs = _dispatch(pos, h2, n_tiles * TM_EXPERT)
    ys = _experts(tile_blk, tile_ea, tile_eb, tile_lanes, tile_nvalid, n_used,
                  hs, w_e_gate[l].astype(BF16), w_e_up[l].astype(BF16),
                  w_e_down[l].astype(BF16), wr_lanes, br_lanes)
    out = _combine(pos, ys, x2, ada3, final_g.reshape(1, d), s)
    return out.reshape(b, s, d)
```

```python
import functools

import jax
import jax.numpy as jnp
from jax import lax
from jax.experimental import pallas as pl
from jax.experimental.pallas import tpu as pltpu

F32 = jnp.float32
BF16 = jnp.bfloat16

D_MODEL = 2048
HEAD_DIM = 64
ATTN_HEADS = 16
KV_HEADS = 4
Q_PER_KV = ATTN_HEADS // KV_HEADS
ATTN_WIDTH = ATTN_HEADS * HEAD_DIM
KV_WIDTH = KV_HEADS * HEAD_DIM
BLOCK = 128
POOL_WINDOWS = (2, 4, 8, 16)
POOL_WIDTH = 1024
POOL_GROUP_DIM = 256
QKVU_WIDTH = ATTN_WIDTH + 2 * KV_WIDTH + POOL_WIDTH
GATE_WIDTH = 2 * D_MODEL
N_GROUPS = 4
EXPERTS_PER_GROUP = 4
N_EXPERTS = 16
EXPERT_FF = 512
N_ADA = 6
NORM_EPS = 1e-6
MASK_VALUE = -1e30

PAIRS_PER_GROUP = 6
N_CLASSES = N_GROUPS * PAIRS_PER_GROUP
CLASS_SLOT_A = (0, 0, 0, 1, 1, 3)
CLASS_SLOT_B = (1, 2, 3, 3, 2, 2)
ROUTER_ROWS = 32
ROUTER_LANES = 128

TM_PROJ = 1024
TN_PROJ = 512
TM_MIX = 512
TM_MERGE = 256
TM_DISPATCH = 512
TM_EXPERT = 256
TM_COMBINE = 256
TN_ADA = 1024
ROW_DMA_UNROLL = 8

VMEM_LIMIT = 52 * 1024 * 1024


def _rms_modulate(x, g, scale, shift):
    ms = jnp.mean(x * x, axis=-1, keepdims=True)
    return (x * lax.rsqrt(ms + NORM_EPS) * g) * (1.0 + scale) + shift


def _ada_kernel(cb_ref, w_ref, b_ref, o_ref):
    nb = cb_ref.shape[0]
    d = w_ref.shape[0]
    nchunk = w_ref.shape[1] // 128

    def body(kb, accs):
        k0 = pl.multiple_of(kb * 8, 8)
        new = list(accs)
        cbs = [cb_ref[b, pl.ds(k0, 8), :] for b in range(nb)]
        for j in range(nchunk):
            w = w_ref[pl.ds(k0, 8), j * 128:(j + 1) * 128]
            for b in range(nb):
                new[b * nchunk + j] = new[b * nchunk + j] + w * cbs[b]
        return tuple(new)

    init = tuple(jnp.zeros((8, 128), F32) for _ in range(nb * nchunk))
    accs = lax.fori_loop(0, d // 8, body, init, unroll=2)
    for b in range(nb):
        for j in range(nchunk):
            o_ref[b:b + 1, j * 128:(j + 1) * 128] = (
                jnp.sum(accs[b * nchunk + j], axis=0, keepdims=True)
                + b_ref[:, j * 128:(j + 1) * 128])


def _ada(c, w_ada, b_ada):
    nb, d = c.shape
    n = w_ada.shape[1]
    cb = jnp.broadcast_to(c[:, :, None], (nb, d, 128))
    return pl.pallas_call(
        _ada_kernel,
        out_shape=jax.ShapeDtypeStruct((nb, n), F32),
        grid=(n // TN_ADA,),
        in_specs=[
            pl.BlockSpec((nb, d, 128), lambda j: (0, 0, 0)),
            pl.BlockSpec((d, TN_ADA), lambda j: (0, j)),
            pl.BlockSpec((1, TN_ADA), lambda j: (0, j)),
        ],
        out_specs=pl.BlockSpec((nb, TN_ADA), lambda j: (0, j)),
        compiler_params=pltpu.CompilerParams(
            dimension_semantics=("arbitrary",), vmem_limit_bytes=VMEM_LIMIT),
        name="ada",
    )(cb, w_ada, b_ada.reshape(1, n))


def _in_proj_kernel(x_ref, ada_ref, g_ref, w_ref, qkvu_ref, gates_ref, h_ref):
    j = pl.program_id(1)
    n_qkvu = QKVU_WIDTH // TN_PROJ

    @pl.when(j == 0)
    def _():
        chunk = 256
        for r in range(0, TM_PROJ, chunk):
            h = _rms_modulate(x_ref[r:r + chunk, :], g_ref[...],
                              ada_ref[0, 1:2, :], ada_ref[0, 0:1, :])
            h_ref[r:r + chunk, :] = h.astype(BF16)

    acc = jnp.dot(h_ref[...], w_ref[...], preferred_element_type=F32)

    @pl.when(j < n_qkvu)
    def _():
        qkvu_ref[...] = acc.astype(BF16)

    @pl.when(j >= n_qkvu)
    def _():
        gates_ref[...] = jax.nn.sigmoid(acc).astype(BF16)


def _in_proj(x2d, ada3, norm_g, w_in_b, seq):
    t, d = x2d.shape
    n_qkvu = QKVU_WIDTH // TN_PROJ
    n_all = (QKVU_WIDTH + GATE_WIDTH) // TN_PROJ
    tiles_per_seq = seq // TM_PROJ
    return pl.pallas_call(
        _in_proj_kernel,
        out_shape=(jax.ShapeDtypeStruct((t, QKVU_WIDTH), BF16),
                   jax.ShapeDtypeStruct((t, GATE_WIDTH), BF16)),
        grid=(t // TM_PROJ, n_all),
        in_specs=[
            pl.BlockSpec((TM_PROJ, d), lambda i, j: (i, 0)),
            pl.BlockSpec((1, N_ADA, d), lambda i, j: (i // tiles_per_seq, 0, 0)),
            pl.BlockSpec((1, d), lambda i, j: (0, 0)),
            pl.BlockSpec((d, TN_PROJ), lambda i, j: (0, j)),
        ],
        out_specs=(
            pl.BlockSpec((TM_PROJ, TN_PROJ),
                         lambda i, j: (i, jnp.minimum(j, n_qkvu - 1))),
            pl.BlockSpec((TM_PROJ, TN_PROJ),
                         lambda i, j: (i, jnp.maximum(j - n_qkvu, 0))),
        ),
        scratch_shapes=[pltpu.VMEM((TM_PROJ, d), BF16)],
        compiler_params=pltpu.CompilerParams(
            dimension_semantics=("arbitrary", "arbitrary"),
            vmem_limit_bytes=VMEM_LIMIT),
        name="in_proj",
    )(x2d, ada3, norm_g, w_in_b)


def _attn_kernel(sinks_ref, q_ref, kc_ref, vc_ref, kp_ref, vp_ref, o_ref,
                 kfull, vfull, *, steps_per_seq):
    i = pl.program_id(0)
    kfull[0:BLOCK, :] = kp_ref[...]
    kfull[BLOCK:BLOCK + TM_MIX, :] = kc_ref[...]
    vfull[0:BLOCK, :] = vp_ref[...]
    vfull[BLOCK:BLOCK + TM_MIX, :] = vc_ref[...]
    first_step = (i % steps_per_seq) == 0
    nq = Q_PER_KV * BLOCK

    def sb_body(sb, carry):
        r0 = pl.multiple_of(sb * BLOCK, BLOCK)
        row = lax.broadcasted_iota(jnp.int32, (nq, 2 * BLOCK), 0) & (BLOCK - 1)
        col = lax.broadcasted_iota(jnp.int32, (nq, 2 * BLOCK), 1)
        dist = col - row
        lo = jnp.where(jnp.logical_and(first_step, sb == 0), BLOCK, 0)
        valid = (dist > 0) & (dist <= BLOCK) & (col >= lo)
        rg = lax.broadcasted_iota(jnp.int32, (nq, 1), 0) // BLOCK
        for h in range(KV_HEADS):
            kb = kfull[pl.ds(r0, 2 * BLOCK), h * HEAD_DIM:(h + 1) * HEAD_DIM]
            vb = vfull[pl.ds(r0, 2 * BLOCK), h * HEAD_DIM:(h + 1) * HEAD_DIM]
            qg = jnp.concatenate(
                [q_ref[pl.ds(r0, BLOCK),
                       (Q_PER_KV * h + g) * HEAD_DIM:(Q_PER_KV * h + g + 1) * HEAD_DIM]
                 for g in range(Q_PER_KV)], axis=0)
            qg = qg * jnp.asarray(HEAD_DIM ** -0.5, BF16)
            logits = lax.dot_general(qg, kb, (((1,), (1,)), ((), ())),
                                     preferred_element_type=F32)
            logits = jnp.where(valid, logits, MASK_VALUE)
            sink = jnp.full((nq, 1), sinks_ref[Q_PER_KV * h + Q_PER_KV - 1], F32)
            for g in range(Q_PER_KV - 2, -1, -1):
                sink = jnp.where(rg == g, sinks_ref[Q_PER_KV * h + g], sink)
            m = jnp.maximum(jnp.max(logits, axis=1, keepdims=True), sink)
            p = jnp.exp(logits - m)
            denom = jnp.sum(p, axis=1, keepdims=True) + jnp.exp(sink - m)
            o = jnp.dot(p.astype(BF16), vb, preferred_element_type=F32)
            o = (o * (1.0 / denom)).astype(BF16)
            o_ref[pl.ds(r0, BLOCK),
                  h * Q_PER_KV * HEAD_DIM:(h + 1) * Q_PER_KV * HEAD_DIM] = (
                jnp.concatenate([o[g * BLOCK:(g + 1) * BLOCK, :]
                                 for g in range(Q_PER_KV)], axis=1))
        return carry

    lax.fori_loop(0, TM_MIX // BLOCK, sb_body, 0)


def _attn(qkvu, sinks, seq):
    t = qkvu.shape[0]
    steps_per_seq = seq // TM_MIX
    sub = TM_MIX // BLOCK
    kcol = ATTN_WIDTH // KV_WIDTH
    prev = lambda i: jnp.maximum(i * sub - 1, 0)
    return pl.pallas_call(
        functools.partial(_attn_kernel, steps_per_seq=steps_per_seq),
        out_shape=jax.ShapeDtypeStruct((t, ATTN_WIDTH), BF16),
        grid=(t // TM_MIX,),
        in_specs=[
            pl.BlockSpec(memory_space=pltpu.SMEM),
            pl.BlockSpec((TM_MIX, ATTN_WIDTH), lambda i: (i, 0)),
            pl.BlockSpec((TM_MIX, KV_WIDTH), lambda i: (i, kcol)),
            pl.BlockSpec((TM_MIX, KV_WIDTH), lambda i: (i, kcol + 1)),
            pl.BlockSpec((BLOCK, KV_WIDTH), lambda i: (prev(i), kcol)),
            pl.BlockSpec((BLOCK, KV_WIDTH), lambda i: (prev(i), kcol + 1)),
        ],
        out_specs=pl.BlockSpec((TM_MIX, ATTN_WIDTH), lambda i: (i, 0)),
        scratch_shapes=[pltpu.VMEM((BLOCK + TM_MIX, KV_WIDTH), BF16),
                        pltpu.VMEM((BLOCK + TM_MIX, KV_WIDTH), BF16)],
        compiler_params=pltpu.CompilerParams(
            dimension_semantics=("arbitrary",), vmem_limit_bytes=VMEM_LIMIT),
        name="attn",
    )(sinks, qkvu, qkvu, qkvu, qkvu, qkvu)


def _pool_kernel(ulo_ref, uhi_ref, plo_ref, phi_ref, wp_ref, ps_ref, o_ref,
                 ufull, *, steps_per_seq):
    i = pl.program_id(0)
    first_step = (i % steps_per_seq) == 0
    half = POOL_WIDTH // 2
    zeros = jnp.zeros((BLOCK, half), BF16)
    ufull[0:BLOCK, 0:half] = jnp.where(first_step, zeros, plo_ref[...])
    ufull[0:BLOCK, half:POOL_WIDTH] = jnp.where(first_step, zeros, phi_ref[...])
    ufull[BLOCK:BLOCK + TM_MIX, 0:half] = ulo_ref[...]
    ufull[BLOCK:BLOCK + TM_MIX, half:POOL_WIDTH] = uhi_ref[...]
    pos0 = (i % steps_per_seq) * TM_MIX
    row = lax.broadcasted_iota(jnp.int32, (BLOCK, 2 * BLOCK), 0)
    col = lax.broadcasted_iota(jnp.int32, (BLOCK, 2 * BLOCK), 1)
    rel = row + BLOCK - col
    bands = [((rel >= 0) & (rel < w)).astype(F32).astype(BF16)
             for w in POOL_WINDOWS]

    def sb_body(sb, carry):
        r0 = pl.multiple_of(sb * BLOCK, BLOCK)
        pos = pos0 + sb * BLOCK + lax.broadcasted_iota(jnp.int32, (BLOCK, 1), 0)
        for g, w in enumerate(POOL_WINDOWS):
            cols = slice(g * POOL_GROUP_DIM, (g + 1) * POOL_GROUP_DIM)
            band = ufull[pl.ds(r0, 2 * BLOCK), cols]
            ssum = jnp.dot(bands[g], band, preferred_element_type=F32)
            xg = ufull[pl.ds(r0 + BLOCK, BLOCK), cols].astype(F32)
            count = jnp.minimum(pos + 1, w).astype(F32)
            pooled = ssum / count - xg
            mixed = jnp.dot(pooled.astype(BF16), wp_ref[g],
                            preferred_element_type=F32)
            o_ref[pl.ds(r0, BLOCK), cols] = (mixed * ps_ref[:, cols]).astype(BF16)
        return carry

    lax.fori_loop(0, TM_MIX // BLOCK, sb_body, 0)


def _pool(qkvu, w_pool_b, pool_scale, seq):
    t = qkvu.shape[0]
    steps_per_seq = seq // TM_MIX
    sub = TM_MIX // BLOCK
    half = POOL_WIDTH // 2
    ucol = (ATTN_WIDTH + 2 * KV_WIDTH) // half
    prev = lambda i: jnp.maximum(i * sub - 1, 0)
    return pl.pallas_call(
        functools.partial(_pool_kernel, steps_per_seq=steps_per_seq),
        out_shape=jax.ShapeDtypeStruct((t, POOL_WIDTH), BF16),
        grid=(t // TM_MIX,),
        in_specs=[
            pl.BlockSpec((TM_MIX, half), lambda i: (i, ucol)),
            pl.BlockSpec((TM_MIX, half), lambda i: (i, ucol + 1)),
            pl.BlockSpec((BLOCK, half), lambda i: (prev(i), ucol)),
            pl.BlockSpec((BLOCK, half), lambda i: (prev(i), ucol + 1)),
            pl.BlockSpec((len(POOL_WINDOWS), POOL_GROUP_DIM, POOL_GROUP_DIM),
                         lambda i: (0, 0, 0)),
            pl.BlockSpec((1, POOL_WIDTH), lambda i: (0, 0)),
        ],
        out_specs=pl.BlockSpec((TM_MIX, POOL_WIDTH), lambda i: (i, 0)),
        scratch_shapes=[pltpu.VMEM((BLOCK + TM_MIX, POOL_WIDTH), BF16)],
        compiler_params=pltpu.CompilerParams(
            dimension_semantics=("arbitrary",), vmem_limit_bytes=VMEM_LIMIT),
        name="pool",
    )(qkvu, qkvu, qkvu, qkvu, w_pool_b, pool_scale)


def _first_argmax4(v):
    m = jnp.maximum(jnp.maximum(v[0], v[1]), jnp.maximum(v[2], v[3]))
    idx = jnp.where(v[0] == m, 0, jnp.where(v[1] == m, 1,
                                            jnp.where(v[2] == m, 2, 3)))
    return m, idx


def _merge_kernel(attn_ref, pool_ref, ga_ref, gp_ref, x_ref, ada_ref,
                  wa_ref, wp_ref, wo_ref, g2_ref, wr_ref, br_ref,
                  x2_ref, h2_ref, route_ref, counts_ref):
    i = pl.program_id(0)
    a = jnp.dot(attn_ref[...], wa_ref[...], preferred_element_type=F32)
    p = jnp.dot(pool_ref[...], wp_ref[...], preferred_element_type=F32)
    merged = ga_ref[...].astype(F32) * a + gp_ref[...].astype(F32) * p
    out = jnp.dot(merged.astype(BF16), wo_ref[...], preferred_element_type=F32)
    x2 = x_ref[...] + ada_ref[0, 2:3, :] * out
    x2_ref[...] = x2
    h2 = _rms_modulate(x2, g2_ref[...], ada_ref[0, 4:5, :], ada_ref[0, 3:4, :])
    h2_ref[...] = h2

    logits = lax.dot_general(wr_ref[...], h2.astype(BF16),
                             (((1,), (1,)), ((), ())),
                             preferred_element_type=F32) + br_ref[:, 0:1]
    tm = logits.shape[1]
    rows = [logits[r:r + 1, :] for r in range(N_GROUPS + N_EXPERTS)]
    _, gi = _first_argmax4(rows[0:N_GROUPS])
    sel = []
    for e in range(EXPERTS_PER_GROUP):
        v = rows[N_GROUPS + 3 * EXPERTS_PER_GROUP + e]
        for g in range(N_GROUPS - 2, -1, -1):
            v = jnp.where(gi == g, rows[N_GROUPS + g * EXPERTS_PER_GROUP + e], v)
        sel.append(v)
    _, i1 = _first_argmax4(sel)
    rest = [jnp.where(i1 == e, -jnp.inf, sel[e]) for e in range(EXPERTS_PER_GROUP)]
    _, i2 = _first_argmax4(rest)
    code = jnp.minimum(i1, i2) * EXPERTS_PER_GROUP + jnp.maximum(i1, i2)
    cig = jnp.where(code == 1, 0, jnp.where(code == 2, 1, jnp.where(
        code == 3, 2, jnp.where(code == 7, 3, jnp.where(code == 6, 4, 5)))))
    cls = gi * PAIRS_PER_GROUP + cig

    onehot = (lax.broadcasted_iota(jnp.int32, (ROUTER_ROWS, tm), 0) == cls
              ).astype(F32)
    src = lax.broadcasted_iota(jnp.int32, (tm, tm), 0)
    dst = lax.broadcasted_iota(jnp.int32, (tm, tm), 1)
    before = (src < dst).astype(F32).astype(BF16)
    prefix = jnp.dot(onehot.astype(BF16), before, preferred_element_type=F32)

    @pl.when(i == 0)
    def _():
        counts_ref[...] = jnp.zeros_like(counts_ref)

    carry = counts_ref[:, 0:1]
    rank = jnp.sum(onehot * (prefix + carry), axis=0, keepdims=True)
    counts_ref[...] = counts_ref[...] + jnp.sum(onehot, axis=1, keepdims=True)
    route_ref[0:1, :] = cls
    route_ref[1:2, :] = rank.astype(jnp.int32)
    route_ref[2:8, :] = jnp.zeros((6, tm), jnp.int32)


def _merge(attn, pool, gates, x2d, ada3, wa_b, wp_b, wo_b, norm2_g, wr_t, br_col,
           seq):
    t, d = x2d.shape
    tiles_per_seq = seq // TM_MERGE
    const = lambda shape: pl.BlockSpec(shape, lambda i: (0,) * len(shape),
                                       pipeline_mode=pl.Buffered(1))
    return pl.pallas_call(
        _merge_kernel,
        out_shape=(jax.ShapeDtypeStruct((t, d), F32),
                   jax.ShapeDtypeStruct((t, d), F32),
                   jax.ShapeDtypeStruct((8, t), jnp.int32),
                   jax.ShapeDtypeStruct((ROUTER_ROWS, 128), F32)),
        grid=(t // TM_MERGE,),
        in_specs=[
            pl.BlockSpec((TM_MERGE, ATTN_WIDTH), lambda i: (i, 0)),
            pl.BlockSpec((TM_MERGE, POOL_WIDTH), lambda i: (i, 0)),
            pl.BlockSpec((TM_MERGE, d), lambda i: (i, 0)),
            pl.BlockSpec((TM_MERGE, d), lambda i: (i, 1)),
            pl.BlockSpec((TM_MERGE, d), lambda i: (i, 0)),
            pl.BlockSpec((1, N_ADA, d), lambda i: (i // tiles_per_seq, 0, 0)),
            const((ATTN_WIDTH, d)),
            const((POOL_WIDTH, d)),
            const((d, d)),
            const((1, d)),
            const((ROUTER_ROWS, d)),
            const((ROUTER_ROWS, 128)),
        ],
        out_specs=(
            pl.BlockSpec((TM_MERGE, d), lambda i: (i, 0)),
            pl.BlockSpec((TM_MERGE, d), lambda i: (i, 0)),
            pl.BlockSpec((8, TM_MERGE), lambda i: (0, i)),
            pl.BlockSpec((ROUTER_ROWS, 128), lambda i: (0, 0)),
        ),
        compiler_params=pltpu.CompilerParams(
            dimension_semantics=("arbitrary",), vmem_limit_bytes=VMEM_LIMIT),
        name="merge",
    )(attn, pool, gates, gates, x2d, ada3, wa_b, wp_b, wo_b, norm2_g, wr_t, br_col)


def _row_copy(src_ref, src_row, dst_ref, dst_row, sem):
    return pltpu.make_async_copy(src_ref.at[pl.ds(src_row, 1)],
                                 dst_ref.at[pl.ds(dst_row, 1)], sem)


def _dispatch_kernel(pos_ref, h_ref, hs_ref, sem):
    i = pl.program_id(0)
    base = i * TM_DISPATCH

    def issue(r, carry):
        _row_copy(h_ref, r, hs_ref, pos_ref[base + r], sem).start()
        return carry

    lax.fori_loop(0, TM_DISPATCH, issue, 0, unroll=ROW_DMA_UNROLL)
    pltpu.make_async_copy(h_ref, hs_ref.at[pl.ds(0, TM_DISPATCH)], sem).wait()


def _dispatch(pos, h2, n_rows):
    t, d = h2.shape
    return pl.pallas_call(
        _dispatch_kernel,
        out_shape=jax.ShapeDtypeStruct((n_rows, d), F32),
        grid_spec=pltpu.PrefetchScalarGridSpec(
            num_scalar_prefetch=1,
            grid=(t // TM_DISPATCH,),
            in_specs=[pl.BlockSpec((TM_DISPATCH, d), lambda i, pos: (i, 0))],
            out_specs=pl.BlockSpec(memory_space=pl.ANY),
            scratch_shapes=[pltpu.SemaphoreType.DMA(())],
        ),
        compiler_params=pltpu.CompilerParams(
            dimension_semantics=("arbitrary",), vmem_limit_bytes=VMEM_LIMIT,
            has_side_effects=True),
        name="dispatch",
    )(pos, h2)


def _expert_kernel(blk_ref, ea_ref, eb_ref, lane_ref, nvalid_ref, nused_ref,
                   hs_ref, wga_ref, wua_ref, wda_ref, wgb_ref, wub_ref, wdb_ref,
                   wr_ref, br_ref, ys_ref):
    i = pl.program_id(0)

    @pl.when(i < nused_ref[0])
    def _():
        tm = hs_ref.shape[0]
        rows = lax.broadcasted_iota(jnp.int32, (tm, 1), 0)
        h = jnp.where(rows < nvalid_ref[i], hs_ref[...], 0.0).astype(BF16)
        logits = jnp.dot(h, wr_ref[...], preferred_element_type=F32) + br_ref[...]
        lane = lax.broadcasted_iota(jnp.int32, logits.shape, 1)
        is_group = lane < N_GROUPS
        gmax = jnp.max(jnp.where(is_group, logits, -jnp.inf), axis=1, keepdims=True)
        gexp = jnp.where(is_group, jnp.exp(logits - gmax), 0.0)
        pick = lambda v, l: jnp.sum(jnp.where(lane == l, v, 0.0), axis=1,
                                    keepdims=True)
        p_group = pick(gexp, lane_ref[0, i]) / jnp.sum(gexp, axis=1, keepdims=True)
        la = pick(logits, lane_ref[1, i])
        lb = pick(logits, lane_ref[2, i])
        mx = jnp.maximum(la, lb)
        ea = jnp.exp(la - mx)
        eb = jnp.exp(lb - mx)
        inv = p_group / (ea + eb)

        def expert(wg_ref, wu_ref, wd_ref, weight):
            gate = jnp.dot(h, wg_ref[0], preferred_element_type=F32)
            up = jnp.dot(h, wu_ref[0], preferred_element_type=F32)
            act = (jax.nn.silu(gate) * up * weight).astype(BF16)
            return jnp.dot(act, wd_ref[0], preferred_element_type=F32)

        ys_ref[...] = (expert(wga_ref, wua_ref, wda_ref, ea * inv)
                       + expert(wgb_ref, wub_ref, wdb_ref, eb * inv))


def _experts(tile_blk, tile_ea, tile_eb, tile_lanes, tile_nvalid, n_used,
             hs, wg_b, wu_b, wd_b, wr_lanes, br_lanes):
    p_rows, d = hs.shape
    n_tiles = p_rows // TM_EXPERT
    row_map = lambda i, blk, ea, eb, ln, nv, nu: (blk[i], 0)
    wa_map = lambda i, blk, ea, eb, ln, nv, nu: (ea[i], 0, 0)
    wb_map = lambda i, blk, ea, eb, ln, nv, nu: (eb[i], 0, 0)
    const = lambda i, blk, ea, eb, ln, nv, nu: (0, 0)
    return pl.pallas_call(
        _expert_kernel,
        out_shape=jax.ShapeDtypeStruct((p_rows, d), F32),
        grid_spec=pltpu.PrefetchScalarGridSpec(
            num_scalar_prefetch=6,
            grid=(n_tiles,),
            in_specs=[
                pl.BlockSpec((TM_EXPERT, d), row_map),
                pl.BlockSpec((1, d, EXPERT_FF), wa_map),
                pl.BlockSpec((1, d, EXPERT_FF), wa_map),
                pl.BlockSpec((1, EXPERT_FF, d), wa_map),
                pl.BlockSpec((1, d, EXPERT_FF), wb_map),
                pl.BlockSpec((1, d, EXPERT_FF), wb_map),
                pl.BlockSpec((1, EXPERT_FF, d), wb_map),
                pl.BlockSpec((d, ROUTER_LANES), const),
                pl.BlockSpec((1, ROUTER_LANES), const),
            ],
            out_specs=pl.BlockSpec((TM_EXPERT, d), row_map),
        ),
        compiler_params=pltpu.CompilerParams(
            dimension_semantics=("arbitrary",), vmem_limit_bytes=VMEM_LIMIT),
        name="experts",
    )(tile_blk, tile_ea, tile_eb, tile_lanes, tile_nvalid, n_used,
      hs, wg_b, wu_b, wd_b, wg_b, wu_b, wd_b, wr_lanes, br_lanes)


def _combine_kernel(pos_ref, ys_ref, x2_ref, ada_ref, g_ref, o_ref, ybuf, sems):
    i = pl.program_id(0)
    n = pl.num_programs(0)

    def gather(tile, slot):
        def issue(r, carry):
            _row_copy(ys_ref, pos_ref[tile * TM_COMBINE + r],
                      ybuf.at[slot], r, sems.at[slot]).start()
            return carry
        lax.fori_loop(0, TM_COMBINE, issue, 0, unroll=ROW_DMA_UNROLL)

    @pl.when(i == 0)
    def _():
        gather(0, 0)

    @pl.when(i + 1 < n)
    def _():
        gather(i + 1, (i + 1) % 2)

    slot = i % 2
    pltpu.make_async_copy(ys_ref.at[pl.ds(0, TM_COMBINE)], ybuf.at[slot],
                          sems.at[slot]).wait()

    x = x2_ref[...] + ada_ref[0, 5:6, :] * ybuf[slot]
    ms = jnp.mean(x * x, axis=-1, keepdims=True)
    o_ref[...] = x * lax.rsqrt(ms + NORM_EPS) * g_ref[...]


def _combine(pos, ys, x2, ada3, final_g, seq):
    t, d = x2.shape
    tiles_per_seq = seq // TM_COMBINE
    return pl.pallas_call(
        _combine_kernel,
        out_shape=jax.ShapeDtypeStruct((t, d), F32),
        grid_spec=pltpu.PrefetchScalarGridSpec(
            num_scalar_prefetch=1,
            grid=(t // TM_COMBINE,),
            in_specs=[
                pl.BlockSpec(memory_space=pl.ANY),
                pl.BlockSpec((TM_COMBINE, d), lambda i, pos: (i, 0)),
                pl.BlockSpec((1, N_ADA, d),
                             lambda i, pos: (i // tiles_per_seq, 0, 0)),
                pl.BlockSpec((1, d), lambda i, pos: (0, 0)),
            ],
            out_specs=pl.BlockSpec((TM_COMBINE, d), lambda i, pos: (i, 0)),
            scratch_shapes=[pltpu.VMEM((2, TM_COMBINE, d), F32),
                            pltpu.SemaphoreType.DMA((2,))],
        ),
        compiler_params=pltpu.CompilerParams(
            dimension_semantics=("arbitrary",), vmem_limit_bytes=VMEM_LIMIT),
        name="combine",
    )(pos, ys, x2, ada3, final_g)


def _routing_tables(route, counts, n_tiles):
    cls = route[0]
    rank = route[1]
    cnt = counts[:N_CLASSES, 0].astype(jnp.int32)
    tiles_c = (cnt + TM_EXPERT - 1) // TM_EXPERT
    tile_end = jnp.cumsum(tiles_c)
    tile_start = tile_end - tiles_c
    n_used = tile_end[-1]
    pos = tile_start[cls] * TM_EXPERT + rank

    k = jnp.minimum(jnp.arange(n_tiles, dtype=jnp.int32), n_used - 1)
    c = jnp.sum((k[:, None] >= tile_end[None, :]).astype(jnp.int32), axis=1)
    grp = c // PAIRS_PER_GROUP
    cig = c % PAIRS_PER_GROUP
    slot_a = jnp.asarray(CLASS_SLOT_A, jnp.int32)[cig]
    slot_b = jnp.asarray(CLASS_SLOT_B, jnp.int32)[cig]
    ea = grp * EXPERTS_PER_GROUP + slot_a
    eb = grp * EXPERTS_PER_GROUP + slot_b
    lanes = jnp.stack([grp, N_GROUPS + ea, N_GROUPS + eb]).astype(jnp.int32)
    nvalid = jnp.clip(cnt[c] - (k - tile_start[c]) * TM_EXPERT, 0, TM_EXPERT)
    return (pos.astype(jnp.int32), k, ea.astype(jnp.int32), eb.astype(jnp.int32),
            lanes, nvalid.astype(jnp.int32), n_used.reshape(1).astype(jnp.int32))


def kernel(x, c, w_ada, b_ada, norm1_g, w_in, sinks, w_pool, pool_scale,
           w_attn_branch, w_pool_branch, w_out, norm2_g, w_router_group,
           b_router_group, w_router_expert, b_router_expert, w_e_gate, w_e_up,
           w_e_down, final_g):
    b, s, d = x.shape
    t = b * s
    assert w_ada.shape[0] == 1, "single-layer block"
    assert d == D_MODEL and s % TM_PROJ == 0
    l = 0
    n_tiles = t // TM_EXPERT + N_CLASSES
    x2d = x.reshape(t, d)
    ada3 = _ada(c, w_ada[l], b_ada[l]).reshape(b, N_ADA, d)

    qkvu, gates = _in_proj(x2d, ada3, norm1_g[l].reshape(1, d),
                           w_in[l].astype(BF16), s)
    attn = _attn(qkvu, sinks[l], s)
    pool = _pool(qkvu, w_pool[l].astype(BF16), pool_scale[l].reshape(1, -1), s)

    w_r = jnp.concatenate([w_router_group[l], w_router_expert[l]], axis=1)
    b_r = jnp.concatenate([b_router_group[l], b_router_expert[l]])
    n_r = N_GROUPS + N_EXPERTS
    wr_t = jnp.pad(w_r.T, ((0, ROUTER_ROWS - n_r), (0, 0))).astype(BF16)
    br_col = jnp.broadcast_to(
        jnp.pad(b_r, (0, ROUTER_ROWS - n_r))[:, None], (ROUTER_ROWS, 128))
    wr_lanes = jnp.pad(w_r, ((0, 0), (0, ROUTER_LANES - n_r))).astype(BF16)
    br_lanes = jnp.pad(b_r, (0, ROUTER_LANES - n_r)).reshape(1, ROUTER_LANES)

    x2, h2, route, counts = _merge(
        attn, pool, gates, x2d, ada3, w_attn_branch[l].astype(BF16),
        w_pool_branch[l].astype(BF16), w_out[l].astype(BF16),
        norm2_g[l].reshape(1, d), wr_t, br_col, s)

    pos, tile_blk, tile_ea, tile_eb, tile_lanes, tile_nvalid, n_used = (
        _routing_tables(route, counts, n_tiles))
    hs = _dispatch(pos, h2, n_tiles * TM_EXPERT)
    ys = _experts(tile_blk, tile_ea, tile_eb, tile_lanes, tile_nvalid, n_used,
                  hs, w_e_gate[l].astype(BF16), w_e_up[l].astype(BF16),
                  w_e_down[l].astype(BF16), wr_lanes, br_lanes)
    out = _combine(pos, ys, x2, ada3, final_g.reshape(1, d), s)
    return out.reshape(b, s, d)
```

```python
import functools

import jax
import jax.numpy as jnp
from jax import lax
from jax.experimental import pallas as pl
from jax.experimental.pallas import tpu as pltpu

F32 = jnp.float32
BF16 = jnp.bfloat16

D_MODEL = 2048
HEAD_DIM = 64
ATTN_HEADS = 16
KV_HEADS = 4
Q_PER_KV = ATTN_HEADS // KV_HEADS
ATTN_WIDTH = ATTN_HEADS * HEAD_DIM
KV_WIDTH = KV_HEADS * HEAD_DIM
BLOCK = 128
POOL_WINDOWS = (2, 4, 8, 16)
POOL_WIDTH = 1024
POOL_GROUP_DIM = 256
QKVU_WIDTH = ATTN_WIDTH + 2 * KV_WIDTH + POOL_WIDTH
GATE_WIDTH = 2 * D_MODEL
N_GROUPS = 4
EXPERTS_PER_GROUP = 4
N_EXPERTS = 16
EXPERT_FF = 512
N_ADA = 6
NORM_EPS = 1e-6
MASK_VALUE = -1e30

PAIRS_PER_GROUP = 6
N_CLASSES = N_GROUPS * PAIRS_PER_GROUP
CLASS_SLOT_A = (0, 0, 0, 1, 1, 3)
CLASS_SLOT_B = (1, 2, 3, 3, 2, 2)
ROUTER_ROWS = 32
ROUTER_LANES = 128

TM_PROJ = 1024
TN_PROJ = 512
TM_MIX = 512
TM_MERGE = 256
TM_EXPERT = 256
TM_COMBINE = 256
TN_ADA = 1024

VMEM_LIMIT = 52 * 1024 * 1024


def _rms_modulate(x, g, scale, shift):
    ms = jnp.mean(x * x, axis=-1, keepdims=True)
    return (x * lax.rsqrt(ms + NORM_EPS) * g) * (1.0 + scale) + shift


def _ada_kernel(cb_ref, w_ref, b_ref, o_ref):
    nb = cb_ref.shape[0]
    d = w_ref.shape[0]
    nchunk = w_ref.shape[1] // 128

    def body(kb, accs):
        k0 = pl.multiple_of(kb * 8, 8)
        new = list(accs)
        cbs = [cb_ref[b, pl.ds(k0, 8), :] for b in range(nb)]
        for j in range(nchunk):
            w = w_ref[pl.ds(k0, 8), j * 128:(j + 1) * 128]
            for b in range(nb):
                new[b * nchunk + j] = new[b * nchunk + j] + w * cbs[b]
        return tuple(new)

    init = tuple(jnp.zeros((8, 128), F32) for _ in range(nb * nchunk))
    accs = lax.fori_loop(0, d // 8, body, init, unroll=2)
    for b in range(nb):
        for j in range(nchunk):
            o_ref[b:b + 1, j * 128:(j + 1) * 128] = (
                jnp.sum(accs[b * nchunk + j], axis=0, keepdims=True)
                + b_ref[:, j * 128:(j + 1) * 128])


def _ada(c, w_ada, b_ada):
    nb, d = c.shape
    n = w_ada.shape[1]
    cb = jnp.broadcast_to(c[:, :, None], (nb, d, 128))
    return pl.pallas_call(
        _ada_kernel,
        out_shape=jax.ShapeDtypeStruct((nb, n), F32),
        grid=(n // TN_ADA,),
        in_specs=[
            pl.BlockSpec((nb, d, 128), lambda j: (0, 0, 0)),
            pl.BlockSpec((d, TN_ADA), lambda j: (0, j)),
            pl.BlockSpec((1, TN_ADA), lambda j: (0, j)),
        ],
        out_specs=pl.BlockSpec((nb, TN_ADA), lambda j: (0, j)),
        compiler_params=pltpu.CompilerParams(
            dimension_semantics=("arbitrary",), vmem_limit_bytes=VMEM_LIMIT),
        name="ada",
    )(cb, w_ada, b_ada.reshape(1, n))


def _in_proj_kernel(x_ref, ada_ref, g_ref, w_ref, qkvu_ref, gates_ref, h_ref):
    j = pl.program_id(1)
    n_qkvu = QKVU_WIDTH // TN_PROJ

    @pl.when(j == 0)
    def _():
        chunk = 256
        for r in range(0, TM_PROJ, chunk):
            h = _rms_modulate(x_ref[r:r + chunk, :], g_ref[...],
                              ada_ref[0, 1:2, :], ada_ref[0, 0:1, :])
            h_ref[r:r + chunk, :] = h.astype(BF16)

    acc = jnp.dot(h_ref[...], w_ref[...], preferred_element_type=F32)

    @pl.when(j < n_qkvu)
    def _():
        qkvu_ref[...] = acc.astype(BF16)

    @pl.when(j >= n_qkvu)
    def _():
        gates_ref[...] = jax.nn.sigmoid(acc).astype(BF16)


def _in_proj(x2d, ada3, norm_g, w_in_b, seq):
    t, d = x2d.shape
    n_qkvu = QKVU_WIDTH // TN_PROJ
    n_all = (QKVU_WIDTH + GATE_WIDTH) // TN_PROJ
    tiles_per_seq = seq // TM_PROJ
    return pl.pallas_call(
        _in_proj_kernel,
        out_shape=(jax.ShapeDtypeStruct((t, QKVU_WIDTH), BF16),
                   jax.ShapeDtypeStruct((t, GATE_WIDTH), BF16)),
        grid=(t // TM_PROJ, n_all),
        in_specs=[
            pl.BlockSpec((TM_PROJ, d), lambda i, j: (i, 0)),
            pl.BlockSpec((1, N_ADA, d), lambda i, j: (i // tiles_per_seq, 0, 0)),
            pl.BlockSpec((1, d), lambda i, j: (0, 0)),
            pl.BlockSpec((d, TN_PROJ), lambda i, j: (0, j)),
        ],
        out_specs=(
            pl.BlockSpec((TM_PROJ, TN_PROJ),
                         lambda i, j: (i, jnp.minimum(j, n_qkvu - 1))),
            pl.BlockSpec((TM_PROJ, TN_PROJ),
                         lambda i, j: (i, jnp.maximum(j - n_qkvu, 0))),
        ),
        scratch_shapes=[pltpu.VMEM((TM_PROJ, d), BF16)],
        compiler_params=pltpu.CompilerParams(
            dimension_semantics=("arbitrary", "arbitrary"),
            vmem_limit_bytes=VMEM_LIMIT),
        name="in_proj",
    )(x2d, ada3, norm_g, w_in_b)


def _attn_kernel(sinks_ref, q_ref, kc_ref, vc_ref, kp_ref, vp_ref, o_ref,
                 kfull, vfull, *, steps_per_seq):
    i = pl.program_id(0)
    kfull[0:BLOCK, :] = kp_ref[...]
    kfull[BLOCK:BLOCK + TM_MIX, :] = kc_ref[...]
    vfull[0:BLOCK, :] = vp_ref[...]
    vfull[BLOCK:BLOCK + TM_MIX, :] = vc_ref[...]
    first_step = (i % steps_per_seq) == 0
    nq = Q_PER_KV * BLOCK

    def sb_body(sb, carry):
        r0 = pl.multiple_of(sb * BLOCK, BLOCK)
        row = lax.broadcasted_iota(jnp.int32, (nq, 2 * BLOCK), 0) & (BLOCK - 1)
        col = lax.broadcasted_iota(jnp.int32, (nq, 2 * BLOCK), 1)
        dist = col - row
        lo = jnp.where(jnp.logical_and(first_step, sb == 0), BLOCK, 0)
        valid = (dist > 0) & (dist <= BLOCK) & (col >= lo)
        rg = lax.broadcasted_iota(jnp.int32, (nq, 1), 0) // BLOCK
        for h in range(KV_HEADS):
            kb = kfull[pl.ds(r0, 2 * BLOCK), h * HEAD_DIM:(h + 1) * HEAD_DIM]
            vb = vfull[pl.ds(r0, 2 * BLOCK), h * HEAD_DIM:(h + 1) * HEAD_DIM]
            qg = jnp.concatenate(
                [q_ref[pl.ds(r0, BLOCK),
                       (Q_PER_KV * h + g) * HEAD_DIM:(Q_PER_KV * h + g + 1) * HEAD_DIM]
                 for g in range(Q_PER_KV)], axis=0)
            qg = qg * jnp.asarray(HEAD_DIM ** -0.5, BF16)
            logits = lax.dot_general(qg, kb, (((1,), (1,)), ((), ())),
                                     preferred_element_type=F32)
            logits = jnp.where(valid, logits, MASK_VALUE)
            sink = jnp.full((nq, 1), sinks_ref[Q_PER_KV * h + Q_PER_KV - 1], F32)
            for g in range(Q_PER_KV - 2, -1, -1):
                sink = jnp.where(rg == g, sinks_ref[Q_PER_KV * h + g], sink)
            m = jnp.maximum(jnp.max(logits, axis=1, keepdims=True), sink)
            p = jnp.exp(logits - m)
            denom = jnp.sum(p, axis=1, keepdims=True) + jnp.exp(sink - m)
            o = jnp.dot(p.astype(BF16), vb, preferred_element_type=F32)
            o = (o * (1.0 / denom)).astype(BF16)
            o_ref[pl.ds(r0, BLOCK),
                  h * Q_PER_KV * HEAD_DIM:(h + 1) * Q_PER_KV * HEAD_DIM] = (
                jnp.concatenate([o[g * BLOCK:(g + 1) * BLOCK, :]
                                 for g in range(Q_PER_KV)], axis=1))
        return carry

    lax.fori_loop(0, TM_MIX // BLOCK, sb_body, 0)


def _attn(qkvu, sinks, seq):
    t = qkvu.shape[0]
    steps_per_seq = seq // TM_MIX
    sub = TM_MIX // BLOCK
    kcol = ATTN_WIDTH // KV_WIDTH
    prev = lambda i: jnp.maximum(i * sub - 1, 0)
    return pl.pallas_call(
        functools.partial(_attn_kernel, steps_per_seq=steps_per_seq),
        out_shape=jax.ShapeDtypeStruct((t, ATTN_WIDTH), BF16),
        grid=(t // TM_MIX,),
        in_specs=[
            pl.BlockSpec(memory_space=pltpu.SMEM),
            pl.BlockSpec((TM_MIX, ATTN_WIDTH), lambda i: (i, 0)),
            pl.BlockSpec((TM_MIX, KV_WIDTH), lambda i: (i, kcol)),
            pl.BlockSpec((TM_MIX, KV_WIDTH), lambda i: (i, kcol + 1)),
            pl.BlockSpec((BLOCK, KV_WIDTH), lambda i: (prev(i), kcol)),
            pl.BlockSpec((BLOCK, KV_WIDTH), lambda i: (prev(i), kcol + 1)),
        ],
        out_specs=pl.BlockSpec((TM_MIX, ATTN_WIDTH), lambda i: (i, 0)),
        scratch_shapes=[pltpu.VMEM((BLOCK + TM_MIX, KV_WIDTH), BF16),
                        pltpu.VMEM((BLOCK + TM_MIX, KV_WIDTH), BF16)],
        compiler_params=pltpu.CompilerParams(
            dimension_semantics=("arbitrary",), vmem_limit_bytes=VMEM_LIMIT),
        name="attn",
    )(sinks, qkvu, qkvu, qkvu, qkvu, qkvu)


def _pool_kernel(ulo_ref, uhi_ref, plo_ref, phi_ref, wp_ref, ps_ref, o_ref,
                 ufull, *, steps_per_seq):
    i = pl.program_id(0)
    first_step = (i % steps_per_seq) == 0
    half = POOL_WIDTH // 2
    zeros = jnp.zeros((BLOCK, half), BF16)
    ufull[0:BLOCK, 0:half] = jnp.where(first_step, zeros, plo_ref[...])
    ufull[0:BLOCK, half:POOL_WIDTH] = jnp.where(first_step, zeros, phi_ref[...])
    ufull[BLOCK:BLOCK + TM_MIX, 0:half] = ulo_ref[...]
    ufull[BLOCK:BLOCK + TM_MIX, half:POOL_WIDTH] = uhi_ref[...]
    pos0 = (i % steps_per_seq) * TM_MIX
    row = lax.broadcasted_iota(jnp.int32, (BLOCK, 2 * BLOCK), 0)
    col = lax.broadcasted_iota(jnp.int32, (BLOCK, 2 * BLOCK), 1)
    rel = row + BLOCK - col
    bands = [((rel >= 0) & (rel < w)).astype(F32).astype(BF16)
             for w in POOL_WINDOWS]

    def sb_body(sb, carry):
        r0 = pl.multiple_of(sb * BLOCK, BLOCK)
        pos = pos0 + sb * BLOCK + lax.broadcasted_iota(jnp.int32, (BLOCK, 1), 0)
        for g, w in enumerate(POOL_WINDOWS):
            cols = slice(g * POOL_GROUP_DIM, (g + 1) * POOL_GROUP_DIM)
            band = ufull[pl.ds(r0, 2 * BLOCK), cols]
            ssum = jnp.dot(bands[g], band, preferred_element_type=F32)
            xg = ufull[pl.ds(r0 + BLOCK, BLOCK), cols].astype(F32)
            count = jnp.minimum(pos + 1, w).astype(F32)
            pooled = ssum / count - xg
            mixed = jnp.dot(pooled.astype(BF16), wp_ref[g],
                            preferred_element_type=F32)
            o_ref[pl.ds(r0, BLOCK), cols] = (mixed * ps_ref[:, cols]).astype(BF16)
        return carry

    lax.fori_loop(0, TM_MIX // BLOCK, sb_body, 0)


def _pool(qkvu, w_pool_b, pool_scale, seq):
    t = qkvu.shape[0]
    steps_per_seq = seq // TM_MIX
    sub = TM_MIX // BLOCK
    half = POOL_WIDTH // 2
    ucol = (ATTN_WIDTH + 2 * KV_WIDTH) // half
    prev = lambda i: jnp.maximum(i * sub - 1, 0)
    return pl.pallas_call(
        functools.partial(_pool_kernel, steps_per_seq=steps_per_seq),
        out_shape=jax.ShapeDtypeStruct((t, POOL_WIDTH), BF16),
        grid=(t // TM_MIX,),
        in_specs=[
            pl.BlockSpec((TM_MIX, half), lambda i: (i, ucol)),
            pl.BlockSpec((TM_MIX, half), lambda i: (i, ucol + 1)),
            pl.BlockSpec((BLOCK, half), lambda i: (prev(i), ucol)),
            pl.BlockSpec((BLOCK, half), lambda i: (prev(i), ucol + 1)),
            pl.BlockSpec((len(POOL_WINDOWS), POOL_GROUP_DIM, POOL_GROUP_DIM),
                         lambda i: (0, 0, 0)),
            pl.BlockSpec((1, POOL_WIDTH), lambda i: (0, 0)),
        ],
        out_specs=pl.BlockSpec((TM_MIX, POOL_WIDTH), lambda i: (i, 0)),
        scratch_shapes=[pltpu.VMEM((BLOCK + TM_MIX, POOL_WIDTH), BF16)],
        compiler_params=pltpu.CompilerParams(
            dimension_semantics=("arbitrary",), vmem_limit_bytes=VMEM_LIMIT),
        name="pool",
    )(qkvu, qkvu, qkvu, qkvu, w_pool_b, pool_scale)


def _first_argmax4(v):
    m = jnp.maximum(jnp.maximum(v[0], v[1]), jnp.maximum(v[2], v[3]))
    idx = jnp.where(v[0] == m, 0, jnp.where(v[1] == m, 1,
                                            jnp.where(v[2] == m, 2, 3)))
    return m, idx


def _merge_kernel(attn_ref, pool_ref, ga_ref, gp_ref, x_ref, ada_ref,
                  wa_ref, wp_ref, wo_ref, g2_ref, wr_ref, br_ref,
                  x2_ref, h2_ref, route_ref, counts_ref):
    i = pl.program_id(0)
    a = jnp.dot(attn_ref[...], wa_ref[...], preferred_element_type=F32)
    p = jnp.dot(pool_ref[...], wp_ref[...], preferred_element_type=F32)
    merged = ga_ref[...].astype(F32) * a + gp_ref[...].astype(F32) * p
    out = jnp.dot(merged.astype(BF16), wo_ref[...], preferred_element_type=F32)
    x2 = x_ref[...] + ada_ref[0, 2:3, :] * out
    x2_ref[...] = x2
    h2 = _rms_modulate(x2, g2_ref[...], ada_ref[0, 4:5, :], ada_ref[0, 3:4, :])
    h2_ref[...] = h2

    logits = lax.dot_general(wr_ref[...], h2.astype(BF16),
                             (((1,), (1,)), ((), ())),
                             preferred_element_type=F32) + br_ref[:, 0:1]
    tm = logits.shape[1]
    rows = [logits[r:r + 1, :] for r in range(N_GROUPS + N_EXPERTS)]
    _, gi = _first_argmax4(rows[0:N_GROUPS])
    sel = []
    for e in range(EXPERTS_PER_GROUP):
        v = rows[N_GROUPS + 3 * EXPERTS_PER_GROUP + e]
        for g in range(N_GROUPS - 2, -1, -1):
            v = jnp.where(gi == g, rows[N_GROUPS + g * EXPERTS_PER_GROUP + e], v)
        sel.append(v)
    _, i1 = _first_argmax4(sel)
    rest = [jnp.where(i1 == e, -jnp.inf, sel[e]) for e in range(EXPERTS_PER_GROUP)]
    _, i2 = _first_argmax4(rest)
    code = jnp.minimum(i1, i2) * EXPERTS_PER_GROUP + jnp.maximum(i1, i2)
    cig = jnp.where(code == 1, 0, jnp.where(code == 2, 1, jnp.where(
        code == 3, 2, jnp.where(code == 7, 3, jnp.where(code == 6, 4, 5)))))
    cls = gi * PAIRS_PER_GROUP + cig

    onehot = (lax.broadcasted_iota(jnp.int32, (ROUTER_ROWS, tm), 0) == cls
              ).astype(F32)
    src = lax.broadcasted_iota(jnp.int32, (tm, tm), 0)
    dst = lax.broadcasted_iota(jnp.int32, (tm, tm), 1)
    before = (src < dst).astype(F32).astype(BF16)
    prefix = jnp.dot(onehot.astype(BF16), before, preferred_element_type=F32)

    @pl.when(i == 0)
    def _():
        counts_ref[...] = jnp.zeros_like(counts_ref)

    carry = counts_ref[:, 0:1]
    rank = jnp.sum(onehot * (prefix + carry), axis=0, keepdims=True)
    counts_ref[...] = counts_ref[...] + jnp.sum(onehot, axis=1, keepdims=True)
    route_ref[0:1, :] = cls
    route_ref[1:2, :] = rank.astype(jnp.int32)
    route_ref[2:8, :] = jnp.zeros((6, tm), jnp.int32)


def _merge(attn, pool, gates, x2d, ada3, wa_b, wp_b, wo_b, norm2_g, wr_t, br_col,
           seq):
    t, d = x2d.shape
    tiles_per_seq = seq // TM_MERGE
    const = lambda shape: pl.BlockSpec(shape, lambda i: (0,) * len(shape),
                                       pipeline_mode=pl.Buffered(1))
    return pl.pallas_call(
        _merge_kernel,
        out_shape=(jax.ShapeDtypeStruct((t, d), F32),
                   jax.ShapeDtypeStruct((t, d), F32),
                   jax.ShapeDtypeStruct((8, t), jnp.int32),
                   jax.ShapeDtypeStruct((ROUTER_ROWS, 128), F32)),
        grid=(t // TM_MERGE,),
        in_specs=[
            pl.BlockSpec((TM_MERGE, ATTN_WIDTH), lambda i: (i, 0)),
            pl.BlockSpec((TM_MERGE, POOL_WIDTH), lambda i: (i, 0)),
            pl.BlockSpec((TM_MERGE, d), lambda i: (i, 0)),
            pl.BlockSpec((TM_MERGE, d), lambda i: (i, 1)),
            pl.BlockSpec((TM_MERGE, d), lambda i: (i, 0)),
            pl.BlockSpec((1, N_ADA, d), lambda i: (i // tiles_per_seq, 0, 0)),
            const((ATTN_WIDTH, d)),
            const((POOL_WIDTH, d)),
            const((d, d)),
            const((1, d)),
            const((ROUTER_ROWS, d)),
            const((ROUTER_ROWS, 128)),
        ],
        out_specs=(
            pl.BlockSpec((TM_MERGE, d), lambda i: (i, 0)),
            pl.BlockSpec((TM_MERGE, d), lambda i: (i, 0)),
            pl.BlockSpec((8, TM_MERGE), lambda i: (0, i)),
            pl.BlockSpec((ROUTER_ROWS, 128), lambda i: (0, 0)),
        ),
        compiler_params=pltpu.CompilerParams(
            dimension_semantics=("arbitrary",), vmem_limit_bytes=VMEM_LIMIT),
        name="merge",
    )(attn, pool, gates, gates, x2d, ada3, wa_b, wp_b, wo_b, norm2_g, wr_t, br_col)


def _row_copy(src_ref, src_row, dst_ref, dst_row, sem):
    return pltpu.make_async_copy(src_ref.at[pl.ds(src_row, 1)],
                                 dst_ref.at[pl.ds(dst_row, 1)], sem)


def _expert_kernel(src_ref, blk_ref, ea_ref, eb_ref, lane_ref, nused_ref,
                   h2_ref, wga_ref, wua_ref, wda_ref, wgb_ref, wub_ref, wdb_ref,
                   wr_ref, br_ref, ys_ref, hbuf, sems):
    i = pl.program_id(0)
    nused = nused_ref[0]

    def gather(tile, slot):
        for r in range(TM_EXPERT):
            _row_copy(h2_ref, src_ref[tile * TM_EXPERT + r], hbuf.at[slot], r,
                      sems.at[slot]).start()

    def wait_gather(slot):
        pltpu.make_async_copy(h2_ref.at[pl.ds(0, TM_EXPERT)], hbuf.at[slot],
                              sems.at[slot]).wait()

    @pl.when(i == 0)
    def _():
        gather(0, 0)

    @pl.when(i == nused)
    def _():
        wait_gather(i % 2)

    @pl.when(jnp.logical_and(i >= nused, i < pl.num_programs(0) - 1))
    def _():
        ys_ref[...] = jnp.zeros_like(ys_ref)

    @pl.when(i < nused)
    def _():
        slot = i % 2
        wait_gather(slot)
        h = hbuf[slot].astype(BF16)
        gather(i + 1, 1 - slot)
        logits =jnp.dot(h, wr_ref[...], preferred_element_type=F32) + br_ref[...]
        lane = lax.broadcasted_iota(jnp.int32, logits.shape, 1)
        is_group = lane < N_GROUPS
        gmax = jnp.max(jnp.where(is_group, logits, -jnp.inf), axis=1, keepdims=True)
        gexp = jnp.where(is_group, jnp.exp(logits - gmax), 0.0)
        pick = lambda v, l: jnp.sum(jnp.where(lane == l, v, 0.0), axis=1,
                                    keepdims=True)
        p_group = pick(gexp, lane_ref[0, i]) / jnp.sum(gexp, axis=1, keepdims=True)
        la = pick(logits, lane_ref[1, i])
        lb = pick(logits, lane_ref[2, i])
        mx = jnp.maximum(la, lb)
        ea = jnp.exp(la - mx)
        eb = jnp.exp(lb - mx)
        inv = p_group / (ea + eb)

        def expert(wg_ref, wu_ref, wd_ref, weight):
            gate = jnp.dot(h, wg_ref[0], preferred_element_type=F32)
            up = jnp.dot(h, wu_ref[0], preferred_element_type=F32)
            act = (jax.nn.silu(gate) * up * weight).astype(BF16)
            return jnp.dot(act, wd_ref[0], preferred_element_type=F32)

        ys_ref[...] = (expert(wga_ref, wua_ref, wda_ref, ea * inv)
                       + expert(wgb_ref, wub_ref, wdb_ref, eb * inv))


def _experts(src, tile_blk, tile_ea, tile_eb, tile_lanes, n_used,
             h2, wg_b, wu_b, wd_b, wr_lanes, br_lanes, n_tiles):
    d = h2.shape[1]
    row_map = lambda i, src, blk, ea, eb, ln, nu: (blk[i], 0)
    wa_map = lambda i, src, blk, ea, eb, ln, nu: (ea[i], 0, 0)
    wb_map = lambda i, src, blk, ea, eb, ln, nu: (eb[i], 0, 0)
    const = lambda i, src, blk, ea, eb, ln, nu: (0, 0)
    return pl.pallas_call(
        _expert_kernel,
        out_shape=jax.ShapeDtypeStruct((n_tiles * TM_EXPERT, d), F32),
        grid_spec=pltpu.PrefetchScalarGridSpec(
            num_scalar_prefetch=6,
            grid=(n_tiles + 1,),
            in_specs=[
                pl.BlockSpec(memory_space=pl.ANY),
                pl.BlockSpec((1, d, EXPERT_FF), wa_map),
                pl.BlockSpec((1, d, EXPERT_FF), wa_map),
                pl.BlockSpec((1, EXPERT_FF, d), wa_map),
                pl.BlockSpec((1, d, EXPERT_FF), wb_map),
                pl.BlockSpec((1, d, EXPERT_FF), wb_map),
                pl.BlockSpec((1, EXPERT_FF, d), wb_map),
                pl.BlockSpec((d, ROUTER_LANES), const),
                pl.BlockSpec((1, ROUTER_LANES), const),
            ],
            out_specs=pl.BlockSpec((TM_EXPERT, d), row_map),
            scratch_shapes=[pltpu.VMEM((2, TM_EXPERT, d), F32),
                            pltpu.SemaphoreType.DMA((2,))],
        ),
        compiler_params=pltpu.CompilerParams(
            dimension_semantics=("arbitrary",), vmem_limit_bytes=VMEM_LIMIT),
        name="experts",
    )(src, tile_blk, tile_ea, tile_eb, tile_lanes, n_used,
      h2, wg_b, wu_b, wd_b, wg_b, wu_b, wd_b, wr_lanes, br_lanes)


def _combine_kernel(pos_ref, ys_ref, x2_ref, ada_ref, g_ref, o_ref, ybuf, sems):
    i = pl.program_id(0)
    n = pl.num_programs(0)

    def gather(tile, slot):
        for r in range(TM_COMBINE):
            _row_copy(ys_ref, pos_ref[tile * TM_COMBINE + r],
                      ybuf.at[slot], r, sems.at[slot]).start()

    @pl.when(i == 0)
    def _():
        gather(0, 0)

    @pl.when(i + 1 < n)
    def _():
        gather(i + 1, (i + 1) % 2)

    slot = i % 2
    pltpu.make_async_copy(ys_ref.at[pl.ds(0, TM_COMBINE)], ybuf.at[slot],
                          sems.at[slot]).wait()

    x = x2_ref[...] + ada_ref[0, 5:6, :] * ybuf[slot]
    ms = jnp.mean(x * x, axis=-1, keepdims=True)
    o_ref[...] = x * lax.rsqrt(ms + NORM_EPS) * g_ref[...]


def _combine(pos, ys, x2, ada3, final_g, seq):
    t, d = x2.shape
    tiles_per_seq = seq // TM_COMBINE
    return pl.pallas_call(
        _combine_kernel,
        out_shape=jax.ShapeDtypeStruct((t, d), F32),
        grid_spec=pltpu.PrefetchScalarGridSpec(
            num_scalar_prefetch=1,
            grid=(t // TM_COMBINE,),
            in_specs=[
                pl.BlockSpec(memory_space=pl.ANY),
                pl.BlockSpec((TM_COMBINE, d), lambda i, pos: (i, 0)),
                pl.BlockSpec((1, N_ADA, d),
                             lambda i, pos: (i // tiles_per_seq, 0, 0)),
                pl.BlockSpec((1, d), lambda i, pos: (0, 0)),
            ],
            out_specs=pl.BlockSpec((TM_COMBINE, d), lambda i, pos: (i, 0)),
            scratch_shapes=[pltpu.VMEM((2, TM_COMBINE, d), F32),
                            pltpu.SemaphoreType.DMA((2,))],
        ),
        compiler_params=pltpu.CompilerParams(
            dimension_semantics=("arbitrary",), vmem_limit_bytes=VMEM_LIMIT),
        name="combine",
    )(pos, ys, x2, ada3, final_g)


def _routing_tables(route, counts, n_tiles):
    cls = route[0]
    rank = route[1]
    cnt = counts[:N_CLASSES, 0].astype(jnp.int32)
    tiles_c = (cnt + TM_EXPERT - 1) // TM_EXPERT
    tile_end = jnp.cumsum(tiles_c)
    tile_start = tile_end - tiles_c
    n_used = tile_end[-1]
    pos = (tile_start[cls] * TM_EXPERT + rank).astype(jnp.int32)
    src = jnp.zeros(((n_tiles + 1) * TM_EXPERT,), jnp.int32).at[pos].set(
        jnp.arange(cls.shape[0], dtype=jnp.int32), unique_indices=True)

    steps = jnp.arange(n_tiles + 1, dtype=jnp.int32)
    blk = jnp.minimum(steps, n_tiles - 1)
    k = jnp.minimum(steps, n_used - 1)
    c = jnp.sum((k[:, None] >= tile_end[None, :]).astype(jnp.int32), axis=1)
    grp = c // PAIRS_PER_GROUP
    cig = c % PAIRS_PER_GROUP
    slot_a = jnp.asarray(CLASS_SLOT_A, jnp.int32)[cig]
    slot_b = jnp.asarray(CLASS_SLOT_B, jnp.int32)[cig]
    ea = grp * EXPERTS_PER_GROUP + slot_a
    eb = grp * EXPERTS_PER_GROUP + slot_b
    lanes = jnp.stack([grp, N_GROUPS + ea, N_GROUPS + eb]).astype(jnp.int32)
    return (pos, src, blk, ea.astype(jnp.int32), eb.astype(jnp.int32), lanes,
            n_used.reshape(1).astype(jnp.int32))


def kernel(x, c, w_ada, b_ada, norm1_g, w_in, sinks, w_pool, pool_scale,
           w_attn_branch, w_pool_branch, w_out, norm2_g, w_router_group,
           b_router_group, w_router_expert, b_router_expert, w_e_gate, w_e_up,
           w_e_down, final_g):
    b, s, d = x.shape
    t = b * s
    assert w_ada.shape[0] == 1, "single-layer block"
    assert d == D_MODEL and s % TM_PROJ == 0
    l = 0
    n_tiles = t // TM_EXPERT + N_CLASSES
    x2d = x.reshape(t, d)
    ada3 = _ada(c, w_ada[l], b_ada[l]).reshape(b, N_ADA, d)

    qkvu, gates = _in_proj(x2d, ada3, norm1_g[l].reshape(1, d),
                           w_in[l].astype(BF16), s)
    attn = _attn(qkvu, sinks[l], s)
    pool = _pool(qkvu, w_pool[l].astype(BF16), pool_scale[l].reshape(1, -1), s)

    w_r = jnp.concatenate([w_router_group[l], w_router_expert[l]], axis=1)
    b_r = jnp.concatenate([b_router_group[l], b_router_expert[l]])
    n_r = N_GROUPS + N_EXPERTS
    wr_t = jnp.pad(w_r.T, ((0, ROUTER_ROWS - n_r), (0, 0))).astype(BF16)
    br_col = jnp.broadcast_to(
        jnp.pad(b_r, (0, ROUTER_ROWS - n_r))[:, None], (ROUTER_ROWS, 128))
    wr_lanes = jnp.pad(w_r, ((0, 0), (0, ROUTER_LANES - n_r))).astype(BF16)
    br_lanes = jnp.pad(b_r, (0, ROUTER_LANES - n_r)).reshape(1, ROUTER_LANES)

    x2, h2, route, counts = _merge(
        attn, pool, gates, x2d, ada3, w_attn_branch[l].astype(BF16),
        w_pool_branch[l].astype(BF16), w_out[l].astype(BF16),
        norm2_g[l].reshape(1, d), wr_t, br_col, s)

    pos, src, tile_blk, tile_ea, tile_eb, tile_lanes, n_used = (
        _routing_tables(route, counts, n_tiles))
    ys = _experts(src, tile_blk, tile_ea, tile_eb, tile_lanes, n_used,
                  h2, w_e_gate[l].astype(BF16), w_e_up[l].astype(BF16),
                  w_e_down[l].astype(BF16), wr_lanes, br_lanes, n_tiles)
    out = _combine(pos, ys, x2, ada3, final_g.reshape(1, d), s)
    return out.reshape(b, s, d)
```

```python
import functools

import jax
import jax.numpy as jnp
from jax import lax
from jax.experimental import pallas as pl
from jax.experimental.pallas import tpu as pltpu

F32 = jnp.float32
BF16 = jnp.bfloat16

D_MODEL = 2048
HEAD_DIM = 64
ATTN_HEADS = 16
KV_HEADS = 4
Q_PER_KV = ATTN_HEADS // KV_HEADS
ATTN_WIDTH = ATTN_HEADS * HEAD_DIM
KV_WIDTH = KV_HEADS * HEAD_DIM
BLOCK = 128
POOL_WINDOWS = (2, 4, 8, 16)
POOL_WIDTH = 1024
POOL_GROUP_DIM = 256
QKVU_WIDTH = ATTN_WIDTH + 2 * KV_WIDTH + POOL_WIDTH
GATE_WIDTH = 2 * D_MODEL
N_GROUPS = 4
EXPERTS_PER_GROUP = 4
N_EXPERTS = 16
EXPERT_FF = 512
N_ADA = 6
NORM_EPS = 1e-6
MASK_VALUE = -1e30

PAIRS_PER_GROUP = 6
N_CLASSES = N_GROUPS * PAIRS_PER_GROUP
CLASS_SLOT_A = (0, 0, 0, 1, 1, 3)
CLASS_SLOT_B = (1, 2, 3, 3, 2, 2)
ROUTER_ROWS = 32
ROUTER_LANES = 128

TM_PROJ = 1024
TN_PROJ = 512
TM_MIX = 512
TM_MERGE = 256
TM_EXPERT = 256
TM_COMBINE = 256
TN_ADA = 1024

VMEM_LIMIT = 52 * 1024 * 1024


def _rms_modulate(x, g, scale, shift):
    ms = jnp.mean(x * x, axis=-1, keepdims=True)
    return (x * lax.rsqrt(ms + NORM_EPS) * g) * (1.0 + scale) + shift


def _ada_kernel(cb_ref, w_ref, b_ref, o_ref):
    nb = cb_ref.shape[0]
    d = w_ref.shape[0]
    nchunk = w_ref.shape[1] // 128

    def body(kb, accs):
        k0 = pl.multiple_of(kb * 8, 8)
        new = list(accs)
        cbs = [cb_ref[b, pl.ds(k0, 8), :] for b in range(nb)]
        for j in range(nchunk):
            w = w_ref[pl.ds(k0, 8), j * 128:(j + 1) * 128]
            for b in range(nb):
                new[b * nchunk + j] = new[b * nchunk + j] + w * cbs[b]
        return tuple(new)

    init = tuple(jnp.zeros((8, 128), F32) for _ in range(nb * nchunk))
    accs = lax.fori_loop(0, d // 8, body, init, unroll=2)
    for b in range(nb):
        for j in range(nchunk):
            o_ref[b:b + 1, j * 128:(j + 1) * 128] = (
                jnp.sum(accs[b * nchunk + j], axis=0, keepdims=True)
                + b_ref[:, j * 128:(j + 1) * 128])


def _ada(c, w_ada, b_ada):
    nb, d = c.shape
    n = w_ada.shape[1]
    cb = jnp.broadcast_to(c[:, :, None], (nb, d, 128))
    return pl.pallas_call(
        _ada_kernel,
        out_shape=jax.ShapeDtypeStruct((nb, n), F32),
        grid=(n // TN_ADA,),
        in_specs=[
            pl.BlockSpec((nb, d, 128), lambda j: (0, 0, 0)),
            pl.BlockSpec((d, TN_ADA), lambda j: (0, j)),
            pl.BlockSpec((1, TN_ADA), lambda j: (0, j)),
        ],
        out_specs=pl.BlockSpec((nb, TN_ADA), lambda j: (0, j)),
        compiler_params=pltpu.CompilerParams(
            dimension_semantics=("arbitrary",), vmem_limit_bytes=VMEM_LIMIT),
        name="ada",
    )(cb, w_ada, b_ada.reshape(1, n))


def _in_proj_kernel(x_ref, ada_ref, g_ref, w_ref, qkvu_ref, gates_ref, h_ref):
    j = pl.program_id(1)
    n_qkvu = QKVU_WIDTH // TN_PROJ

    @pl.when(j == 0)
    def _():
        chunk = 256
        for r in range(0, TM_PROJ, chunk):
            h = _rms_modulate(x_ref[r:r + chunk, :], g_ref[...],
                              ada_ref[0, 1:2, :], ada_ref[0, 0:1, :])
            h_ref[r:r + chunk, :] = h.astype(BF16)

    acc = jnp.dot(h_ref[...], w_ref[...], preferred_element_type=F32)

    @pl.when(j < n_qkvu)
    def _():
        qkvu_ref[...] = acc.astype(BF16)

    @pl.when(j >= n_qkvu)
    def _():
        gates_ref[...] = jax.nn.sigmoid(acc).astype(BF16)


def _in_proj(x2d, ada3, norm_g, w_in_b, seq):
    t, d = x2d.shape
    n_qkvu = QKVU_WIDTH // TN_PROJ
    n_all = (QKVU_WIDTH + GATE_WIDTH) // TN_PROJ
    tiles_per_seq = seq // TM_PROJ
    return pl.pallas_call(
        _in_proj_kernel,
        out_shape=(jax.ShapeDtypeStruct((t, QKVU_WIDTH), BF16),
                   jax.ShapeDtypeStruct((t, GATE_WIDTH), BF16)),
        grid=(t // TM_PROJ, n_all),
        in_specs=[
            pl.BlockSpec((TM_PROJ, d), lambda i, j: (i, 0)),
            pl.BlockSpec((1, N_ADA, d), lambda i, j: (i // tiles_per_seq, 0, 0)),
            pl.BlockSpec((1, d), lambda i, j: (0, 0)),
            pl.BlockSpec((d, TN_PROJ), lambda i, j: (0, j)),
        ],
        out_specs=(
            pl.BlockSpec((TM_PROJ, TN_PROJ),
                         lambda i, j: (i, jnp.minimum(j, n_qkvu - 1))),
            pl.BlockSpec((TM_PROJ, TN_PROJ),
                         lambda i, j: (i, jnp.maximum(j - n_qkvu, 0))),
        ),
        scratch_shapes=[pltpu.VMEM((TM_PROJ, d), BF16)],
        compiler_params=pltpu.CompilerParams(
            dimension_semantics=("arbitrary", "arbitrary"),
            vmem_limit_bytes=VMEM_LIMIT),
        name="in_proj",
    )(x2d, ada3, norm_g, w_in_b)


def _attn_kernel(sinks_ref, q_ref, kc_ref, vc_ref, kp_ref, vp_ref, o_ref,
                 kfull, vfull, *, steps_per_seq):
    i = pl.program_id(0)
    kfull[0:BLOCK, :] = kp_ref[...]
    kfull[BLOCK:BLOCK + TM_MIX, :] = kc_ref[...]
    vfull[0:BLOCK, :] = vp_ref[...]
    vfull[BLOCK:BLOCK + TM_MIX, :] = vc_ref[...]
    first_step = (i % steps_per_seq) == 0
    nq = Q_PER_KV * BLOCK

    def sb_body(sb, carry):
        r0 = pl.multiple_of(sb * BLOCK, BLOCK)
        row = lax.broadcasted_iota(jnp.int32, (nq, 2 * BLOCK), 0) & (BLOCK - 1)
        col = lax.broadcasted_iota(jnp.int32, (nq, 2 * BLOCK), 1)
        dist = col - row
        lo = jnp.where(jnp.logical_and(first_step, sb == 0), BLOCK, 0)
        valid = (dist > 0) & (dist <= BLOCK) & (col >= lo)
        rg = lax.broadcasted_iota(jnp.int32, (nq, 1), 0) // BLOCK
        for h in range(KV_HEADS):
            kb = kfull[pl.ds(r0, 2 * BLOCK), h * HEAD_DIM:(h + 1) * HEAD_DIM]
            vb = vfull[pl.ds(r0, 2 * BLOCK), h * HEAD_DIM:(h + 1) * HEAD_DIM]
            qg = jnp.concatenate(
                [q_ref[pl.ds(r0, BLOCK),
                       (Q_PER_KV * h + g) * HEAD_DIM:(Q_PER_KV * h + g + 1) * HEAD_DIM]
                 for g in range(Q_PER_KV)], axis=0)
            qg = qg * jnp.asarray(HEAD_DIM ** -0.5, BF16)
            logits = lax.dot_general(qg, kb, (((1,), (1,)), ((), ())),
                                     preferred_element_type=F32)
            logits = jnp.where(valid, logits, MASK_VALUE)
            sink = jnp.full((nq, 1), sinks_ref[Q_PER_KV * h + Q_PER_KV - 1], F32)
            for g in range(Q_PER_KV - 2, -1, -1):
                sink = jnp.where(rg == g, sinks_ref[Q_PER_KV * h + g], sink)
            m = jnp.maximum(jnp.max(logits, axis=1, keepdims=True), sink)
            p = jnp.exp(logits - m)
            denom = jnp.sum(p, axis=1, keepdims=True) + jnp.exp(sink - m)
            o = jnp.dot(p.astype(BF16), vb, preferred_element_type=F32)
            o = (o * (1.0 / denom)).astype(BF16)
            o_ref[pl.ds(r0, BLOCK),
                  h * Q_PER_KV * HEAD_DIM:(h + 1) * Q_PER_KV * HEAD_DIM] = (
                jnp.concatenate([o[g * BLOCK:(g + 1) * BLOCK, :]
                                 for g in range(Q_PER_KV)], axis=1))
        return carry

    lax.fori_loop(0, TM_MIX // BLOCK, sb_body, 0)


def _attn(qkvu, sinks, seq):
    t = qkvu.shape[0]
    steps_per_seq = seq // TM_MIX
    sub = TM_MIX // BLOCK
    kcol = ATTN_WIDTH // KV_WIDTH
    prev = lambda i: jnp.maximum(i * sub - 1, 0)
    return pl.pallas_call(
        functools.partial(_attn_kernel, steps_per_seq=steps_per_seq),
        out_shape=jax.ShapeDtypeStruct((t, ATTN_WIDTH), BF16),
        grid=(t // TM_MIX,),
        in_specs=[
            pl.BlockSpec(memory_space=pltpu.SMEM),
            pl.BlockSpec((TM_MIX, ATTN_WIDTH), lambda i: (i, 0)),
            pl.BlockSpec((TM_MIX, KV_WIDTH), lambda i: (i, kcol)),
            pl.BlockSpec((TM_MIX, KV_WIDTH), lambda i: (i, kcol + 1)),
            pl.BlockSpec((BLOCK, KV_WIDTH), lambda i: (prev(i), kcol)),
            pl.BlockSpec((BLOCK, KV_WIDTH), lambda i: (prev(i), kcol + 1)),
        ],
        out_specs=pl.BlockSpec((TM_MIX, ATTN_WIDTH), lambda i: (i, 0)),
        scratch_shapes=[pltpu.VMEM((BLOCK + TM_MIX, KV_WIDTH), BF16),
                        pltpu.VMEM((BLOCK + TM_MIX, KV_WIDTH), BF16)],
        compiler_params=pltpu.CompilerParams(
            dimension_semantics=("arbitrary",), vmem_limit_bytes=VMEM_LIMIT),
        name="attn",
    )(sinks, qkvu, qkvu, qkvu, qkvu, qkvu)


def _pool_kernel(ulo_ref, uhi_ref, plo_ref, phi_ref, wp_ref, ps_ref, o_ref,
                 ufull, *, steps_per_seq):
    i = pl.program_id(0)
    first_step = (i % steps_per_seq) == 0
    half = POOL_WIDTH // 2
    zeros = jnp.zeros((BLOCK, half), BF16)
    ufull[0:BLOCK, 0:half] = jnp.where(first_step, zeros, plo_ref[...])
    ufull[0:BLOCK, half:POOL_WIDTH] = jnp.where(first_step, zeros, phi_ref[...])
    ufull[BLOCK:BLOCK + TM_MIX, 0:half] = ulo_ref[...]
    ufull[BLOCK:BLOCK + TM_MIX, half:POOL_WIDTH] = uhi_ref[...]
    pos0 = (i % steps_per_seq) * TM_MIX
    row = lax.broadcasted_iota(jnp.int32, (BLOCK, 2 * BLOCK), 0)
    col = lax.broadcasted_iota(jnp.int32, (BLOCK, 2 * BLOCK), 1)
    rel = row + BLOCK - col
    bands = [((rel >= 0) & (rel < w)).astype(F32).astype(BF16)
             for w in POOL_WINDOWS]

    def sb_body(sb, carry):
        r0 = pl.multiple_of(sb * BLOCK, BLOCK)
        pos = pos0 + sb * BLOCK + lax.broadcasted_iota(jnp.int32, (BLOCK, 1), 0)
        for g, w in enumerate(POOL_WINDOWS):
            cols = slice(g * POOL_GROUP_DIM, (g + 1) * POOL_GROUP_DIM)
            band = ufull[pl.ds(r0, 2 * BLOCK), cols]
            ssum = jnp.dot(bands[g], band, preferred_element_type=F32)
            xg = ufull[pl.ds(r0 + BLOCK, BLOCK), cols].astype(F32)
            count = jnp.minimum(pos + 1, w).astype(F32)
            pooled = ssum / count - xg
            mixed = jnp.dot(pooled.astype(BF16), wp_ref[g],
                            preferred_element_type=F32)
            o_ref[pl.ds(r0, BLOCK), cols] = (mixed * ps_ref[:, cols]).astype(BF16)
        return carry

    lax.fori_loop(0, TM_MIX // BLOCK, sb_body, 0)


def _pool(qkvu, w_pool_b, pool_scale, seq):
    t = qkvu.shape[0]
    steps_per_seq = seq // TM_MIX
    sub = TM_MIX // BLOCK
    half = POOL_WIDTH // 2
    ucol = (ATTN_WIDTH + 2 * KV_WIDTH) // half
    prev = lambda i: jnp.maximum(i * sub - 1, 0)
    return pl.pallas_call(
        functools.partial(_pool_kernel, steps_per_seq=steps_per_seq),
        out_shape=jax.ShapeDtypeStruct((t, POOL_WIDTH), BF16),
        grid=(t // TM_MIX,),
        in_specs=[
            pl.BlockSpec((TM_MIX, half), lambda i: (i, ucol)),
            pl.BlockSpec((TM_MIX, half), lambda i: (i, ucol + 1)),
            pl.BlockSpec((BLOCK, half), lambda i: (prev(i), ucol)),
            pl.BlockSpec((BLOCK, half), lambda i: (prev(i), ucol + 1)),
            pl.BlockSpec((len(POOL_WINDOWS), POOL_GROUP_DIM, POOL_GROUP_DIM),
                         lambda i: (0, 0, 0)),
            pl.BlockSpec((1, POOL_WIDTH), lambda i: (0, 0)),
        ],
        out_specs=pl.BlockSpec((TM_MIX, POOL_WIDTH), lambda i: (i, 0)),
        scratch_shapes=[pltpu.VMEM((BLOCK + TM_MIX, POOL_WIDTH), BF16)],
        compiler_params=pltpu.CompilerParams(
            dimension_semantics=("arbitrary",), vmem_limit_bytes=VMEM_LIMIT),
        name="pool",
    )(qkvu, qkvu, qkvu, qkvu, w_pool_b, pool_scale)


def _first_argmax4(v):
    m = jnp.maximum(jnp.maximum(v[0], v[1]), jnp.maximum(v[2], v[3]))
    idx = jnp.where(v[0] == m, 0, jnp.where(v[1] == m, 1,
                                            jnp.where(v[2] == m, 2, 3)))
    return m, idx


def _merge_kernel(attn_ref, pool_ref, ga_ref, gp_ref, x_ref, ada_ref,
                  wa_ref, wp_ref, wo_ref, g2_ref, wr_ref, br_ref,
                  x2_ref, h2_ref, route_ref, counts_ref):
    i = pl.program_id(0)
    a = jnp.dot(attn_ref[...], wa_ref[...], preferred_element_type=F32)
    p = jnp.dot(pool_ref[...], wp_ref[...], preferred_element_type=F32)
    merged = ga_ref[...].astype(F32) * a + gp_ref[...].astype(F32) * p
    out = jnp.dot(merged.astype(BF16), wo_ref[...], preferred_element_type=F32)
    x2 = x_ref[...] + ada_ref[0, 2:3, :] * out
    x2_ref[...] = x2
    h2 = _rms_modulate(x2, g2_ref[...], ada_ref[0, 4:5, :], ada_ref[0, 3:4, :])
    h2_ref[...] = h2

    logits = lax.dot_general(wr_ref[...], h2.astype(BF16),
                             (((1,), (1,)), ((), ())),
                             preferred_element_type=F32) + br_ref[:, 0:1]
    tm = logits.shape[1]
    rows = [logits[r:r + 1, :] for r in range(N_GROUPS + N_EXPERTS)]
    _, gi = _first_argmax4(rows[0:N_GROUPS])
    sel = []
    for e in range(EXPERTS_PER_GROUP):
        v = rows[N_GROUPS + 3 * EXPERTS_PER_GROUP + e]
        for g in range(N_GROUPS - 2, -1, -1):
            v = jnp.where(gi == g, rows[N_GROUPS + g * EXPERTS_PER_GROUP + e], v)
        sel.append(v)
    _, i1 = _first_argmax4(sel)
    rest = [jnp.where(i1 == e, -jnp.inf, sel[e]) for e in range(EXPERTS_PER_GROUP)]
    _, i2 = _first_argmax4(rest)
    code = jnp.minimum(i1, i2) * EXPERTS_PER_GROUP + jnp.maximum(i1, i2)
    cig = jnp.where(code == 1, 0, jnp.where(code == 2, 1, jnp.where(
        code == 3, 2, jnp.where(code == 7, 3, jnp.where(code == 6, 4, 5)))))
    cls = gi * PAIRS_PER_GROUP + cig

    onehot = (lax.broadcasted_iota(jnp.int32, (ROUTER_ROWS, tm), 0) == cls
              ).astype(F32)
    src = lax.broadcasted_iota(jnp.int32, (tm, tm), 0)
    dst = lax.broadcasted_iota(jnp.int32, (tm, tm), 1)
    before = (src < dst).astype(F32).astype(BF16)
    prefix = jnp.dot(onehot.astype(BF16), before, preferred_element_type=F32)

    @pl.when(i == 0)
    def _():
        counts_ref[...] = jnp.zeros_like(counts_ref)

    carry = counts_ref[:, 0:1]
    rank = jnp.sum(onehot * (prefix + carry), axis=0, keepdims=True)
    counts_ref[...] = counts_ref[...] + jnp.sum(onehot, axis=1, keepdims=True)
    route_ref[0:1, :] = cls
    route_ref[1:2, :] = rank.astype(jnp.int32)
    route_ref[2:8, :] = jnp.zeros((6, tm), jnp.int32)


def _merge(attn, pool, gates, x2d, ada3, wa_b, wp_b, wo_b, norm2_g, wr_t, br_col,
           seq):
    t, d = x2d.shape
    tiles_per_seq = seq // TM_MERGE
    const = lambda shape: pl.BlockSpec(shape, lambda i: (0,) * len(shape),
                                       pipeline_mode=pl.Buffered(1))
    return pl.pallas_call(
        _merge_kernel,
        out_shape=(jax.ShapeDtypeStruct((t, d), F32),
                   jax.ShapeDtypeStruct((t, d), F32),
                   jax.ShapeDtypeStruct((8, t), jnp.int32),
                   jax.ShapeDtypeStruct((ROUTER_ROWS, 128), F32)),
        grid=(t // TM_MERGE,),
        in_specs=[
            pl.BlockSpec((TM_MERGE, ATTN_WIDTH), lambda i: (i, 0)),
            pl.BlockSpec((TM_MERGE, POOL_WIDTH), lambda i: (i, 0)),
            pl.BlockSpec((TM_MERGE, d), lambda i: (i, 0)),
            pl.BlockSpec((TM_MERGE, d), lambda i: (i, 1)),
            pl.BlockSpec((TM_MERGE, d), lambda i: (i, 0)),
            pl.BlockSpec((1, N_ADA, d), lambda i: (i // tiles_per_seq, 0, 0)),
            const((ATTN_WIDTH, d)),
            const((POOL_WIDTH, d)),
            const((d, d)),
            const((1, d)),
            const((ROUTER_ROWS, d)),
            const((ROUTER_ROWS, 128)),
        ],
        out_specs=(
            pl.BlockSpec((TM_MERGE, d), lambda i: (i, 0)),
            pl.BlockSpec((TM_MERGE, d), lambda i: (i, 0)),
            pl.BlockSpec((8, TM_MERGE), lambda i: (0, i)),
            pl.BlockSpec((ROUTER_ROWS, 128), lambda i: (0, 0)),
        ),
        compiler_params=pltpu.CompilerParams(
            dimension_semantics=("arbitrary",), vmem_limit_bytes=VMEM_LIMIT),
        name="merge",
    )(attn, pool, gates, gates, x2d, ada3, wa_b, wp_b, wo_b, norm2_g, wr_t, br_col)


def _row_copy(src_ref, src_row, dst_ref, dst_row, sem):
    return pltpu.make_async_copy(src_ref.at[pl.ds(src_row, 1)],
                                 dst_ref.at[pl.ds(dst_row, 1)], sem)


def _expert_kernel(src_ref, blk_ref, ea_ref, eb_ref, lane_ref, nused_ref,
                   h2_ref, wga_ref, wua_ref, wda_ref, wgb_ref, wub_ref, wdb_ref,
                   wr_ref, br_ref, ys_ref, hbuf, sems):
    i = pl.program_id(0)
    nused = nused_ref[0]

    def gather(tile, slot):
        for r in range(TM_EXPERT):
            _row_copy(h2_ref, src_ref[tile * TM_EXPERT + r], hbuf.at[slot], r,
                      sems.at[slot]).start()

    def wait_gather(slot):
        pltpu.make_async_copy(h2_ref.at[pl.ds(0, TM_EXPERT)], hbuf.at[slot],
                              sems.at[slot]).wait()

    @pl.when(i == 0)
    def _():
        gather(0, 0)

    @pl.when(i == nused)
    def _():
        wait_gather(i % 2)

    @pl.when(jnp.logical_and(i >= nused, i < pl.num_programs(0) - 1))
    def _():
        ys_ref[...] = jnp.zeros_like(ys_ref)

    @pl.when(i < nused)
    def _():
        slot = i % 2
        wait_gather(slot)
        gather(i + 1, 1 - slot)
        h = hbuf[slot].astype(BF16)
        logits =jnp.dot(h, wr_ref[...], preferred_element_type=F32) + br_ref[...]
        lane = lax.broadcasted_iota(jnp.int32, logits.shape, 1)
        is_group = lane < N_GROUPS
        gmax = jnp.max(jnp.where(is_group, logits, -jnp.inf), axis=1, keepdims=True)
        gexp = jnp.where(is_group, jnp.exp(logits - gmax), 0.0)
        pick = lambda v, l: jnp.sum(jnp.where(lane == l, v, 0.0), axis=1,
                                    keepdims=True)
        p_group = pick(gexp, lane_ref[0, i]) / jnp.sum(gexp, axis=1, keepdims=True)
        la = pick(logits, lane_ref[1, i])
        lb = pick(logits, lane_ref[2, i])
        mx = jnp.maximum(la, lb)
        ea = jnp.exp(la - mx)
        eb = jnp.exp(lb - mx)
        inv = p_group / (ea + eb)

        def expert(wg_ref, wu_ref, wd_ref, weight):
            gate = jnp.dot(h, wg_ref[0], preferred_element_type=F32)
            up = jnp.dot(h, wu_ref[0], preferred_element_type=F32)
            act = (jax.nn.silu(gate) * up * weight).astype(BF16)
            return jnp.dot(act, wd_ref[0], preferred_element_type=F32)

        ys_ref[...] = (expert(wga_ref, wua_ref, wda_ref, ea * inv)
                       + expert(wgb_ref, wub_ref, wdb_ref, eb * inv))


def _experts(src, tile_blk, tile_ea, tile_eb, tile_lanes, n_used,
             h2, wg_b, wu_b, wd_b, wr_lanes, br_lanes, n_tiles):
    d = h2.shape[1]
    row_map = lambda i, src, blk, ea, eb, ln, nu: (blk[i], 0)
    wa_map = lambda i, src, blk, ea, eb, ln, nu: (ea[i], 0, 0)
    wb_map = lambda i, src, blk, ea, eb, ln, nu: (eb[i], 0, 0)
    const = lambda i, src, blk, ea, eb, ln, nu: (0, 0)
    return pl.pallas_call(
        _expert_kernel,
        out_shape=jax.ShapeDtypeStruct((n_tiles * TM_EXPERT, d), F32),
        grid_spec=pltpu.PrefetchScalarGridSpec(
            num_scalar_prefetch=6,
            grid=(n_tiles + 1,),
            in_specs=[
                pl.BlockSpec(memory_space=pl.ANY),
                pl.BlockSpec((1, d, EXPERT_FF), wa_map),
                pl.BlockSpec((1, d, EXPERT_FF), wa_map),
                pl.BlockSpec((1, EXPERT_FF, d), wa_map),
                pl.BlockSpec((1, d, EXPERT_FF), wb_map),
                pl.BlockSpec((1, d, EXPERT_FF), wb_map),
                pl.BlockSpec((1, EXPERT_FF, d), wb_map),
                pl.BlockSpec((d, ROUTER_LANES), const),
                pl.BlockSpec((1, ROUTER_LANES), const),
            ],
            out_specs=pl.BlockSpec((TM_EXPERT, d), row_map),
            scratch_shapes=[pltpu.VMEM((2, TM_EXPERT, d), F32),
                            pltpu.SemaphoreType.DMA((2,))],
        ),
        compiler_params=pltpu.CompilerParams(
            dimension_semantics=("arbitrary",), vmem_limit_bytes=VMEM_LIMIT),
        name="experts",
    )(src, tile_blk, tile_ea, tile_eb, tile_lanes, n_used,
      h2, wg_b, wu_b, wd_b, wg_b, wu_b, wd_b, wr_lanes, br_lanes)


def _combine_kernel(pos_ref, ys_ref, x2_ref, ada_ref, g_ref, o_ref, ybuf, sems):
    i = pl.program_id(0)
    n = pl.num_programs(0)

    def gather(tile, slot):
        for r in range(TM_COMBINE):
            _row_copy(ys_ref, pos_ref[tile * TM_COMBINE + r],
                      ybuf.at[slot], r, sems.at[slot]).start()

    @pl.when(i == 0)
    def _():
        gather(0, 0)

    @pl.when(i + 1 < n)
    def _():
        gather(i + 1, (i + 1) % 2)

    slot = i % 2
    pltpu.make_async_copy(ys_ref.at[pl.ds(0, TM_COMBINE)], ybuf.at[slot],
                          sems.at[slot]).wait()

    x = x2_ref[...] + ada_ref[0, 5:6, :] * ybuf[slot]
    ms = jnp.mean(x * x, axis=-1, keepdims=True)
    o_ref[...] = x * lax.rsqrt(ms + NORM_EPS) * g_ref[...]


def _combine(pos, ys, x2, ada3, final_g, seq):
    t, d = x2.shape
    tiles_per_seq = seq // TM_COMBINE
    return pl.pallas_call(
        _combine_kernel,
        out_shape=jax.ShapeDtypeStruct((t, d), F32),
        grid_spec=pltpu.PrefetchScalarGridSpec(
            num_scalar_prefetch=1,
            grid=(t // TM_COMBINE,),
            in_specs=[
                pl.BlockSpec(memory_space=pl.ANY),
                pl.BlockSpec((TM_COMBINE, d), lambda i, pos: (i, 0)),
                pl.BlockSpec((1, N_ADA, d),
                             lambda i, pos: (i // tiles_per_seq, 0, 0)),
                pl.BlockSpec((1, d), lambda i, pos: (0, 0)),
            ],
            out_specs=pl.BlockSpec((TM_COMBINE, d), lambda i, pos: (i, 0)),
            scratch_shapes=[pltpu.VMEM((2, TM_COMBINE, d), F32),
                            pltpu.SemaphoreType.DMA((2,))],
        ),
        compiler_params=pltpu.CompilerParams(
            dimension_semantics=("arbitrary",), vmem_limit_bytes=VMEM_LIMIT),
        name="combine",
    )(pos, ys, x2, ada3, final_g)


def _invert_kernel(pos_ref, src_ref):
    def clear(r, carry):
        src_ref[r] = 0
        return carry

    lax.fori_loop(0, src_ref.shape[0], clear, 0, unroll=8)

    def place(t, carry):
        src_ref[pos_ref[t]] = t
        return carry

    lax.fori_loop(0, pos_ref.shape[0], place, 0, unroll=8)


def _invert(pos, n_rows):
    return pl.pallas_call(
        _invert_kernel,
        out_shape=jax.ShapeDtypeStruct((n_rows,), jnp.int32),
        in_specs=[pl.BlockSpec(memory_space=pltpu.SMEM)],
        out_specs=pl.BlockSpec(memory_space=pltpu.SMEM),
        name="invert",
    )(pos)


def _routing_tables(route, counts, n_tiles):
    cls = route[0]
    rank = route[1]
    cnt = counts[:N_CLASSES, 0].astype(jnp.int32)
    tiles_c = (cnt + TM_EXPERT - 1) // TM_EXPERT
    tile_end = jnp.cumsum(tiles_c)
    tile_start = tile_end - tiles_c
    n_used = tile_end[-1]
    pos = (tile_start[cls] * TM_EXPERT + rank).astype(jnp.int32)
    src = _invert(pos, (n_tiles + 1) * TM_EXPERT)

    steps = jnp.arange(n_tiles + 1, dtype=jnp.int32)
    blk = jnp.minimum(steps, n_tiles - 1)
    k = jnp.minimum(steps, n_used - 1)
    c = jnp.sum((k[:, None] >= tile_end[None, :]).astype(jnp.int32), axis=1)
    grp = c // PAIRS_PER_GROUP
    cig = c % PAIRS_PER_GROUP
    slot_a = jnp.asarray(CLASS_SLOT_A, jnp.int32)[cig]
    slot_b = jnp.asarray(CLASS_SLOT_B, jnp.int32)[cig]
    ea = grp * EXPERTS_PER_GROUP + slot_a
    eb = grp * EXPERTS_PER_GROUP + slot_b
    lanes = jnp.stack([grp, N_GROUPS + ea, N_GROUPS + eb]).astype(jnp.int32)
    return (pos, src, blk, ea.astype(jnp.int32), eb.astype(jnp.int32), lanes,
            n_used.reshape(1).astype(jnp.int32))


def kernel(x, c, w_ada, b_ada, norm1_g, w_in, sinks, w_pool, pool_scale,
           w_attn_branch, w_pool_branch, w_out, norm2_g, w_router_group,
           b_router_group, w_router_expert, b_router_expert, w_e_gate, w_e_up,
           w_e_down, final_g):
    b, s, d = x.shape
    t = b * s
    assert w_ada.shape[0] == 1, "single-layer block"
    assert d == D_MODEL and s % TM_PROJ == 0
    l = 0
    n_tiles = t // TM_EXPERT + N_CLASSES
    x2d = x.reshape(t, d)
    ada3 = _ada(c, w_ada[l], b_ada[l]).reshape(b, N_ADA, d)

    qkvu, gates = _in_proj(x2d, ada3, norm1_g[l].reshape(1, d),
                           w_in[l].astype(BF16), s)
    attn = _attn(qkvu, sinks[l], s)
    pool = _pool(qkvu, w_pool[l].astype(BF16), pool_scale[l].reshape(1, -1), s)

    w_r = jnp.concatenate([w_router_group[l], w_router_expert[l]], axis=1)
    b_r = jnp.concatenate([b_router_group[l], b_router_expert[l]])
    n_r = N_GROUPS + N_EXPERTS
    wr_t = jnp.pad(w_r.T, ((0, ROUTER_ROWS - n_r), (0, 0))).astype(BF16)
    br_col = jnp.broadcast_to(
        jnp.pad(b_r, (0, ROUTER_ROWS - n_r))[:, None], (ROUTER_ROWS, 128))
    wr_lanes = jnp.pad(w_r, ((0, 0), (0, ROUTER_LANES - n_r))).astype(BF16)
    br_lanes = jnp.pad(b_r, (0, ROUTER_LANES - n_r)).reshape(1, ROUTER_LANES)

    x2, h2, route, counts = _merge(
        attn, pool, gates, x2d, ada3, w_attn_branch[l].astype(BF16),
        w_pool_branch[l].astype(BF16), w_out[l].astype(BF16),
        norm2_g[l].reshape(1, d), wr_t, br_col, s)

    pos, src, tile_blk, tile_ea, tile_eb, tile_lanes, n_used = (
        _routing_tables(route, counts, n_tiles))
    ys = _experts(src, tile_blk, tile_ea, tile_eb, tile_lanes, n_used,
                  h2, w_e_gate[l].astype(BF16), w_e_up[l].astype(BF16),
                  w_e_down[l].astype(BF16), wr_lanes, br_lanes, n_tiles)
    out = _combine(pos, ys, x2, ada3, final_g.reshape(1, d), s)
    return out.reshape(b, s, d)
```

```python
import functools

import jax
import jax.numpy as jnp
from jax import lax
from jax.experimental import pallas as pl
from jax.experimental.pallas import tpu as pltpu

F32 = jnp.float32
BF16 = jnp.bfloat16

D_MODEL = 2048
HEAD_DIM = 64
ATTN_HEADS = 16
KV_HEADS = 4
Q_PER_KV = ATTN_HEADS // KV_HEADS
ATTN_WIDTH = ATTN_HEADS * HEAD_DIM
KV_WIDTH = KV_HEADS * HEAD_DIM
BLOCK = 128
POOL_WINDOWS = (2, 4, 8, 16)
POOL_WIDTH = 1024
POOL_GROUP_DIM = 256
QKVU_WIDTH = ATTN_WIDTH + 2 * KV_WIDTH + POOL_WIDTH
GATE_WIDTH = 2 * D_MODEL
N_GROUPS = 4
EXPERTS_PER_GROUP = 4
N_EXPERTS = 16
EXPERT_FF = 512
N_ADA = 6
NORM_EPS = 1e-6
MASK_VALUE = -1e30

PAIRS_PER_GROUP = 6
N_CLASSES = N_GROUPS * PAIRS_PER_GROUP
CLASS_SLOT_A = (0, 0, 0, 1, 1, 3)
CLASS_SLOT_B = (1, 2, 3, 3, 2, 2)
ROUTER_ROWS = 32
ROUTER_LANES = 128

TM_PROJ = 1024
TN_PROJ = 512
TM_MIX = 512
TM_MERGE = 256
TM_EXPERT = 256
TM_COMBINE = 256
TN_ADA = 1024
DMA_PRIORITIES = 2

VMEM_LIMIT = 52 * 1024 * 1024


def _rms_modulate(x, g, scale, shift):
    ms = jnp.mean(x * x, axis=-1, keepdims=True)
    return (x * lax.rsqrt(ms + NORM_EPS) * g) * (1.0 + scale) + shift


def _ada_kernel(cb_ref, w_ref, b_ref, o_ref):
    nb = cb_ref.shape[0]
    d = w_ref.shape[0]
    nchunk = w_ref.shape[1] // 128

    def body(kb, accs):
        k0 = pl.multiple_of(kb * 8, 8)
        new = list(accs)
        cbs = [cb_ref[b, pl.ds(k0, 8), :] for b in range(nb)]
        for j in range(nchunk):
            w = w_ref[pl.ds(k0, 8), j * 128:(j + 1) * 128]
            for b in range(nb):
                new[b * nchunk + j] = new[b * nchunk + j] + w * cbs[b]
        return tuple(new)

    init = tuple(jnp.zeros((8, 128), F32) for _ in range(nb * nchunk))
    accs = lax.fori_loop(0, d // 8, body, init, unroll=2)
    for b in range(nb):
        for j in range(nchunk):
            o_ref[b:b + 1, j * 128:(j + 1) * 128] = (
                jnp.sum(accs[b * nchunk + j], axis=0, keepdims=True)
                + b_ref[:, j * 128:(j + 1) * 128])


def _ada(c, w_ada, b_ada):
    nb, d = c.shape
    n = w_ada.shape[1]
    cb = jnp.broadcast_to(c[:, :, None], (nb, d, 128))
    return pl.pallas_call(
        _ada_kernel,
        out_shape=jax.ShapeDtypeStruct((nb, n), F32),
        grid=(n // TN_ADA,),
        in_specs=[
            pl.BlockSpec((nb, d, 128), lambda j: (0, 0, 0)),
            pl.BlockSpec((d, TN_ADA), lambda j: (0, j)),
            pl.BlockSpec((1, TN_ADA), lambda j: (0, j)),
        ],
        out_specs=pl.BlockSpec((nb, TN_ADA), lambda j: (0, j)),
        compiler_params=pltpu.CompilerParams(
            dimension_semantics=("arbitrary",), vmem_limit_bytes=VMEM_LIMIT),
        name="ada",
    )(cb, w_ada, b_ada.reshape(1, n))


def _in_proj_kernel(x_ref, ada_ref, g_ref, w_ref, qkvu_ref, gates_ref, h_ref):
    j = pl.program_id(1)
    n_qkvu = QKVU_WIDTH // TN_PROJ

    @pl.when(j == 0)
    def _():
        chunk = 256
        for r in range(0, TM_PROJ, chunk):
            h = _rms_modulate(x_ref[r:r + chunk, :], g_ref[...],
                              ada_ref[0, 1:2, :], ada_ref[0, 0:1, :])
            h_ref[r:r + chunk, :] = h.astype(BF16)

    acc = jnp.dot(h_ref[...], w_ref[...], preferred_element_type=F32)

    @pl.when(j < n_qkvu)
    def _():
        qkvu_ref[...] = acc.astype(BF16)

    @pl.when(j >= n_qkvu)
    def _():
        gates_ref[...] = jax.nn.sigmoid(acc).astype(BF16)


def _in_proj(x2d, ada3, norm_g, w_in_b, seq):
    t, d = x2d.shape
    n_qkvu = QKVU_WIDTH // TN_PROJ
    n_all = (QKVU_WIDTH + GATE_WIDTH) // TN_PROJ
    tiles_per_seq = seq // TM_PROJ
    return pl.pallas_call(
        _in_proj_kernel,
        out_shape=(jax.ShapeDtypeStruct((t, QKVU_WIDTH), BF16),
                   jax.ShapeDtypeStruct((t, GATE_WIDTH), BF16)),
        grid=(t // TM_PROJ, n_all),
        in_specs=[
            pl.BlockSpec((TM_PROJ, d), lambda i, j: (i, 0)),
            pl.BlockSpec((1, N_ADA, d), lambda i, j: (i // tiles_per_seq, 0, 0)),
            pl.BlockSpec((1, d), lambda i, j: (0, 0)),
            pl.BlockSpec((d, TN_PROJ), lambda i, j: (0, j)),
        ],
        out_specs=(
            pl.BlockSpec((TM_PROJ, TN_PROJ),
                         lambda i, j: (i, jnp.minimum(j, n_qkvu - 1))),
            pl.BlockSpec((TM_PROJ, TN_PROJ),
                         lambda i, j: (i, jnp.maximum(j - n_qkvu, 0))),
        ),
        scratch_shapes=[pltpu.VMEM((TM_PROJ, d), BF16)],
        compiler_params=pltpu.CompilerParams(
            dimension_semantics=("arbitrary", "arbitrary"),
            vmem_limit_bytes=VMEM_LIMIT),
        name="in_proj",
    )(x2d, ada3, norm_g, w_in_b)


def _attn_kernel(sinks_ref, q_ref, kc_ref, vc_ref, kp_ref, vp_ref, o_ref,
                 kfull, vfull, *, steps_per_seq):
    i = pl.program_id(0)
    kfull[0:BLOCK, :] = kp_ref[...]
    kfull[BLOCK:BLOCK + TM_MIX, :] = kc_ref[...]
    vfull[0:BLOCK, :] = vp_ref[...]
    vfull[BLOCK:BLOCK + TM_MIX, :] = vc_ref[...]
    first_step = (i % steps_per_seq) == 0
    nq = Q_PER_KV * BLOCK

    def sb_body(sb, carry):
        r0 = pl.multiple_of(sb * BLOCK, BLOCK)
        row = lax.broadcasted_iota(jnp.int32, (nq, 2 * BLOCK), 0) & (BLOCK - 1)
        col = lax.broadcasted_iota(jnp.int32, (nq, 2 * BLOCK), 1)
        dist = col - row
        lo = jnp.where(jnp.logical_and(first_step, sb == 0), BLOCK, 0)
        valid = (dist > 0) & (dist <= BLOCK) & (col >= lo)
        rg = lax.broadcasted_iota(jnp.int32, (nq, 1), 0) // BLOCK
        for h in range(KV_HEADS):
            kb = kfull[pl.ds(r0, 2 * BLOCK), h * HEAD_DIM:(h + 1) * HEAD_DIM]
            vb = vfull[pl.ds(r0, 2 * BLOCK), h * HEAD_DIM:(h + 1) * HEAD_DIM]
            qg = jnp.concatenate(
                [q_ref[pl.ds(r0, BLOCK),
                       (Q_PER_KV * h + g) * HEAD_DIM:(Q_PER_KV * h + g + 1) * HEAD_DIM]
                 for g in range(Q_PER_KV)], axis=0)
            qg = qg * jnp.asarray(HEAD_DIM ** -0.5, BF16)
            logits = lax.dot_general(qg, kb, (((1,), (1,)), ((), ())),
                                     preferred_element_type=F32)
            logits = jnp.where(valid, logits, MASK_VALUE)
            sink = jnp.full((nq, 1), sinks_ref[Q_PER_KV * h + Q_PER_KV - 1], F32)
            for g in range(Q_PER_KV - 2, -1, -1):
                sink = jnp.where(rg == g, sinks_ref[Q_PER_KV * h + g], sink)
            m = jnp.maximum(jnp.max(logits, axis=1, keepdims=True), sink)
            p = jnp.exp(logits - m)
            denom = jnp.sum(p, axis=1, keepdims=True) + jnp.exp(sink - m)
            o = jnp.dot(p.astype(BF16), vb, preferred_element_type=F32)
            o = (o * (1.0 / denom)).astype(BF16)
            o_ref[pl.ds(r0, BLOCK),
                  h * Q_PER_KV * HEAD_DIM:(h + 1) * Q_PER_KV * HEAD_DIM] = (
                jnp.concatenate([o[g * BLOCK:(g + 1) * BLOCK, :]
                                 for g in range(Q_PER_KV)], axis=1))
        return carry

    lax.fori_loop(0, TM_MIX // BLOCK, sb_body, 0)


def _attn(qkvu, sinks, seq):
    t = qkvu.shape[0]
    steps_per_seq = seq // TM_MIX
    sub = TM_MIX // BLOCK
    kcol = ATTN_WIDTH // KV_WIDTH
    prev = lambda i: jnp.maximum(i * sub - 1, 0)
    return pl.pallas_call(
        functools.partial(_attn_kernel, steps_per_seq=steps_per_seq),
        out_shape=jax.ShapeDtypeStruct((t, ATTN_WIDTH), BF16),
        grid=(t // TM_MIX,),
        in_specs=[
            pl.BlockSpec(memory_space=pltpu.SMEM),
            pl.BlockSpec((TM_MIX, ATTN_WIDTH), lambda i: (i, 0)),
            pl.BlockSpec((TM_MIX, KV_WIDTH), lambda i: (i, kcol)),
            pl.BlockSpec((TM_MIX, KV_WIDTH), lambda i: (i, kcol + 1)),
            pl.BlockSpec((BLOCK, KV_WIDTH), lambda i: (prev(i), kcol)),
            pl.BlockSpec((BLOCK, KV_WIDTH), lambda i: (prev(i), kcol + 1)),
        ],
        out_specs=pl.BlockSpec((TM_MIX, ATTN_WIDTH), lambda i: (i, 0)),
        scratch_shapes=[pltpu.VMEM((BLOCK + TM_MIX, KV_WIDTH), BF16),
                        pltpu.VMEM((BLOCK + TM_MIX, KV_WIDTH), BF16)],
        compiler_params=pltpu.CompilerParams(
            dimension_semantics=("arbitrary",), vmem_limit_bytes=VMEM_LIMIT),
        name="attn",
    )(sinks, qkvu, qkvu, qkvu, qkvu, qkvu)


def _pool_kernel(ulo_ref, uhi_ref, plo_ref, phi_ref, wp_ref, ps_ref, o_ref,
                 ufull, *, steps_per_seq):
    i = pl.program_id(0)
    first_step = (i % steps_per_seq) == 0
    half = POOL_WIDTH // 2
    zeros = jnp.zeros((BLOCK, half), BF16)
    ufull[0:BLOCK, 0:half] = jnp.where(first_step, zeros, plo_ref[...])
    ufull[0:BLOCK, half:POOL_WIDTH] = jnp.where(first_step, zeros, phi_ref[...])
    ufull[BLOCK:BLOCK + TM_MIX, 0:half] = ulo_ref[...]
    ufull[BLOCK:BLOCK + TM_MIX, half:POOL_WIDTH] = uhi_ref[...]
    pos0 = (i % steps_per_seq) * TM_MIX
    row = lax.broadcasted_iota(jnp.int32, (BLOCK, 2 * BLOCK), 0)
    col = lax.broadcasted_iota(jnp.int32, (BLOCK, 2 * BLOCK), 1)
    rel = row + BLOCK - col
    bands = [((rel >= 0) & (rel < w)).astype(F32).astype(BF16)
             for w in POOL_WINDOWS]

    def sb_body(sb, carry):
        r0 = pl.multiple_of(sb * BLOCK, BLOCK)
        pos = pos0 + sb * BLOCK + lax.broadcasted_iota(jnp.int32, (BLOCK, 1), 0)
        for g, w in enumerate(POOL_WINDOWS):
            cols = slice(g * POOL_GROUP_DIM, (g + 1) * POOL_GROUP_DIM)
            band = ufull[pl.ds(r0, 2 * BLOCK), cols]
            ssum = jnp.dot(bands[g], band, preferred_element_type=F32)
            xg = ufull[pl.ds(r0 + BLOCK, BLOCK), cols].astype(F32)
            count = jnp.minimum(pos + 1, w).astype(F32)
            pooled = ssum / count - xg
            mixed = jnp.dot(pooled.astype(BF16), wp_ref[g],
                            preferred_element_type=F32)
            o_ref[pl.ds(r0, BLOCK), cols] = (mixed * ps_ref[:, cols]).astype(BF16)
        return carry

    lax.fori_loop(0, TM_MIX // BLOCK, sb_body, 0)


def _pool(qkvu, w_pool_b, pool_scale, seq):
    t = qkvu.shape[0]
    steps_per_seq = seq // TM_MIX
    sub = TM_MIX // BLOCK
    half = POOL_WIDTH // 2
    ucol = (ATTN_WIDTH + 2 * KV_WIDTH) // half
    prev = lambda i: jnp.maximum(i * sub - 1, 0)
    return pl.pallas_call(
        functools.partial(_pool_kernel, steps_per_seq=steps_per_seq),
        out_shape=jax.ShapeDtypeStruct((t, POOL_WIDTH), BF16),
        grid=(t // TM_MIX,),
        in_specs=[
            pl.BlockSpec((TM_MIX, half), lambda i: (i, ucol)),
            pl.BlockSpec((TM_MIX, half), lambda i: (i, ucol + 1)),
            pl.BlockSpec((BLOCK, half), lambda i: (prev(i), ucol)),
            pl.BlockSpec((BLOCK, half), lambda i: (prev(i), ucol + 1)),
            pl.BlockSpec((len(POOL_WINDOWS), POOL_GROUP_DIM, POOL_GROUP_DIM),
                         lambda i: (0, 0, 0)),
            pl.BlockSpec((1, POOL_WIDTH), lambda i: (0, 0)),
        ],
        out_specs=pl.BlockSpec((TM_MIX, POOL_WIDTH), lambda i: (i, 0)),
        scratch_shapes=[pltpu.VMEM((BLOCK + TM_MIX, POOL_WIDTH), BF16)],
        compiler_params=pltpu.CompilerParams(
            dimension_semantics=("arbitrary",), vmem_limit_bytes=VMEM_LIMIT),
        name="pool",
    )(qkvu, qkvu, qkvu, qkvu, w_pool_b, pool_scale)


def _first_argmax4(v):
    m = jnp.maximum(jnp.maximum(v[0], v[1]), jnp.maximum(v[2], v[3]))
    idx = jnp.where(v[0] == m, 0, jnp.where(v[1] == m, 1,
                                            jnp.where(v[2] == m, 2, 3)))
    return m, idx


def _merge_kernel(attn_ref, pool_ref, ga_ref, gp_ref, x_ref, ada_ref,
                  wa_ref, wp_ref, wo_ref, g2_ref, wr_ref, br_ref,
                  x2_ref, h2_ref, route_ref, counts_ref):
    i = pl.program_id(0)
    a = jnp.dot(attn_ref[...], wa_ref[...], preferred_element_type=F32)
    p = jnp.dot(pool_ref[...], wp_ref[...], preferred_element_type=F32)
    merged = ga_ref[...].astype(F32) * a + gp_ref[...].astype(F32) * p
    out = jnp.dot(merged.astype(BF16), wo_ref[...], preferred_element_type=F32)
    x2 = x_ref[...] + ada_ref[0, 2:3, :] * out
    x2_ref[...] = x2
    h2 = _rms_modulate(x2, g2_ref[...], ada_ref[0, 4:5, :], ada_ref[0, 3:4, :])
    h2_ref[...] = h2

    logits = lax.dot_general(wr_ref[...], h2.astype(BF16),
                             (((1,), (1,)), ((), ())),
                             preferred_element_type=F32) + br_ref[:, 0:1]
    tm = logits.shape[1]
    rows = [logits[r:r + 1, :] for r in range(N_GROUPS + N_EXPERTS)]
    _, gi = _first_argmax4(rows[0:N_GROUPS])
    sel = []
    for e in range(EXPERTS_PER_GROUP):
        v = rows[N_GROUPS + 3 * EXPERTS_PER_GROUP + e]
        for g in range(N_GROUPS - 2, -1, -1):
            v = jnp.where(gi == g, rows[N_GROUPS + g * EXPERTS_PER_GROUP + e], v)
        sel.append(v)
    _, i1 = _first_argmax4(sel)
    rest = [jnp.where(i1 == e, -jnp.inf, sel[e]) for e in range(EXPERTS_PER_GROUP)]
    _, i2 = _first_argmax4(rest)
    code = jnp.minimum(i1, i2) * EXPERTS_PER_GROUP + jnp.maximum(i1, i2)
    cig = jnp.where(code == 1, 0, jnp.where(code == 2, 1, jnp.where(
        code == 3, 2, jnp.where(code == 7, 3, jnp.where(code == 6, 4, 5)))))
    cls = gi * PAIRS_PER_GROUP + cig

    onehot = (lax.broadcasted_iota(jnp.int32, (ROUTER_ROWS, tm), 0) == cls
              ).astype(F32)
    src = lax.broadcasted_iota(jnp.int32, (tm, tm), 0)
    dst = lax.broadcasted_iota(jnp.int32, (tm, tm), 1)
    before = (src < dst).astype(F32).astype(BF16)
    prefix = jnp.dot(onehot.astype(BF16), before, preferred_element_type=F32)

    @pl.when(i == 0)
    def _():
        counts_ref[...] = jnp.zeros_like(counts_ref)

    carry = counts_ref[:, 0:1]
    rank = jnp.sum(onehot * (prefix + carry), axis=0, keepdims=True)
    counts_ref[...] = counts_ref[...] + jnp.sum(onehot, axis=1, keepdims=True)
    route_ref[0:1, :] = cls
    route_ref[1:2, :] = rank.astype(jnp.int32)
    route_ref[2:8, :] = jnp.zeros((6, tm), jnp.int32)


def _merge(attn, pool, gates, x2d, ada3, wa_b, wp_b, wo_b, norm2_g, wr_t, br_col,
           seq):
    t, d = x2d.shape
    tiles_per_seq = seq // TM_MERGE
    const = lambda shape: pl.BlockSpec(shape, lambda i: (0,) * len(shape),
                                       pipeline_mode=pl.Buffered(1))
    return pl.pallas_call(
        _merge_kernel,
        out_shape=(jax.ShapeDtypeStruct((t, d), F32),
                   jax.ShapeDtypeStruct((t, d), F32),
                   jax.ShapeDtypeStruct((8, t), jnp.int32),
                   jax.ShapeDtypeStruct((ROUTER_ROWS, 128), F32)),
        grid=(t // TM_MERGE,),
        in_specs=[
            pl.BlockSpec((TM_MERGE, ATTN_WIDTH), lambda i: (i, 0)),
            pl.BlockSpec((TM_MERGE, POOL_WIDTH), lambda i: (i, 0)),
            pl.BlockSpec((TM_MERGE, d), lambda i: (i, 0)),
            pl.BlockSpec((TM_MERGE, d), lambda i: (i, 1)),
            pl.BlockSpec((TM_MERGE, d), lambda i: (i, 0)),
            pl.BlockSpec((1, N_ADA, d), lambda i: (i // tiles_per_seq, 0, 0)),
            const((ATTN_WIDTH, d)),
            const((POOL_WIDTH, d)),
            const((d, d)),
            const((1, d)),
            const((ROUTER_ROWS, d)),
            const((ROUTER_ROWS, 128)),
        ],
        out_specs=(
            pl.BlockSpec((TM_MERGE, d), lambda i: (i, 0)),
            pl.BlockSpec((TM_MERGE, d), lambda i: (i, 0)),
            pl.BlockSpec((8, TM_MERGE), lambda i: (0, i)),
            pl.BlockSpec((ROUTER_ROWS, 128), lambda i: (0, 0)),
        ),
        compiler_params=pltpu.CompilerParams(
            dimension_semantics=("arbitrary",), vmem_limit_bytes=VMEM_LIMIT),
        name="merge",
    )(attn, pool, gates, gates, x2d, ada3, wa_b, wp_b, wo_b, norm2_g, wr_t, br_col)


def _row_copy(src_ref, src_row, dst_ref, dst_row, sem):
    return pltpu.make_async_copy(src_ref.at[pl.ds(src_row, 1)],
                                 dst_ref.at[pl.ds(dst_row, 1)], sem)


def _expert_kernel(src_ref, blk_ref, ea_ref, eb_ref, lane_ref, nused_ref,
                   h2_ref, wga_ref, wua_ref, wda_ref, wgb_ref, wub_ref, wdb_ref,
                   wr_ref, br_ref, ys_ref, hbuf, sems):
    i = pl.program_id(0)
    nused = nused_ref[0]

    def gather(tile, slot):
        for r in range(TM_EXPERT):
            _row_copy(h2_ref, src_ref[tile * TM_EXPERT + r], hbuf.at[slot], r,
                      sems.at[slot]).start(priority=r % DMA_PRIORITIES)

    def wait_gather(slot):
        pltpu.make_async_copy(h2_ref.at[pl.ds(0, TM_EXPERT)], hbuf.at[slot],
                              sems.at[slot]).wait()

    @pl.when(i == 0)
    def _():
        gather(0, 0)

    @pl.when(i == nused)
    def _():
        wait_gather(i % 2)

    @pl.when(jnp.logical_and(i >= nused, i < pl.num_programs(0) - 1))
    def _():
        ys_ref[...] = jnp.zeros_like(ys_ref)

    @pl.when(i < nused)
    def _():
        slot = i % 2
        wait_gather(slot)
        gather(i + 1, 1 - slot)
        h = hbuf[slot].astype(BF16)
        logits =jnp.dot(h, wr_ref[...], preferred_element_type=F32) + br_ref[...]
        lane = lax.broadcasted_iota(jnp.int32, logits.shape, 1)
        is_group = lane < N_GROUPS
        gmax = jnp.max(jnp.where(is_group, logits, -jnp.inf), axis=1, keepdims=True)
        gexp = jnp.where(is_group, jnp.exp(logits - gmax), 0.0)
        pick = lambda v, l: jnp.sum(jnp.where(lane == l, v, 0.0), axis=1,
                                    keepdims=True)
        p_group = pick(gexp, lane_ref[0, i]) / jnp.sum(gexp, axis=1, keepdims=True)
        la = pick(logits, lane_ref[1, i])
        lb = pick(logits, lane_ref[2, i])
        mx = jnp.maximum(la, lb)
        ea = jnp.exp(la - mx)
        eb = jnp.exp(lb - mx)
        inv = p_group / (ea + eb)

        def expert(wg_ref, wu_ref, wd_ref, weight):
            gate = jnp.dot(h, wg_ref[0], preferred_element_type=F32)
            up = jnp.dot(h, wu_ref[0], preferred_element_type=F32)
            act = (jax.nn.silu(gate) * up * weight).astype(BF16)
            return jnp.dot(act, wd_ref[0], preferred_element_type=F32)

        ys_ref[...] = (expert(wga_ref, wua_ref, wda_ref, ea * inv)
                       + expert(wgb_ref, wub_ref, wdb_ref, eb * inv))


def _experts(src, tile_blk, tile_ea, tile_eb, tile_lanes, n_used,
             h2, wg_b, wu_b, wd_b, wr_lanes, br_lanes, n_tiles):
    d = h2.shape[1]
    row_map = lambda i, src, blk, ea, eb, ln, nu: (blk[i], 0)
    wa_map = lambda i, src, blk, ea, eb, ln, nu: (ea[i], 0, 0)
    wb_map = lambda i, src, blk, ea, eb, ln, nu: (eb[i], 0, 0)
    const = lambda i, src, blk, ea, eb, ln, nu: (0, 0)
    return pl.pallas_call(
        _expert_kernel,
        out_shape=jax.ShapeDtypeStruct((n_tiles * TM_EXPERT, d), F32),
        grid_spec=pltpu.PrefetchScalarGridSpec(
            num_scalar_prefetch=6,
            grid=(n_tiles + 1,),
            in_specs=[
                pl.BlockSpec(memory_space=pl.ANY),
                pl.BlockSpec((1, d, EXPERT_FF), wa_map),
                pl.BlockSpec((1, d, EXPERT_FF), wa_map),
                pl.BlockSpec((1, EXPERT_FF, d), wa_map),
                pl.BlockSpec((1, d, EXPERT_FF), wb_map),
                pl.BlockSpec((1, d, EXPERT_FF), wb_map),
                pl.BlockSpec((1, EXPERT_FF, d), wb_map),
                pl.BlockSpec((d, ROUTER_LANES), const),
                pl.BlockSpec((1, ROUTER_LANES), const),
            ],
            out_specs=pl.BlockSpec((TM_EXPERT, d), row_map),
            scratch_shapes=[pltpu.VMEM((2, TM_EXPERT, d), F32),
                            pltpu.SemaphoreType.DMA((2,))],
        ),
        compiler_params=pltpu.CompilerParams(
            dimension_semantics=("arbitrary",), vmem_limit_bytes=VMEM_LIMIT),
        name="experts",
    )(src, tile_blk, tile_ea, tile_eb, tile_lanes, n_used,
      h2, wg_b, wu_b, wd_b, wg_b, wu_b, wd_b, wr_lanes, br_lanes)


def _combine_kernel(pos_ref, ys_ref, x2_ref, ada_ref, g_ref, o_ref, ybuf, sems):
    i = pl.program_id(0)
    n = pl.num_programs(0)

    def gather(tile, slot):
        for r in range(TM_COMBINE):
            _row_copy(ys_ref, pos_ref[tile * TM_COMBINE + r],
                      ybuf.at[slot], r, sems.at[slot]).start(
                          priority=r % DMA_PRIORITIES)

    @pl.when(i == 0)
    def _():
        gather(0, 0)

    @pl.when(i + 1 < n)
    def _():
        gather(i + 1, (i + 1) % 2)

    slot = i % 2
    pltpu.make_async_copy(ys_ref.at[pl.ds(0, TM_COMBINE)], ybuf.at[slot],
                          sems.at[slot]).wait()

    x = x2_ref[...] + ada_ref[0, 5:6, :] * ybuf[slot]
    ms = jnp.mean(x * x, axis=-1, keepdims=True)
    o_ref[...] = x * lax.rsqrt(ms + NORM_EPS) * g_ref[...]


def _combine(pos, ys, x2, ada3, final_g, seq):
    t, d = x2.shape
    tiles_per_seq = seq // TM_COMBINE
    return pl.pallas_call(
        _combine_kernel,
        out_shape=jax.ShapeDtypeStruct((t, d), F32),
        grid_spec=pltpu.PrefetchScalarGridSpec(
            num_scalar_prefetch=1,
            grid=(t // TM_COMBINE,),
            in_specs=[
                pl.BlockSpec(memory_space=pl.ANY),
                pl.BlockSpec((TM_COMBINE, d), lambda i, pos: (i, 0)),
                pl.BlockSpec((1, N_ADA, d),
                             lambda i, pos: (i // tiles_per_seq, 0, 0)),
                pl.BlockSpec((1, d), lambda i, pos: (0, 0)),
            ],
            out_specs=pl.BlockSpec((TM_COMBINE, d), lambda i, pos: (i, 0)),
            scratch_shapes=[pltpu.VMEM((2, TM_COMBINE, d), F32),
                            pltpu.SemaphoreType.DMA((2,))],
        ),
        compiler_params=pltpu.CompilerParams(
            dimension_semantics=("arbitrary",), vmem_limit_bytes=VMEM_LIMIT),
        name="combine",
    )(pos, ys, x2, ada3, final_g)


def _invert_kernel(pos_ref, src_ref):
    def clear(r, carry):
        src_ref[r] = 0
        return carry

    lax.fori_loop(0, src_ref.shape[0], clear, 0, unroll=8)

    def place(t, carry):
        src_ref[pos_ref[t]] = t
        return carry

    lax.fori_loop(0, pos_ref.shape[0], place, 0, unroll=8)


def _invert(pos, n_rows):
    return pl.pallas_call(
        _invert_kernel,
        out_shape=jax.ShapeDtypeStruct((n_rows,), jnp.int32),
        in_specs=[pl.BlockSpec(memory_space=pltpu.SMEM)],
        out_specs=pl.BlockSpec(memory_space=pltpu.SMEM),
        name="invert",
    )(pos)


def _routing_tables(route, counts, n_tiles):
    cls = route[0]
    rank = route[1]
    cnt = counts[:N_CLASSES, 0].astype(jnp.int32)
    tiles_c = (cnt + TM_EXPERT - 1) // TM_EXPERT
    tile_end = jnp.cumsum(tiles_c)
    tile_start = tile_end - tiles_c
    n_used = tile_end[-1]
    pos = (tile_start[cls] * TM_EXPERT + rank).astype(jnp.int32)
    src = _invert(pos, (n_tiles + 1) * TM_EXPERT)

    steps = jnp.arange(n_tiles + 1, dtype=jnp.int32)
    blk = jnp.minimum(steps, n_tiles - 1)
    k = jnp.minimum(steps, n_used - 1)
    c = jnp.sum((k[:, None] >= tile_end[None, :]).astype(jnp.int32), axis=1)
    grp = c // PAIRS_PER_GROUP
    cig = c % PAIRS_PER_GROUP
    slot_a = jnp.asarray(CLASS_SLOT_A, jnp.int32)[cig]
    slot_b = jnp.asarray(CLASS_SLOT_B, jnp.int32)[cig]
    ea = grp * EXPERTS_PER_GROUP + slot_a
    eb = grp * EXPERTS_PER_GROUP + slot_b
    lanes = jnp.stack([grp, N_GROUPS + ea, N_GROUPS + eb]).astype(jnp.int32)
    return (pos, src, blk, ea.astype(jnp.int32), eb.astype(jnp.int32), lanes,
            n_used.reshape(1).astype(jnp.int32))


def kernel(x, c, w_ada, b_ada, norm1_g, w_in, sinks, w_pool, pool_scale,
           w_attn_branch, w_pool_branch, w_out, norm2_g, w_router_group,
           b_router_group, w_router_expert, b_router_expert, w_e_gate, w_e_up,
           w_e_down, final_g):
    b, s, d = x.shape
    t = b * s
    assert w_ada.shape[0] == 1, "single-layer block"
    assert d == D_MODEL and s % TM_PROJ == 0
    l = 0
    n_tiles = t // TM_EXPERT + N_CLASSES
    x2d = x.reshape(t, d)
    ada3 = _ada(c, w_ada[l], b_ada[l]).reshape(b, N_ADA, d)

    qkvu, gates = _in_proj(x2d, ada3, norm1_g[l].reshape(1, d),
                           w_in[l].astype(BF16), s)
    attn = _attn(qkvu, sinks[l], s)
    pool = _pool(qkvu, w_pool[l].astype(BF16), pool_scale[l].reshape(1, -1), s)

    w_r = jnp.concatenate([w_router_group[l], w_router_expert[l]], axis=1)
    b_r = jnp.concatenate([b_router_group[l], b_router_expert[l]])
    n_r = N_GROUPS + N_EXPERTS
    wr_t = jnp.pad(w_r.T, ((0, ROUTER_ROWS - n_r), (0, 0))).astype(BF16)
    br_col = jnp.broadcast_to(
        jnp.pad(b_r, (0, ROUTER_ROWS - n_r))[:, None], (ROUTER_ROWS, 128))
    wr_lanes = jnp.pad(w_r, ((0, 0), (0, ROUTER_LANES - n_r))).astype(BF16)
    br_lanes = jnp.pad(b_r, (0, ROUTER_LANES - n_r)).reshape(1, ROUTER_LANES)

    x2, h2, route, counts = _merge(
        attn, pool, gates, x2d, ada3, w_attn_branch[l].astype(BF16),
        w_pool_branch[l].astype(BF16), w_out[l].astype(BF16),
        norm2_g[l].reshape(1, d), wr_t, br_col, s)

    pos, src, tile_blk, tile_ea, tile_eb, tile_lanes, n_used = (
        _routing_tables(route, counts, n_tiles))
    ys = _experts(src, tile_blk, tile_ea, tile_eb, tile_lanes, n_used,
                  h2, w_e_gate[l].astype(BF16), w_e_up[l].astype(BF16),
                  w_e_down[l].astype(BF16), wr_lanes, br_lanes, n_tiles)
    out = _combine(pos, ys, x2, ada3, final_g.reshape(1, d), s)
    return out.reshape(b, s, d)
```

```python
import functools

import jax
import jax.numpy as jnp
from jax import lax
from jax.experimental import pallas as pl
from jax.experimental.pallas import tpu as pltpu

F32 = jnp.float32
BF16 = jnp.bfloat16

D_MODEL = 2048
HEAD_DIM = 64
ATTN_HEADS = 16
KV_HEADS = 4
Q_PER_KV = ATTN_HEADS // KV_HEADS
ATTN_WIDTH = ATTN_HEADS * HEAD_DIM
KV_WIDTH = KV_HEADS * HEAD_DIM
BLOCK = 128
POOL_WINDOWS = (2, 4, 8, 16)
POOL_WIDTH = 1024
POOL_GROUP_DIM = 256
QKVU_WIDTH = ATTN_WIDTH + 2 * KV_WIDTH + POOL_WIDTH
GATE_WIDTH = 2 * D_MODEL
PROJ_WIDTH = GATE_WIDTH + QKVU_WIDTH
Q_COL = GATE_WIDTH
K_COL = Q_COL + ATTN_WIDTH
V_COL = K_COL + KV_WIDTH
U_COL = V_COL + KV_WIDTH
N_GROUPS = 4
EXPERTS_PER_GROUP = 4
N_EXPERTS = 16
EXPERT_FF = 512
N_ADA = 6
NORM_EPS = 1e-6
MASK_VALUE = -1e30

PAIRS_PER_GROUP = 6
N_CLASSES = N_GROUPS * PAIRS_PER_GROUP
CLASS_SLOT_A = (0, 0, 0, 1, 1, 3)
CLASS_SLOT_B = (1, 2, 3, 3, 2, 2)
ROUTER_ROWS = 32
ROUTER_LANES = 128

TM_PROJ = 1024
TN_PROJ = 512
TM_MIX = 512
TM_MERGE = 256
TM_DISPATCH = 256
TM_EXPERT = 256
TM_COMBINE = 256
TN_ADA = 1024

VMEM_LIMIT = 52 * 1024 * 1024


def _rms_modulate(x, g, scale, shift):
    ms = jnp.mean(x * x, axis=-1, keepdims=True)
    return (x * lax.rsqrt(ms + NORM_EPS) * g) * (1.0 + scale) + shift


def _ada_kernel(cb_ref, w_ref, b_ref, o_ref):
    nb = cb_ref.shape[0]
    d = w_ref.shape[0]
    nchunk = w_ref.shape[1] // 128

    def body(kb, accs):
        k0 = pl.multiple_of(kb * 8, 8)
        new = list(accs)
        cbs = [cb_ref[b, pl.ds(k0, 8), :] for b in range(nb)]
        for j in range(nchunk):
            w = w_ref[pl.ds(k0, 8), j * 128:(j + 1) * 128]
            for b in range(nb):
                new[b * nchunk + j] = new[b * nchunk + j] + w * cbs[b]
        return tuple(new)

    init = tuple(jnp.zeros((8, 128), F32) for _ in range(nb * nchunk))
    accs = lax.fori_loop(0, d // 8, body, init, unroll=2)
    for b in range(nb):
        for j in range(nchunk):
            o_ref[b:b + 1, j * 128:(j + 1) * 128] = (
                jnp.sum(accs[b * nchunk + j], axis=0, keepdims=True)
                + b_ref[:, j * 128:(j + 1) * 128])


def _ada(c, w_ada, b_ada):
    nb, d = c.shape
    n = w_ada.shape[1]
    cb = jnp.broadcast_to(c[:, :, None], (nb, d, 128))
    return pl.pallas_call(
        _ada_kernel,
        out_shape=jax.ShapeDtypeStruct((nb, n), F32),
        grid=(n // TN_ADA,),
        in_specs=[
            pl.BlockSpec((nb, d, 128), lambda j: (0, 0, 0)),
            pl.BlockSpec((d, TN_ADA), lambda j: (0, j)),
            pl.BlockSpec((1, TN_ADA), lambda j: (0, j)),
        ],
        out_specs=pl.BlockSpec((nb, TN_ADA), lambda j: (0, j)),
        compiler_params=pltpu.CompilerParams(
            dimension_semantics=("arbitrary",), vmem_limit_bytes=VMEM_LIMIT),
        name="ada",
    )(cb, w_ada, b_ada.reshape(1, n))


def _in_proj_kernel(x_ref, ada_ref, g_ref, w_ref, wa_ref, wp_ref, wo_ref,
                    proj_ref, wa_o, wp_o, wo_o, h_ref):
    j = pl.program_id(1)

    @pl.when(j == 0)
    def _():
        chunk = 256
        for r in range(0, TM_PROJ, chunk):
            h = _rms_modulate(x_ref[r:r + chunk, :], g_ref[...],
                              ada_ref[0, 1:2, :], ada_ref[0, 0:1, :])
            h_ref[r:r + chunk, :] = h.astype(BF16)
        wa_o[...] = wa_ref[...].astype(BF16)
        wp_o[...] = wp_ref[...].astype(BF16)
        wo_o[...] = wo_ref[...].astype(BF16)

    acc = jnp.dot(h_ref[...], w_ref[...].astype(BF16), preferred_element_type=F32)
    is_gate = j < GATE_WIDTH // TN_PROJ
    proj_ref[...] = jnp.where(is_gate, jax.nn.sigmoid(acc), acc).astype(BF16)


def _in_proj(x2d, ada3, norm_g, w_in, wa, wp, wo, seq):
    t, d = x2d.shape
    n_all = PROJ_WIDTH // TN_PROJ
    n_gate = GATE_WIDTH // TN_PROJ
    n_row = t // TM_PROJ
    tiles_per_seq = seq // TM_PROJ
    w_tile = lambda i, j: (0, (j + (n_all - n_gate)) % n_all)
    slab = lambda w: pl.BlockSpec((w.shape[0] // n_row, w.shape[1]),
                                  lambda i, j: (i, 0))
    cast = lambda w: jax.ShapeDtypeStruct(w.shape, BF16)
    return pl.pallas_call(
        _in_proj_kernel,
        out_shape=(jax.ShapeDtypeStruct((t, PROJ_WIDTH), BF16),
                   cast(wa), cast(wp), cast(wo)),
        grid=(n_row, n_all),
        in_specs=[
            pl.BlockSpec((TM_PROJ, d), lambda i, j: (i, 0)),
            pl.BlockSpec((1, N_ADA, d), lambda i, j: (i // tiles_per_seq, 0, 0)),
            pl.BlockSpec((1, d), lambda i, j: (0, 0)),
            pl.BlockSpec((d, TN_PROJ), w_tile),
            slab(wa), slab(wp), slab(wo),
        ],
        out_specs=(
            pl.BlockSpec((TM_PROJ, TN_PROJ), lambda i, j: (i, j)),
            slab(wa), slab(wp), slab(wo),
        ),
        scratch_shapes=[pltpu.VMEM((TM_PROJ, d), BF16)],
        compiler_params=pltpu.CompilerParams(
            dimension_semantics=("arbitrary", "arbitrary"),
            vmem_limit_bytes=VMEM_LIMIT),
        name="in_proj",
    )(x2d, ada3, norm_g, w_in, wa, wp, wo)


def _attn_kernel(sinks_ref, q_ref, kc_ref, vc_ref, kp_ref, vp_ref, o_ref,
                 kfull, vfull, *, steps_per_seq):
    i = pl.program_id(0)
    kfull[0:BLOCK, :] = kp_ref[...]
    kfull[BLOCK:BLOCK + TM_MIX, :] = kc_ref[...]
    vfull[0:BLOCK, :] = vp_ref[...]
    vfull[BLOCK:BLOCK + TM_MIX, :] = vc_ref[...]
    first_step = (i % steps_per_seq) == 0
    nq = Q_PER_KV * BLOCK

    def sb_body(sb, carry):
        r0 = pl.multiple_of(sb * BLOCK, BLOCK)
        row = lax.broadcasted_iota(jnp.int32, (nq, 2 * BLOCK), 0) & (BLOCK - 1)
        col = lax.broadcasted_iota(jnp.int32, (nq, 2 * BLOCK), 1)
        dist = col - row
        lo = jnp.where(jnp.logical_and(first_step, sb == 0), BLOCK, 0)
        valid = (dist > 0) & (dist <= BLOCK) & (col >= lo)
        rg = lax.broadcasted_iota(jnp.int32, (nq, 1), 0) // BLOCK
        for h in range(KV_HEADS):
            kb = kfull[pl.ds(r0, 2 * BLOCK), h * HEAD_DIM:(h + 1) * HEAD_DIM]
            vb = vfull[pl.ds(r0, 2 * BLOCK), h * HEAD_DIM:(h + 1) * HEAD_DIM]
            qg = jnp.concatenate(
                [q_ref[pl.ds(r0, BLOCK),
                       (Q_PER_KV * h + g) * HEAD_DIM:(Q_PER_KV * h + g + 1) * HEAD_DIM]
                 for g in range(Q_PER_KV)], axis=0)
            qg = qg * jnp.asarray(HEAD_DIM ** -0.5, BF16)
            logits = lax.dot_general(qg, kb, (((1,), (1,)), ((), ())),
                                     preferred_element_type=F32)
            logits = jnp.where(valid, logits, MASK_VALUE)
            sink = jnp.full((nq, 1), sinks_ref[Q_PER_KV * h + Q_PER_KV - 1], F32)
            for g in range(Q_PER_KV - 2, -1, -1):
                sink = jnp.where(rg == g, sinks_ref[Q_PER_KV * h + g], sink)
            m = jnp.maximum(jnp.max(logits, axis=1, keepdims=True), sink)
            p = jnp.exp(logits - m)
            denom = jnp.sum(p, axis=1, keepdims=True) + jnp.exp(sink - m)
            o = jnp.dot(p.astype(BF16), vb, preferred_element_type=F32)
            o = (o * (1.0 / denom)).astype(BF16)
            o_ref[pl.ds(r0, BLOCK),
                  h * Q_PER_KV * HEAD_DIM:(h + 1) * Q_PER_KV * HEAD_DIM] = (
                jnp.concatenate([o[g * BLOCK:(g + 1) * BLOCK, :]
                                 for g in range(Q_PER_KV)], axis=1))
        return carry

    lax.fori_loop(0, TM_MIX // BLOCK, sb_body, 0)


def _attn(qkvu, sinks, seq):
    t = qkvu.shape[0]
    steps_per_seq = seq // TM_MIX
    sub = TM_MIX // BLOCK
    qcol = Q_COL // ATTN_WIDTH
    kcol = K_COL // KV_WIDTH
    prev = lambda i: jnp.maximum(i * sub - 1, 0)
    return pl.pallas_call(
        functools.partial(_attn_kernel, steps_per_seq=steps_per_seq),
        out_shape=jax.ShapeDtypeStruct((t, ATTN_WIDTH), BF16),
        grid=(t // TM_MIX,),
        in_specs=[
            pl.BlockSpec(memory_space=pltpu.SMEM),
            pl.BlockSpec((TM_MIX, ATTN_WIDTH), lambda i: (i, qcol)),
            pl.BlockSpec((TM_MIX, KV_WIDTH), lambda i: (i, kcol)),
            pl.BlockSpec((TM_MIX, KV_WIDTH), lambda i: (i, kcol + 1)),
            pl.BlockSpec((BLOCK, KV_WIDTH), lambda i: (prev(i), kcol)),
            pl.BlockSpec((BLOCK, KV_WIDTH), lambda i: (prev(i), kcol + 1)),
        ],
        out_specs=pl.BlockSpec((TM_MIX, ATTN_WIDTH), lambda i: (i, 0)),
        scratch_shapes=[pltpu.VMEM((BLOCK + TM_MIX, KV_WIDTH), BF16),
                        pltpu.VMEM((BLOCK + TM_MIX, KV_WIDTH), BF16)],
        compiler_params=pltpu.CompilerParams(
            dimension_semantics=("arbitrary",), vmem_limit_bytes=VMEM_LIMIT),
        name="attn",
    )(sinks, qkvu, qkvu, qkvu, qkvu, qkvu)


def _pool_kernel(ulo_ref, uhi_ref, plo_ref, phi_ref, wp_ref, ps_ref, o_ref,
                 ufull, *, steps_per_seq):
    i = pl.program_id(0)
    first_step = (i % steps_per_seq) == 0
    half = POOL_WIDTH // 2
    zeros = jnp.zeros((BLOCK, half), BF16)
    ufull[0:BLOCK, 0:half] = jnp.where(first_step, zeros, plo_ref[...])
    ufull[0:BLOCK, half:POOL_WIDTH] = jnp.where(first_step, zeros, phi_ref[...])
    ufull[BLOCK:BLOCK + TM_MIX, 0:half] = ulo_ref[...]
    ufull[BLOCK:BLOCK + TM_MIX, half:POOL_WIDTH] = uhi_ref[...]
    pos0 = (i % steps_per_seq) * TM_MIX
    row = lax.broadcasted_iota(jnp.int32, (BLOCK, 2 * BLOCK), 0)
    col = lax.broadcasted_iota(jnp.int32, (BLOCK, 2 * BLOCK), 1)
    rel = row + BLOCK - col
    bands = [((rel >= 0) & (rel < w)).astype(F32).astype(BF16)
             for w in POOL_WINDOWS]

    def sb_body(sb, carry):
        r0 = pl.multiple_of(sb * BLOCK, BLOCK)
        pos = pos0 + sb * BLOCK + lax.broadcasted_iota(jnp.int32, (BLOCK, 1), 0)
        for g, w in enumerate(POOL_WINDOWS):
            cols = slice(g * POOL_GROUP_DIM, (g + 1) * POOL_GROUP_DIM)
            band = ufull[pl.ds(r0, 2 * BLOCK), cols]
            ssum = jnp.dot(bands[g], band, preferred_element_type=F32)
            xg = ufull[pl.ds(r0 + BLOCK, BLOCK), cols].astype(F32)
            count = jnp.minimum(pos + 1, w).astype(F32)
            pooled = ssum / count - xg
            mixed = jnp.dot(pooled.astype(BF16), wp_ref[g],
                            preferred_element_type=F32)
            o_ref[pl.ds(r0, BLOCK), cols] = (mixed * ps_ref[:, cols]).astype(BF16)
        return carry

    lax.fori_loop(0, TM_MIX // BLOCK, sb_body, 0)


def _pool(qkvu, w_pool_b, pool_scale, seq):
    t = qkvu.shape[0]
    steps_per_seq = seq // TM_MIX
    sub = TM_MIX // BLOCK
    half = POOL_WIDTH // 2
    ucol = U_COL // half
    prev = lambda i: jnp.maximum(i * sub - 1, 0)
    return pl.pallas_call(
        functools.partial(_pool_kernel, steps_per_seq=steps_per_seq),
        out_shape=jax.ShapeDtypeStruct((t, POOL_WIDTH), BF16),
        grid=(t // TM_MIX,),
        in_specs=[
            pl.BlockSpec((TM_MIX, half), lambda i: (i, ucol)),
            pl.BlockSpec((TM_MIX, half), lambda i: (i, ucol + 1)),
            pl.BlockSpec((BLOCK, half), lambda i: (prev(i), ucol)),
            pl.BlockSpec((BLOCK, half), lambda i: (prev(i), ucol + 1)),
            pl.BlockSpec((len(POOL_WINDOWS), POOL_GROUP_DIM, POOL_GROUP_DIM),
                         lambda i: (0, 0, 0)),
            pl.BlockSpec((1, POOL_WIDTH), lambda i: (0, 0)),
        ],
        out_specs=pl.BlockSpec((TM_MIX, POOL_WIDTH), lambda i: (i, 0)),
        scratch_shapes=[pltpu.VMEM((BLOCK + TM_MIX, POOL_WIDTH), BF16)],
        compiler_params=pltpu.CompilerParams(
            dimension_semantics=("arbitrary",), vmem_limit_bytes=VMEM_LIMIT),
        name="pool",
    )(qkvu, qkvu, qkvu, qkvu, w_pool_b, pool_scale)


def _first_argmax4(v):
    m = jnp.maximum(jnp.maximum(v[0], v[1]), jnp.maximum(v[2], v[3]))
    idx = jnp.where(v[0] == m, 0, jnp.where(v[1] == m, 1,
                                            jnp.where(v[2] == m, 2, 3)))
    return m, idx


def _merge_kernel(attn_ref, pool_ref, ga_ref, gp_ref, x_ref, ada_ref,
                  wa_ref, wp_ref, wo_ref, g2_ref, wr_ref, br_ref,
                  eg_ref, eu_ref, ed_ref,
                  x2_ref, h2_ref, route_ref, counts_ref, eg_o, eu_o, ed_o):
    i = pl.program_id(0)
    eg_o[...] = eg_ref[...].astype(BF16)
    eu_o[...] = eu_ref[...].astype(BF16)
    ed_o[...] = ed_ref[...].astype(BF16)
    a = jnp.dot(attn_ref[...], wa_ref[...], preferred_element_type=F32)
    p = jnp.dot(pool_ref[...], wp_ref[...], preferred_element_type=F32)
    merged = ga_ref[...].astype(F32) * a + gp_ref[...].astype(F32) * p
    out = jnp.dot(merged.astype(BF16), wo_ref[...], preferred_element_type=F32)
    x2 = x_ref[...] + ada_ref[0, 2:3, :] * out
    x2_ref[...] = x2
    h2 = _rms_modulate(x2, g2_ref[...], ada_ref[0, 4:5, :], ada_ref[0, 3:4, :])
    h2_ref[...] = h2

    logits = lax.dot_general(wr_ref[...], h2.astype(BF16),
                             (((1,), (1,)), ((), ())),
                             preferred_element_type=F32) + br_ref[:, 0:1]
    tm = logits.shape[1]
    rows = [logits[r:r + 1, :] for r in range(N_GROUPS + N_EXPERTS)]
    _, gi = _first_argmax4(rows[0:N_GROUPS])
    sel = []
    for e in range(EXPERTS_PER_GROUP):
        v = rows[N_GROUPS + 3 * EXPERTS_PER_GROUP + e]
        for g in range(N_GROUPS - 2, -1, -1):
            v = jnp.where(gi == g, rows[N_GROUPS + g * EXPERTS_PER_GROUP + e], v)
        sel.append(v)
    _, i1 = _first_argmax4(sel)
    rest = [jnp.where(i1 == e, -jnp.inf, sel[e]) for e in range(EXPERTS_PER_GROUP)]
    _, i2 = _first_argmax4(rest)
    code = jnp.minimum(i1, i2) * EXPERTS_PER_GROUP + jnp.maximum(i1, i2)
    cig = jnp.where(code == 1, 0, jnp.where(code == 2, 1, jnp.where(
        code == 3, 2, jnp.where(code == 7, 3, jnp.where(code == 6, 4, 5)))))
    cls = gi * PAIRS_PER_GROUP + cig

    onehot = (lax.broadcasted_iota(jnp.int32, (ROUTER_ROWS, tm), 0) == cls
              ).astype(F32)
    src = lax.broadcasted_iota(jnp.int32, (tm, tm), 0)
    dst = lax.broadcasted_iota(jnp.int32, (tm, tm), 1)
    before = (src < dst).astype(F32).astype(BF16)
    prefix = jnp.dot(onehot.astype(BF16), before, preferred_element_type=F32)

    @pl.when(i == 0)
    def _():
        counts_ref[...] = jnp.zeros_like(counts_ref)

    carry = counts_ref[:, 0:1]
    rank = jnp.sum(onehot * (prefix + carry), axis=0, keepdims=True)
    counts_ref[...] = counts_ref[...] + jnp.sum(onehot, axis=1, keepdims=True)
    route_ref[0:1, :] = cls
    route_ref[1:2, :] = rank.astype(jnp.int32)
    route_ref[2:8, :] = jnp.zeros((6, tm), jnp.int32)


def _merge(attn, pool, proj, x2d, ada3, wa_b, wp_b, wo_b, norm2_g, wr_t, br_col,
           w_e_gate, w_e_up, w_e_down, seq):
    t, d = x2d.shape
    n_step = t // TM_MERGE
    tiles_per_seq = seq // TM_MERGE
    flat = lambda w: w.reshape(-1, w.shape[-1])
    slab = lambda w: pl.BlockSpec((w.shape[0] // n_step, w.shape[1]),
                                  lambda i: (i, 0))
    cast = lambda w: jax.ShapeDtypeStruct(w.shape, BF16)
    eg, eu, ed = flat(w_e_gate), flat(w_e_up), flat(w_e_down)
    const = lambda shape: pl.BlockSpec(shape, lambda i: (0,) * len(shape),
                                       pipeline_mode=pl.Buffered(1))
    return pl.pallas_call(
        _merge_kernel,
        out_shape=(jax.ShapeDtypeStruct((t, d), F32),
                   jax.ShapeDtypeStruct((t, d), F32),
                   jax.ShapeDtypeStruct((8, t), jnp.int32),
                   jax.ShapeDtypeStruct((ROUTER_ROWS, 128), F32),
                   cast(eg), cast(eu), cast(ed)),
        grid=(n_step,),
        in_specs=[
            pl.BlockSpec((TM_MERGE, ATTN_WIDTH), lambda i: (i, 0)),
            pl.BlockSpec((TM_MERGE, POOL_WIDTH), lambda i: (i, 0)),
            pl.BlockSpec((TM_MERGE, d), lambda i: (i, 0)),
            pl.BlockSpec((TM_MERGE, d), lambda i: (i, 1)),
            pl.BlockSpec((TM_MERGE, d), lambda i: (i, 0)),
            pl.BlockSpec((1, N_ADA, d), lambda i: (i // tiles_per_seq, 0, 0)),
            const((ATTN_WIDTH, d)),
            const((POOL_WIDTH, d)),
            const((d, d)),
            const((1, d)),
            const((ROUTER_ROWS, d)),
            const((ROUTER_ROWS, 128)),
            slab(eg), slab(eu), slab(ed),
        ],
        out_specs=(
            pl.BlockSpec((TM_MERGE, d), lambda i: (i, 0)),
            pl.BlockSpec((TM_MERGE, d), lambda i: (i, 0)),
            pl.BlockSpec((8, TM_MERGE), lambda i: (0, i)),
            pl.BlockSpec((ROUTER_ROWS, 128), lambda i: (0, 0)),
            slab(eg), slab(eu), slab(ed),
        ),
        compiler_params=pltpu.CompilerParams(
            dimension_semantics=("arbitrary",), vmem_limit_bytes=VMEM_LIMIT),
        name="merge",
    )(attn, pool, proj, proj, x2d, ada3, wa_b, wp_b, wo_b, norm2_g, wr_t, br_col,
      eg, eu, ed)


def _row_copy(src_ref, src_row, dst_ref, dst_row, sem):
    return pltpu.make_async_copy(src_ref.at[pl.ds(src_row, 1)],
                                 dst_ref.at[pl.ds(dst_row, 1)], sem)


def _dispatch_kernel(pos_ref, h_ref, hs_ref, sem):
    i = pl.program_id(0)
    base = i * TM_DISPATCH

    for r in range(TM_DISPATCH):
        _row_copy(h_ref, r, hs_ref, pos_ref[base + r], sem).start()
    pltpu.make_async_copy(h_ref, hs_ref.at[pl.ds(0, TM_DISPATCH)], sem).wait()


def _dispatch(pos, h2, n_rows):
    t, d = h2.shape
    return pl.pallas_call(
        _dispatch_kernel,
        out_shape=jax.ShapeDtypeStruct((n_rows, d), F32),
        grid_spec=pltpu.PrefetchScalarGridSpec(
            num_scalar_prefetch=1,
            grid=(t // TM_DISPATCH,),
            in_specs=[pl.BlockSpec((TM_DISPATCH, d), lambda i, pos: (i, 0))],
            out_specs=pl.BlockSpec(memory_space=pl.ANY),
            scratch_shapes=[pltpu.SemaphoreType.DMA(())],
        ),
        compiler_params=pltpu.CompilerParams(
            dimension_semantics=("arbitrary",), vmem_limit_bytes=VMEM_LIMIT,
            has_side_effects=True),
        name="dispatch",
    )(pos, h2)


def _expert_kernel(blk_ref, ea_ref, eb_ref, lane_ref, nvalid_ref, nused_ref,
                   hs_ref, wga_ref, wua_ref, wda_ref, wgb_ref, wub_ref, wdb_ref,
                   wr_ref, br_ref, ys_ref):
    i = pl.program_id(0)

    @pl.when(i >= nused_ref[0])
    def _():
        ys_ref[...] = jnp.zeros_like(ys_ref)

    @pl.when(i < nused_ref[0])
    def _():
        tm = hs_ref.shape[0]
        rows = lax.broadcasted_iota(jnp.int32, (tm, 1), 0)
        h = jnp.where(rows < nvalid_ref[i], hs_ref[...], 0.0).astype(BF16)
        logits = jnp.dot(h, wr_ref[...], preferred_element_type=F32) + br_ref[...]
        lane = lax.broadcasted_iota(jnp.int32, logits.shape, 1)
        is_group = lane < N_GROUPS
        gmax = jnp.max(jnp.where(is_group, logits, -jnp.inf), axis=1, keepdims=True)
        gexp = jnp.where(is_group, jnp.exp(logits - gmax), 0.0)
        pick = lambda v, l: jnp.sum(jnp.where(lane == l, v, 0.0), axis=1,
                                    keepdims=True)
        p_group = pick(gexp, lane_ref[0, i]) / jnp.sum(gexp, axis=1, keepdims=True)
        la = pick(logits, lane_ref[1, i])
        lb = pick(logits, lane_ref[2, i])
        mx = jnp.maximum(la, lb)
        ea = jnp.exp(la - mx)
        eb = jnp.exp(lb - mx)
        inv = p_group / (ea + eb)

        def expert(wg_ref, wu_ref, wd_ref, weight):
            gate = jnp.dot(h, wg_ref[0], preferred_element_type=F32)
            up = jnp.dot(h, wu_ref[0], preferred_element_type=F32)
            act = (jax.nn.silu(gate) * up * weight).astype(BF16)
            return jnp.dot(act, wd_ref[0], preferred_element_type=F32)

        ys_ref[...] = (expert(wga_ref, wua_ref, wda_ref, ea * inv)
                       + expert(wgb_ref, wub_ref, wdb_ref, eb * inv))


def _experts(tile_blk, tile_ea, tile_eb, tile_lanes, tile_nvalid, n_used,
             hs, wg_b, wu_b, wd_b, wr_lanes, br_lanes):
    p_rows, d = hs.shape
    n_tiles = p_rows // TM_EXPERT
    row_map = lambda i, blk, ea, eb, ln, nv, nu: (blk[i], 0)
    wa_map = lambda i, blk, ea, eb, ln, nv, nu: (ea[i], 0, 0)
    wb_map = lambda i, blk, ea, eb, ln, nv, nu: (eb[i], 0, 0)
    const = lambda i, blk, ea, eb, ln, nv, nu: (0, 0)
    return pl.pallas_call(
        _expert_kernel,
        out_shape=jax.ShapeDtypeStruct((p_rows, d), F32),
        grid_spec=pltpu.PrefetchScalarGridSpec(
            num_scalar_prefetch=6,
            grid=(n_tiles,),
            in_specs=[
                pl.BlockSpec((TM_EXPERT, d), row_map),
                pl.BlockSpec((1, d, EXPERT_FF), wa_map),
                pl.BlockSpec((1, d, EXPERT_FF), wa_map),
                pl.BlockSpec((1, EXPERT_FF, d), wa_map),
                pl.BlockSpec((1, d, EXPERT_FF), wb_map),
                pl.BlockSpec((1, d, EXPERT_FF), wb_map),
                pl.BlockSpec((1, EXPERT_FF, d), wb_map),
                pl.BlockSpec((d, ROUTER_LANES), const),
                pl.BlockSpec((1, ROUTER_LANES), const),
            ],
            out_specs=pl.BlockSpec((TM_EXPERT, d), row_map),
        ),
        compiler_params=pltpu.CompilerParams(
            dimension_semantics=("arbitrary",), vmem_limit_bytes=VMEM_LIMIT),
        name="experts",
    )(tile_blk, tile_ea, tile_eb, tile_lanes, tile_nvalid, n_used,
      hs, wg_b, wu_b, wd_b, wg_b, wu_b, wd_b, wr_lanes, br_lanes)


def _combine_kernel(pos_ref, ys_ref, x2_ref, ada_ref, g_ref, o_ref, ybuf, sems):
    i = pl.program_id(0)
    n = pl.num_programs(0)

    def gather(tile, slot):
        for r in range(TM_COMBINE):
            _row_copy(ys_ref, pos_ref[tile * TM_COMBINE + r],
                      ybuf.at[slot], r, sems.at[slot]).start()

    @pl.when(i == 0)
    def _():
        gather(0, 0)

    @pl.when(i + 1 < n)
    def _():
        gather(i + 1, (i + 1) % 2)

    slot = i % 2
    pltpu.make_async_copy(ys_ref.at[pl.ds(0, TM_COMBINE)], ybuf.at[slot],
                          sems.at[slot]).wait()

    x = x2_ref[...] + ada_ref[0, 5:6, :] * ybuf[slot]
    ms = jnp.mean(x * x, axis=-1, keepdims=True)
    o_ref[...] = x * lax.rsqrt(ms + NORM_EPS) * g_ref[...]


def _combine(pos, ys, x2, ada3, final_g, seq):
    t, d = x2.shape
    tiles_per_seq = seq // TM_COMBINE
    return pl.pallas_call(
        _combine_kernel,
        out_shape=jax.ShapeDtypeStruct((t, d), F32),
        grid_spec=pltpu.PrefetchScalarGridSpec(
            num_scalar_prefetch=1,
            grid=(t // TM_COMBINE,),
            in_specs=[
                pl.BlockSpec(memory_space=pl.ANY),
                pl.BlockSpec((TM_COMBINE, d), lambda i, pos: (i, 0)),
                pl.BlockSpec((1, N_ADA, d),
                             lambda i, pos: (i // tiles_per_seq, 0, 0)),
                pl.BlockSpec((1, d), lambda i, pos: (0, 0)),
            ],
            out_specs=pl.BlockSpec((TM_COMBINE, d), lambda i, pos: (i, 0)),
            scratch_shapes=[pltpu.VMEM((2, TM_COMBINE, d), F32),
                            pltpu.SemaphoreType.DMA((2,))],
        ),
        compiler_params=pltpu.CompilerParams(
            dimension_semantics=("arbitrary",), vmem_limit_bytes=VMEM_LIMIT),
        name="combine",
    )(pos, ys, x2, ada3, final_g)


def _routing_tables(route, counts, n_tiles):
    cls = route[0]
    rank = route[1]
    cnt = counts[:N_CLASSES, 0].astype(jnp.int32)
    tiles_c = (cnt + TM_EXPERT - 1) // TM_EXPERT
    tile_end = jnp.cumsum(tiles_c)
    tile_start = tile_end - tiles_c
    n_used = tile_end[-1]
    pos = tile_start[cls] * TM_EXPERT + rank

    steps = jnp.arange(n_tiles, dtype=jnp.int32)
    k = jnp.minimum(steps, n_used - 1)
    c = jnp.sum((k[:, None] >= tile_end[None, :]).astype(jnp.int32), axis=1)
    grp = c // PAIRS_PER_GROUP
    cig = c % PAIRS_PER_GROUP
    slot_a = jnp.asarray(CLASS_SLOT_A, jnp.int32)[cig]
    slot_b = jnp.asarray(CLASS_SLOT_B, jnp.int32)[cig]
    ea = grp * EXPERTS_PER_GROUP + slot_a
    eb = grp * EXPERTS_PER_GROUP + slot_b
    lanes = jnp.stack([grp, N_GROUPS + ea, N_GROUPS + eb]).astype(jnp.int32)
    nvalid = jnp.clip(cnt[c] - (k - tile_start[c]) * TM_EXPERT, 0, TM_EXPERT)
    return (pos.astype(jnp.int32), steps, ea.astype(jnp.int32),
            eb.astype(jnp.int32), lanes, nvalid.astype(jnp.int32),
            n_used.reshape(1).astype(jnp.int32))


def kernel(x, c, w_ada, b_ada, norm1_g, w_in, sinks, w_pool, pool_scale,
           w_attn_branch, w_pool_branch, w_out, norm2_g, w_router_group,
           b_router_group, w_router_expert, b_router_expert, w_e_gate, w_e_up,
           w_e_down, final_g):
    b, s, d = x.shape
    t = b * s
    assert w_ada.shape[0] == 1, "single-layer block"
    assert d == D_MODEL and s % TM_PROJ == 0
    l = 0
    n_tiles = t // TM_EXPERT + N_CLASSES
    x2d = x.reshape(t, d)
    ada3 = _ada(c, w_ada[l], b_ada[l]).reshape(b, N_ADA, d)

    proj, wa_b, wp_b, wo_b = _in_proj(
        x2d, ada3, norm1_g[l].reshape(1, d), w_in[l], w_attn_branch[l],
        w_pool_branch[l], w_out[l], s)
    attn = _attn(proj, sinks[l], s)
    pool = _pool(proj, w_pool[l].astype(BF16), pool_scale[l].reshape(1, -1), s)

    w_r = jnp.concatenate([w_router_group[l], w_router_expert[l]], axis=1)
    b_r = jnp.concatenate([b_router_group[l], b_router_expert[l]])
    n_r = N_GROUPS + N_EXPERTS
    wr_t = jnp.pad(w_r.T, ((0, ROUTER_ROWS - n_r), (0, 0))).astype(BF16)
    br_col = jnp.broadcast_to(
        jnp.pad(b_r, (0, ROUTER_ROWS - n_r))[:, None], (ROUTER_ROWS, 128))
    wr_lanes = jnp.pad(w_r, ((0, 0), (0, ROUTER_LANES - n_r))).astype(BF16)
    br_lanes = jnp.pad(b_r, (0, ROUTER_LANES - n_r)).reshape(1, ROUTER_LANES)

    x2, h2, route, counts, eg_b, eu_b, ed_b = _merge(
        attn, pool, proj, x2d, ada3, wa_b, wp_b, wo_b, norm2_g[l].reshape(1, d),
        wr_t, br_col, w_e_gate[l], w_e_up[l], w_e_down[l], s)

    pos, tile_blk, tile_ea, tile_eb, tile_lanes, tile_nvalid, n_used = (
        _routing_tables(route, counts, n_tiles))
    hs = _dispatch(pos, h2, n_tiles * TM_EXPERT)
    ys = _experts(tile_blk, tile_ea, tile_eb, tile_lanes, tile_nvalid, n_used,
                  hs, eg_b.reshape(w_e_gate[l].shape), eu_b.reshape(w_e_up[l].shape),
                  ed_b.reshape(w_e_down[l].shape), wr_lanes, br_lanes)
    out = _combine(pos, ys, x2, ada3, final_g.reshape(1, d), s)
    return out.reshape(b, s, d)
```

```python
import functools

import jax
import jax.numpy as jnp
from jax import lax
from jax.experimental import pallas as pl
from jax.experimental.pallas import tpu as pltpu

F32 = jnp.float32
BF16 = jnp.bfloat16

D_MODEL = 2048
HEAD_DIM = 64
ATTN_HEADS = 16
KV_HEADS = 4
Q_PER_KV = ATTN_HEADS // KV_HEADS
ATTN_WIDTH = ATTN_HEADS * HEAD_DIM
KV_WIDTH = KV_HEADS * HEAD_DIM
BLOCK = 128
POOL_WINDOWS = (2, 4, 8, 16)
POOL_WIDTH = 1024
POOL_GROUP_DIM = 256
QKVU_WIDTH = ATTN_WIDTH + 2 * KV_WIDTH + POOL_WIDTH
GATE_WIDTH = 2 * D_MODEL
PROJ_WIDTH = GATE_WIDTH + QKVU_WIDTH
Q_COL = GATE_WIDTH
K_COL = Q_COL + ATTN_WIDTH
V_COL = K_COL + KV_WIDTH
U_COL = V_COL + KV_WIDTH
N_GROUPS = 4
EXPERTS_PER_GROUP = 4
N_EXPERTS = 16
EXPERT_FF = 512
N_ADA = 6
NORM_EPS = 1e-6
MASK_VALUE = -1e30

PAIRS_PER_GROUP = 6
N_CLASSES = N_GROUPS * PAIRS_PER_GROUP
CLASS_SLOT_A = (0, 0, 0, 1, 1, 3)
CLASS_SLOT_B = (1, 2, 3, 3, 2, 2)
ROUTER_ROWS = 32
ROUTER_LANES = 128

TM_PROJ = 1024
TN_PROJ = 512
NORM_ROWS = 128
NORM_CHUNKS = TM_PROJ // NORM_ROWS
TM_MIX = 512
TM_MERGE = 256
TM_DISPATCH = 256
TM_EXPERT = 256
TM_COMBINE = 256
TN_ADA = 1024

VMEM_LIMIT = 52 * 1024 * 1024


def _rms_modulate(x, g, scale, shift):
    ms = jnp.mean(x * x, axis=-1, keepdims=True)
    return (x * lax.rsqrt(ms + NORM_EPS) * g) * (1.0 + scale) + shift


def _ada_kernel(cb_ref, w_ref, b_ref, o_ref):
    nb = cb_ref.shape[0]
    d = w_ref.shape[0]
    nchunk = w_ref.shape[1] // 128

    def body(kb, accs):
        k0 = pl.multiple_of(kb * 8, 8)
        new = list(accs)
        cbs = [cb_ref[b, pl.ds(k0, 8), :] for b in range(nb)]
        for j in range(nchunk):
            w = w_ref[pl.ds(k0, 8), j * 128:(j + 1) * 128]
            for b in range(nb):
                new[b * nchunk + j] = new[b * nchunk + j] + w * cbs[b]
        return tuple(new)

    init = tuple(jnp.zeros((8, 128), F32) for _ in range(nb * nchunk))
    accs = lax.fori_loop(0, d // 8, body, init, unroll=2)
    for b in range(nb):
        for j in range(nchunk):
            o_ref[b:b + 1, j * 128:(j + 1) * 128] = (
                jnp.sum(accs[b * nchunk + j], axis=0, keepdims=True)
                + b_ref[:, j * 128:(j + 1) * 128])


def _ada(c, w_ada, b_ada):
    nb, d = c.shape
    n = w_ada.shape[1]
    cb = jnp.broadcast_to(c[:, :, None], (nb, d, 128))
    return pl.pallas_call(
        _ada_kernel,
        out_shape=jax.ShapeDtypeStruct((nb, n), F32),
        grid=(n // TN_ADA,),
        in_specs=[
            pl.BlockSpec((nb, d, 128), lambda j: (0, 0, 0)),
            pl.BlockSpec((d, TN_ADA), lambda j: (0, j)),
            pl.BlockSpec((1, TN_ADA), lambda j: (0, j)),
        ],
        out_specs=pl.BlockSpec((nb, TN_ADA), lambda j: (0, j)),
        compiler_params=pltpu.CompilerParams(
            dimension_semantics=("arbitrary",), vmem_limit_bytes=VMEM_LIMIT),
        name="ada",
    )(cb, w_ada, b_ada.reshape(1, n))


def _in_proj_kernel(x_ref, ada_ref, g_ref, w_ref, wa_ref, wp_ref, wo_ref,
                    proj_ref, wa_o, wp_o, wo_o, h_even, h_odd, *, n_row):
    i = pl.program_id(0)
    j = pl.program_id(1)

    def norm_chunk(h_ref):
        r0 = pl.multiple_of(jnp.minimum(j, NORM_CHUNKS - 1) * NORM_ROWS, NORM_ROWS)
        h = _rms_modulate(x_ref[...], g_ref[...],
                          ada_ref[0, 1:2, :], ada_ref[0, 0:1, :])
        h_ref[pl.ds(r0, NORM_ROWS), :] = h.astype(BF16)

    def step(h_write, h_read):
        norm_chunk(h_write)
        acc = jnp.dot(h_read[...], w_ref[...].astype(BF16),
                      preferred_element_type=F32)
        is_gate = j < GATE_WIDTH // TN_PROJ
        proj_ref[...] = jnp.where(is_gate, jax.nn.sigmoid(acc), acc).astype(BF16)

    @pl.when(jnp.logical_and(j == 0, i < n_row))
    def _():
        wa_o[...] = wa_ref[...].astype(BF16)
        wp_o[...] = wp_ref[...].astype(BF16)
        wo_o[...] = wo_ref[...].astype(BF16)

    @pl.when(i == 0)
    def _():
        norm_chunk(h_even)

    @pl.when(i % 2 == 1)
    def _():
        step(h_odd, h_even)

    @pl.when(jnp.logical_and(i > 0, i % 2 == 0))
    def _():
        step(h_even, h_odd)


def _in_proj(x2d, ada3, norm_g, w_in, wa, wp, wo, seq):
    t, d = x2d.shape
    n_all = PROJ_WIDTH // TN_PROJ
    n_gate = GATE_WIDTH // TN_PROJ
    n_row = t // TM_PROJ
    tiles_per_seq = seq // TM_PROJ
    assert NORM_CHUNKS <= n_all
    norm_tile = lambda i: jnp.minimum(i, n_row - 1)
    out_tile = lambda i: jnp.maximum(i - 1, 0)
    w_tile = lambda i, j: (
        0, (jnp.where(i == 0, 0, j) + (n_all - n_gate)) % n_all)
    slab = lambda w: pl.BlockSpec((w.shape[0] // n_row, w.shape[1]),
                                  lambda i, j: (norm_tile(i), 0))
    cast = lambda w: jax.ShapeDtypeStruct(w.shape, BF16)
    return pl.pallas_call(
        functools.partial(_in_proj_kernel, n_row=n_row),
        out_shape=(jax.ShapeDtypeStruct((t, PROJ_WIDTH), BF16),
                   cast(wa), cast(wp), cast(wo)),
        grid=(n_row + 1, n_all),
        in_specs=[
            pl.BlockSpec((NORM_ROWS, d), lambda i, j: (
                norm_tile(i) * NORM_CHUNKS + jnp.minimum(j, NORM_CHUNKS - 1), 0)),
            pl.BlockSpec((1, N_ADA, d),
                         lambda i, j: (norm_tile(i) // tiles_per_seq, 0, 0)),
            pl.BlockSpec((1, d), lambda i, j: (0, 0)),
            pl.BlockSpec((d, TN_PROJ), w_tile),
            slab(wa), slab(wp), slab(wo),
        ],
        out_specs=(
            pl.BlockSpec((TM_PROJ, TN_PROJ),
                         lambda i, j: (out_tile(i), jnp.where(i == 0, 0, j))),
            slab(wa), slab(wp), slab(wo),
        ),
        scratch_shapes=[pltpu.VMEM((TM_PROJ, d), BF16),
                        pltpu.VMEM((TM_PROJ, d), BF16)],
        compiler_params=pltpu.CompilerParams(
            dimension_semantics=("arbitrary", "arbitrary"),
            vmem_limit_bytes=VMEM_LIMIT),
        name="in_proj",
    )(x2d, ada3, norm_g, w_in, wa, wp, wo)


def _attn_kernel(sinks_ref, q_ref, kc_ref, vc_ref, kp_ref, vp_ref, o_ref,
                 kfull, vfull, *, steps_per_seq):
    i = pl.program_id(0)
    kfull[0:BLOCK, :] = kp_ref[...]
    kfull[BLOCK:BLOCK + TM_MIX, :] = kc_ref[...]
    vfull[0:BLOCK, :] = vp_ref[...]
    vfull[BLOCK:BLOCK + TM_MIX, :] = vc_ref[...]
    first_step = (i % steps_per_seq) == 0
    nq = Q_PER_KV * BLOCK

    def sb_body(sb, carry):
        r0 = pl.multiple_of(sb * BLOCK, BLOCK)
        row = lax.broadcasted_iota(jnp.int32, (nq, 2 * BLOCK), 0) & (BLOCK - 1)
        col = lax.broadcasted_iota(jnp.int32, (nq, 2 * BLOCK), 1)
        dist = col - row
        lo = jnp.where(jnp.logical_and(first_step, sb == 0), BLOCK, 0)
        valid = (dist > 0) & (dist <= BLOCK) & (col >= lo)
        rg = lax.broadcasted_iota(jnp.int32, (nq, 1), 0) // BLOCK
        for h in range(KV_HEADS):
            kb = kfull[pl.ds(r0, 2 * BLOCK), h * HEAD_DIM:(h + 1) * HEAD_DIM]
            vb = vfull[pl.ds(r0, 2 * BLOCK), h * HEAD_DIM:(h + 1) * HEAD_DIM]
            qg = jnp.concatenate(
                [q_ref[pl.ds(r0, BLOCK),
                       (Q_PER_KV * h + g) * HEAD_DIM:(Q_PER_KV * h + g + 1) * HEAD_DIM]
                 for g in range(Q_PER_KV)], axis=0)
            qg = qg * jnp.asarray(HEAD_DIM ** -0.5, BF16)
            logits = lax.dot_general(qg, kb, (((1,), (1,)), ((), ())),
                                     preferred_element_type=F32)
            logits = jnp.where(valid, logits, MASK_VALUE)
            sink = jnp.full((nq, 1), sinks_ref[Q_PER_KV * h + Q_PER_KV - 1], F32)
            for g in range(Q_PER_KV - 2, -1, -1):
                sink = jnp.where(rg == g, sinks_ref[Q_PER_KV * h + g], sink)
            m = jnp.maximum(jnp.max(logits, axis=1, keepdims=True), sink)
            p = jnp.exp(logits - m)
            denom = jnp.sum(p, axis=1, keepdims=True) + jnp.exp(sink - m)
            o = jnp.dot(p.astype(BF16), vb, preferred_element_type=F32)
            o = (o * (1.0 / denom)).astype(BF16)
            o_ref[pl.ds(r0, BLOCK),
                  h * Q_PER_KV * HEAD_DIM:(h + 1) * Q_PER_KV * HEAD_DIM] = (
                jnp.concatenate([o[g * BLOCK:(g + 1) * BLOCK, :]
                                 for g in range(Q_PER_KV)], axis=1))
        return carry

    lax.fori_loop(0, TM_MIX // BLOCK, sb_body, 0)


def _attn(qkvu, sinks, seq):
    t = qkvu.shape[0]
    steps_per_seq = seq // TM_MIX
    sub = TM_MIX // BLOCK
    qcol = Q_COL // ATTN_WIDTH
    kcol = K_COL // KV_WIDTH
    prev = lambda i: jnp.maximum(i * sub - 1, 0)
    return pl.pallas_call(
        functools.partial(_attn_kernel, steps_per_seq=steps_per_seq),
        out_shape=jax.ShapeDtypeStruct((t, ATTN_WIDTH), BF16),
        grid=(t // TM_MIX,),
        in_specs=[
            pl.BlockSpec(memory_space=pltpu.SMEM),
            pl.BlockSpec((TM_MIX, ATTN_WIDTH), lambda i: (i, qcol)),
            pl.BlockSpec((TM_MIX, KV_WIDTH), lambda i: (i, kcol)),
            pl.BlockSpec((TM_MIX, KV_WIDTH), lambda i: (i, kcol + 1)),
            pl.BlockSpec((BLOCK, KV_WIDTH), lambda i: (prev(i), kcol)),
            pl.BlockSpec((BLOCK, KV_WIDTH), lambda i: (prev(i), kcol + 1)),
        ],
        out_specs=pl.BlockSpec((TM_MIX, ATTN_WIDTH), lambda i: (i, 0)),
        scratch_shapes=[pltpu.VMEM((BLOCK + TM_MIX, KV_WIDTH), BF16),
                        pltpu.VMEM((BLOCK + TM_MIX, KV_WIDTH), BF16)],
        compiler_params=pltpu.CompilerParams(
            dimension_semantics=("arbitrary",), vmem_limit_bytes=VMEM_LIMIT),
        name="attn",
    )(sinks, qkvu, qkvu, qkvu, qkvu, qkvu)


def _pool_kernel(ulo_ref, uhi_ref, plo_ref, phi_ref, wp_ref, ps_ref, o_ref,
                 ufull, *, steps_per_seq):
    i = pl.program_id(0)
    first_step = (i % steps_per_seq) == 0
    half = POOL_WIDTH // 2
    zeros = jnp.zeros((BLOCK, half), BF16)
    ufull[0:BLOCK, 0:half] = jnp.where(first_step, zeros, plo_ref[...])
    ufull[0:BLOCK, half:POOL_WIDTH] = jnp.where(first_step, zeros, phi_ref[...])
    ufull[BLOCK:BLOCK + TM_MIX, 0:half] = ulo_ref[...]
    ufull[BLOCK:BLOCK + TM_MIX, half:POOL_WIDTH] = uhi_ref[...]
    pos0 = (i % steps_per_seq) * TM_MIX
    row = lax.broadcasted_iota(jnp.int32, (BLOCK, 2 * BLOCK), 0)
    col = lax.broadcasted_iota(jnp.int32, (BLOCK, 2 * BLOCK), 1)
    rel = row + BLOCK - col
    bands = [((rel >= 0) & (rel < w)).astype(F32).astype(BF16)
             for w in POOL_WINDOWS]

    def sb_body(sb, carry):
        r0 = pl.multiple_of(sb * BLOCK, BLOCK)
        pos = pos0 + sb * BLOCK + lax.broadcasted_iota(jnp.int32, (BLOCK, 1), 0)
        for g, w in enumerate(POOL_WINDOWS):
            cols = slice(g * POOL_GROUP_DIM, (g + 1) * POOL_GROUP_DIM)
            band = ufull[pl.ds(r0, 2 * BLOCK), cols]
            ssum = jnp.dot(bands[g], band, preferred_element_type=F32)
            xg = ufull[pl.ds(r0 + BLOCK, BLOCK), cols].astype(F32)
            count = jnp.minimum(pos + 1, w).astype(F32)
            pooled = ssum / count - xg
            mixed = jnp.dot(pooled.astype(BF16), wp_ref[g],
                            preferred_element_type=F32)
            o_ref[pl.ds(r0, BLOCK), cols] = (mixed * ps_ref[:, cols]).astype(BF16)
        return carry

    lax.fori_loop(0, TM_MIX // BLOCK, sb_body, 0)


def _pool(qkvu, w_pool_b, pool_scale, seq):
    t = qkvu.shape[0]
    steps_per_seq = seq // TM_MIX
    sub = TM_MIX // BLOCK
    half = POOL_WIDTH // 2
    ucol = U_COL // half
    prev = lambda i: jnp.maximum(i * sub - 1, 0)
    return pl.pallas_call(
        functools.partial(_pool_kernel, steps_per_seq=steps_per_seq),
        out_shape=jax.ShapeDtypeStruct((t, POOL_WIDTH), BF16),
        grid=(t // TM_MIX,),
        in_specs=[
            pl.BlockSpec((TM_MIX, half), lambda i: (i, ucol)),
            pl.BlockSpec((TM_MIX, half), lambda i: (i, ucol + 1)),
            pl.BlockSpec((BLOCK, half), lambda i: (prev(i), ucol)),
            pl.BlockSpec((BLOCK, half), lambda i: (prev(i), ucol + 1)),
            pl.BlockSpec((len(POOL_WINDOWS), POOL_GROUP_DIM, POOL_GROUP_DIM),
                         lambda i: (0, 0, 0)),
            pl.BlockSpec((1, POOL_WIDTH), lambda i: (0, 0)),
        ],
        out_specs=pl.BlockSpec((TM_MIX, POOL_WIDTH), lambda i: (i, 0)),
        scratch_shapes=[pltpu.VMEM((BLOCK + TM_MIX, POOL_WIDTH), BF16)],
        compiler_params=pltpu.CompilerParams(
            dimension_semantics=("arbitrary",), vmem_limit_bytes=VMEM_LIMIT),
        name="pool",
    )(qkvu, qkvu, qkvu, qkvu, w_pool_b, pool_scale)


def _first_argmax4(v):
    m = jnp.maximum(jnp.maximum(v[0], v[1]), jnp.maximum(v[2], v[3]))
    idx = jnp.where(v[0] == m, 0, jnp.where(v[1] == m, 1,
                                            jnp.where(v[2] == m, 2, 3)))
    return m, idx


def _merge_kernel(attn_ref, pool_ref, ga_ref, gp_ref, x_ref, ada_ref,
                  wa_ref, wp_ref, wo_ref, g2_ref, wr_ref, br_ref,
                  eg_ref, eu_ref, ed_ref,
                  x2_ref, h2_ref, route_ref, counts_ref, eg_o, eu_o, ed_o):
    i = pl.program_id(0)
    eg_o[...] = eg_ref[...].astype(BF16)
    eu_o[...] = eu_ref[...].astype(BF16)
    ed_o[...] = ed_ref[...].astype(BF16)
    a = jnp.dot(attn_ref[...], wa_ref[...], preferred_element_type=F32)
    p = jnp.dot(pool_ref[...], wp_ref[...], preferred_element_type=F32)
    merged = ga_ref[...].astype(F32) * a + gp_ref[...].astype(F32) * p
    out = jnp.dot(merged.astype(BF16), wo_ref[...], preferred_element_type=F32)
    x2 = x_ref[...] + ada_ref[0, 2:3, :] * out
    x2_ref[...] = x2
    h2 = _rms_modulate(x2, g2_ref[...], ada_ref[0, 4:5, :], ada_ref[0, 3:4, :])
    h2_ref[...] = h2

    logits = lax.dot_general(wr_ref[...], h2.astype(BF16),
                             (((1,), (1,)), ((), ())),
                             preferred_element_type=F32) + br_ref[:, 0:1]
    tm = logits.shape[1]
    rows = [logits[r:r + 1, :] for r in range(N_GROUPS + N_EXPERTS)]
    _, gi = _first_argmax4(rows[0:N_GROUPS])
    sel = []
    for e in range(EXPERTS_PER_GROUP):
        v = rows[N_GROUPS + 3 * EXPERTS_PER_GROUP + e]
        for g in range(N_GROUPS - 2, -1, -1):
            v = jnp.where(gi == g, rows[N_GROUPS + g * EXPERTS_PER_GROUP + e], v)
        sel.append(v)
    _, i1 = _first_argmax4(sel)
    rest = [jnp.where(i1 == e, -jnp.inf, sel[e]) for e in range(EXPERTS_PER_GROUP)]
    _, i2 = _first_argmax4(rest)
    code = jnp.minimum(i1, i2) * EXPERTS_PER_GROUP + jnp.maximum(i1, i2)
    cig = jnp.where(code == 1, 0, jnp.where(code == 2, 1, jnp.where(
        code == 3, 2, jnp.where(code == 7, 3, jnp.where(code == 6, 4, 5)))))
    cls = gi * PAIRS_PER_GROUP + cig

    onehot = (lax.broadcasted_iota(jnp.int32, (ROUTER_ROWS, tm), 0) == cls
              ).astype(F32)
    src = lax.broadcasted_iota(jnp.int32, (tm, tm), 0)
    dst = lax.broadcasted_iota(jnp.int32, (tm, tm), 1)
    before = (src < dst).astype(F32).astype(BF16)
    prefix = jnp.dot(onehot.astype(BF16), before, preferred_element_type=F32)

    @pl.when(i == 0)
    def _():
        counts_ref[...] = jnp.zeros_like(counts_ref)

    carry = counts_ref[:, 0:1]
    rank = jnp.sum(onehot * (prefix + carry), axis=0, keepdims=True)
    counts_ref[...] = counts_ref[...] + jnp.sum(onehot, axis=1, keepdims=True)
    route_ref[0:1, :] = cls
    route_ref[1:2, :] = rank.astype(jnp.int32)
    route_ref[2:8, :] = jnp.zeros((6, tm), jnp.int32)


def _merge(attn, pool, proj, x2d, ada3, wa_b, wp_b, wo_b, norm2_g, wr_t, br_col,
           w_e_gate, w_e_up, w_e_down, seq):
    t, d = x2d.shape
    n_step = t // TM_MERGE
    tiles_per_seq = seq // TM_MERGE
    flat = lambda w: w.reshape(-1, w.shape[-1])
    slab = lambda w: pl.BlockSpec((w.shape[0] // n_step, w.shape[1]),
                                  lambda i: (i, 0))
    cast = lambda w: jax.ShapeDtypeStruct(w.shape, BF16)
    eg, eu, ed = flat(w_e_gate), flat(w_e_up), flat(w_e_down)
    const = lambda shape: pl.BlockSpec(shape, lambda i: (0,) * len(shape),
                                       pipeline_mode=pl.Buffered(1))
    return pl.pallas_call(
        _merge_kernel,
        out_shape=(jax.ShapeDtypeStruct((t, d), F32),
                   jax.ShapeDtypeStruct((t, d), F32),
                   jax.ShapeDtypeStruct((8, t), jnp.int32),
                   jax.ShapeDtypeStruct((ROUTER_ROWS, 128), F32),
                   cast(eg), cast(eu), cast(ed)),
        grid=(n_step,),
        in_specs=[
            pl.BlockSpec((TM_MERGE, ATTN_WIDTH), lambda i: (i, 0)),
            pl.BlockSpec((TM_MERGE, POOL_WIDTH), lambda i: (i, 0)),
            pl.BlockSpec((TM_MERGE, d), lambda i: (i, 0)),
            pl.BlockSpec((TM_MERGE, d), lambda i: (i, 1)),
            pl.BlockSpec((TM_MERGE, d), lambda i: (i, 0)),
            pl.BlockSpec((1, N_ADA, d), lambda i: (i // tiles_per_seq, 0, 0)),
            const((ATTN_WIDTH, d)),
            const((POOL_WIDTH, d)),
            const((d, d)),
            const((1, d)),
            const((ROUTER_ROWS, d)),
            const((ROUTER_ROWS, 128)),
            slab(eg), slab(eu), slab(ed),
        ],
        out_specs=(
            pl.BlockSpec((TM_MERGE, d), lambda i: (i, 0)),
            pl.BlockSpec((TM_MERGE, d), lambda i: (i, 0)),
            pl.BlockSpec((8, TM_MERGE), lambda i: (0, i)),
            pl.BlockSpec((ROUTER_ROWS, 128), lambda i: (0, 0)),
            slab(eg), slab(eu), slab(ed),
        ),
        compiler_params=pltpu.CompilerParams(
            dimension_semantics=("arbitrary",), vmem_limit_bytes=VMEM_LIMIT),
        name="merge",
    )(attn, pool, proj, proj, x2d, ada3, wa_b, wp_b, wo_b, norm2_g, wr_t, br_col,
      eg, eu, ed)


def _row_copy(src_ref, src_row, dst_ref, dst_row, sem):
    return pltpu.make_async_copy(src_ref.at[pl.ds(src_row, 1)],
                                 dst_ref.at[pl.ds(dst_row, 1)], sem)


def _dispatch_kernel(pos_ref, h_ref, hs_ref, sem):
    i = pl.program_id(0)
    base = i * TM_DISPATCH

    for r in range(TM_DISPATCH):
        _row_copy(h_ref, r, hs_ref, pos_ref[base + r], sem).start()
    pltpu.make_async_copy(h_ref, hs_ref.at[pl.ds(0, TM_DISPATCH)], sem).wait()


def _dispatch(pos, h2, n_rows):
    t, d = h2.shape
    return pl.pallas_call(
        _dispatch_kernel,
        out_shape=jax.ShapeDtypeStruct((n_rows, d), F32),
        grid_spec=pltpu.PrefetchScalarGridSpec(
            num_scalar_prefetch=1,
            grid=(t // TM_DISPATCH,),
            in_specs=[pl.BlockSpec((TM_DISPATCH, d), lambda i, pos: (i, 0))],
            out_specs=pl.BlockSpec(memory_space=pl.ANY),
            scratch_shapes=[pltpu.SemaphoreType.DMA(())],
        ),
        compiler_params=pltpu.CompilerParams(
            dimension_semantics=("arbitrary",), vmem_limit_bytes=VMEM_LIMIT,
            has_side_effects=True),
        name="dispatch",
    )(pos, h2)


def _expert_kernel(blk_ref, ea_ref, eb_ref, lane_ref, nvalid_ref, nused_ref,
                   hs_ref, wga_ref, wua_ref, wda_ref, wgb_ref, wub_ref, wdb_ref,
                   wr_ref, br_ref, ys_ref):
    i = pl.program_id(0)

    @pl.when(i >= nused_ref[0])
    def _():
        ys_ref[...] = jnp.zeros_like(ys_ref)

    @pl.when(i < nused_ref[0])
    def _():
        tm = hs_ref.shape[0]
        rows = lax.broadcasted_iota(jnp.int32, (tm, 1), 0)
        h = jnp.where(rows < nvalid_ref[i], hs_ref[...], 0.0).astype(BF16)
        logits = jnp.dot(h, wr_ref[...], preferred_element_type=F32) + br_ref[...]
        lane = lax.broadcasted_iota(jnp.int32, logits.shape, 1)
        is_group = lane < N_GROUPS
        gmax = jnp.max(jnp.where(is_group, logits, -jnp.inf), axis=1, keepdims=True)
        gexp = jnp.where(is_group, jnp.exp(logits - gmax), 0.0)
        pick = lambda v, l: jnp.sum(jnp.where(lane == l, v, 0.0), axis=1,
                                    keepdims=True)
        p_group = pick(gexp, lane_ref[0, i]) / jnp.sum(gexp, axis=1, keepdims=True)
        la = pick(logits, lane_ref[1, i])
        lb = pick(logits, lane_ref[2, i])
        mx = jnp.maximum(la, lb)
        ea = jnp.exp(la - mx)
        eb = jnp.exp(lb - mx)
        inv = p_group / (ea + eb)

        def expert(wg_ref, wu_ref, wd_ref, weight):
            gate = jnp.dot(h, wg_ref[0], preferred_element_type=F32)
            up = jnp.dot(h, wu_ref[0], preferred_element_type=F32)
            act = (jax.nn.silu(gate) * up * weight).astype(BF16)
            return jnp.dot(act, wd_ref[0], preferred_element_type=F32)

        ys_ref[...] = (expert(wga_ref, wua_ref, wda_ref, ea * inv)
                       + expert(wgb_ref, wub_ref, wdb_ref, eb * inv))


def _experts(tile_blk, tile_ea, tile_eb, tile_lanes, tile_nvalid, n_used,
             hs, wg_b, wu_b, wd_b, wr_lanes, br_lanes):
    p_rows, d = hs.shape
    n_tiles = p_rows // TM_EXPERT
    row_map = lambda i, blk, ea, eb, ln, nv, nu: (blk[i], 0)
    wa_map = lambda i, blk, ea, eb, ln, nv, nu: (ea[i], 0, 0)
    wb_map = lambda i, blk, ea, eb, ln, nv, nu: (eb[i], 0, 0)
    const = lambda i, blk, ea, eb, ln, nv, nu: (0, 0)
    return pl.pallas_call(
        _expert_kernel,
        out_shape=jax.ShapeDtypeStruct((p_rows, d), F32),
        grid_spec=pltpu.PrefetchScalarGridSpec(
            num_scalar_prefetch=6,
            grid=(n_tiles,),
            in_specs=[
                pl.BlockSpec((TM_EXPERT, d), row_map),
                pl.BlockSpec((1, d, EXPERT_FF), wa_map),
                pl.BlockSpec((1, d, EXPERT_FF), wa_map),
                pl.BlockSpec((1, EXPERT_FF, d), wa_map),
                pl.BlockSpec((1, d, EXPERT_FF), wb_map),
                pl.BlockSpec((1, d, EXPERT_FF), wb_map),
                pl.BlockSpec((1, EXPERT_FF, d), wb_map),
                pl.BlockSpec((d, ROUTER_LANES), const),
                pl.BlockSpec((1, ROUTER_LANES), const),
            ],
            out_specs=pl.BlockSpec((TM_EXPERT, d), row_map),
        ),
        compiler_params=pltpu.CompilerParams(
            dimension_semantics=("arbitrary",), vmem_limit_bytes=VMEM_LIMIT),
        name="experts",
    )(tile_blk, tile_ea, tile_eb, tile_lanes, tile_nvalid, n_used,
      hs, wg_b, wu_b, wd_b, wg_b, wu_b, wd_b, wr_lanes, br_lanes)


def _combine_kernel(pos_ref, ys_ref, x2_ref, ada_ref, g_ref, o_ref, ybuf, sems):
    i = pl.program_id(0)
    n = pl.num_programs(0)

    def gather(tile, slot):
        for r in range(TM_COMBINE):
            _row_copy(ys_ref, pos_ref[tile * TM_COMBINE + r],
                      ybuf.at[slot], r, sems.at[slot]).start()

    @pl.when(i == 0)
    def _():
        gather(0, 0)

    @pl.when(i + 1 < n)
    def _():
        gather(i + 1, (i + 1) % 2)

    slot = i % 2
    pltpu.make_async_copy(ys_ref.at[pl.ds(0, TM_COMBINE)], ybuf.at[slot],
                          sems.at[slot]).wait()

    x = x2_ref[...] + ada_ref[0, 5:6, :] * ybuf[slot]
    ms = jnp.mean(x * x, axis=-1, keepdims=True)
    o_ref[...] = x * lax.rsqrt(ms + NORM_EPS) * g_ref[...]


def _combine(pos, ys, x2, ada3, final_g, seq):
    t, d = x2.shape
    tiles_per_seq = seq // TM_COMBINE
    return pl.pallas_call(
        _combine_kernel,
        out_shape=jax.ShapeDtypeStruct((t, d), F32),
        grid_spec=pltpu.PrefetchScalarGridSpec(
            num_scalar_prefetch=1,
            grid=(t // TM_COMBINE,),
            in_specs=[
                pl.BlockSpec(memory_space=pl.ANY),
                pl.BlockSpec((TM_COMBINE, d), lambda i, pos: (i, 0)),
                pl.BlockSpec((1, N_ADA, d),
                             lambda i, pos: (i // tiles_per_seq, 0, 0)),
                pl.BlockSpec((1, d), lambda i, pos: (0, 0)),
            ],
            out_specs=pl.BlockSpec((TM_COMBINE, d), lambda i, pos: (i, 0)),
            scratch_shapes=[pltpu.VMEM((2, TM_COMBINE, d), F32),
                            pltpu.SemaphoreType.DMA((2,))],
        ),
        compiler_params=pltpu.CompilerParams(
            dimension_semantics=("arbitrary",), vmem_limit_bytes=VMEM_LIMIT),
        name="combine",
    )(pos, ys, x2, ada3, final_g)


def _routing_tables(route, counts, n_tiles):
    cls = route[0]
    rank = route[1]
    cnt = counts[:N_CLASSES, 0].astype(jnp.int32)
    tiles_c = (cnt + TM_EXPERT - 1) // TM_EXPERT
    tile_end = jnp.cumsum(tiles_c)
    tile_start = tile_end - tiles_c
    n_used = tile_end[-1]
    pos = tile_start[cls] * TM_EXPERT + rank

    steps = jnp.arange(n_tiles, dtype=jnp.int32)
    k = jnp.minimum(steps, n_used - 1)
    c = jnp.sum((k[:, None] >= tile_end[None, :]).astype(jnp.int32), axis=1)
    grp = c // PAIRS_PER_GROUP
    cig = c % PAIRS_PER_GROUP
    slot_a = jnp.asarray(CLASS_SLOT_A, jnp.int32)[cig]
    slot_b = jnp.asarray(CLASS_SLOT_B, jnp.int32)[cig]
    ea = grp * EXPERTS_PER_GROUP + slot_a
    eb = grp * EXPERTS_PER_GROUP + slot_b
    lanes = jnp.stack([grp, N_GROUPS + ea, N_GROUPS + eb]).astype(jnp.int32)
    nvalid = jnp.clip(cnt[c] - (k - tile_start[c]) * TM_EXPERT, 0, TM_EXPERT)
    return (pos.astype(jnp.int32), steps, ea.astype(jnp.int32),
            eb.astype(jnp.int32), lanes, nvalid.astype(jnp.int32),
            n_used.reshape(1).astype(jnp.int32))


def kernel(x, c, w_ada, b_ada, norm1_g, w_in, sinks, w_pool, pool_scale,
           w_attn_branch, w_pool_branch, w_out, norm2_g, w_router_group,
           b_router_group, w_router_expert, b_router_expert, w_e_gate, w_e_up,
           w_e_down, final_g):
    b, s, d = x.shape
    t = b * s
    assert w_ada.shape[0] == 1, "single-layer block"
    assert d == D_MODEL and s % TM_PROJ == 0
    l = 0
    n_tiles = t // TM_EXPERT + N_CLASSES
    x2d = x.reshape(t, d)
    ada3 = _ada(c, w_ada[l], b_ada[l]).reshape(b, N_ADA, d)

    proj, wa_b, wp_b, wo_b = _in_proj(
        x2d, ada3, norm1_g[l].reshape(1, d), w_in[l], w_attn_branch[l],
        w_pool_branch[l], w_out[l], s)
    attn = _attn(proj, sinks[l], s)
    pool = _pool(proj, w_pool[l].astype(BF16), pool_scale[l].reshape(1, -1), s)

    w_r = jnp.concatenate([w_router_group[l], w_router_expert[l]], axis=1)
    b_r = jnp.concatenate([b_router_group[l], b_router_expert[l]])
    n_r = N_GROUPS + N_EXPERTS
    wr_t = jnp.pad(w_r.T, ((0, ROUTER_ROWS - n_r), (0, 0))).astype(BF16)
    br_col = jnp.broadcast_to(
        jnp.pad(b_r, (0, ROUTER_ROWS - n_r))[:, None], (ROUTER_ROWS, 128))
    wr_lanes = jnp.pad(w_r, ((0, 0), (0, ROUTER_LANES - n_r))).astype(BF16)
    br_lanes = jnp.pad(b_r, (0, ROUTER_LANES - n_r)).reshape(1, ROUTER_LANES)

    x2, h2, route, counts, eg_b, eu_b, ed_b = _merge(
        attn, pool, proj, x2d, ada3, wa_b, wp_b, wo_b, norm2_g[l].reshape(1, d),
        wr_t, br_col, w_e_gate[l], w_e_up[l], w_e_down[l], s)

    pos, tile_blk, tile_ea, tile_eb, tile_lanes, tile_nvalid, n_used = (
        _routing_tables(route, counts, n_tiles))
    hs = _dispatch(pos, h2, n_tiles * TM_EXPERT)
    ys = _experts(tile_blk, tile_ea, tile_eb, tile_lanes, tile_nvalid, n_used,
                  hs, eg_b.reshape(w_e_gate[l].shape), eu_b.reshape(w_e_up[l].shape),
                  ed_b.reshape(w_e_down[l].shape), wr_lanes, br_lanes)
    out = _combine(pos, ys, x2, ada3, final_g.reshape(1, d), s)
    return out.reshape(b, s, d)
```

```python
import functools

import jax
import jax.numpy as jnp
from jax import lax
from jax.experimental import pallas as pl
from jax.experimental.pallas import tpu as pltpu

F32 = jnp.float32
BF16 = jnp.bfloat16

D_MODEL = 2048
HEAD_DIM = 64
ATTN_HEADS = 16
KV_HEADS = 4
Q_PER_KV = ATTN_HEADS // KV_HEADS
ATTN_WIDTH = ATTN_HEADS * HEAD_DIM
KV_WIDTH = KV_HEADS * HEAD_DIM
BLOCK = 128
POOL_WINDOWS = (2, 4, 8, 16)
POOL_WIDTH = 1024
POOL_GROUP_DIM = 256
QKVU_WIDTH = ATTN_WIDTH + 2 * KV_WIDTH + POOL_WIDTH
GATE_WIDTH = 2 * D_MODEL
PROJ_WIDTH = GATE_WIDTH + QKVU_WIDTH
Q_COL = GATE_WIDTH
K_COL = Q_COL + ATTN_WIDTH
V_COL = K_COL + KV_WIDTH
U_COL = V_COL + KV_WIDTH
N_GROUPS = 4
EXPERTS_PER_GROUP = 4
N_EXPERTS = 16
EXPERT_FF = 512
N_ADA = 6
NORM_EPS = 1e-6
MASK_VALUE = -1e30

PAIRS_PER_GROUP = 6
N_CLASSES = N_GROUPS * PAIRS_PER_GROUP
CLASS_SLOT_A = (0, 0, 0, 1, 1, 3)
CLASS_SLOT_B = (1, 2, 3, 3, 2, 2)
ROUTER_ROWS = 32
ROUTER_LANES = 128

TM_PROJ = 2048
TN_PROJ = 512
NORM_ROWS = 256
NORM_CHUNKS = TM_PROJ // NORM_ROWS
TM_MIX = 512
TM_MERGE = 256
TM_DISPATCH = 512
TM_EXPERT = 256
TM_COMBINE = 512
TN_ADA = 1024

VMEM_LIMIT = 52 * 1024 * 1024


def _rms_modulate(x, g, scale, shift):
    ms = jnp.mean(x * x, axis=-1, keepdims=True)
    return (x * lax.rsqrt(ms + NORM_EPS) * g) * (1.0 + scale) + shift


def _ada_kernel(cb_ref, w_ref, b_ref, o_ref):
    nb = cb_ref.shape[0]
    d = w_ref.shape[0]
    nchunk = w_ref.shape[1] // 128

    def body(kb, accs):
        k0 = pl.multiple_of(kb * 8, 8)
        new = list(accs)
        cbs = [cb_ref[b, pl.ds(k0, 8), :] for b in range(nb)]
        for j in range(nchunk):
            w = w_ref[pl.ds(k0, 8), j * 128:(j + 1) * 128]
            for b in range(nb):
                new[b * nchunk + j] = new[b * nchunk + j] + w * cbs[b]
        return tuple(new)

    init = tuple(jnp.zeros((8, 128), F32) for _ in range(nb * nchunk))
    accs = lax.fori_loop(0, d // 8, body, init, unroll=2)
    for b in range(nb):
        for j in range(nchunk):
            o_ref[b:b + 1, j * 128:(j + 1) * 128] = (
                jnp.sum(accs[b * nchunk + j], axis=0, keepdims=True)
                + b_ref[:, j * 128:(j + 1) * 128])


def _ada(c, w_ada, b_ada):
    nb, d = c.shape
    n = w_ada.shape[1]
    cb = jnp.broadcast_to(c[:, :, None], (nb, d, 128))
    return pl.pallas_call(
        _ada_kernel,
        out_shape=jax.ShapeDtypeStruct((nb, n), F32),
        grid=(n // TN_ADA,),
        in_specs=[
            pl.BlockSpec((nb, d, 128), lambda j: (0, 0, 0)),
            pl.BlockSpec((d, TN_ADA), lambda j: (0, j)),
            pl.BlockSpec((1, TN_ADA), lambda j: (0, j)),
        ],
        out_specs=pl.BlockSpec((nb, TN_ADA), lambda j: (0, j)),
        compiler_params=pltpu.CompilerParams(
            dimension_semantics=("arbitrary",), vmem_limit_bytes=VMEM_LIMIT),
        name="ada",
    )(cb, w_ada, b_ada.reshape(1, n))


def _in_proj_kernel(x_ref, ada_ref, g_ref, w_ref, wa_ref, wp_ref, wo_ref,
                    proj_ref, wa_o, wp_o, wo_o, h_even, h_odd, *, n_row):
    i = pl.program_id(0)
    j = pl.program_id(1)

    def norm_chunk(h_ref):
        r0 = pl.multiple_of(jnp.minimum(j, NORM_CHUNKS - 1) * NORM_ROWS, NORM_ROWS)
        h = _rms_modulate(x_ref[...], g_ref[...],
                          ada_ref[0, 1:2, :], ada_ref[0, 0:1, :])
        h_ref[pl.ds(r0, NORM_ROWS), :] = h.astype(BF16)

    def step(h_write, h_read):
        norm_chunk(h_write)
        acc = jnp.dot(h_read[...], w_ref[...].astype(BF16),
                      preferred_element_type=F32)
        is_gate = j < GATE_WIDTH // TN_PROJ
        proj_ref[...] = jnp.where(is_gate, jax.nn.sigmoid(acc), acc).astype(BF16)

    @pl.when(jnp.logical_and(j == 0, i < n_row))
    def _():
        wa_o[...] = wa_ref[...].astype(BF16)
        wp_o[...] = wp_ref[...].astype(BF16)
        wo_o[...] = wo_ref[...].astype(BF16)

    @pl.when(i == 0)
    def _():
        norm_chunk(h_even)

    @pl.when(i % 2 == 1)
    def _():
        step(h_odd, h_even)

    @pl.when(jnp.logical_and(i > 0, i % 2 == 0))
    def _():
        step(h_even, h_odd)


def _in_proj(x2d, ada3, norm_g, w_in, wa, wp, wo, seq):
    t, d = x2d.shape
    n_all = PROJ_WIDTH // TN_PROJ
    n_gate = GATE_WIDTH // TN_PROJ
    n_row = t // TM_PROJ
    tiles_per_seq = seq // TM_PROJ
    assert NORM_CHUNKS <= n_all
    norm_tile = lambda i: jnp.minimum(i, n_row - 1)
    out_tile = lambda i: jnp.maximum(i - 1, 0)
    w_tile = lambda i, j: (
        0, (jnp.where(i == 0, 0, j) + (n_all - n_gate)) % n_all)
    slab = lambda w: pl.BlockSpec((w.shape[0] // n_row, w.shape[1]),
                                  lambda i, j: (norm_tile(i), 0))
    cast = lambda w: jax.ShapeDtypeStruct(w.shape, BF16)
    return pl.pallas_call(
        functools.partial(_in_proj_kernel, n_row=n_row),
        out_shape=(jax.ShapeDtypeStruct((t, PROJ_WIDTH), BF16),
                   cast(wa), cast(wp), cast(wo)),
        grid=(n_row + 1, n_all),
        in_specs=[
            pl.BlockSpec((NORM_ROWS, d), lambda i, j: (
                norm_tile(i) * NORM_CHUNKS + jnp.minimum(j, NORM_CHUNKS - 1), 0)),
            pl.BlockSpec((1, N_ADA, d),
                         lambda i, j: (norm_tile(i) // tiles_per_seq, 0, 0)),
            pl.BlockSpec((1, d), lambda i, j: (0, 0)),
            pl.BlockSpec((d, TN_PROJ), w_tile),
            slab(wa), slab(wp), slab(wo),
        ],
        out_specs=(
            pl.BlockSpec((TM_PROJ, TN_PROJ),
                         lambda i, j: (out_tile(i), jnp.where(i == 0, 0, j))),
            slab(wa), slab(wp), slab(wo),
        ),
        scratch_shapes=[pltpu.VMEM((TM_PROJ, d), BF16),
                        pltpu.VMEM((TM_PROJ, d), BF16)],
        compiler_params=pltpu.CompilerParams(
            dimension_semantics=("arbitrary", "arbitrary"),
            vmem_limit_bytes=VMEM_LIMIT),
        name="in_proj",
    )(x2d, ada3, norm_g, w_in, wa, wp, wo)


def _attn_kernel(fill_ref, q_ref, kc_ref, vc_ref, kp_ref, vp_ref, o_ref,
                 kfull, vfull, *, steps_per_seq):
    i = pl.program_id(0)
    kfull[0:BLOCK, :] = kp_ref[...]
    kfull[BLOCK:BLOCK + TM_MIX, :] = kc_ref[...]
    vfull[0:BLOCK, :] = vp_ref[...]
    vfull[BLOCK:BLOCK + TM_MIX, :] = vc_ref[...]
    first_step = (i % steps_per_seq) == 0
    nq = Q_PER_KV * BLOCK

    def sb_body(sb, carry):
        r0 = pl.multiple_of(sb * BLOCK, BLOCK)
        row = lax.broadcasted_iota(jnp.int32, (nq, 2 * BLOCK), 0) & (BLOCK - 1)
        col = lax.broadcasted_iota(jnp.int32, (nq, 2 * BLOCK), 1)
        dist = col - row
        lo = jnp.where(jnp.logical_and(first_step, sb == 0), BLOCK, 0)
        valid = (dist > 0) & (dist <= BLOCK) & (col >= lo)
        key0 = lax.broadcasted_iota(jnp.int32, (2 * BLOCK, HEAD_DIM), 0) == 0
        for h in range(KV_HEADS):
            kb = kfull[pl.ds(r0, 2 * BLOCK), h * HEAD_DIM:(h + 1) * HEAD_DIM]
            vb = vfull[pl.ds(r0, 2 * BLOCK), h * HEAD_DIM:(h + 1) * HEAD_DIM]
            vb = jnp.where(key0, jnp.zeros_like(vb), vb)
            qg = jnp.concatenate(
                [q_ref[pl.ds(r0, BLOCK),
                       (Q_PER_KV * h + g) * HEAD_DIM:(Q_PER_KV * h + g + 1) * HEAD_DIM]
                 for g in range(Q_PER_KV)], axis=0)
            qg = qg * jnp.asarray(HEAD_DIM ** -0.5, BF16)
            logits = lax.dot_general(qg, kb, (((1,), (1,)), ((), ())),
                                     preferred_element_type=F32)
            logits = jnp.where(valid, logits, fill_ref[h])
            m = jnp.max(logits, axis=1, keepdims=True)
            p = jnp.exp(logits - m)
            denom = jnp.sum(p, axis=1, keepdims=True)
            o = jnp.dot(p.astype(BF16), vb, preferred_element_type=F32)
            o = (o * (1.0 / denom)).astype(BF16)
            o_ref[pl.ds(r0, BLOCK),
                  h * Q_PER_KV * HEAD_DIM:(h + 1) * Q_PER_KV * HEAD_DIM] = (
                jnp.concatenate([o[g * BLOCK:(g + 1) * BLOCK, :]
                                 for g in range(Q_PER_KV)], axis=1))
        return carry

    lax.fori_loop(0, TM_MIX // BLOCK, sb_body, 0)


def _attn(qkvu, sinks, seq):
    t = qkvu.shape[0]
    steps_per_seq = seq // TM_MIX
    sub = TM_MIX // BLOCK
    qcol = Q_COL // ATTN_WIDTH
    kcol = K_COL // KV_WIDTH
    prev = lambda i: jnp.maximum(i * sub - 1, 0)
    nq = Q_PER_KV * BLOCK
    sink_rows = jnp.repeat(sinks.astype(F32).reshape(KV_HEADS, Q_PER_KV), BLOCK,
                           axis=1)
    fill = jnp.full((KV_HEADS, nq, 2 * BLOCK), MASK_VALUE, F32)
    fill = fill.at[:, :, 0].set(sink_rows)
    return pl.pallas_call(
        functools.partial(_attn_kernel, steps_per_seq=steps_per_seq),
        out_shape=jax.ShapeDtypeStruct((t, ATTN_WIDTH), BF16),
        grid=(t // TM_MIX,),
        in_specs=[
            pl.BlockSpec((KV_HEADS, nq, 2 * BLOCK), lambda i: (0, 0, 0)),
            pl.BlockSpec((TM_MIX, ATTN_WIDTH), lambda i: (i, qcol)),
            pl.BlockSpec((TM_MIX, KV_WIDTH), lambda i: (i, kcol)),
            pl.BlockSpec((TM_MIX, KV_WIDTH), lambda i: (i, kcol + 1)),
            pl.BlockSpec((BLOCK, KV_WIDTH), lambda i: (prev(i), kcol)),
            pl.BlockSpec((BLOCK, KV_WIDTH), lambda i: (prev(i), kcol + 1)),
        ],
        out_specs=pl.BlockSpec((TM_MIX, ATTN_WIDTH), lambda i: (i, 0)),
        scratch_shapes=[pltpu.VMEM((BLOCK + TM_MIX, KV_WIDTH), BF16),
                        pltpu.VMEM((BLOCK + TM_MIX, KV_WIDTH), BF16)],
        compiler_params=pltpu.CompilerParams(
            dimension_semantics=("arbitrary",), vmem_limit_bytes=VMEM_LIMIT),
        name="attn",
    )(fill, qkvu, qkvu, qkvu, qkvu, qkvu)


def _pool_kernel(ulo_ref, uhi_ref, plo_ref, phi_ref, wp_ref, ps_ref, o_ref,
                 ufull, *, steps_per_seq):
    i = pl.program_id(0)
    first_step = (i % steps_per_seq) == 0
    half = POOL_WIDTH // 2
    zeros = jnp.zeros((BLOCK, half), BF16)
    ufull[0:BLOCK, 0:half] = jnp.where(first_step, zeros, plo_ref[...])
    ufull[0:BLOCK, half:POOL_WIDTH] = jnp.where(first_step, zeros, phi_ref[...])
    ufull[BLOCK:BLOCK + TM_MIX, 0:half] = ulo_ref[...]
    ufull[BLOCK:BLOCK + TM_MIX, half:POOL_WIDTH] = uhi_ref[...]
    pos0 = (i % steps_per_seq) * TM_MIX
    row = lax.broadcasted_iota(jnp.int32, (BLOCK, 2 * BLOCK), 0)
    col = lax.broadcasted_iota(jnp.int32, (BLOCK, 2 * BLOCK), 1)
    rel = row + BLOCK - col
    bands = [((rel >= 0) & (rel < w)).astype(F32).astype(BF16)
             for w in POOL_WINDOWS]

    def sb_body(sb, carry):
        r0 = pl.multiple_of(sb * BLOCK, BLOCK)
        pos = pos0 + sb * BLOCK + lax.broadcasted_iota(jnp.int32, (BLOCK, 1), 0)
        for g, w in enumerate(POOL_WINDOWS):
            cols = slice(g * POOL_GROUP_DIM, (g + 1) * POOL_GROUP_DIM)
            band = ufull[pl.ds(r0, 2 * BLOCK), cols]
            ssum = jnp.dot(bands[g], band, preferred_element_type=F32)
            xg = ufull[pl.ds(r0 + BLOCK, BLOCK), cols].astype(F32)
            count = jnp.minimum(pos + 1, w).astype(F32)
            pooled = ssum / count - xg
            mixed = jnp.dot(pooled.astype(BF16), wp_ref[g],
                            preferred_element_type=F32)
            o_ref[pl.ds(r0, BLOCK), cols] = (mixed * ps_ref[:, cols]).astype(BF16)
        return carry

    lax.fori_loop(0, TM_MIX // BLOCK, sb_body, 0)


def _pool(qkvu, w_pool_b, pool_scale, seq):
    t = qkvu.shape[0]
    steps_per_seq = seq // TM_MIX
    sub = TM_MIX // BLOCK
    half = POOL_WIDTH // 2
    ucol = U_COL // half
    prev = lambda i: jnp.maximum(i * sub - 1, 0)
    return pl.pallas_call(
        functools.partial(_pool_kernel, steps_per_seq=steps_per_seq),
        out_shape=jax.ShapeDtypeStruct((t, POOL_WIDTH), BF16),
        grid=(t // TM_MIX,),
        in_specs=[
            pl.BlockSpec((TM_MIX, half), lambda i: (i, ucol)),
            pl.BlockSpec((TM_MIX, half), lambda i: (i, ucol + 1)),
            pl.BlockSpec((BLOCK, half), lambda i: (prev(i), ucol)),
            pl.BlockSpec((BLOCK, half), lambda i: (prev(i), ucol + 1)),
            pl.BlockSpec((len(POOL_WINDOWS), POOL_GROUP_DIM, POOL_GROUP_DIM),
                         lambda i: (0, 0, 0)),
            pl.BlockSpec((1, POOL_WIDTH), lambda i: (0, 0)),
        ],
        out_specs=pl.BlockSpec((TM_MIX, POOL_WIDTH), lambda i: (i, 0)),
        scratch_shapes=[pltpu.VMEM((BLOCK + TM_MIX, POOL_WIDTH), BF16)],
        compiler_params=pltpu.CompilerParams(
            dimension_semantics=("arbitrary",), vmem_limit_bytes=VMEM_LIMIT),
        name="pool",
    )(qkvu, qkvu, qkvu, qkvu, w_pool_b, pool_scale)


def _first_argmax4(v):
    m = jnp.maximum(jnp.maximum(v[0], v[1]), jnp.maximum(v[2], v[3]))
    idx = jnp.where(v[0] == m, 0, jnp.where(v[1] == m, 1,
                                            jnp.where(v[2] == m, 2, 3)))
    return m, idx


def _merge_kernel(attn_ref, pool_ref, ga_ref, gp_ref, x_ref, ada_ref,
                  wa_ref, wp_ref, wo_ref, g2_ref, wr_ref, br_ref,
                  eg_ref, eu_ref, ed_ref,
                  x2_ref, h2_ref, route_ref, counts_ref, eg_o, eu_o, ed_o):
    i = pl.program_id(0)
    eg_o[...] = eg_ref[...].astype(BF16)
    eu_o[...] = eu_ref[...].astype(BF16)
    ed_o[...] = ed_ref[...].astype(BF16)
    a = jnp.dot(attn_ref[...], wa_ref[...], preferred_element_type=F32)
    p = jnp.dot(pool_ref[...], wp_ref[...], preferred_element_type=F32)
    merged = ga_ref[...].astype(F32) * a + gp_ref[...].astype(F32) * p
    out = jnp.dot(merged.astype(BF16), wo_ref[...], preferred_element_type=F32)
    x2 = x_ref[...] + ada_ref[0, 2:3, :] * out
    x2_ref[...] = x2
    h2 = _rms_modulate(x2, g2_ref[...], ada_ref[0, 4:5, :], ada_ref[0, 3:4, :])
    h2_ref[...] = h2

    logits = lax.dot_general(wr_ref[...], h2.astype(BF16),
                             (((1,), (1,)), ((), ())),
                             preferred_element_type=F32) + br_ref[:, 0:1]
    tm = logits.shape[1]
    rows = [logits[r:r + 1, :] for r in range(N_GROUPS + N_EXPERTS)]
    _, gi = _first_argmax4(rows[0:N_GROUPS])
    sel = []
    for e in range(EXPERTS_PER_GROUP):
        v = rows[N_GROUPS + 3 * EXPERTS_PER_GROUP + e]
        for g in range(N_GROUPS - 2, -1, -1):
            v = jnp.where(gi == g, rows[N_GROUPS + g * EXPERTS_PER_GROUP + e], v)
        sel.append(v)
    _, i1 = _first_argmax4(sel)
    rest = [jnp.where(i1 == e, -jnp.inf, sel[e]) for e in range(EXPERTS_PER_GROUP)]
    _, i2 = _first_argmax4(rest)
    code = jnp.minimum(i1, i2) * EXPERTS_PER_GROUP + jnp.maximum(i1, i2)
    cig = jnp.where(code == 1, 0, jnp.where(code == 2, 1, jnp.where(
        code == 3, 2, jnp.where(code == 7, 3, jnp.where(code == 6, 4, 5)))))
    cls = gi * PAIRS_PER_GROUP + cig

    onehot = (lax.broadcasted_iota(jnp.int32, (ROUTER_ROWS, tm), 0) == cls
              ).astype(F32)
    src = lax.broadcasted_iota(jnp.int32, (tm, tm), 0)
    dst = lax.broadcasted_iota(jnp.int32, (tm, tm), 1)
    before = (src < dst).astype(F32).astype(BF16)
    prefix = jnp.dot(onehot.astype(BF16), before, preferred_element_type=F32)

    @pl.when(i == 0)
    def _():
        counts_ref[...] = jnp.zeros_like(counts_ref)

    carry = counts_ref[:, 0:1]
    rank = jnp.sum(onehot * (prefix + carry), axis=0, keepdims=True)
    counts_ref[...] = counts_ref[...] + jnp.sum(onehot, axis=1, keepdims=True)
    route_ref[0:1, :] = cls
    route_ref[1:2, :] = rank.astype(jnp.int32)
    route_ref[2:8, :] = jnp.zeros((6, tm), jnp.int32)


def _merge(attn, pool, proj, x2d, ada3, wa_b, wp_b, wo_b, norm2_g, wr_t, br_col,
           w_e_gate, w_e_up, w_e_down, seq):
    t, d = x2d.shape
    n_step = t // TM_MERGE
    tiles_per_seq = seq // TM_MERGE
    flat = lambda w: w.reshape(-1, w.shape[-1])
    slab = lambda w: pl.BlockSpec((w.shape[0] // n_step, w.shape[1]),
                                  lambda i: (i, 0))
    cast = lambda w: jax.ShapeDtypeStruct(w.shape, BF16)
    eg, eu, ed = flat(w_e_gate), flat(w_e_up), flat(w_e_down)
    const = lambda shape: pl.BlockSpec(shape, lambda i: (0,) * len(shape),
                                       pipeline_mode=pl.Buffered(1))
    return pl.pallas_call(
        _merge_kernel,
        out_shape=(jax.ShapeDtypeStruct((t, d), F32),
                   jax.ShapeDtypeStruct((t, d), F32),
                   jax.ShapeDtypeStruct((8, t), jnp.int32),
                   jax.ShapeDtypeStruct((ROUTER_ROWS, 128), F32),
                   cast(eg), cast(eu), cast(ed)),
        grid=(n_step,),
        in_specs=[
            pl.BlockSpec((TM_MERGE, ATTN_WIDTH), lambda i: (i, 0)),
            pl.BlockSpec((TM_MERGE, POOL_WIDTH), lambda i: (i, 0)),
            pl.BlockSpec((TM_MERGE, d), lambda i: (i, 0)),
            pl.BlockSpec((TM_MERGE, d), lambda i: (i, 1)),
            pl.BlockSpec((TM_MERGE, d), lambda i: (i, 0)),
            pl.BlockSpec((1, N_ADA, d), lambda i: (i // tiles_per_seq, 0, 0)),
            const((ATTN_WIDTH, d)),
            const((POOL_WIDTH, d)),
            const((d, d)),
            const((1, d)),
            const((ROUTER_ROWS, d)),
            const((ROUTER_ROWS, 128)),
            slab(eg), slab(eu), slab(ed),
        ],
        out_specs=(
            pl.BlockSpec((TM_MERGE, d), lambda i: (i, 0)),
            pl.BlockSpec((TM_MERGE, d), lambda i: (i, 0)),
            pl.BlockSpec((8, TM_MERGE), lambda i: (0, i)),
            pl.BlockSpec((ROUTER_ROWS, 128), lambda i: (0, 0)),
            slab(eg), slab(eu), slab(ed),
        ),
        compiler_params=pltpu.CompilerParams(
            dimension_semantics=("arbitrary",), vmem_limit_bytes=VMEM_LIMIT),
        name="merge",
    )(attn, pool, proj, proj, x2d, ada3, wa_b, wp_b, wo_b, norm2_g, wr_t, br_col,
      eg, eu, ed)


def _row_copy(src_ref, src_row, dst_ref, dst_row, sem):
    return pltpu.make_async_copy(src_ref.at[pl.ds(src_row, 1)],
                                 dst_ref.at[pl.ds(dst_row, 1)], sem)


def _dispatch_kernel(pos_ref, h_ref, hs_ref, sem):
    i = pl.program_id(0)
    base = i * TM_DISPATCH

    for r in range(TM_DISPATCH):
        _row_copy(h_ref, r, hs_ref, pos_ref[base + r], sem).start()
    pltpu.make_async_copy(h_ref, hs_ref.at[pl.ds(0, TM_DISPATCH)], sem).wait()


def _dispatch(pos, h2, n_rows):
    t, d = h2.shape
    return pl.pallas_call(
        _dispatch_kernel,
        out_shape=jax.ShapeDtypeStruct((n_rows, d), F32),
        grid_spec=pltpu.PrefetchScalarGridSpec(
            num_scalar_prefetch=1,
            grid=(t // TM_DISPATCH,),
            in_specs=[pl.BlockSpec((TM_DISPATCH, d), lambda i, pos: (i, 0))],
            out_specs=pl.BlockSpec(memory_space=pl.ANY),
            scratch_shapes=[pltpu.SemaphoreType.DMA(())],
        ),
        compiler_params=pltpu.CompilerParams(
            dimension_semantics=("arbitrary",), vmem_limit_bytes=VMEM_LIMIT,
            has_side_effects=True),
        name="dispatch",
    )(pos, h2)


def _expert_kernel(blk_ref, ea_ref, eb_ref, lane_ref, nvalid_ref, nused_ref,
                   hs_ref, wga_ref, wua_ref, wda_ref, wgb_ref, wub_ref, wdb_ref,
                   wr_ref, br_ref, ys_ref):
    i = pl.program_id(0)

    @pl.when(i >= nused_ref[0])
    def _():
        ys_ref[...] = jnp.zeros_like(ys_ref)

    @pl.when(i < nused_ref[0])
    def _():
        tm = hs_ref.shape[0]
        rows = lax.broadcasted_iota(jnp.int32, (tm, 1), 0)
        h = jnp.where(rows < nvalid_ref[i], hs_ref[...], 0.0).astype(BF16)
        logits = jnp.dot(h, wr_ref[...], preferred_element_type=F32) + br_ref[...]
        lane = lax.broadcasted_iota(jnp.int32, logits.shape, 1)
        is_group = lane < N_GROUPS
        gmax = jnp.max(jnp.where(is_group, logits, -jnp.inf), axis=1, keepdims=True)
        gexp = jnp.where(is_group, jnp.exp(logits - gmax), 0.0)
        pick = lambda v, l: jnp.sum(jnp.where(lane == l, v, 0.0), axis=1,
                                    keepdims=True)
        p_group = pick(gexp, lane_ref[0, i]) / jnp.sum(gexp, axis=1, keepdims=True)
        la = pick(logits, lane_ref[1, i])
        lb = pick(logits, lane_ref[2, i])
        mx = jnp.maximum(la, lb)
        ea = jnp.exp(la - mx)
        eb = jnp.exp(lb - mx)
        inv = p_group / (ea + eb)

        def expert(wg_ref, wu_ref, wd_ref, weight):
            gate = jnp.dot(h, wg_ref[0], preferred_element_type=F32)
            up = jnp.dot(h, wu_ref[0], preferred_element_type=F32)
            act = (jax.nn.silu(gate) * up * weight).astype(BF16)
            return jnp.dot(act, wd_ref[0], preferred_element_type=F32)

        ys_ref[...] = (expert(wga_ref, wua_ref, wda_ref, ea * inv)
                       + expert(wgb_ref, wub_ref, wdb_ref, eb * inv))


def _experts(tile_blk, tile_ea, tile_eb, tile_lanes, tile_nvalid, n_used,
             hs, wg_b, wu_b, wd_b, wr_lanes, br_lanes):
    p_rows, d = hs.shape
    n_tiles = p_rows // TM_EXPERT
    row_map = lambda i, blk, ea, eb, ln, nv, nu: (blk[i], 0)
    wa_map = lambda i, blk, ea, eb, ln, nv, nu: (ea[i], 0, 0)
    wb_map = lambda i, blk, ea, eb, ln, nv, nu: (eb[i], 0, 0)
    const = lambda i, blk, ea, eb, ln, nv, nu: (0, 0)
    return pl.pallas_call(
        _expert_kernel,
        out_shape=jax.ShapeDtypeStruct((p_rows, d), F32),
        grid_spec=pltpu.PrefetchScalarGridSpec(
            num_scalar_prefetch=6,
            grid=(n_tiles,),
            in_specs=[
                pl.BlockSpec((TM_EXPERT, d), row_map),
                pl.BlockSpec((1, d, EXPERT_FF), wa_map),
                pl.BlockSpec((1, d, EXPERT_FF), wa_map),
                pl.BlockSpec((1, EXPERT_FF, d), wa_map),
                pl.BlockSpec((1, d, EXPERT_FF), wb_map),
                pl.BlockSpec((1, d, EXPERT_FF), wb_map),
                pl.BlockSpec((1, EXPERT_FF, d), wb_map),
                pl.BlockSpec((d, ROUTER_LANES), const),
                pl.BlockSpec((1, ROUTER_LANES), const),
            ],
            out_specs=pl.BlockSpec((TM_EXPERT, d), row_map),
        ),
        compiler_params=pltpu.CompilerParams(
            dimension_semantics=("arbitrary",), vmem_limit_bytes=VMEM_LIMIT),
        name="experts",
    )(tile_blk, tile_ea, tile_eb, tile_lanes, tile_nvalid, n_used,
      hs, wg_b, wu_b, wd_b, wg_b, wu_b, wd_b, wr_lanes, br_lanes)


def _combine_kernel(pos_ref, ys_ref, x2_ref, ada_ref, g_ref, o_ref, ybuf, sems):
    i = pl.program_id(0)
    n = pl.num_programs(0)

    def gather(tile, slot):
        for r in range(TM_COMBINE):
            _row_copy(ys_ref, pos_ref[tile * TM_COMBINE + r],
                      ybuf.at[slot], r, sems.at[slot]).start()

    @pl.when(i == 0)
    def _():
        gather(0, 0)

    @pl.when(i + 1 < n)
    def _():
        gather(i + 1, (i + 1) % 2)

    slot = i % 2
    pltpu.make_async_copy(ys_ref.at[pl.ds(0, TM_COMBINE)], ybuf.at[slot],
                          sems.at[slot]).wait()

    x = x2_ref[...] + ada_ref[0, 5:6, :] * ybuf[slot]
    ms = jnp.mean(x * x, axis=-1, keepdims=True)
    o_ref[...] = x * lax.rsqrt(ms + NORM_EPS) * g_ref[...]


def _combine(pos, ys, x2, ada3, final_g, seq):
    t, d = x2.shape
    tiles_per_seq = seq // TM_COMBINE
    return pl.pallas_call(
        _combine_kernel,
        out_shape=jax.ShapeDtypeStruct((t, d), F32),
        grid_spec=pltpu.PrefetchScalarGridSpec(
            num_scalar_prefetch=1,
            grid=(t // TM_COMBINE,),
            in_specs=[
                pl.BlockSpec(memory_space=pl.ANY),
                pl.BlockSpec((TM_COMBINE, d), lambda i, pos: (i, 0)),
                pl.BlockSpec((1, N_ADA, d),
                             lambda i, pos: (i // tiles_per_seq, 0, 0)),
                pl.BlockSpec((1, d), lambda i, pos: (0, 0)),
            ],
            out_specs=pl.BlockSpec((TM_COMBINE, d), lambda i, pos: (i, 0)),
            scratch_shapes=[pltpu.VMEM((2, TM_COMBINE, d), F32),
                            pltpu.SemaphoreType.DMA((2,))],
        ),
        compiler_params=pltpu.CompilerParams(
            dimension_semantics=("arbitrary",), vmem_limit_bytes=VMEM_LIMIT),
        name="combine",
    )(pos, ys, x2, ada3, final_g)


def _routing_tables(route, counts, n_tiles):
    cls = route[0]
    rank = route[1]
    cnt = counts[:N_CLASSES, 0].astype(jnp.int32)
    tiles_c = (cnt + TM_EXPERT - 1) // TM_EXPERT
    tile_end = jnp.cumsum(tiles_c)
    tile_start = tile_end - tiles_c
    n_used = tile_end[-1]
    pos = tile_start[cls] * TM_EXPERT + rank

    steps = jnp.arange(n_tiles, dtype=jnp.int32)
    k = jnp.minimum(steps, n_used - 1)
    c = jnp.sum((k[:, None] >= tile_end[None, :]).astype(jnp.int32), axis=1)
    grp = c // PAIRS_PER_GROUP
    cig = c % PAIRS_PER_GROUP
    slot_a = jnp.asarray(CLASS_SLOT_A, jnp.int32)[cig]
    slot_b = jnp.asarray(CLASS_SLOT_B, jnp.int32)[cig]
    ea = grp * EXPERTS_PER_GROUP + slot_a
    eb = grp * EXPERTS_PER_GROUP + slot_b
    lanes = jnp.stack([grp, N_GROUPS + ea, N_GROUPS + eb]).astype(jnp.int32)
    nvalid = jnp.clip(cnt[c] - (k - tile_start[c]) * TM_EXPERT, 0, TM_EXPERT)
    return (pos.astype(jnp.int32), steps, ea.astype(jnp.int32),
            eb.astype(jnp.int32), lanes, nvalid.astype(jnp.int32),
            n_used.reshape(1).astype(jnp.int32))


def kernel(x, c, w_ada, b_ada, norm1_g, w_in, sinks, w_pool, pool_scale,
           w_attn_branch, w_pool_branch, w_out, norm2_g, w_router_group,
           b_router_group, w_router_expert, b_router_expert, w_e_gate, w_e_up,
           w_e_down, final_g):
    b, s, d = x.shape
    t = b * s
    assert w_ada.shape[0] == 1, "single-layer block"
    assert d == D_MODEL and s % TM_PROJ == 0
    l = 0
    n_tiles = t // TM_EXPERT + N_CLASSES
    x2d = x.reshape(t, d)
    ada3 = _ada(c, w_ada[l], b_ada[l]).reshape(b, N_ADA, d)

    proj, wa_b, wp_b, wo_b = _in_proj(
        x2d, ada3, norm1_g[l].reshape(1, d), w_in[l], w_attn_branch[l],
        w_pool_branch[l], w_out[l], s)
    attn = _attn(proj, sinks[l], s)
    pool = _pool(proj, w_pool[l].astype(BF16), pool_scale[l].reshape(1, -1), s)

    w_r = jnp.concatenate([w_router_group[l], w_router_expert[l]], axis=1)
    b_r = jnp.concatenate([b_router_group[l], b_router_expert[l]])
    n_r = N_GROUPS + N_EXPERTS
    wr_t = jnp.pad(w_r.T, ((0, ROUTER_ROWS - n_r), (0, 0))).astype(BF16)
    br_col = jnp.broadcast_to(
        jnp.pad(b_r, (0, ROUTER_ROWS - n_r))[:, None], (ROUTER_ROWS, 128))
    wr_lanes = jnp.pad(w_r, ((0, 0), (0, ROUTER_LANES - n_r))).astype(BF16)
    br_lanes = jnp.pad(b_r, (0, ROUTER_LANES - n_r)).reshape(1, ROUTER_LANES)

    x2, h2, route, counts, eg_b, eu_b, ed_b = _merge(
        attn, pool, proj, x2d, ada3, wa_b, wp_b, wo_b, norm2_g[l].reshape(1, d),
        wr_t, br_col, w_e_gate[l], w_e_up[l], w_e_down[l], s)

    pos, tile_blk, tile_ea, tile_eb, tile_lanes, tile_nvalid, n_used = (
        _routing_tables(route, counts, n_tiles))
    hs = _dispatch(pos, h2, n_tiles * TM_EXPERT)
    ys = _experts(tile_blk, tile_ea, tile_eb, tile_lanes, tile_nvalid, n_used,
                  hs, eg_b.reshape(w_e_gate[l].shape), eu_b.reshape(w_e_up[l].shape),
                  ed_b.reshape(w_e_down[l].shape), wr_lanes, br_lanes)
    out = _combine(pos, ys, x2, ada3, final_g.reshape(1, d), s)
    return out.reshape(b, s, d)
```

```python
import functools

import jax
import jax.numpy as jnp
from jax import lax
from jax.experimental import pallas as pl
from jax.experimental.pallas import tpu as pltpu

F32 = jnp.float32
BF16 = jnp.bfloat16

D_MODEL = 2048
HEAD_DIM = 64
ATTN_HEADS = 16
KV_HEADS = 4
Q_PER_KV = ATTN_HEADS // KV_HEADS
ATTN_WIDTH = ATTN_HEADS * HEAD_DIM
KV_WIDTH = KV_HEADS * HEAD_DIM
BLOCK = 128
POOL_WINDOWS = (2, 4, 8, 16)
POOL_WIDTH = 1024
POOL_GROUP_DIM = 256
QKVU_WIDTH = ATTN_WIDTH + 2 * KV_WIDTH + POOL_WIDTH
GATE_WIDTH = 2 * D_MODEL
PROJ_WIDTH = GATE_WIDTH + QKVU_WIDTH
Q_COL = GATE_WIDTH
K_COL = Q_COL + ATTN_WIDTH
V_COL = K_COL + KV_WIDTH
U_COL = V_COL + KV_WIDTH
N_GROUPS = 4
EXPERTS_PER_GROUP = 4
N_EXPERTS = 16
EXPERT_FF = 512
N_ADA = 6
NORM_EPS = 1e-6
MASK_VALUE = -1e30

PAIRS_PER_GROUP = 6
N_CLASSES = N_GROUPS * PAIRS_PER_GROUP
CLASS_SLOT_A = (0, 0, 0, 1, 1, 3)
CLASS_SLOT_B = (1, 2, 3, 3, 2, 2)
ROUTER_ROWS = 32
ROUTER_LANES = 128

TM_PROJ = 2048
TN_PROJ = 512
NORM_ROWS = 256
NORM_CHUNKS = TM_PROJ // NORM_ROWS
TM_MIX = 512
TM_MERGE = 256
TM_DISPATCH = 512
TM_EXPERT = 256
TM_COMBINE = 512
TN_ADA = 1024

VMEM_LIMIT = 52 * 1024 * 1024


def _rms_modulate(x, g, scale, shift):
    ms = jnp.mean(x * x, axis=-1, keepdims=True)
    return (x * lax.rsqrt(ms + NORM_EPS) * g) * (1.0 + scale) + shift


def _ada_kernel(cb_ref, w_ref, b_ref, o_ref):
    nb = cb_ref.shape[0]
    d = w_ref.shape[0]
    nchunk = w_ref.shape[1] // 128

    def body(kb, accs):
        k0 = pl.multiple_of(kb * 8, 8)
        new = list(accs)
        cbs = [cb_ref[b, pl.ds(k0, 8), :] for b in range(nb)]
        for j in range(nchunk):
            w = w_ref[pl.ds(k0, 8), j * 128:(j + 1) * 128]
            for b in range(nb):
                new[b * nchunk + j] = new[b * nchunk + j] + w * cbs[b]
        return tuple(new)

    init = tuple(jnp.zeros((8, 128), F32) for _ in range(nb * nchunk))
    accs = lax.fori_loop(0, d // 8, body, init, unroll=2)
    for b in range(nb):
        for j in range(nchunk):
            o_ref[b:b + 1, j * 128:(j + 1) * 128] = (
                jnp.sum(accs[b * nchunk + j], axis=0, keepdims=True)
                + b_ref[:, j * 128:(j + 1) * 128])


def _ada(c, w_ada, b_ada):
    nb, d = c.shape
    n = w_ada.shape[1]
    cb = jnp.broadcast_to(c[:, :, None], (nb, d, 128))
    return pl.pallas_call(
        _ada_kernel,
        out_shape=jax.ShapeDtypeStruct((nb, n), F32),
        grid=(n // TN_ADA,),
        in_specs=[
            pl.BlockSpec((nb, d, 128), lambda j: (0, 0, 0)),
            pl.BlockSpec((d, TN_ADA), lambda j: (0, j)),
            pl.BlockSpec((1, TN_ADA), lambda j: (0, j)),
        ],
        out_specs=pl.BlockSpec((nb, TN_ADA), lambda j: (0, j)),
        compiler_params=pltpu.CompilerParams(
            dimension_semantics=("arbitrary",), vmem_limit_bytes=VMEM_LIMIT),
        name="ada",
    )(cb, w_ada, b_ada.reshape(1, n))


def _in_proj_kernel(x_ref, ada_ref, g_ref, w_ref, wa_ref, wp_ref, wo_ref,
                    proj_ref, wa_o, wp_o, wo_o, h_even, h_odd, *, n_row):
    i = pl.program_id(0)
    j = pl.program_id(1)

    def norm_chunk(h_ref):
        r0 = pl.multiple_of(jnp.minimum(j, NORM_CHUNKS - 1) * NORM_ROWS, NORM_ROWS)
        h = _rms_modulate(x_ref[...], g_ref[...],
                          ada_ref[0, 1:2, :], ada_ref[0, 0:1, :])
        h_ref[pl.ds(r0, NORM_ROWS), :] = h.astype(BF16)

    def step(h_write, h_read):
        norm_chunk(h_write)
        acc = jnp.dot(h_read[...], w_ref[...].astype(BF16),
                      preferred_element_type=F32)
        is_gate = j < GATE_WIDTH // TN_PROJ
        proj_ref[...] = jnp.where(is_gate, jax.nn.sigmoid(acc), acc).astype(BF16)

    @pl.when(jnp.logical_and(j == 0, i < n_row))
    def _():
        wa_o[...] = wa_ref[...].astype(BF16)
        wp_o[...] = wp_ref[...].astype(BF16)
        wo_o[...] = wo_ref[...].astype(BF16)

    @pl.when(i == 0)
    def _():
        norm_chunk(h_even)

    @pl.when(i % 2 == 1)
    def _():
        step(h_odd, h_even)

    @pl.when(jnp.logical_and(i > 0, i % 2 == 0))
    def _():
        step(h_even, h_odd)


def _in_proj(x2d, ada3, norm_g, w_in, wa, wp, wo, seq):
    t, d = x2d.shape
    n_all = PROJ_WIDTH // TN_PROJ
    n_gate = GATE_WIDTH // TN_PROJ
    n_row = t // TM_PROJ
    tiles_per_seq = seq // TM_PROJ
    assert NORM_CHUNKS <= n_all
    norm_tile = lambda i: jnp.minimum(i, n_row - 1)
    out_tile = lambda i: jnp.maximum(i - 1, 0)
    w_tile = lambda i, j: (
        0, (jnp.where(i == 0, 0, j) + (n_all - n_gate)) % n_all)
    slab = lambda w: pl.BlockSpec((w.shape[0] // n_row, w.shape[1]),
                                  lambda i, j: (norm_tile(i), 0))
    cast = lambda w: jax.ShapeDtypeStruct(w.shape, BF16)
    return pl.pallas_call(
        functools.partial(_in_proj_kernel, n_row=n_row),
        out_shape=(jax.ShapeDtypeStruct((t, PROJ_WIDTH), BF16),
                   cast(wa), cast(wp), cast(wo)),
        grid=(n_row + 1, n_all),
        in_specs=[
            pl.BlockSpec((NORM_ROWS, d), lambda i, j: (
                norm_tile(i) * NORM_CHUNKS + jnp.minimum(j, NORM_CHUNKS - 1), 0)),
            pl.BlockSpec((1, N_ADA, d),
                         lambda i, j: (norm_tile(i) // tiles_per_seq, 0, 0)),
            pl.BlockSpec((1, d), lambda i, j: (0, 0)),
            pl.BlockSpec((d, TN_PROJ), w_tile),
            slab(wa), slab(wp), slab(wo),
        ],
        out_specs=(
            pl.BlockSpec((TM_PROJ, TN_PROJ),
                         lambda i, j: (out_tile(i), jnp.where(i == 0, 0, j))),
            slab(wa), slab(wp), slab(wo),
        ),
        scratch_shapes=[pltpu.VMEM((TM_PROJ, d), BF16),
                        pltpu.VMEM((TM_PROJ, d), BF16)],
        compiler_params=pltpu.CompilerParams(
            dimension_semantics=("arbitrary", "arbitrary"),
            vmem_limit_bytes=VMEM_LIMIT),
        name="in_proj",
    )(x2d, ada3, norm_g, w_in, wa, wp, wo)


def _mix_kernel(fill_ref, q_ref, kc_ref, vc_ref, kp_ref, vp_ref,
                ulo_ref, uhi_ref, plo_ref, phi_ref, wp_ref, ps_ref,
                attn_ref, pool_ref, kfull, vfull, ufull, *, steps_per_seq):
    i = pl.program_id(0)
    first_step = (i % steps_per_seq) == 0
    kfull[0:BLOCK, :] = kp_ref[...]
    kfull[BLOCK:BLOCK + TM_MIX, :] = kc_ref[...]
    vfull[0:BLOCK, :] = vp_ref[...]
    vfull[BLOCK:BLOCK + TM_MIX, :] = vc_ref[...]
    half = POOL_WIDTH // 2
    zeros = jnp.zeros((BLOCK, half), BF16)
    ufull[0:BLOCK, 0:half] = jnp.where(first_step, zeros, plo_ref[...])
    ufull[0:BLOCK, half:POOL_WIDTH] = jnp.where(first_step, zeros, phi_ref[...])
    ufull[BLOCK:BLOCK + TM_MIX, 0:half] = ulo_ref[...]
    ufull[BLOCK:BLOCK + TM_MIX, half:POOL_WIDTH] = uhi_ref[...]
    pos0 = (i % steps_per_seq) * TM_MIX
    nq = Q_PER_KV * BLOCK
    prow = lax.broadcasted_iota(jnp.int32, (BLOCK, 2 * BLOCK), 0)
    pcol = lax.broadcasted_iota(jnp.int32, (BLOCK, 2 * BLOCK), 1)
    rel = prow + BLOCK - pcol
    bands = [((rel >= 0) & (rel < w)).astype(F32).astype(BF16)
             for w in POOL_WINDOWS]

    def sb_body(sb, carry):
        r0 = pl.multiple_of(sb * BLOCK, BLOCK)
        row = lax.broadcasted_iota(jnp.int32, (nq, 2 * BLOCK), 0) & (BLOCK - 1)
        col = lax.broadcasted_iota(jnp.int32, (nq, 2 * BLOCK), 1)
        dist = col - row
        lo = jnp.where(jnp.logical_and(first_step, sb == 0), BLOCK, 0)
        valid = (dist > 0) & (dist <= BLOCK) & (col >= lo)
        key0 = lax.broadcasted_iota(jnp.int32, (2 * BLOCK, HEAD_DIM), 0) == 0
        pos = pos0 + sb * BLOCK + lax.broadcasted_iota(jnp.int32, (BLOCK, 1), 0)
        for h in range(KV_HEADS):
            kb = kfull[pl.ds(r0, 2 * BLOCK), h * HEAD_DIM:(h + 1) * HEAD_DIM]
            vb = vfull[pl.ds(r0, 2 * BLOCK), h * HEAD_DIM:(h + 1) * HEAD_DIM]
            vb = jnp.where(key0, jnp.zeros_like(vb), vb)
            qg = jnp.concatenate(
                [q_ref[pl.ds(r0, BLOCK),
                       (Q_PER_KV * h + g) * HEAD_DIM:(Q_PER_KV * h + g + 1) * HEAD_DIM]
                 for g in range(Q_PER_KV)], axis=0)
            qg = qg * jnp.asarray(HEAD_DIM ** -0.5, BF16)
            logits = lax.dot_general(qg, kb, (((1,), (1,)), ((), ())),
                                     preferred_element_type=F32)
            logits = jnp.where(valid, logits, fill_ref[h])
            m = jnp.max(logits, axis=1, keepdims=True)
            p = jnp.exp(logits - m)
            denom = jnp.sum(p, axis=1, keepdims=True)
            o = jnp.dot(p.astype(BF16), vb, preferred_element_type=F32)
            o = (o * (1.0 / denom)).astype(BF16)
            attn_ref[pl.ds(r0, BLOCK),
                     h * Q_PER_KV * HEAD_DIM:(h + 1) * Q_PER_KV * HEAD_DIM] = (
                jnp.concatenate([o[g * BLOCK:(g + 1) * BLOCK, :]
                                 for g in range(Q_PER_KV)], axis=1))

            w = POOL_WINDOWS[h]
            cols = slice(h * POOL_GROUP_DIM, (h + 1) * POOL_GROUP_DIM)
            band = ufull[pl.ds(r0, 2 * BLOCK), cols]
            ssum = jnp.dot(bands[h], band, preferred_element_type=F32)
            xg = ufull[pl.ds(r0 + BLOCK, BLOCK), cols].astype(F32)
            count = jnp.minimum(pos + 1, w).astype(F32)
            pooled = ssum / count - xg
            mixed = jnp.dot(pooled.astype(BF16), wp_ref[h],
                            preferred_element_type=F32)
            pool_ref[pl.ds(r0, BLOCK), cols] = (
                mixed * ps_ref[:, cols]).astype(BF16)
        return carry

    lax.fori_loop(0, TM_MIX // BLOCK, sb_body, 0)


def _mixers(proj, sinks, w_pool_b, pool_scale, seq):
    t = proj.shape[0]
    assert len(POOL_WINDOWS) == KV_HEADS
    steps_per_seq = seq // TM_MIX
    sub = TM_MIX // BLOCK
    half = POOL_WIDTH // 2
    qcol = Q_COL // ATTN_WIDTH
    kcol = K_COL // KV_WIDTH
    ucol = U_COL // half
    prev = lambda i: jnp.maximum(i * sub - 1, 0)
    nq = Q_PER_KV * BLOCK
    sink_rows = jnp.repeat(sinks.astype(F32).reshape(KV_HEADS, Q_PER_KV), BLOCK,
                           axis=1)
    fill = jnp.full((KV_HEADS, nq, 2 * BLOCK), MASK_VALUE, F32)
    fill = fill.at[:, :, 0].set(sink_rows)
    return pl.pallas_call(
        functools.partial(_mix_kernel, steps_per_seq=steps_per_seq),
        out_shape=(jax.ShapeDtypeStruct((t, ATTN_WIDTH), BF16),
                   jax.ShapeDtypeStruct((t, POOL_WIDTH), BF16)),
        grid=(t // TM_MIX,),
        in_specs=[
            pl.BlockSpec((KV_HEADS, nq, 2 * BLOCK), lambda i: (0, 0, 0)),
            pl.BlockSpec((TM_MIX, ATTN_WIDTH), lambda i: (i, qcol)),
            pl.BlockSpec((TM_MIX, KV_WIDTH), lambda i: (i, kcol)),
            pl.BlockSpec((TM_MIX, KV_WIDTH), lambda i: (i, kcol + 1)),
            pl.BlockSpec((BLOCK, KV_WIDTH), lambda i: (prev(i), kcol)),
            pl.BlockSpec((BLOCK, KV_WIDTH), lambda i: (prev(i), kcol + 1)),
            pl.BlockSpec((TM_MIX, half), lambda i: (i, ucol)),
            pl.BlockSpec((TM_MIX, half), lambda i: (i, ucol + 1)),
            pl.BlockSpec((BLOCK, half), lambda i: (prev(i), ucol)),
            pl.BlockSpec((BLOCK, half), lambda i: (prev(i), ucol + 1)),
            pl.BlockSpec((len(POOL_WINDOWS), POOL_GROUP_DIM, POOL_GROUP_DIM),
                         lambda i: (0, 0, 0)),
            pl.BlockSpec((1, POOL_WIDTH), lambda i: (0, 0)),
        ],
        out_specs=(pl.BlockSpec((TM_MIX, ATTN_WIDTH), lambda i: (i, 0)),
                   pl.BlockSpec((TM_MIX, POOL_WIDTH), lambda i: (i, 0))),
        scratch_shapes=[pltpu.VMEM((BLOCK + TM_MIX, KV_WIDTH), BF16),
                        pltpu.VMEM((BLOCK + TM_MIX, KV_WIDTH), BF16),
                        pltpu.VMEM((BLOCK + TM_MIX, POOL_WIDTH), BF16)],
        compiler_params=pltpu.CompilerParams(
            dimension_semantics=("arbitrary",), vmem_limit_bytes=VMEM_LIMIT),
        name="mixers",
    )(fill, proj, proj, proj, proj, proj, proj, proj, proj, proj, w_pool_b,
      pool_scale)


def _first_argmax4(v):
    m = jnp.maximum(jnp.maximum(v[0], v[1]), jnp.maximum(v[2], v[3]))
    idx = jnp.where(v[0] == m, 0, jnp.where(v[1] == m, 1,
                                            jnp.where(v[2] == m, 2, 3)))
    return m, idx


def _merge_kernel(attn_ref, pool_ref, ga_ref, gp_ref, x_ref, ada_ref,
                  wa_ref, wp_ref, wo_ref, g2_ref, wr_ref, br_ref,
                  eg_ref, eu_ref, ed_ref,
                  x2_ref, h2_ref, route_ref, counts_ref, eg_o, eu_o, ed_o):
    i = pl.program_id(0)
    eg_o[...] = eg_ref[...].astype(BF16)
    eu_o[...] = eu_ref[...].astype(BF16)
    ed_o[...] = ed_ref[...].astype(BF16)
    a = jnp.dot(attn_ref[...], wa_ref[...], preferred_element_type=F32)
    p = jnp.dot(pool_ref[...], wp_ref[...], preferred_element_type=F32)
    merged = ga_ref[...].astype(F32) * a + gp_ref[...].astype(F32) * p
    out = jnp.dot(merged.astype(BF16), wo_ref[...], preferred_element_type=F32)
    x2 = x_ref[...] + ada_ref[0, 2:3, :] * out
    x2_ref[...] = x2
    h2 = _rms_modulate(x2, g2_ref[...], ada_ref[0, 4:5, :], ada_ref[0, 3:4, :])
    h2_ref[...] = h2

    logits = lax.dot_general(wr_ref[...], h2.astype(BF16),
                             (((1,), (1,)), ((), ())),
                             preferred_element_type=F32) + br_ref[:, 0:1]
    tm = logits.shape[1]
    rows = [logits[r:r + 1, :] for r in range(N_GROUPS + N_EXPERTS)]
    _, gi = _first_argmax4(rows[0:N_GROUPS])
    sel = []
    for e in range(EXPERTS_PER_GROUP):
        v = rows[N_GROUPS + 3 * EXPERTS_PER_GROUP + e]
        for g in range(N_GROUPS - 2, -1, -1):
            v = jnp.where(gi == g, rows[N_GROUPS + g * EXPERTS_PER_GROUP + e], v)
        sel.append(v)
    _, i1 = _first_argmax4(sel)
    rest = [jnp.where(i1 == e, -jnp.inf, sel[e]) for e in range(EXPERTS_PER_GROUP)]
    _, i2 = _first_argmax4(rest)
    code = jnp.minimum(i1, i2) * EXPERTS_PER_GROUP + jnp.maximum(i1, i2)
    cig = jnp.where(code == 1, 0, jnp.where(code == 2, 1, jnp.where(
        code == 3, 2, jnp.where(code == 7, 3, jnp.where(code == 6, 4, 5)))))
    cls = gi * PAIRS_PER_GROUP + cig

    onehot = (lax.broadcasted_iota(jnp.int32, (ROUTER_ROWS, tm), 0) == cls
              ).astype(F32)
    src = lax.broadcasted_iota(jnp.int32, (tm, tm), 0)
    dst = lax.broadcasted_iota(jnp.int32, (tm, tm), 1)
    before = (src < dst).astype(F32).astype(BF16)
    prefix = jnp.dot(onehot.astype(BF16), before, preferred_element_type=F32)

    @pl.when(i == 0)
    def _():
        counts_ref[...] = jnp.zeros_like(counts_ref)

    carry = counts_ref[:, 0:1]
    rank = jnp.sum(onehot * (prefix + carry), axis=0, keepdims=True)
    counts_ref[...] = counts_ref[...] + jnp.sum(onehot, axis=1, keepdims=True)
    route_ref[0:1, :] = cls
    route_ref[1:2, :] = rank.astype(jnp.int32)
    route_ref[2:8, :] = jnp.zeros((6, tm), jnp.int32)


def _merge(attn, pool, proj, x2d, ada3, wa_b, wp_b, wo_b, norm2_g, wr_t, br_col,
           w_e_gate, w_e_up, w_e_down, seq):
    t, d = x2d.shape
    n_step = t // TM_MERGE
    tiles_per_seq = seq // TM_MERGE
    flat = lambda w: w.reshape(-1, w.shape[-1])
    slab = lambda w: pl.BlockSpec((w.shape[0] // n_step, w.shape[1]),
                                  lambda i: (i, 0))
    cast = lambda w: jax.ShapeDtypeStruct(w.shape, BF16)
    eg, eu, ed = flat(w_e_gate), flat(w_e_up), flat(w_e_down)
    const = lambda shape: pl.BlockSpec(shape, lambda i: (0,) * len(shape),
                                       pipeline_mode=pl.Buffered(1))
    return pl.pallas_call(
        _merge_kernel,
        out_shape=(jax.ShapeDtypeStruct((t, d), F32),
                   jax.ShapeDtypeStruct((t, d), F32),
                   jax.ShapeDtypeStruct((8, t), jnp.int32),
                   jax.ShapeDtypeStruct((ROUTER_ROWS, 128), F32),
                   cast(eg), cast(eu), cast(ed)),
        grid=(n_step,),
        in_specs=[
            pl.BlockSpec((TM_MERGE, ATTN_WIDTH), lambda i: (i, 0)),
            pl.BlockSpec((TM_MERGE, POOL_WIDTH), lambda i: (i, 0)),
            pl.BlockSpec((TM_MERGE, d), lambda i: (i, 0)),
            pl.BlockSpec((TM_MERGE, d), lambda i: (i, 1)),
            pl.BlockSpec((TM_MERGE, d), lambda i: (i, 0)),
            pl.BlockSpec((1, N_ADA, d), lambda i: (i // tiles_per_seq, 0, 0)),
            const((ATTN_WIDTH, d)),
            const((POOL_WIDTH, d)),
            const((d, d)),
            const((1, d)),
            const((ROUTER_ROWS, d)),
            const((ROUTER_ROWS, 128)),
            slab(eg), slab(eu), slab(ed),
        ],
        out_specs=(
            pl.BlockSpec((TM_MERGE, d), lambda i: (i, 0)),
            pl.BlockSpec((TM_MERGE, d), lambda i: (i, 0)),
            pl.BlockSpec((8, TM_MERGE), lambda i: (0, i)),
            pl.BlockSpec((ROUTER_ROWS, 128), lambda i: (0, 0)),
            slab(eg), slab(eu), slab(ed),
        ),
        compiler_params=pltpu.CompilerParams(
            dimension_semantics=("arbitrary",), vmem_limit_bytes=VMEM_LIMIT),
        name="merge",
    )(attn, pool, proj, proj, x2d, ada3, wa_b, wp_b, wo_b, norm2_g, wr_t, br_col,
      eg, eu, ed)


def _row_copy(src_ref, src_row, dst_ref, dst_row, sem):
    return pltpu.make_async_copy(src_ref.at[pl.ds(src_row, 1)],
                                 dst_ref.at[pl.ds(dst_row, 1)], sem)


def _sorted_row(cls_ref, rank_ref, base_ref, token):
    return base_ref[cls_ref[token]] + rank_ref[token]


def _dispatch_kernel(cls_ref, rank_ref, base_ref, h_ref, hs_ref, sem):
    i = pl.program_id(0)
    first = i * TM_DISPATCH

    for r in range(TM_DISPATCH):
        _row_copy(h_ref, r, hs_ref,
                  _sorted_row(cls_ref, rank_ref, base_ref, first + r), sem).start()
    pltpu.make_async_copy(h_ref, hs_ref.at[pl.ds(0, TM_DISPATCH)], sem).wait()


def _dispatch(cls, rank, base, h2, n_rows):
    t, d = h2.shape
    return pl.pallas_call(
        _dispatch_kernel,
        out_shape=jax.ShapeDtypeStruct((n_rows, d), F32),
        grid_spec=pltpu.PrefetchScalarGridSpec(
            num_scalar_prefetch=3,
            grid=(t // TM_DISPATCH,),
            in_specs=[pl.BlockSpec((TM_DISPATCH, d), lambda i, *_: (i, 0))],
            out_specs=pl.BlockSpec(memory_space=pl.ANY),
            scratch_shapes=[pltpu.SemaphoreType.DMA(())],
        ),
        compiler_params=pltpu.CompilerParams(
            dimension_semantics=("arbitrary",), vmem_limit_bytes=VMEM_LIMIT,
            has_side_effects=True),
        name="dispatch",
    )(cls, rank, base, h2)


def _expert_kernel(blk_ref, ea_ref, eb_ref, lane_ref, nvalid_ref, nused_ref,
                   hs_ref, wga_ref, wua_ref, wda_ref, wgb_ref, wub_ref, wdb_ref,
                   wr_ref, br_ref, ys_ref):
    i = pl.program_id(0)

    @pl.when(i >= nused_ref[0])
    def _():
        ys_ref[...] = jnp.zeros_like(ys_ref)

    @pl.when(i < nused_ref[0])
    def _():
        tm = hs_ref.shape[0]
        rows = lax.broadcasted_iota(jnp.int32, (tm, 1), 0)
        h = jnp.where(rows < nvalid_ref[i], hs_ref[...], 0.0).astype(BF16)
        logits = jnp.dot(h, wr_ref[...], preferred_element_type=F32) + br_ref[...]
        lane = lax.broadcasted_iota(jnp.int32, logits.shape, 1)
        is_group = lane < N_GROUPS
        gmax = jnp.max(jnp.where(is_group, logits, -jnp.inf), axis=1, keepdims=True)
        gexp = jnp.where(is_group, jnp.exp(logits - gmax), 0.0)
        pick = lambda v, l: jnp.sum(jnp.where(lane == l, v, 0.0), axis=1,
                                    keepdims=True)
        p_group = pick(gexp, lane_ref[0, i]) / jnp.sum(gexp, axis=1, keepdims=True)
        la = pick(logits, lane_ref[1, i])
        lb = pick(logits, lane_ref[2, i])
        mx = jnp.maximum(la, lb)
        ea = jnp.exp(la - mx)
        eb = jnp.exp(lb - mx)
        inv = p_group / (ea + eb)

        def expert(wg_ref, wu_ref, wd_ref, weight):
            gate = jnp.dot(h, wg_ref[0], preferred_element_type=F32)
            up = jnp.dot(h, wu_ref[0], preferred_element_type=F32)
            act = (jax.nn.silu(gate) * up * weight).astype(BF16)
            return jnp.dot(act, wd_ref[0], preferred_element_type=F32)

        ys_ref[...] = (expert(wga_ref, wua_ref, wda_ref, ea * inv)
                       + expert(wgb_ref, wub_ref, wdb_ref, eb * inv))


def _experts(tile_blk, tile_ea, tile_eb, tile_lanes, tile_nvalid, n_used,
             hs, wg_b, wu_b, wd_b, wr_lanes, br_lanes):
    p_rows, d = hs.shape
    n_tiles = p_rows // TM_EXPERT
    row_map = lambda i, blk, ea, eb, ln, nv, nu: (blk[i], 0)
    wa_map = lambda i, blk, ea, eb, ln, nv, nu: (ea[i], 0, 0)
    wb_map = lambda i, blk, ea, eb, ln, nv, nu: (eb[i], 0, 0)
    const = lambda i, blk, ea, eb, ln, nv, nu: (0, 0)
    return pl.pallas_call(
        _expert_kernel,
        out_shape=jax.ShapeDtypeStruct((p_rows, d), F32),
        grid_spec=pltpu.PrefetchScalarGridSpec(
            num_scalar_prefetch=6,
            grid=(n_tiles,),
            in_specs=[
                pl.BlockSpec((TM_EXPERT, d), row_map),
                pl.BlockSpec((1, d, EXPERT_FF), wa_map),
                pl.BlockSpec((1, d, EXPERT_FF), wa_map),
                pl.BlockSpec((1, EXPERT_FF, d), wa_map),
                pl.BlockSpec((1, d, EXPERT_FF), wb_map),
                pl.BlockSpec((1, d, EXPERT_FF), wb_map),
                pl.BlockSpec((1, EXPERT_FF, d), wb_map),
                pl.BlockSpec((d, ROUTER_LANES), const),
                pl.BlockSpec((1, ROUTER_LANES), const),
            ],
            out_specs=pl.BlockSpec((TM_EXPERT, d), row_map),
        ),
        compiler_params=pltpu.CompilerParams(
            dimension_semantics=("arbitrary",), vmem_limit_bytes=VMEM_LIMIT),
        name="experts",
    )(tile_blk, tile_ea, tile_eb, tile_lanes, tile_nvalid, n_used,
      hs, wg_b, wu_b, wd_b, wg_b, wu_b, wd_b, wr_lanes, br_lanes)


def _combine_kernel(cls_ref, rank_ref, base_ref, ys_ref, x2_ref, ada_ref, g_ref,
                    o_ref, ybuf, sems):
    i = pl.program_id(0)
    n = pl.num_programs(0)

    def gather(tile, slot):
        for r in range(TM_COMBINE):
            row = _sorted_row(cls_ref, rank_ref, base_ref, tile * TM_COMBINE + r)
            _row_copy(ys_ref, row, ybuf.at[slot], r, sems.at[slot]).start()

    @pl.when(i == 0)
    def _():
        gather(0, 0)

    @pl.when(i + 1 < n)
    def _():
        gather(i + 1, (i + 1) % 2)

    slot = i % 2
    pltpu.make_async_copy(ys_ref.at[pl.ds(0, TM_COMBINE)], ybuf.at[slot],
                          sems.at[slot]).wait()

    x = x2_ref[...] + ada_ref[0, 5:6, :] * ybuf[slot]
    ms = jnp.mean(x * x, axis=-1, keepdims=True)
    o_ref[...] = x * lax.rsqrt(ms + NORM_EPS) * g_ref[...]


def _combine(cls, rank, base, ys, x2, ada3, final_g, seq):
    t, d = x2.shape
    tiles_per_seq = seq // TM_COMBINE
    return pl.pallas_call(
        _combine_kernel,
        out_shape=jax.ShapeDtypeStruct((t, d), F32),
        grid_spec=pltpu.PrefetchScalarGridSpec(
            num_scalar_prefetch=3,
            grid=(t // TM_COMBINE,),
            in_specs=[
                pl.BlockSpec(memory_space=pl.ANY),
                pl.BlockSpec((TM_COMBINE, d), lambda i, *_: (i, 0)),
                pl.BlockSpec((1, N_ADA, d),
                             lambda i, *_: (i // tiles_per_seq, 0, 0)),
                pl.BlockSpec((1, d), lambda i, *_: (0, 0)),
            ],
            out_specs=pl.BlockSpec((TM_COMBINE, d), lambda i, *_: (i, 0)),
            scratch_shapes=[pltpu.VMEM((2, TM_COMBINE, d), F32),
                            pltpu.SemaphoreType.DMA((2,))],
        ),
        compiler_params=pltpu.CompilerParams(
            dimension_semantics=("arbitrary",), vmem_limit_bytes=VMEM_LIMIT),
        name="combine",
    )(cls, rank, base, ys, x2, ada3, final_g)


def _routing_tables(route, counts, n_tiles):
    cls = route[0]
    rank = route[1]
    cnt = counts[:N_CLASSES, 0].astype(jnp.int32)
    tiles_c = (cnt + TM_EXPERT - 1) // TM_EXPERT
    tile_end = jnp.cumsum(tiles_c)
    tile_start = tile_end - tiles_c
    n_used = tile_end[-1]
    base = (tile_start * TM_EXPERT).astype(jnp.int32)

    steps = jnp.arange(n_tiles, dtype=jnp.int32)
    k = jnp.minimum(steps, n_used - 1)
    c = jnp.sum((k[:, None] >= tile_end[None, :]).astype(jnp.int32), axis=1)
    grp = c // PAIRS_PER_GROUP
    cig = c % PAIRS_PER_GROUP
    slot_a = jnp.asarray(CLASS_SLOT_A, jnp.int32)[cig]
    slot_b = jnp.asarray(CLASS_SLOT_B, jnp.int32)[cig]
    ea = grp * EXPERTS_PER_GROUP + slot_a
    eb = grp * EXPERTS_PER_GROUP + slot_b
    lanes = jnp.stack([grp, N_GROUPS + ea, N_GROUPS + eb]).astype(jnp.int32)
    nvalid = jnp.clip(cnt[c] - (k - tile_start[c]) * TM_EXPERT, 0, TM_EXPERT)
    return (cls, rank, base, steps, ea.astype(jnp.int32),
            eb.astype(jnp.int32), lanes, nvalid.astype(jnp.int32),
            n_used.reshape(1).astype(jnp.int32))


def kernel(x, c, w_ada, b_ada, norm1_g, w_in, sinks, w_pool, pool_scale,
           w_attn_branch, w_pool_branch, w_out, norm2_g, w_router_group,
           b_router_group, w_router_expert, b_router_expert, w_e_gate, w_e_up,
           w_e_down, final_g):
    b, s, d = x.shape
    t = b * s
    assert w_ada.shape[0] == 1, "single-layer block"
    assert d == D_MODEL and s % TM_PROJ == 0
    l = 0
    n_tiles = t // TM_EXPERT + N_CLASSES
    x2d = x.reshape(t, d)
    ada3 = _ada(c, w_ada[l], b_ada[l]).reshape(b, N_ADA, d)

    proj, wa_b, wp_b, wo_b = _in_proj(
        x2d, ada3, norm1_g[l].reshape(1, d), w_in[l], w_attn_branch[l],
        w_pool_branch[l], w_out[l], s)
    attn, pool = _mixers(proj, sinks[l], w_pool[l].astype(BF16),
                         pool_scale[l].reshape(1, -1), s)

    w_r = jnp.concatenate([w_router_group[l], w_router_expert[l]], axis=1)
    b_r = jnp.concatenate([b_router_group[l], b_router_expert[l]])
    n_r = N_GROUPS + N_EXPERTS
    wr_t = jnp.pad(w_r.T, ((0, ROUTER_ROWS - n_r), (0, 0))).astype(BF16)
    br_col = jnp.broadcast_to(
        jnp.pad(b_r, (0, ROUTER_ROWS - n_r))[:, None], (ROUTER_ROWS, 128))
    wr_lanes = jnp.pad(w_r, ((0, 0), (0, ROUTER_LANES - n_r))).astype(BF16)
    br_lanes = jnp.pad(b_r, (0, ROUTER_LANES - n_r)).reshape(1, ROUTER_LANES)

    x2, h2, route, counts, eg_b, eu_b, ed_b = _merge(
        attn, pool, proj, x2d, ada3, wa_b, wp_b, wo_b, norm2_g[l].reshape(1, d),
        wr_t, br_col, w_e_gate[l], w_e_up[l], w_e_down[l], s)

    cls, rank, base, tile_blk, tile_ea, tile_eb, tile_lanes, tile_nvalid, n_used = (
        _routing_tables(route, counts, n_tiles))
    hs = _dispatch(cls, rank, base, h2, n_tiles * TM_EXPERT)
    ys = _experts(tile_blk, tile_ea, tile_eb, tile_lanes, tile_nvalid, n_used,
                  hs, eg_b.reshape(w_e_gate[l].shape), eu_b.reshape(w_e_up[l].shape),
                  ed_b.reshape(w_e_down[l].shape), wr_lanes, br_lanes)
    out = _combine(cls, rank, base, ys, x2, ada3, final_g.reshape(1, d), s)
    return out.reshape(b, s, d)
```

```python
import functools

import jax
import jax.numpy as jnp
from jax import lax
from jax.experimental import pallas as pl
from jax.experimental.pallas import tpu as pltpu

F32 = jnp.float32
BF16 = jnp.bfloat16

D_MODEL = 2048
HEAD_DIM = 64
ATTN_HEADS = 16
KV_HEADS = 4
Q_PER_KV = ATTN_HEADS // KV_HEADS
ATTN_WIDTH = ATTN_HEADS * HEAD_DIM
KV_WIDTH = KV_HEADS * HEAD_DIM
BLOCK = 128
POOL_WINDOWS = (2, 4, 8, 16)
POOL_WIDTH = 1024
POOL_GROUP_DIM = 256
QKVU_WIDTH = ATTN_WIDTH + 2 * KV_WIDTH + POOL_WIDTH
GATE_WIDTH = 2 * D_MODEL
PROJ_WIDTH = GATE_WIDTH + QKVU_WIDTH
Q_COL = GATE_WIDTH
K_COL = Q_COL + ATTN_WIDTH
V_COL = K_COL + KV_WIDTH
U_COL = V_COL + KV_WIDTH
N_GROUPS = 4
EXPERTS_PER_GROUP = 4
N_EXPERTS = 16
EXPERT_FF = 512
N_ADA = 6
NORM_EPS = 1e-6
MASK_VALUE = -1e30

PAIRS_PER_GROUP = 6
N_CLASSES = N_GROUPS * PAIRS_PER_GROUP
CLASS_SLOT_A = (0, 0, 0, 1, 1, 3)
CLASS_SLOT_B = (1, 2, 3, 3, 2, 2)
ROUTER_ROWS = 32
ROUTER_LANES = 128

TM_PROJ = 2048
TN_PROJ = 512
NORM_ROWS = 256
NORM_CHUNKS = TM_PROJ // NORM_ROWS
TM_MIX = 1024
TM_MERGE = 256
TM_DISPATCH = 512
TM_EXPERT = 256
TM_COMBINE = 512
TN_ADA = 1024

VMEM_LIMIT = 52 * 1024 * 1024


def _rms_modulate(x, g, scale, shift):
    ms = jnp.mean(x * x, axis=-1, keepdims=True)
    return (x * lax.rsqrt(ms + NORM_EPS) * g) * (1.0 + scale) + shift


def _ada_kernel(cb_ref, w_ref, b_ref, o_ref):
    nb = cb_ref.shape[0]
    d = w_ref.shape[0]
    nchunk = w_ref.shape[1] // 128

    def body(kb, accs):
        k0 = pl.multiple_of(kb * 8, 8)
        new = list(accs)
        cbs = [cb_ref[b, pl.ds(k0, 8), :] for b in range(nb)]
        for j in range(nchunk):
            w = w_ref[pl.ds(k0, 8), j * 128:(j + 1) * 128]
            for b in range(nb):
                new[b * nchunk + j] = new[b * nchunk + j] + w * cbs[b]
        return tuple(new)

    init = tuple(jnp.zeros((8, 128), F32) for _ in range(nb * nchunk))
    accs = lax.fori_loop(0, d // 8, body, init, unroll=2)
    for b in range(nb):
        for j in range(nchunk):
            o_ref[b:b + 1, j * 128:(j + 1) * 128] = (
                jnp.sum(accs[b * nchunk + j], axis=0, keepdims=True)
                + b_ref[:, j * 128:(j + 1) * 128])


def _ada(c, w_ada, b_ada):
    nb, d = c.shape
    n = w_ada.shape[1]
    cb = jnp.broadcast_to(c[:, :, None], (nb, d, 128))
    return pl.pallas_call(
        _ada_kernel,
        out_shape=jax.ShapeDtypeStruct((nb, n), F32),
        grid=(n // TN_ADA,),
        in_specs=[
            pl.BlockSpec((nb, d, 128), lambda j: (0, 0, 0)),
            pl.BlockSpec((d, TN_ADA), lambda j: (0, j)),
            pl.BlockSpec((1, TN_ADA), lambda j: (0, j)),
        ],
        out_specs=pl.BlockSpec((nb, TN_ADA), lambda j: (0, j)),
        compiler_params=pltpu.CompilerParams(
            dimension_semantics=("arbitrary",), vmem_limit_bytes=VMEM_LIMIT),
        name="ada",
    )(cb, w_ada, b_ada.reshape(1, n))


def _in_proj_kernel(x_ref, ada_ref, g_ref, w_ref, wa_ref, wp_ref, wo_ref,
                    proj_ref, wa_o, wp_o, wo_o, h_even, h_odd, *, n_row):
    i = pl.program_id(0)
    j = pl.program_id(1)

    def norm_chunk(h_ref):
        r0 = pl.multiple_of(jnp.minimum(j, NORM_CHUNKS - 1) * NORM_ROWS, NORM_ROWS)
        h = _rms_modulate(x_ref[...], g_ref[...],
                          ada_ref[0, 1:2, :], ada_ref[0, 0:1, :])
        h_ref[pl.ds(r0, NORM_ROWS), :] = h.astype(BF16)

    def step(h_write, h_read):
        norm_chunk(h_write)
        acc = jnp.dot(h_read[...], w_ref[...].astype(BF16),
                      preferred_element_type=F32)
        is_gate = j < GATE_WIDTH // TN_PROJ
        proj_ref[...] = jnp.where(is_gate, jax.nn.sigmoid(acc), acc).astype(BF16)

    @pl.when(jnp.logical_and(j == 0, i < n_row))
    def _():
        wa_o[...] = wa_ref[...].astype(BF16)
        wp_o[...] = wp_ref[...].astype(BF16)
        wo_o[...] = wo_ref[...].astype(BF16)

    @pl.when(i == 0)
    def _():
        norm_chunk(h_even)

    @pl.when(i % 2 == 1)
    def _():
        step(h_odd, h_even)

    @pl.when(jnp.logical_and(i > 0, i % 2 == 0))
    def _():
        step(h_even, h_odd)


def _in_proj(x2d, ada3, norm_g, w_in, wa, wp, wo, seq):
    t, d = x2d.shape
    n_all = PROJ_WIDTH // TN_PROJ
    n_gate = GATE_WIDTH // TN_PROJ
    n_row = t // TM_PROJ
    tiles_per_seq = seq // TM_PROJ
    assert NORM_CHUNKS <= n_all
    norm_tile = lambda i: jnp.minimum(i, n_row - 1)
    out_tile = lambda i: jnp.maximum(i - 1, 0)
    w_tile = lambda i, j: (
        0, (jnp.where(i == 0, 0, j) + (n_all - n_gate)) % n_all)
    slab = lambda w: pl.BlockSpec((w.shape[0] // n_row, w.shape[1]),
                                  lambda i, j: (norm_tile(i), 0))
    cast = lambda w: jax.ShapeDtypeStruct(w.shape, BF16)
    return pl.pallas_call(
        functools.partial(_in_proj_kernel, n_row=n_row),
        out_shape=(jax.ShapeDtypeStruct((t, PROJ_WIDTH), BF16),
                   cast(wa), cast(wp), cast(wo)),
        grid=(n_row + 1, n_all),
        in_specs=[
            pl.BlockSpec((NORM_ROWS, d), lambda i, j: (
                norm_tile(i) * NORM_CHUNKS + jnp.minimum(j, NORM_CHUNKS - 1), 0)),
            pl.BlockSpec((1, N_ADA, d),
                         lambda i, j: (norm_tile(i) // tiles_per_seq, 0, 0)),
            pl.BlockSpec((1, d), lambda i, j: (0, 0)),
            pl.BlockSpec((d, TN_PROJ), w_tile),
            slab(wa), slab(wp), slab(wo),
        ],
        out_specs=(
            pl.BlockSpec((TM_PROJ, TN_PROJ),
                         lambda i, j: (out_tile(i), jnp.where(i == 0, 0, j))),
            slab(wa), slab(wp), slab(wo),
        ),
        scratch_shapes=[pltpu.VMEM((TM_PROJ, d), BF16),
                        pltpu.VMEM((TM_PROJ, d), BF16)],
        compiler_params=pltpu.CompilerParams(
            dimension_semantics=("arbitrary", "arbitrary"),
            vmem_limit_bytes=VMEM_LIMIT),
        name="in_proj",
    )(x2d, ada3, norm_g, w_in, wa, wp, wo)


def _mix_kernel(fill_ref, q_ref, kc_ref, vc_ref, kp_ref, vp_ref,
                ulo_ref, uhi_ref, plo_ref, phi_ref, wp_ref, ps_ref,
                attn_ref, pool_ref, kfull, vfull, ufull, *, steps_per_seq):
    i = pl.program_id(0)
    first_step = (i % steps_per_seq) == 0
    kfull[0:BLOCK, :] = kp_ref[...]
    kfull[BLOCK:BLOCK + TM_MIX, :] = kc_ref[...]
    vfull[0:BLOCK, :] = vp_ref[...]
    vfull[BLOCK:BLOCK + TM_MIX, :] = vc_ref[...]
    half = POOL_WIDTH // 2
    zeros = jnp.zeros((BLOCK, half), BF16)
    ufull[0:BLOCK, 0:half] = jnp.where(first_step, zeros, plo_ref[...])
    ufull[0:BLOCK, half:POOL_WIDTH] = jnp.where(first_step, zeros, phi_ref[...])
    ufull[BLOCK:BLOCK + TM_MIX, 0:half] = ulo_ref[...]
    ufull[BLOCK:BLOCK + TM_MIX, half:POOL_WIDTH] = uhi_ref[...]
    pos0 = (i % steps_per_seq) * TM_MIX
    nq = Q_PER_KV * BLOCK
    prow = lax.broadcasted_iota(jnp.int32, (BLOCK, 2 * BLOCK), 0)
    pcol = lax.broadcasted_iota(jnp.int32, (BLOCK, 2 * BLOCK), 1)
    rel = prow + BLOCK - pcol
    bands = [((rel >= 0) & (rel < w)).astype(F32).astype(BF16)
             for w in POOL_WINDOWS]

    def sb_body(sb, carry):
        r0 = pl.multiple_of(sb * BLOCK, BLOCK)
        row = lax.broadcasted_iota(jnp.int32, (nq, 2 * BLOCK), 0) & (BLOCK - 1)
        col = lax.broadcasted_iota(jnp.int32, (nq, 2 * BLOCK), 1)
        dist = col - row
        lo = jnp.where(jnp.logical_and(first_step, sb == 0), BLOCK, 0)
        valid = (dist > 0) & (dist <= BLOCK) & (col >= lo)
        key0 = lax.broadcasted_iota(jnp.int32, (2 * BLOCK, HEAD_DIM), 0) == 0
        pos = pos0 + sb * BLOCK + lax.broadcasted_iota(jnp.int32, (BLOCK, 1), 0)
        for h in range(KV_HEADS):
            kb = kfull[pl.ds(r0, 2 * BLOCK), h * HEAD_DIM:(h + 1) * HEAD_DIM]
            vb = vfull[pl.ds(r0, 2 * BLOCK), h * HEAD_DIM:(h + 1) * HEAD_DIM]
            vb = jnp.where(key0, jnp.zeros_like(vb), vb)
            qg = jnp.concatenate(
                [q_ref[pl.ds(r0, BLOCK),
                       (Q_PER_KV * h + g) * HEAD_DIM:(Q_PER_KV * h + g + 1) * HEAD_DIM]
                 for g in range(Q_PER_KV)], axis=0)
            qg = qg * jnp.asarray(HEAD_DIM ** -0.5, BF16)
            logits = lax.dot_general(qg, kb, (((1,), (1,)), ((), ())),
                                     preferred_element_type=F32)
            logits = jnp.where(valid, logits, fill_ref[h])
            m = jnp.max(logits, axis=1, keepdims=True)
            p = jnp.exp(logits - m)
            denom = jnp.sum(p, axis=1, keepdims=True)
            o = jnp.dot(p.astype(BF16), vb, preferred_element_type=F32)
            o = (o * (1.0 / denom)).astype(BF16)
            attn_ref[pl.ds(r0, BLOCK),
                     h * Q_PER_KV * HEAD_DIM:(h + 1) * Q_PER_KV * HEAD_DIM] = (
                jnp.concatenate([o[g * BLOCK:(g + 1) * BLOCK, :]
                                 for g in range(Q_PER_KV)], axis=1))

            w = POOL_WINDOWS[h]
            cols = slice(h * POOL_GROUP_DIM, (h + 1) * POOL_GROUP_DIM)
            band = ufull[pl.ds(r0, 2 * BLOCK), cols]
            ssum = jnp.dot(bands[h], band, preferred_element_type=F32)
            xg = ufull[pl.ds(r0 + BLOCK, BLOCK), cols].astype(F32)
            count = jnp.minimum(pos + 1, w).astype(F32)
            pooled = ssum / count - xg
            mixed = jnp.dot(pooled.astype(BF16), wp_ref[h],
                            preferred_element_type=F32)
            pool_ref[pl.ds(r0, BLOCK), cols] = (
                mixed * ps_ref[:, cols]).astype(BF16)
        return carry

    lax.fori_loop(0, TM_MIX // BLOCK, sb_body, 0)


def _mixers(proj, sinks, w_pool_b, pool_scale, seq):
    t = proj.shape[0]
    assert len(POOL_WINDOWS) == KV_HEADS
    steps_per_seq = seq // TM_MIX
    sub = TM_MIX // BLOCK
    half = POOL_WIDTH // 2
    qcol = Q_COL // ATTN_WIDTH
    kcol = K_COL // KV_WIDTH
    ucol = U_COL // half
    prev = lambda i: jnp.maximum(i * sub - 1, 0)
    nq = Q_PER_KV * BLOCK
    sink_rows = jnp.repeat(sinks.astype(F32).reshape(KV_HEADS, Q_PER_KV), BLOCK,
                           axis=1)
    fill = jnp.full((KV_HEADS, nq, 2 * BLOCK), MASK_VALUE, F32)
    fill = fill.at[:, :, 0].set(sink_rows)
    return pl.pallas_call(
        functools.partial(_mix_kernel, steps_per_seq=steps_per_seq),
        out_shape=(jax.ShapeDtypeStruct((t, ATTN_WIDTH), BF16),
                   jax.ShapeDtypeStruct((t, POOL_WIDTH), BF16)),
        grid=(t // TM_MIX,),
        in_specs=[
            pl.BlockSpec((KV_HEADS, nq, 2 * BLOCK), lambda i: (0, 0, 0)),
            pl.BlockSpec((TM_MIX, ATTN_WIDTH), lambda i: (i, qcol)),
            pl.BlockSpec((TM_MIX, KV_WIDTH), lambda i: (i, kcol)),
            pl.BlockSpec((TM_MIX, KV_WIDTH), lambda i: (i, kcol + 1)),
            pl.BlockSpec((BLOCK, KV_WIDTH), lambda i: (prev(i), kcol)),
            pl.BlockSpec((BLOCK, KV_WIDTH), lambda i: (prev(i), kcol + 1)),
            pl.BlockSpec((TM_MIX, half), lambda i: (i, ucol)),
            pl.BlockSpec((TM_MIX, half), lambda i: (i, ucol + 1)),
            pl.BlockSpec((BLOCK, half), lambda i: (prev(i), ucol)),
            pl.BlockSpec((BLOCK, half), lambda i: (prev(i), ucol + 1)),
            pl.BlockSpec((len(POOL_WINDOWS), POOL_GROUP_DIM, POOL_GROUP_DIM),
                         lambda i: (0, 0, 0)),
            pl.BlockSpec((1, POOL_WIDTH), lambda i: (0, 0)),
        ],
        out_specs=(pl.BlockSpec((TM_MIX, ATTN_WIDTH), lambda i: (i, 0)),
                   pl.BlockSpec((TM_MIX, POOL_WIDTH), lambda i: (i, 0))),
        scratch_shapes=[pltpu.VMEM((BLOCK + TM_MIX, KV_WIDTH), BF16),
                        pltpu.VMEM((BLOCK + TM_MIX, KV_WIDTH), BF16),
                        pltpu.VMEM((BLOCK + TM_MIX, POOL_WIDTH), BF16)],
        compiler_params=pltpu.CompilerParams(
            dimension_semantics=("arbitrary",), vmem_limit_bytes=VMEM_LIMIT),
        name="mixers",
    )(fill, proj, proj, proj, proj, proj, proj, proj, proj, proj, w_pool_b,
      pool_scale)


def _first_argmax4(v):
    m = jnp.maximum(jnp.maximum(v[0], v[1]), jnp.maximum(v[2], v[3]))
    idx = jnp.where(v[0] == m, 0, jnp.where(v[1] == m, 1,
                                            jnp.where(v[2] == m, 2, 3)))
    return m, idx


def _merge_kernel(attn_ref, pool_ref, ga_ref, gp_ref, x_ref, ada_ref,
                  wa_ref, wp_ref, wo_ref, g2_ref, wr_ref, br_ref,
                  eg_ref, eu_ref, ed_ref,
                  x2_ref, h2_ref, route_ref, counts_ref, egu_o, ed_o):
    i = pl.program_id(0)
    egu_o[:, 0:EXPERT_FF] = eg_ref[...].astype(BF16)
    egu_o[:, EXPERT_FF:2 * EXPERT_FF] = eu_ref[...].astype(BF16)
    ed_o[...] = ed_ref[...].astype(BF16)
    a = jnp.dot(attn_ref[...], wa_ref[...], preferred_element_type=F32)
    p = jnp.dot(pool_ref[...], wp_ref[...], preferred_element_type=F32)
    merged = ga_ref[...].astype(F32) * a + gp_ref[...].astype(F32) * p
    out = jnp.dot(merged.astype(BF16), wo_ref[...], preferred_element_type=F32)
    x2 = x_ref[...] + ada_ref[0, 2:3, :] * out
    x2_ref[...] = x2
    h2 = _rms_modulate(x2, g2_ref[...], ada_ref[0, 4:5, :], ada_ref[0, 3:4, :])
    h2_ref[...] = h2

    logits = lax.dot_general(wr_ref[...], h2.astype(BF16),
                             (((1,), (1,)), ((), ())),
                             preferred_element_type=F32) + br_ref[:, 0:1]
    tm = logits.shape[1]
    rows = [logits[r:r + 1, :] for r in range(N_GROUPS + N_EXPERTS)]
    _, gi = _first_argmax4(rows[0:N_GROUPS])
    sel = []
    for e in range(EXPERTS_PER_GROUP):
        v = rows[N_GROUPS + 3 * EXPERTS_PER_GROUP + e]
        for g in range(N_GROUPS - 2, -1, -1):
            v = jnp.where(gi == g, rows[N_GROUPS + g * EXPERTS_PER_GROUP + e], v)
        sel.append(v)
    _, i1 = _first_argmax4(sel)
    rest = [jnp.where(i1 == e, -jnp.inf, sel[e]) for e in range(EXPERTS_PER_GROUP)]
    _, i2 = _first_argmax4(rest)
    code = jnp.minimum(i1, i2) * EXPERTS_PER_GROUP + jnp.maximum(i1, i2)
    cig = jnp.where(code == 1, 0, jnp.where(code == 2, 1, jnp.where(
        code == 3, 2, jnp.where(code == 7, 3, jnp.where(code == 6, 4, 5)))))
    cls = gi * PAIRS_PER_GROUP + cig

    onehot = (lax.broadcasted_iota(jnp.int32, (ROUTER_ROWS, tm), 0) == cls
              ).astype(F32)
    src = lax.broadcasted_iota(jnp.int32, (tm, tm), 0)
    dst = lax.broadcasted_iota(jnp.int32, (tm, tm), 1)
    before = (src < dst).astype(F32).astype(BF16)
    prefix = jnp.dot(onehot.astype(BF16), before, preferred_element_type=F32)

    @pl.when(i == 0)
    def _():
        counts_ref[...] = jnp.zeros_like(counts_ref)

    carry = counts_ref[:, 0:1]
    rank = jnp.sum(onehot * (prefix + carry), axis=0, keepdims=True)
    counts_ref[...] = counts_ref[...] + jnp.sum(onehot, axis=1, keepdims=True)
    route_ref[0:1, :] = cls
    route_ref[1:2, :] = rank.astype(jnp.int32)
    route_ref[2:8, :] = jnp.zeros((6, tm), jnp.int32)


def _merge(attn, pool, proj, x2d, ada3, wa_b, wp_b, wo_b, norm2_g, wr_t, br_col,
           w_e_gate, w_e_up, w_e_down, seq):
    t, d = x2d.shape
    n_step = t // TM_MERGE
    tiles_per_seq = seq // TM_MERGE
    flat = lambda w: w.reshape(-1, w.shape[-1])
    slab = lambda w: pl.BlockSpec((w.shape[0] // n_step, w.shape[1]),
                                  lambda i: (i, 0))
    cast = lambda w: jax.ShapeDtypeStruct(w.shape, BF16)
    eg, eu, ed = flat(w_e_gate), flat(w_e_up), flat(w_e_down)
    egu = jax.ShapeDtypeStruct((eg.shape[0], 2 * EXPERT_FF), BF16)
    const = lambda shape: pl.BlockSpec(shape, lambda i: (0,) * len(shape),
                                       pipeline_mode=pl.Buffered(1))
    return pl.pallas_call(
        _merge_kernel,
        out_shape=(jax.ShapeDtypeStruct((t, d), F32),
                   jax.ShapeDtypeStruct((t, d), F32),
                   jax.ShapeDtypeStruct((8, t), jnp.int32),
                   jax.ShapeDtypeStruct((ROUTER_ROWS, 128), F32),
                   egu, cast(ed)),
        grid=(n_step,),
        in_specs=[
            pl.BlockSpec((TM_MERGE, ATTN_WIDTH), lambda i: (i, 0)),
            pl.BlockSpec((TM_MERGE, POOL_WIDTH), lambda i: (i, 0)),
            pl.BlockSpec((TM_MERGE, d), lambda i: (i, 0)),
            pl.BlockSpec((TM_MERGE, d), lambda i: (i, 1)),
            pl.BlockSpec((TM_MERGE, d), lambda i: (i, 0)),
            pl.BlockSpec((1, N_ADA, d), lambda i: (i // tiles_per_seq, 0, 0)),
            const((ATTN_WIDTH, d)),
            const((POOL_WIDTH, d)),
            const((d, d)),
            const((1, d)),
            const((ROUTER_ROWS, d)),
            const((ROUTER_ROWS, 128)),
            slab(eg), slab(eu), slab(ed),
        ],
        out_specs=(
            pl.BlockSpec((TM_MERGE, d), lambda i: (i, 0)),
            pl.BlockSpec((TM_MERGE, d), lambda i: (i, 0)),
            pl.BlockSpec((8, TM_MERGE), lambda i: (0, i)),
            pl.BlockSpec((ROUTER_ROWS, 128), lambda i: (0, 0)),
            slab(egu), slab(ed),
        ),
        compiler_params=pltpu.CompilerParams(
            dimension_semantics=("arbitrary",), vmem_limit_bytes=VMEM_LIMIT),
        name="merge",
    )(attn, pool, proj, proj, x2d, ada3, wa_b, wp_b, wo_b, norm2_g, wr_t, br_col,
      eg, eu, ed)


def _row_copy(src_ref, src_row, dst_ref, dst_row, sem):
    return pltpu.make_async_copy(src_ref.at[pl.ds(src_row, 1)],
                                 dst_ref.at[pl.ds(dst_row, 1)], sem)


def _sorted_row(cls_ref, rank_ref, base_ref, token):
    return base_ref[cls_ref[token]] + rank_ref[token]


def _dispatch_kernel(cls_ref, rank_ref, base_ref, h_ref, hs_ref, sem):
    i = pl.program_id(0)
    first = i * TM_DISPATCH

    for r in range(TM_DISPATCH):
        _row_copy(h_ref, r, hs_ref,
                  _sorted_row(cls_ref, rank_ref, base_ref, first + r), sem).start()
    pltpu.make_async_copy(h_ref, hs_ref.at[pl.ds(0, TM_DISPATCH)], sem).wait()


def _dispatch(cls, rank, base, h2, n_rows):
    t, d = h2.shape
    return pl.pallas_call(
        _dispatch_kernel,
        out_shape=jax.ShapeDtypeStruct((n_rows, d), F32),
        grid_spec=pltpu.PrefetchScalarGridSpec(
            num_scalar_prefetch=3,
            grid=(t // TM_DISPATCH,),
            in_specs=[pl.BlockSpec((TM_DISPATCH, d), lambda i, *_: (i, 0))],
            out_specs=pl.BlockSpec(memory_space=pl.ANY),
            scratch_shapes=[pltpu.SemaphoreType.DMA(())],
        ),
        compiler_params=pltpu.CompilerParams(
            dimension_semantics=("arbitrary",), vmem_limit_bytes=VMEM_LIMIT,
            has_side_effects=True),
        name="dispatch",
    )(cls, rank, base, h2)


def _expert_kernel(blk_ref, ea_ref, eb_ref, lane_ref, nvalid_ref, nused_ref,
                   hs_ref, wgua_ref, wda_ref, wgub_ref, wdb_ref,
                   wr_ref, br_ref, ys_ref):
    i = pl.program_id(0)

    @pl.when(i >= nused_ref[0])
    def _():
        ys_ref[...] = jnp.zeros_like(ys_ref)

    @pl.when(i < nused_ref[0])
    def _():
        tm = hs_ref.shape[0]
        rows = lax.broadcasted_iota(jnp.int32, (tm, 1), 0)
        h = jnp.where(rows < nvalid_ref[i], hs_ref[...], 0.0).astype(BF16)
        logits = jnp.dot(h, wr_ref[...], preferred_element_type=F32) + br_ref[...]
        lane = lax.broadcasted_iota(jnp.int32, logits.shape, 1)
        is_group = lane < N_GROUPS
        gmax = jnp.max(jnp.where(is_group, logits, -jnp.inf), axis=1, keepdims=True)
        gexp = jnp.where(is_group, jnp.exp(logits - gmax), 0.0)
        pick = lambda v, l: jnp.sum(jnp.where(lane == l, v, 0.0), axis=1,
                                    keepdims=True)
        p_group = pick(gexp, lane_ref[0, i]) / jnp.sum(gexp, axis=1, keepdims=True)
        la = pick(logits, lane_ref[1, i])
        lb = pick(logits, lane_ref[2, i])
        mx = jnp.maximum(la, lb)
        ea = jnp.exp(la - mx)
        eb = jnp.exp(lb - mx)
        inv = p_group / (ea + eb)

        def expert(wgu_ref, wd_ref, weight):
            gu = jnp.dot(h, wgu_ref[0], preferred_element_type=F32)
            gate, up = gu[:, 0:EXPERT_FF], gu[:, EXPERT_FF:2 * EXPERT_FF]
            act = (jax.nn.silu(gate) * up * weight).astype(BF16)
            return jnp.dot(act, wd_ref[0], preferred_element_type=F32)

        ys_ref[...] = (expert(wgua_ref, wda_ref, ea * inv)
                       + expert(wgub_ref, wdb_ref, eb * inv))


def _experts(tile_blk, tile_ea, tile_eb, tile_lanes, tile_nvalid, n_used,
             hs, wgu_b, wd_b, wr_lanes, br_lanes):
    p_rows, d = hs.shape
    n_tiles = p_rows // TM_EXPERT
    row_map = lambda i, blk, ea, eb, ln, nv, nu: (blk[i], 0)
    wa_map = lambda i, blk, ea, eb, ln, nv, nu: (ea[i], 0, 0)
    wb_map = lambda i, blk, ea, eb, ln, nv, nu: (eb[i], 0, 0)
    const = lambda i, blk, ea, eb, ln, nv, nu: (0, 0)
    return pl.pallas_call(
        _expert_kernel,
        out_shape=jax.ShapeDtypeStruct((p_rows, d), F32),
        grid_spec=pltpu.PrefetchScalarGridSpec(
            num_scalar_prefetch=6,
            grid=(n_tiles,),
            in_specs=[
                pl.BlockSpec((TM_EXPERT, d), row_map),
                pl.BlockSpec((1, d, 2 * EXPERT_FF), wa_map),
                pl.BlockSpec((1, EXPERT_FF, d), wa_map),
                pl.BlockSpec((1, d, 2 * EXPERT_FF), wb_map),
                pl.BlockSpec((1, EXPERT_FF, d), wb_map),
                pl.BlockSpec((d, ROUTER_LANES), const),
                pl.BlockSpec((1, ROUTER_LANES), const),
            ],
            out_specs=pl.BlockSpec((TM_EXPERT, d), row_map),
        ),
        compiler_params=pltpu.CompilerParams(
            dimension_semantics=("arbitrary",), vmem_limit_bytes=VMEM_LIMIT),
        name="experts",
    )(tile_blk, tile_ea, tile_eb, tile_lanes, tile_nvalid, n_used,
      hs, wgu_b, wd_b, wgu_b, wd_b, wr_lanes, br_lanes)


def _combine_kernel(cls_ref, rank_ref, base_ref, ys_ref, x2_ref, ada_ref, g_ref,
                    o_ref, ybuf, sems):
    i = pl.program_id(0)
    n = pl.num_programs(0)

    def gather(tile, slot):
        for r in range(TM_COMBINE):
            row = _sorted_row(cls_ref, rank_ref, base_ref, tile * TM_COMBINE + r)
            _row_copy(ys_ref, row, ybuf.at[slot], r, sems.at[slot]).start()

    @pl.when(i == 0)
    def _():
        gather(0, 0)

    @pl.when(i + 1 < n)
    def _():
        gather(i + 1, (i + 1) % 2)

    slot = i % 2
    pltpu.make_async_copy(ys_ref.at[pl.ds(0, TM_COMBINE)], ybuf.at[slot],
                          sems.at[slot]).wait()

    x = x2_ref[...] + ada_ref[0, 5:6, :] * ybuf[slot]
    ms = jnp.mean(x * x, axis=-1, keepdims=True)
    o_ref[...] = x * lax.rsqrt(ms + NORM_EPS) * g_ref[...]


def _combine(cls, rank, base, ys, x2, ada3, final_g, seq):
    t, d = x2.shape
    tiles_per_seq = seq // TM_COMBINE
    return pl.pallas_call(
        _combine_kernel,
        out_shape=jax.ShapeDtypeStruct((t, d), F32),
        grid_spec=pltpu.PrefetchScalarGridSpec(
            num_scalar_prefetch=3,
            grid=(t // TM_COMBINE,),
            in_specs=[
                pl.BlockSpec(memory_space=pl.ANY),
                pl.BlockSpec((TM_COMBINE, d), lambda i, *_: (i, 0)),
                pl.BlockSpec((1, N_ADA, d),
                             lambda i, *_: (i // tiles_per_seq, 0, 0)),
                pl.BlockSpec((1, d), lambda i, *_: (0, 0)),
            ],
            out_specs=pl.BlockSpec((TM_COMBINE, d), lambda i, *_: (i, 0)),
            scratch_shapes=[pltpu.VMEM((2, TM_COMBINE, d), F32),
                            pltpu.SemaphoreType.DMA((2,))],
        ),
        compiler_params=pltpu.CompilerParams(
            dimension_semantics=("arbitrary",), vmem_limit_bytes=VMEM_LIMIT),
        name="combine",
    )(cls, rank, base, ys, x2, ada3, final_g)


def _routing_tables(route, counts, n_tiles):
    cls = route[0]
    rank = route[1]
    cnt = counts[:N_CLASSES, 0].astype(jnp.int32)
    tiles_c = (cnt + TM_EXPERT - 1) // TM_EXPERT
    tile_end = jnp.cumsum(tiles_c)
    tile_start = tile_end - tiles_c
    n_used = tile_end[-1]
    base = (tile_start * TM_EXPERT).astype(jnp.int32)

    steps = jnp.arange(n_tiles, dtype=jnp.int32)
    k = jnp.minimum(steps, n_used - 1)
    c = jnp.sum((k[:, None] >= tile_end[None, :]).astype(jnp.int32), axis=1)
    grp = c // PAIRS_PER_GROUP
    cig = c % PAIRS_PER_GROUP
    slot_a = jnp.asarray(CLASS_SLOT_A, jnp.int32)[cig]
    slot_b = jnp.asarray(CLASS_SLOT_B, jnp.int32)[cig]
    ea = grp * EXPERTS_PER_GROUP + slot_a
    eb = grp * EXPERTS_PER_GROUP + slot_b
    lanes = jnp.stack([grp, N_GROUPS + ea, N_GROUPS + eb]).astype(jnp.int32)
    nvalid = jnp.clip(cnt[c] - (k - tile_start[c]) * TM_EXPERT, 0, TM_EXPERT)
    return (cls, rank, base, steps, ea.astype(jnp.int32),
            eb.astype(jnp.int32), lanes, nvalid.astype(jnp.int32),
            n_used.reshape(1).astype(jnp.int32))


def kernel(x, c, w_ada, b_ada, norm1_g, w_in, sinks, w_pool, pool_scale,
           w_attn_branch, w_pool_branch, w_out, norm2_g, w_router_group,
           b_router_group, w_router_expert, b_router_expert, w_e_gate, w_e_up,
           w_e_down, final_g):
    b, s, d = x.shape
    t = b * s
    assert w_ada.shape[0] == 1, "single-layer block"
    assert d == D_MODEL and s % TM_PROJ == 0
    l = 0
    n_tiles = t // TM_EXPERT + N_CLASSES
    x2d = x.reshape(t, d)
    ada3 = _ada(c, w_ada[l], b_ada[l]).reshape(b, N_ADA, d)

    proj, wa_b, wp_b, wo_b = _in_proj(
        x2d, ada3, norm1_g[l].reshape(1, d), w_in[l], w_attn_branch[l],
        w_pool_branch[l], w_out[l], s)
    attn, pool = _mixers(proj, sinks[l], w_pool[l].astype(BF16),
                         pool_scale[l].reshape(1, -1), s)

    w_r = jnp.concatenate([w_router_group[l], w_router_expert[l]], axis=1)
    b_r = jnp.concatenate([b_router_group[l], b_router_expert[l]])
    n_r = N_GROUPS + N_EXPERTS
    wr_t = jnp.pad(w_r.T, ((0, ROUTER_ROWS - n_r), (0, 0))).astype(BF16)
    br_col = jnp.broadcast_to(
        jnp.pad(b_r, (0, ROUTER_ROWS - n_r))[:, None], (ROUTER_ROWS, 128))
    wr_lanes = jnp.pad(w_r, ((0, 0), (0, ROUTER_LANES - n_r))).astype(BF16)
    br_lanes = jnp.pad(b_r, (0, ROUTER_LANES - n_r)).reshape(1, ROUTER_LANES)

    x2, h2, route, counts, egu_b, ed_b = _merge(
        attn, pool, proj, x2d, ada3, wa_b, wp_b, wo_b, norm2_g[l].reshape(1, d),
        wr_t, br_col, w_e_gate[l], w_e_up[l], w_e_down[l], s)

    cls, rank, base, tile_blk, tile_ea, tile_eb, tile_lanes, tile_nvalid, n_used = (
        _routing_tables(route, counts, n_tiles))
    hs = _dispatch(cls, rank, base, h2, n_tiles * TM_EXPERT)
    ys = _experts(tile_blk, tile_ea, tile_eb, tile_lanes, tile_nvalid, n_used,
                  hs, egu_b.reshape(N_EXPERTS, d, 2 * EXPERT_FF),
                  ed_b.reshape(w_e_down[l].shape), wr_lanes, br_lanes)
    out = _combine(cls, rank, base, ys, x2, ada3, final_g.reshape(1, d), s)
    return out.reshape(b, s, d)
```

```python
import functools

import jax
import jax.numpy as jnp
from jax import lax
from jax.experimental import pallas as pl
from jax.experimental.pallas import tpu as pltpu

F32 = jnp.float32
BF16 = jnp.bfloat16

D_MODEL = 2048
HEAD_DIM = 64
ATTN_HEADS = 16
KV_HEADS = 4
Q_PER_KV = ATTN_HEADS // KV_HEADS
ATTN_WIDTH = ATTN_HEADS * HEAD_DIM
KV_WIDTH = KV_HEADS * HEAD_DIM
BLOCK = 128
POOL_WINDOWS = (2, 4, 8, 16)
POOL_WIDTH = 1024
POOL_GROUP_DIM = 256
QKVU_WIDTH = ATTN_WIDTH + 2 * KV_WIDTH + POOL_WIDTH
GATE_WIDTH = 2 * D_MODEL
PROJ_WIDTH = GATE_WIDTH + QKVU_WIDTH
Q_COL = GATE_WIDTH
K_COL = Q_COL + ATTN_WIDTH
V_COL = K_COL + KV_WIDTH
U_COL = V_COL + KV_WIDTH
N_GROUPS = 4
EXPERTS_PER_GROUP = 4
N_EXPERTS = 16
EXPERT_FF = 512
N_ADA = 6
NORM_EPS = 1e-6
MASK_VALUE = -1e30

PAIRS_PER_GROUP = 6
N_CLASSES = N_GROUPS * PAIRS_PER_GROUP
CLASS_SLOT_A = (0, 0, 0, 1, 1, 3)
CLASS_SLOT_B = (1, 2, 3, 3, 2, 2)
ROUTER_ROWS = 32
ROUTER_LANES = 128

TM_PROJ = 2048
TN_PROJ = 512
NORM_ROWS = 256
NORM_CHUNKS = TM_PROJ // NORM_ROWS
TM_MIX = 1024
TM_MERGE = 256
TM_DISPATCH = 512
TM_EXPERT = 256
TM_COMBINE = 512
TN_ADA = 1024

VMEM_LIMIT = 52 * 1024 * 1024


def _rms_modulate(x, g, scale, shift):
    ms = jnp.mean(x * x, axis=-1, keepdims=True)
    return (x * lax.rsqrt(ms + NORM_EPS)) * (g * (1.0 + scale)) + shift


def _ada_kernel(cb_ref, w_ref, b_ref, o_ref):
    nb = cb_ref.shape[0]
    d = w_ref.shape[0]
    nchunk = w_ref.shape[1] // 128

    def body(kb, accs):
        k0 = pl.multiple_of(kb * 8, 8)
        new = list(accs)
        cbs = [cb_ref[b, pl.ds(k0, 8), :] for b in range(nb)]
        for j in range(nchunk):
            w = w_ref[pl.ds(k0, 8), j * 128:(j + 1) * 128]
            for b in range(nb):
                new[b * nchunk + j] = new[b * nchunk + j] + w * cbs[b]
        return tuple(new)

    init = tuple(jnp.zeros((8, 128), F32) for _ in range(nb * nchunk))
    accs = lax.fori_loop(0, d // 8, body, init, unroll=2)
    for b in range(nb):
        for j in range(nchunk):
            o_ref[b:b + 1, j * 128:(j + 1) * 128] = (
                jnp.sum(accs[b * nchunk + j], axis=0, keepdims=True)
                + b_ref[:, j * 128:(j + 1) * 128])


def _ada(c, w_ada, b_ada):
    nb, d = c.shape
    n = w_ada.shape[1]
    cb = jnp.broadcast_to(c[:, :, None], (nb, d, 128))
    return pl.pallas_call(
        _ada_kernel,
        out_shape=jax.ShapeDtypeStruct((nb, n), F32),
        grid=(n // TN_ADA,),
        in_specs=[
            pl.BlockSpec((nb, d, 128), lambda j: (0, 0, 0)),
            pl.BlockSpec((d, TN_ADA), lambda j: (0, j)),
            pl.BlockSpec((1, TN_ADA), lambda j: (0, j)),
        ],
        out_specs=pl.BlockSpec((nb, TN_ADA), lambda j: (0, j)),
        compiler_params=pltpu.CompilerParams(
            dimension_semantics=("arbitrary",), vmem_limit_bytes=VMEM_LIMIT),
        name="ada",
    )(cb, w_ada, b_ada.reshape(1, n))


def _in_proj_kernel(x_ref, ada_ref, g_ref, w_ref, wa_ref, wp_ref, wo_ref,
                    proj_ref, wa_o, wp_o, wo_o, h_even, h_odd, *, n_row):
    i = pl.program_id(0)
    j = pl.program_id(1)

    def norm_chunk(h_ref):
        r0 = pl.multiple_of(jnp.minimum(j, NORM_CHUNKS - 1) * NORM_ROWS, NORM_ROWS)
        h = _rms_modulate(x_ref[...], g_ref[...],
                          ada_ref[0, 1:2, :], ada_ref[0, 0:1, :])
        h_ref[pl.ds(r0, NORM_ROWS), :] = h.astype(BF16)

    def step(h_write, h_read):
        norm_chunk(h_write)
        acc = jnp.dot(h_read[...], w_ref[...].astype(BF16),
                      preferred_element_type=F32)
        is_gate = j < GATE_WIDTH // TN_PROJ
        gate = 0.5 * jnp.tanh(0.5 * acc) + 0.5
        proj_ref[...] = jnp.where(is_gate, gate, acc).astype(BF16)

    @pl.when(jnp.logical_and(j == 0, i < n_row))
    def _():
        wa_o[...] = wa_ref[...].astype(BF16)
        wp_o[...] = wp_ref[...].astype(BF16)
        wo_o[...] = wo_ref[...].astype(BF16)

    @pl.when(i == 0)
    def _():
        norm_chunk(h_even)

    @pl.when(i % 2 == 1)
    def _():
        step(h_odd, h_even)

    @pl.when(jnp.logical_and(i > 0, i % 2 == 0))
    def _():
        step(h_even, h_odd)


def _in_proj(x2d, ada3, norm_g, w_in, wa, wp, wo, seq):
    t, d = x2d.shape
    n_all = PROJ_WIDTH // TN_PROJ
    n_gate = GATE_WIDTH // TN_PROJ
    n_row = t // TM_PROJ
    tiles_per_seq = seq // TM_PROJ
    assert NORM_CHUNKS <= n_all
    norm_tile = lambda i: jnp.minimum(i, n_row - 1)
    out_tile = lambda i: jnp.maximum(i - 1, 0)
    w_tile = lambda i, j: (
        0, (jnp.where(i == 0, 0, j) + (n_all - n_gate)) % n_all)
    slab = lambda w: pl.BlockSpec((w.shape[0] // n_row, w.shape[1]),
                                  lambda i, j: (norm_tile(i), 0))
    cast = lambda w: jax.ShapeDtypeStruct(w.shape, BF16)
    return pl.pallas_call(
        functools.partial(_in_proj_kernel, n_row=n_row),
        out_shape=(jax.ShapeDtypeStruct((t, PROJ_WIDTH), BF16),
                   cast(wa), cast(wp), cast(wo)),
        grid=(n_row + 1, n_all),
        in_specs=[
            pl.BlockSpec((NORM_ROWS, d), lambda i, j: (
                norm_tile(i) * NORM_CHUNKS + jnp.minimum(j, NORM_CHUNKS - 1), 0)),
            pl.BlockSpec((1, N_ADA, d),
                         lambda i, j: (norm_tile(i) // tiles_per_seq, 0, 0)),
            pl.BlockSpec((1, d), lambda i, j: (0, 0)),
            pl.BlockSpec((d, TN_PROJ), w_tile),
            slab(wa), slab(wp), slab(wo),
        ],
        out_specs=(
            pl.BlockSpec((TM_PROJ, TN_PROJ),
                         lambda i, j: (out_tile(i), jnp.where(i == 0, 0, j))),
            slab(wa), slab(wp), slab(wo),
        ),
        scratch_shapes=[pltpu.VMEM((TM_PROJ, d), BF16),
                        pltpu.VMEM((TM_PROJ, d), BF16)],
        compiler_params=pltpu.CompilerParams(
            dimension_semantics=("arbitrary", "arbitrary"),
            vmem_limit_bytes=VMEM_LIMIT),
        name="in_proj",
    )(x2d, ada3, norm_g, w_in, wa, wp, wo)


def _mix_kernel(fill_ref, q_ref, kc_ref, vc_ref, kp_ref, vp_ref,
                ulo_ref, uhi_ref, plo_ref, phi_ref, wp_ref, ps_ref,
                attn_ref, pool_ref, kfull, vfull, ufull, *, steps_per_seq):
    i = pl.program_id(0)
    first_step = (i % steps_per_seq) == 0
    kfull[0:BLOCK, :] = kp_ref[...]
    kfull[BLOCK:BLOCK + TM_MIX, :] = kc_ref[...]
    vfull[0:BLOCK, :] = vp_ref[...]
    vfull[BLOCK:BLOCK + TM_MIX, :] = vc_ref[...]
    half = POOL_WIDTH // 2
    zeros = jnp.zeros((BLOCK, half), BF16)
    ufull[0:BLOCK, 0:half] = jnp.where(first_step, zeros, plo_ref[...])
    ufull[0:BLOCK, half:POOL_WIDTH] = jnp.where(first_step, zeros, phi_ref[...])
    ufull[BLOCK:BLOCK + TM_MIX, 0:half] = ulo_ref[...]
    ufull[BLOCK:BLOCK + TM_MIX, half:POOL_WIDTH] = uhi_ref[...]
    pos0 = (i % steps_per_seq) * TM_MIX
    nq = Q_PER_KV * BLOCK
    prow = lax.broadcasted_iota(jnp.int32, (BLOCK, 2 * BLOCK), 0)
    pcol = lax.broadcasted_iota(jnp.int32, (BLOCK, 2 * BLOCK), 1)
    rel = prow + BLOCK - pcol
    bands = [((rel >= 0) & (rel < w)).astype(F32).astype(BF16)
             for w in POOL_WINDOWS]

    def sb_body(sb, carry):
        r0 = pl.multiple_of(sb * BLOCK, BLOCK)
        row = lax.broadcasted_iota(jnp.int32, (nq, 2 * BLOCK), 0) & (BLOCK - 1)
        col = lax.broadcasted_iota(jnp.int32, (nq, 2 * BLOCK), 1)
        dist = col - row
        lo = jnp.where(jnp.logical_and(first_step, sb == 0), BLOCK, 0)
        valid = (dist > 0) & (dist <= BLOCK) & (col >= lo)
        key0 = lax.broadcasted_iota(jnp.int32, (2 * BLOCK, HEAD_DIM), 0) == 0
        pos = pos0 + sb * BLOCK + lax.broadcasted_iota(jnp.int32, (BLOCK, 1), 0)
        for h in range(KV_HEADS):
            kb = kfull[pl.ds(r0, 2 * BLOCK), h * HEAD_DIM:(h + 1) * HEAD_DIM]
            vb = vfull[pl.ds(r0, 2 * BLOCK), h * HEAD_DIM:(h + 1) * HEAD_DIM]
            vb = jnp.where(key0, jnp.zeros_like(vb), vb)
            qg = jnp.concatenate(
                [q_ref[pl.ds(r0, BLOCK),
                       (Q_PER_KV * h + g) * HEAD_DIM:(Q_PER_KV * h + g + 1) * HEAD_DIM]
                 for g in range(Q_PER_KV)], axis=0)
            qg = qg * jnp.asarray(HEAD_DIM ** -0.5, BF16)
            logits = lax.dot_general(qg, kb, (((1,), (1,)), ((), ())),
                                     preferred_element_type=F32)
            logits = jnp.where(valid, logits, fill_ref[h])
            m = jnp.max(logits, axis=1, keepdims=True)
            p = jnp.exp(logits - m)
            denom = jnp.sum(p, axis=1, keepdims=True)
            o = jnp.dot(p.astype(BF16), vb, preferred_element_type=F32)
            o = (o * (1.0 / denom)).astype(BF16)
            attn_ref[pl.ds(r0, BLOCK),
                     h * Q_PER_KV * HEAD_DIM:(h + 1) * Q_PER_KV * HEAD_DIM] = (
                jnp.concatenate([o[g * BLOCK:(g + 1) * BLOCK, :]
                                 for g in range(Q_PER_KV)], axis=1))

            w = POOL_WINDOWS[h]
            cols = slice(h * POOL_GROUP_DIM, (h + 1) * POOL_GROUP_DIM)
            band = ufull[pl.ds(r0, 2 * BLOCK), cols]
            ssum = jnp.dot(bands[h], band, preferred_element_type=F32)
            xg = ufull[pl.ds(r0 + BLOCK, BLOCK), cols].astype(F32)
            count = jnp.minimum(pos + 1, w).astype(F32)
            pooled = ssum / count - xg
            mixed = jnp.dot(pooled.astype(BF16), wp_ref[h],
                            preferred_element_type=F32)
            pool_ref[pl.ds(r0, BLOCK), cols] = (
                mixed * ps_ref[:, cols]).astype(BF16)
        return carry

    lax.fori_loop(0, TM_MIX // BLOCK, sb_body, 0)


def _mixers(proj, sinks, w_pool_b, pool_scale, seq):
    t = proj.shape[0]
    assert len(POOL_WINDOWS) == KV_HEADS
    steps_per_seq = seq // TM_MIX
    sub = TM_MIX // BLOCK
    half = POOL_WIDTH // 2
    qcol = Q_COL // ATTN_WIDTH
    kcol = K_COL // KV_WIDTH
    ucol = U_COL // half
    prev = lambda i: jnp.maximum(i * sub - 1, 0)
    nq = Q_PER_KV * BLOCK
    sink_rows = jnp.repeat(sinks.astype(F32).reshape(KV_HEADS, Q_PER_KV), BLOCK,
                           axis=1)
    fill = jnp.full((KV_HEADS, nq, 2 * BLOCK), MASK_VALUE, F32)
    fill = fill.at[:, :, 0].set(sink_rows)
    return pl.pallas_call(
        functools.partial(_mix_kernel, steps_per_seq=steps_per_seq),
        out_shape=(jax.ShapeDtypeStruct((t, ATTN_WIDTH), BF16),
                   jax.ShapeDtypeStruct((t, POOL_WIDTH), BF16)),
        grid=(t // TM_MIX,),
        in_specs=[
            pl.BlockSpec((KV_HEADS, nq, 2 * BLOCK), lambda i: (0, 0, 0)),
            pl.BlockSpec((TM_MIX, ATTN_WIDTH), lambda i: (i, qcol)),
            pl.BlockSpec((TM_MIX, KV_WIDTH), lambda i: (i, kcol)),
            pl.BlockSpec((TM_MIX, KV_WIDTH), lambda i: (i, kcol + 1)),
            pl.BlockSpec((BLOCK, KV_WIDTH), lambda i: (prev(i), kcol)),
            pl.BlockSpec((BLOCK, KV_WIDTH), lambda i: (prev(i), kcol + 1)),
            pl.BlockSpec((TM_MIX, half), lambda i: (i, ucol)),
            pl.BlockSpec((TM_MIX, half), lambda i: (i, ucol + 1)),
            pl.BlockSpec((BLOCK, half), lambda i: (prev(i), ucol)),
            pl.BlockSpec((BLOCK, half), lambda i: (prev(i), ucol + 1)),
            pl.BlockSpec((len(POOL_WINDOWS), POOL_GROUP_DIM, POOL_GROUP_DIM),
                         lambda i: (0, 0, 0)),
            pl.BlockSpec((1, POOL_WIDTH), lambda i: (0, 0)),
        ],
        out_specs=(pl.BlockSpec((TM_MIX, ATTN_WIDTH), lambda i: (i, 0)),
                   pl.BlockSpec((TM_MIX, POOL_WIDTH), lambda i: (i, 0))),
        scratch_shapes=[pltpu.VMEM((BLOCK + TM_MIX, KV_WIDTH), BF16),
                        pltpu.VMEM((BLOCK + TM_MIX, KV_WIDTH), BF16),
                        pltpu.VMEM((BLOCK + TM_MIX, POOL_WIDTH), BF16)],
        compiler_params=pltpu.CompilerParams(
            dimension_semantics=("arbitrary",), vmem_limit_bytes=VMEM_LIMIT),
        name="mixers",
    )(fill, proj, proj, proj, proj, proj, proj, proj, proj, proj, w_pool_b,
      pool_scale)


def _first_argmax4(v):
    m = jnp.maximum(jnp.maximum(v[0], v[1]), jnp.maximum(v[2], v[3]))
    idx = jnp.where(v[0] == m, 0, jnp.where(v[1] == m, 1,
                                            jnp.where(v[2] == m, 2, 3)))
    return m, idx


def _merge_kernel(attn_ref, pool_ref, ga_ref, gp_ref, x_ref, ada_ref,
                  wa_ref, wp_ref, wo_ref, g2_ref, wr_ref, br_ref,
                  eg_ref, eu_ref, ed_ref,
                  x2_ref, h2_ref, route_ref, counts_ref, egu_o, ed_o):
    i = pl.program_id(0)
    egu_o[:, 0:EXPERT_FF] = eg_ref[...].astype(BF16)
    egu_o[:, EXPERT_FF:2 * EXPERT_FF] = eu_ref[...].astype(BF16)
    ed_o[...] = ed_ref[...].astype(BF16)
    a = jnp.dot(attn_ref[...], wa_ref[...], preferred_element_type=F32)
    p = jnp.dot(pool_ref[...], wp_ref[...], preferred_element_type=F32)
    merged = ga_ref[...].astype(F32) * a + gp_ref[...].astype(F32) * p
    out = jnp.dot(merged.astype(BF16), wo_ref[...], preferred_element_type=F32)
    x2 = x_ref[...] + ada_ref[0, 2:3, :] * out
    x2_ref[...] = x2
    h2 = _rms_modulate(x2, g2_ref[...], ada_ref[0, 4:5, :], ada_ref[0, 3:4, :])
    h2_ref[...] = h2

    logits = lax.dot_general(wr_ref[...], h2.astype(BF16),
                             (((1,), (1,)), ((), ())),
                             preferred_element_type=F32) + br_ref[:, 0:1]
    tm = logits.shape[1]
    rows = [logits[r:r + 1, :] for r in range(N_GROUPS + N_EXPERTS)]
    _, gi = _first_argmax4(rows[0:N_GROUPS])
    sel = []
    for e in range(EXPERTS_PER_GROUP):
        v = rows[N_GROUPS + 3 * EXPERTS_PER_GROUP + e]
        for g in range(N_GROUPS - 2, -1, -1):
            v = jnp.where(gi == g, rows[N_GROUPS + g * EXPERTS_PER_GROUP + e], v)
        sel.append(v)
    _, i1 = _first_argmax4(sel)
    rest = [jnp.where(i1 == e, -jnp.inf, sel[e]) for e in range(EXPERTS_PER_GROUP)]
    _, i2 = _first_argmax4(rest)
    code = jnp.minimum(i1, i2) * EXPERTS_PER_GROUP + jnp.maximum(i1, i2)
    cig = jnp.where(code == 1, 0, jnp.where(code == 2, 1, jnp.where(
        code == 3, 2, jnp.where(code == 7, 3, jnp.where(code == 6, 4, 5)))))
    cls = gi * PAIRS_PER_GROUP + cig

    onehot = (lax.broadcasted_iota(jnp.int32, (ROUTER_ROWS, tm), 0) == cls
              ).astype(F32)
    src = lax.broadcasted_iota(jnp.int32, (tm, tm), 0)
    dst = lax.broadcasted_iota(jnp.int32, (tm, tm), 1)
    before = (src < dst).astype(F32).astype(BF16)
    prefix = jnp.dot(onehot.astype(BF16), before, preferred_element_type=F32)

    @pl.when(i == 0)
    def _():
        counts_ref[...] = jnp.zeros_like(counts_ref)

    carry = counts_ref[:, 0:1]
    rank = jnp.sum(onehot * (prefix + carry), axis=0, keepdims=True)
    counts_ref[...] = counts_ref[...] + jnp.sum(onehot, axis=1, keepdims=True)
    route_ref[0:1, :] = cls
    route_ref[1:2, :] = rank.astype(jnp.int32)
    route_ref[2:8, :] = jnp.zeros((6, tm), jnp.int32)


def _merge(attn, pool, proj, x2d, ada3, wa_b, wp_b, wo_b, norm2_g, wr_t, br_col,
           w_e_gate, w_e_up, w_e_down, seq):
    t, d = x2d.shape
    n_step = t // TM_MERGE
    tiles_per_seq = seq // TM_MERGE
    flat = lambda w: w.reshape(-1, w.shape[-1])
    slab = lambda w: pl.BlockSpec((w.shape[0] // n_step, w.shape[1]),
                                  lambda i: (i, 0))
    cast = lambda w: jax.ShapeDtypeStruct(w.shape, BF16)
    eg, eu, ed = flat(w_e_gate), flat(w_e_up), flat(w_e_down)
    egu = jax.ShapeDtypeStruct((eg.shape[0], 2 * EXPERT_FF), BF16)
    const = lambda shape: pl.BlockSpec(shape, lambda i: (0,) * len(shape),
                                       pipeline_mode=pl.Buffered(1))
    return pl.pallas_call(
        _merge_kernel,
        out_shape=(jax.ShapeDtypeStruct((t, d), F32),
                   jax.ShapeDtypeStruct((t, d), F32),
                   jax.ShapeDtypeStruct((8, t), jnp.int32),
                   jax.ShapeDtypeStruct((ROUTER_ROWS, 128), F32),
                   egu, cast(ed)),
        grid=(n_step,),
        in_specs=[
            pl.BlockSpec((TM_MERGE, ATTN_WIDTH), lambda i: (i, 0)),
            pl.BlockSpec((TM_MERGE, POOL_WIDTH), lambda i: (i, 0)),
            pl.BlockSpec((TM_MERGE, d), lambda i: (i, 0)),
            pl.BlockSpec((TM_MERGE, d), lambda i: (i, 1)),
            pl.BlockSpec((TM_MERGE, d), lambda i: (i, 0)),
            pl.BlockSpec((1, N_ADA, d), lambda i: (i // tiles_per_seq, 0, 0)),
            const((ATTN_WIDTH, d)),
            const((POOL_WIDTH, d)),
            const((d, d)),
            const((1, d)),
            const((ROUTER_ROWS, d)),
            const((ROUTER_ROWS, 128)),
            slab(eg), slab(eu), slab(ed),
        ],
        out_specs=(
            pl.BlockSpec((TM_MERGE, d), lambda i: (i, 0)),
            pl.BlockSpec((TM_MERGE, d), lambda i: (i, 0)),
            pl.BlockSpec((8, TM_MERGE), lambda i: (0, i)),
            pl.BlockSpec((ROUTER_ROWS, 128), lambda i: (0, 0)),
            slab(egu), slab(ed),
        ),
        compiler_params=pltpu.CompilerParams(
            dimension_semantics=("arbitrary",), vmem_limit_bytes=VMEM_LIMIT),
        name="merge",
    )(attn, pool, proj, proj, x2d, ada3, wa_b, wp_b, wo_b, norm2_g, wr_t, br_col,
      eg, eu, ed)


def _row_copy(src_ref, src_row, dst_ref, dst_row, sem):
    return pltpu.make_async_copy(src_ref.at[pl.ds(src_row, 1)],
                                 dst_ref.at[pl.ds(dst_row, 1)], sem)


def _sorted_row(cls_ref, rank_ref, base_ref, token):
    return base_ref[cls_ref[token]] + rank_ref[token]


def _dispatch_kernel(cls_ref, rank_ref, base_ref, h_ref, hs_ref, sem):
    i = pl.program_id(0)
    first = i * TM_DISPATCH

    for r in range(TM_DISPATCH):
        _row_copy(h_ref, r, hs_ref,
                  _sorted_row(cls_ref, rank_ref, base_ref, first + r), sem).start()
    pltpu.make_async_copy(h_ref, hs_ref.at[pl.ds(0, TM_DISPATCH)], sem).wait()


def _dispatch(cls, rank, base, h2, n_rows):
    t, d = h2.shape
    return pl.pallas_call(
        _dispatch_kernel,
        out_shape=jax.ShapeDtypeStruct((n_rows, d), F32),
        grid_spec=pltpu.PrefetchScalarGridSpec(
            num_scalar_prefetch=3,
            grid=(t // TM_DISPATCH,),
            in_specs=[pl.BlockSpec((TM_DISPATCH, d), lambda i, *_: (i, 0))],
            out_specs=pl.BlockSpec(memory_space=pl.ANY),
            scratch_shapes=[pltpu.SemaphoreType.DMA(())],
        ),
        compiler_params=pltpu.CompilerParams(
            dimension_semantics=("arbitrary",), vmem_limit_bytes=VMEM_LIMIT,
            has_side_effects=True),
        name="dispatch",
    )(cls, rank, base, h2)


def _expert_kernel(blk_ref, ea_ref, eb_ref, lane_ref, nvalid_ref, nused_ref,
                   hs_ref, wgua_ref, wda_ref, wgub_ref, wdb_ref,
                   wr_ref, br_ref, ys_ref):
    i = pl.program_id(0)

    @pl.when(i >= nused_ref[0])
    def _():
        ys_ref[...] = jnp.zeros_like(ys_ref)

    @pl.when(i < nused_ref[0])
    def _():
        tm = hs_ref.shape[0]
        rows = lax.broadcasted_iota(jnp.int32, (tm, 1), 0)
        h = jnp.where(rows < nvalid_ref[i], hs_ref[...], 0.0).astype(BF16)
        logits = jnp.dot(h, wr_ref[...], preferred_element_type=F32) + br_ref[...]
        lane = lax.broadcasted_iota(jnp.int32, logits.shape, 1)
        is_group = lane < N_GROUPS
        gmax = jnp.max(jnp.where(is_group, logits, -jnp.inf), axis=1, keepdims=True)
        gexp = jnp.where(is_group, jnp.exp(logits - gmax), 0.0)
        pick = lambda v, l: jnp.sum(jnp.where(lane == l, v, 0.0), axis=1,
                                    keepdims=True)
        p_group = pick(gexp, lane_ref[0, i]) / jnp.sum(gexp, axis=1, keepdims=True)
        la = pick(logits, lane_ref[1, i])
        lb = pick(logits, lane_ref[2, i])
        mx = jnp.maximum(la, lb)
        ea = jnp.exp(la - mx)
        eb = jnp.exp(lb - mx)
        inv = p_group / (ea + eb)

        def expert(wgu_ref, wd_ref, weight):
            gu = jnp.dot(h, wgu_ref[0], preferred_element_type=F32)
            gate, up = gu[:, 0:EXPERT_FF], gu[:, EXPERT_FF:2 * EXPERT_FF]
            act = (jax.nn.silu(gate) * up * weight).astype(BF16)
            return jnp.dot(act, wd_ref[0], preferred_element_type=F32)

        ys_ref[...] = (expert(wgua_ref, wda_ref, ea * inv)
                       + expert(wgub_ref, wdb_ref, eb * inv))


def _experts(tile_blk, tile_ea, tile_eb, tile_lanes, tile_nvalid, n_used,
             hs, wgu_b, wd_b, wr_lanes, br_lanes):
    p_rows, d = hs.shape
    n_tiles = p_rows // TM_EXPERT
    row_map = lambda i, blk, ea, eb, ln, nv, nu: (blk[i], 0)
    wa_map = lambda i, blk, ea, eb, ln, nv, nu: (ea[i], 0, 0)
    wb_map = lambda i, blk, ea, eb, ln, nv, nu: (eb[i], 0, 0)
    const = lambda i, blk, ea, eb, ln, nv, nu: (0, 0)
    return pl.pallas_call(
        _expert_kernel,
        out_shape=jax.ShapeDtypeStruct((p_rows, d), F32),
        grid_spec=pltpu.PrefetchScalarGridSpec(
            num_scalar_prefetch=6,
            grid=(n_tiles,),
            in_specs=[
                pl.BlockSpec((TM_EXPERT, d), row_map),
                pl.BlockSpec((1, d, 2 * EXPERT_FF), wa_map),
                pl.BlockSpec((1, EXPERT_FF, d), wa_map),
                pl.BlockSpec((1, d, 2 * EXPERT_FF), wb_map),
                pl.BlockSpec((1, EXPERT_FF, d), wb_map),
                pl.BlockSpec((d, ROUTER_LANES), const),
                pl.BlockSpec((1, ROUTER_LANES), const),
            ],
            out_specs=pl.BlockSpec((TM_EXPERT, d), row_map),
        ),
        compiler_params=pltpu.CompilerParams(
            dimension_semantics=("arbitrary",), vmem_limit_bytes=VMEM_LIMIT),
        name="experts",
    )(tile_blk, tile_ea, tile_eb, tile_lanes, tile_nvalid, n_used,
      hs, wgu_b, wd_b, wgu_b, wd_b, wr_lanes, br_lanes)


def _combine_kernel(cls_ref, rank_ref, base_ref, ys_ref, x2_ref, ada_ref, g_ref,
                    o_ref, ybuf, sems):
    i = pl.program_id(0)
    n = pl.num_programs(0)

    def gather(tile, slot):
        for r in range(TM_COMBINE):
            row = _sorted_row(cls_ref, rank_ref, base_ref, tile * TM_COMBINE + r)
            _row_copy(ys_ref, row, ybuf.at[slot], r, sems.at[slot]).start()

    @pl.when(i == 0)
    def _():
        gather(0, 0)

    @pl.when(i + 1 < n)
    def _():
        gather(i + 1, (i + 1) % 2)

    slot = i % 2
    pltpu.make_async_copy(ys_ref.at[pl.ds(0, TM_COMBINE)], ybuf.at[slot],
                          sems.at[slot]).wait()

    x = x2_ref[...] + ada_ref[0, 5:6, :] * ybuf[slot]
    ms = jnp.mean(x * x, axis=-1, keepdims=True)
    o_ref[...] = x * lax.rsqrt(ms + NORM_EPS) * g_ref[...]


def _combine(cls, rank, base, ys, x2, ada3, final_g, seq):
    t, d = x2.shape
    tiles_per_seq = seq // TM_COMBINE
    return pl.pallas_call(
        _combine_kernel,
        out_shape=jax.ShapeDtypeStruct((t, d), F32),
        grid_spec=pltpu.PrefetchScalarGridSpec(
            num_scalar_prefetch=3,
            grid=(t // TM_COMBINE,),
            in_specs=[
                pl.BlockSpec(memory_space=pl.ANY),
                pl.BlockSpec((TM_COMBINE, d), lambda i, *_: (i, 0)),
                pl.BlockSpec((1, N_ADA, d),
                             lambda i, *_: (i // tiles_per_seq, 0, 0)),
                pl.BlockSpec((1, d), lambda i, *_: (0, 0)),
            ],
            out_specs=pl.BlockSpec((TM_COMBINE, d), lambda i, *_: (i, 0)),
            scratch_shapes=[pltpu.VMEM((2, TM_COMBINE, d), F32),
                            pltpu.SemaphoreType.DMA((2,))],
        ),
        compiler_params=pltpu.CompilerParams(
            dimension_semantics=("arbitrary",), vmem_limit_bytes=VMEM_LIMIT),
        name="combine",
    )(cls, rank, base, ys, x2, ada3, final_g)


def _routing_tables(route, counts, n_tiles):
    cls = route[0]
    rank = route[1]
    cnt = counts[:N_CLASSES, 0].astype(jnp.int32)
    tiles_c = (cnt + TM_EXPERT - 1) // TM_EXPERT
    tile_end = jnp.cumsum(tiles_c)
    tile_start = tile_end - tiles_c
    n_used = tile_end[-1]
    base = (tile_start * TM_EXPERT).astype(jnp.int32)

    steps = jnp.arange(n_tiles, dtype=jnp.int32)
    k = jnp.minimum(steps, n_used - 1)
    c = jnp.sum((k[:, None] >= tile_end[None, :]).astype(jnp.int32), axis=1)
    grp = c // PAIRS_PER_GROUP
    cig = c % PAIRS_PER_GROUP
    slot_a = jnp.asarray(CLASS_SLOT_A, jnp.int32)[cig]
    slot_b = jnp.asarray(CLASS_SLOT_B, jnp.int32)[cig]
    ea = grp * EXPERTS_PER_GROUP + slot_a
    eb = grp * EXPERTS_PER_GROUP + slot_b
    lanes = jnp.stack([grp, N_GROUPS + ea, N_GROUPS + eb]).astype(jnp.int32)
    nvalid = jnp.clip(cnt[c] - (k - tile_start[c]) * TM_EXPERT, 0, TM_EXPERT)
    return (cls, rank, base, steps, ea.astype(jnp.int32),
            eb.astype(jnp.int32), lanes, nvalid.astype(jnp.int32),
            n_used.reshape(1).astype(jnp.int32))


def kernel(x, c, w_ada, b_ada, norm1_g, w_in, sinks, w_pool, pool_scale,
           w_attn_branch, w_pool_branch, w_out, norm2_g, w_router_group,
           b_router_group, w_router_expert, b_router_expert, w_e_gate, w_e_up,
           w_e_down, final_g):
    b, s, d = x.shape
    t = b * s
    assert w_ada.shape[0] == 1, "single-layer block"
    assert d == D_MODEL and s % TM_PROJ == 0
    l = 0
    n_tiles = t // TM_EXPERT + N_CLASSES
    x2d = x.reshape(t, d)
    ada3 = _ada(c, w_ada[l], b_ada[l]).reshape(b, N_ADA, d)

    proj, wa_b, wp_b, wo_b = _in_proj(
        x2d, ada3, norm1_g[l].reshape(1, d), w_in[l], w_attn_branch[l],
        w_pool_branch[l], w_out[l], s)
    attn, pool = _mixers(proj, sinks[l], w_pool[l].astype(BF16),
                         pool_scale[l].reshape(1, -1), s)

    w_r = jnp.concatenate([w_router_group[l], w_router_expert[l]], axis=1)
    b_r = jnp.concatenate([b_router_group[l], b_router_expert[l]])
    n_r = N_GROUPS + N_EXPERTS
    wr_t = jnp.pad(w_r.T, ((0, ROUTER_ROWS - n_r), (0, 0))).astype(BF16)
    br_col = jnp.broadcast_to(
        jnp.pad(b_r, (0, ROUTER_ROWS - n_r))[:, None], (ROUTER_ROWS, 128))
    wr_lanes = jnp.pad(w_r, ((0, 0), (0, ROUTER_LANES - n_r))).astype(BF16)
    br_lanes = jnp.pad(b_r, (0, ROUTER_LANES - n_r)).reshape(1, ROUTER_LANES)

    x2, h2, route, counts, egu_b, ed_b = _merge(
        attn, pool, proj, x2d, ada3, wa_b, wp_b, wo_b, norm2_g[l].reshape(1, d),
        wr_t, br_col, w_e_gate[l], w_e_up[l], w_e_down[l], s)

    cls, rank, base, tile_blk, tile_ea, tile_eb, tile_lanes, tile_nvalid, n_used = (
        _routing_tables(route, counts, n_tiles))
    hs = _dispatch(cls, rank, base, h2, n_tiles * TM_EXPERT)
    ys = _experts(tile_blk, tile_ea, tile_eb, tile_lanes, tile_nvalid, n_used,
                  hs, egu_b.reshape(N_EXPERTS, d, 2 * EXPERT_FF),
                  ed_b.reshape(w_e_down[l].shape), wr_lanes, br_lanes)
    out = _combine(cls, rank, base, ys, x2, ada3, final_g.reshape(1, d), s)
    return out.reshape(b, s, d)
```

```python
import functools

import jax
import jax.numpy as jnp
from jax import lax
from jax.experimental import pallas as pl
from jax.experimental.pallas import tpu as pltpu

F32 = jnp.float32
BF16 = jnp.bfloat16

D_MODEL = 2048
HEAD_DIM = 64
ATTN_HEADS = 16
KV_HEADS = 4
Q_PER_KV = ATTN_HEADS // KV_HEADS
ATTN_WIDTH = ATTN_HEADS * HEAD_DIM
KV_WIDTH = KV_HEADS * HEAD_DIM
BLOCK = 128
POOL_WINDOWS = (2, 4, 8, 16)
POOL_WIDTH = 1024
POOL_GROUP_DIM = 256
QKVU_WIDTH = ATTN_WIDTH + 2 * KV_WIDTH + POOL_WIDTH
GATE_WIDTH = 2 * D_MODEL
PROJ_WIDTH = GATE_WIDTH + QKVU_WIDTH
Q_COL = GATE_WIDTH
K_COL = Q_COL + ATTN_WIDTH
V_COL = K_COL + KV_WIDTH
U_COL = V_COL + KV_WIDTH
N_GROUPS = 4
EXPERTS_PER_GROUP = 4
N_EXPERTS = 16
EXPERT_FF = 512
N_ADA = 6
NORM_EPS = 1e-6
MASK_VALUE = -1e30

PAIRS_PER_GROUP = 6
N_CLASSES = N_GROUPS * PAIRS_PER_GROUP
CLASS_SLOT_A = (0, 0, 0, 1, 1, 3)
CLASS_SLOT_B = (1, 2, 3, 3, 2, 2)
ROUTER_ROWS = 32
ROUTER_LANES = 128

TM_PROJ = 2048
TN_PROJ = 512
NORM_ROWS = 256
NORM_CHUNKS = TM_PROJ // NORM_ROWS
TM_MIX = 1024
TM_MERGE = 256
TM_DISPATCH = 512
TM_EXPERT = 256
TM_COMBINE = 512
TN_ADA = 1024

VMEM_LIMIT = 52 * 1024 * 1024


def _rms_modulate(x, g, scale, shift):
    ms = jnp.mean(x * x, axis=-1, keepdims=True)
    return (x * lax.rsqrt(ms + NORM_EPS)) * (g * (1.0 + scale)) + shift


def _ada_kernel(cb_ref, w_ref, b_ref, o_ref):
    nb = cb_ref.shape[0]
    d = w_ref.shape[0]
    nchunk = w_ref.shape[1] // 128

    def body(kb, accs):
        k0 = pl.multiple_of(kb * 8, 8)
        new = list(accs)
        cbs = [cb_ref[b, pl.ds(k0, 8), :] for b in range(nb)]
        for j in range(nchunk):
            w = w_ref[pl.ds(k0, 8), j * 128:(j + 1) * 128]
            for b in range(nb):
                new[b * nchunk + j] = new[b * nchunk + j] + w * cbs[b]
        return tuple(new)

    init = tuple(jnp.zeros((8, 128), F32) for _ in range(nb * nchunk))
    accs = lax.fori_loop(0, d // 8, body, init, unroll=2)
    for b in range(nb):
        for j in range(nchunk):
            o_ref[b:b + 1, j * 128:(j + 1) * 128] = (
                jnp.sum(accs[b * nchunk + j], axis=0, keepdims=True)
                + b_ref[:, j * 128:(j + 1) * 128])


def _ada(c, w_ada, b_ada):
    nb, d = c.shape
    n = w_ada.shape[1]
    cb = jnp.broadcast_to(c[:, :, None], (nb, d, 128))
    return pl.pallas_call(
        _ada_kernel,
        out_shape=jax.ShapeDtypeStruct((nb, n), F32),
        grid=(n // TN_ADA,),
        in_specs=[
            pl.BlockSpec((nb, d, 128), lambda j: (0, 0, 0)),
            pl.BlockSpec((d, TN_ADA), lambda j: (0, j)),
            pl.BlockSpec((1, TN_ADA), lambda j: (0, j)),
        ],
        out_specs=pl.BlockSpec((nb, TN_ADA), lambda j: (0, j)),
        compiler_params=pltpu.CompilerParams(
            dimension_semantics=("arbitrary",), vmem_limit_bytes=VMEM_LIMIT),
        name="ada",
    )(cb, w_ada, b_ada.reshape(1, n))


def _in_proj_kernel(x_ref, ada_ref, g_ref, w_ref, wa_ref, wp_ref, wo_ref,
                    proj_ref, wa_o, wp_o, wo_o, h_even, h_odd, *, n_row):
    i = pl.program_id(0)
    j = pl.program_id(1)

    def norm_chunk(h_ref):
        r0 = pl.multiple_of(jnp.minimum(j, NORM_CHUNKS - 1) * NORM_ROWS, NORM_ROWS)
        h = _rms_modulate(x_ref[...], g_ref[...],
                          ada_ref[0, 1:2, :], ada_ref[0, 0:1, :])
        h_ref[pl.ds(r0, NORM_ROWS), :] = h.astype(BF16)

    def step(h_write, h_read):
        norm_chunk(h_write)
        acc = jnp.dot(h_read[...], w_ref[...].astype(BF16),
                      preferred_element_type=F32)
        is_gate = j < GATE_WIDTH // TN_PROJ
        gate = 0.5 * jnp.tanh(0.5 * acc) + 0.5
        proj_ref[...] = jnp.where(is_gate, gate, acc).astype(BF16)

    @pl.when(jnp.logical_and(j == 0, i < n_row))
    def _():
        wa_o[...] = wa_ref[...].astype(BF16)
        wp_o[...] = wp_ref[...].astype(BF16)
        wo_o[...] = wo_ref[...].astype(BF16)

    @pl.when(i == 0)
    def _():
        norm_chunk(h_even)

    @pl.when(i % 2 == 1)
    def _():
        step(h_odd, h_even)

    @pl.when(jnp.logical_and(i > 0, i % 2 == 0))
    def _():
        step(h_even, h_odd)


def _in_proj(x2d, ada3, norm_g, w_in, wa, wp, wo, seq):
    t, d = x2d.shape
    n_all = PROJ_WIDTH // TN_PROJ
    n_gate = GATE_WIDTH // TN_PROJ
    n_row = t // TM_PROJ
    tiles_per_seq = seq // TM_PROJ
    assert NORM_CHUNKS <= n_all
    norm_tile = lambda i: jnp.minimum(i, n_row - 1)
    out_tile = lambda i: jnp.maximum(i - 1, 0)
    w_tile = lambda i, j: (
        0, (jnp.where(i == 0, 0, j) + (n_all - n_gate)) % n_all)
    slab = lambda w: pl.BlockSpec((w.shape[0] // n_row, w.shape[1]),
                                  lambda i, j: (norm_tile(i), 0))
    cast = lambda w: jax.ShapeDtypeStruct(w.shape, BF16)
    return pl.pallas_call(
        functools.partial(_in_proj_kernel, n_row=n_row),
        out_shape=(jax.ShapeDtypeStruct((t, PROJ_WIDTH), BF16),
                   cast(wa), cast(wp), cast(wo)),
        grid=(n_row + 1, n_all),
        in_specs=[
            pl.BlockSpec((NORM_ROWS, d), lambda i, j: (
                norm_tile(i) * NORM_CHUNKS + jnp.minimum(j, NORM_CHUNKS - 1), 0)),
            pl.BlockSpec((1, N_ADA, d),
                         lambda i, j: (norm_tile(i) // tiles_per_seq, 0, 0)),
            pl.BlockSpec((1, d), lambda i, j: (0, 0)),
            pl.BlockSpec((d, TN_PROJ), w_tile),
            slab(wa), slab(wp), slab(wo),
        ],
        out_specs=(
            pl.BlockSpec((TM_PROJ, TN_PROJ),
                         lambda i, j: (out_tile(i), jnp.where(i == 0, 0, j))),
            slab(wa), slab(wp), slab(wo),
        ),
        scratch_shapes=[pltpu.VMEM((TM_PROJ, d), BF16),
                        pltpu.VMEM((TM_PROJ, d), BF16)],
        compiler_params=pltpu.CompilerParams(
            dimension_semantics=("arbitrary", "arbitrary"),
            vmem_limit_bytes=VMEM_LIMIT),
        name="in_proj",
    )(x2d, ada3, norm_g, w_in, wa, wp, wo)


def _mix_kernel(fill_ref, q_ref, kc_ref, vc_ref, kp_ref, vp_ref,
                ulo_ref, uhi_ref, plo_ref, phi_ref, wp_ref, ps_ref,
                attn_ref, pool_ref, kfull, vfull, ufull, *, steps_per_seq):
    i = pl.program_id(0)
    first_step = (i % steps_per_seq) == 0
    kfull[0:BLOCK, :] = kp_ref[...]
    kfull[BLOCK:BLOCK + TM_MIX, :] = kc_ref[...]
    vfull[0:BLOCK, :] = vp_ref[...]
    vfull[BLOCK:BLOCK + TM_MIX, :] = vc_ref[...]
    half = POOL_WIDTH // 2
    zeros = jnp.zeros((BLOCK, half), BF16)
    ufull[0:BLOCK, 0:half] = jnp.where(first_step, zeros, plo_ref[...])
    ufull[0:BLOCK, half:POOL_WIDTH] = jnp.where(first_step, zeros, phi_ref[...])
    ufull[BLOCK:BLOCK + TM_MIX, 0:half] = ulo_ref[...]
    ufull[BLOCK:BLOCK + TM_MIX, half:POOL_WIDTH] = uhi_ref[...]
    pos0 = (i % steps_per_seq) * TM_MIX
    nq = Q_PER_KV * BLOCK
    prow = lax.broadcasted_iota(jnp.int32, (BLOCK, 2 * BLOCK), 0)
    pcol = lax.broadcasted_iota(jnp.int32, (BLOCK, 2 * BLOCK), 1)
    rel = prow + BLOCK - pcol
    bands = [((rel >= 0) & (rel < w)).astype(F32).astype(BF16)
             for w in POOL_WINDOWS]

    def sb_body(sb, carry):
        r0 = pl.multiple_of(sb * BLOCK, BLOCK)
        row = lax.broadcasted_iota(jnp.int32, (nq, 2 * BLOCK), 0) & (BLOCK - 1)
        col = lax.broadcasted_iota(jnp.int32, (nq, 2 * BLOCK), 1)
        dist = col - row
        lo = jnp.where(jnp.logical_and(first_step, sb == 0), BLOCK, 0)
        valid = (dist > 0) & (dist <= BLOCK) & (col >= lo)
        key0 = lax.broadcasted_iota(jnp.int32, (2 * BLOCK, HEAD_DIM), 0) == 0
        pos = pos0 + sb * BLOCK + lax.broadcasted_iota(jnp.int32, (BLOCK, 1), 0)
        for h in range(KV_HEADS):
            kb = kfull[pl.ds(r0, 2 * BLOCK), h * HEAD_DIM:(h + 1) * HEAD_DIM]
            vb = vfull[pl.ds(r0, 2 * BLOCK), h * HEAD_DIM:(h + 1) * HEAD_DIM]
            vb = jnp.where(key0, jnp.zeros_like(vb), vb)
            qg = jnp.concatenate(
                [q_ref[pl.ds(r0, BLOCK),
                       (Q_PER_KV * h + g) * HEAD_DIM:(Q_PER_KV * h + g + 1) * HEAD_DIM]
                 for g in range(Q_PER_KV)], axis=0)
            qg = qg * jnp.asarray(HEAD_DIM ** -0.5, BF16)
            logits = lax.dot_general(qg, kb, (((1,), (1,)), ((), ())),
                                     preferred_element_type=F32)
            logits = jnp.where(valid, logits, fill_ref[h])
            m = jnp.max(logits, axis=1, keepdims=True)
            p = jnp.exp(logits - m)
            denom = jnp.sum(p, axis=1, keepdims=True)
            o = jnp.dot(p.astype(BF16), vb, preferred_element_type=F32)
            o = (o * (1.0 / denom)).astype(BF16)
            attn_ref[pl.ds(r0, BLOCK),
                     h * Q_PER_KV * HEAD_DIM:(h + 1) * Q_PER_KV * HEAD_DIM] = (
                jnp.concatenate([o[g * BLOCK:(g + 1) * BLOCK, :]
                                 for g in range(Q_PER_KV)], axis=1))

            w = POOL_WINDOWS[h]
            cols = slice(h * POOL_GROUP_DIM, (h + 1) * POOL_GROUP_DIM)
            band = ufull[pl.ds(r0, 2 * BLOCK), cols]
            ssum = jnp.dot(bands[h], band, preferred_element_type=F32)
            xg = ufull[pl.ds(r0 + BLOCK, BLOCK), cols].astype(F32)
            count = jnp.minimum(pos + 1, w).astype(F32)
            pooled = ssum / count - xg
            mixed = jnp.dot(pooled.astype(BF16), wp_ref[h],
                            preferred_element_type=F32)
            pool_ref[pl.ds(r0, BLOCK), cols] = (
                mixed * ps_ref[:, cols]).astype(BF16)
        return carry

    lax.fori_loop(0, TM_MIX // BLOCK, sb_body, 0)


def _mixers(proj, sinks, w_pool_b, pool_scale, seq):
    t = proj.shape[0]
    assert len(POOL_WINDOWS) == KV_HEADS
    steps_per_seq = seq // TM_MIX
    sub = TM_MIX // BLOCK
    half = POOL_WIDTH // 2
    qcol = Q_COL // ATTN_WIDTH
    kcol = K_COL // KV_WIDTH
    ucol = U_COL // half
    prev = lambda i: jnp.maximum(i * sub - 1, 0)
    nq = Q_PER_KV * BLOCK
    sink_rows = jnp.repeat(sinks.astype(F32).reshape(KV_HEADS, Q_PER_KV), BLOCK,
                           axis=1)
    fill = jnp.full((KV_HEADS, nq, 2 * BLOCK), MASK_VALUE, F32)
    fill = fill.at[:, :, 0].set(sink_rows)
    return pl.pallas_call(
        functools.partial(_mix_kernel, steps_per_seq=steps_per_seq),
        out_shape=(jax.ShapeDtypeStruct((t, ATTN_WIDTH), BF16),
                   jax.ShapeDtypeStruct((t, POOL_WIDTH), BF16)),
        grid=(t // TM_MIX,),
        in_specs=[
            pl.BlockSpec((KV_HEADS, nq, 2 * BLOCK), lambda i: (0, 0, 0)),
            pl.BlockSpec((TM_MIX, ATTN_WIDTH), lambda i: (i, qcol)),
            pl.BlockSpec((TM_MIX, KV_WIDTH), lambda i: (i, kcol)),
            pl.BlockSpec((TM_MIX, KV_WIDTH), lambda i: (i, kcol + 1)),
            pl.BlockSpec((BLOCK, KV_WIDTH), lambda i: (prev(i), kcol)),
            pl.BlockSpec((BLOCK, KV_WIDTH), lambda i: (prev(i), kcol + 1)),
            pl.BlockSpec((TM_MIX, half), lambda i: (i, ucol)),
            pl.BlockSpec((TM_MIX, half), lambda i: (i, ucol + 1)),
            pl.BlockSpec((BLOCK, half), lambda i: (prev(i), ucol)),
            pl.BlockSpec((BLOCK, half), lambda i: (prev(i), ucol + 1)),
            pl.BlockSpec((len(POOL_WINDOWS), POOL_GROUP_DIM, POOL_GROUP_DIM),
                         lambda i: (0, 0, 0)),
            pl.BlockSpec((1, POOL_WIDTH), lambda i: (0, 0)),
        ],
        out_specs=(pl.BlockSpec((TM_MIX, ATTN_WIDTH), lambda i: (i, 0)),
                   pl.BlockSpec((TM_MIX, POOL_WIDTH), lambda i: (i, 0))),
        scratch_shapes=[pltpu.VMEM((BLOCK + TM_MIX, KV_WIDTH), BF16),
                        pltpu.VMEM((BLOCK + TM_MIX, KV_WIDTH), BF16),
                        pltpu.VMEM((BLOCK + TM_MIX, POOL_WIDTH), BF16)],
        compiler_params=pltpu.CompilerParams(
            dimension_semantics=("arbitrary",), vmem_limit_bytes=VMEM_LIMIT),
        name="mixers",
    )(fill, proj, proj, proj, proj, proj, proj, proj, proj, proj, w_pool_b,
      pool_scale)


def _first_argmax4(v):
    m = jnp.maximum(jnp.maximum(v[0], v[1]), jnp.maximum(v[2], v[3]))
    idx = jnp.where(v[0] == m, 0, jnp.where(v[1] == m, 1,
                                            jnp.where(v[2] == m, 2, 3)))
    return m, idx


def _merge_kernel(attn_ref, pool_ref, ga_ref, gp_ref, x_ref, ada_ref,
                  wa_ref, wp_ref, wo_ref, g2_ref, wr_ref, br_ref,
                  eg_ref, eu_ref, ed_ref,
                  x2_ref, h2_ref, route_ref, counts_ref, egu_o, ed_o):
    @pl.when(pl.program_id(0) == 0)
    def _():
        counts_ref[...] = jnp.zeros_like(counts_ref)

    egu_o[:, 0:EXPERT_FF] = eg_ref[...].astype(BF16)
    egu_o[:, EXPERT_FF:2 * EXPERT_FF] = eu_ref[...].astype(BF16)
    ed_o[...] = ed_ref[...].astype(BF16)
    a = jnp.dot(attn_ref[...], wa_ref[...], preferred_element_type=F32)
    p = jnp.dot(pool_ref[...], wp_ref[...], preferred_element_type=F32)
    merged = ga_ref[...].astype(F32) * a + gp_ref[...].astype(F32) * p
    out = jnp.dot(merged.astype(BF16), wo_ref[...], preferred_element_type=F32)
    x2 = x_ref[...] + ada_ref[0, 2:3, :] * out
    x2_ref[...] = x2
    h2 = _rms_modulate(x2, g2_ref[...], ada_ref[0, 4:5, :], ada_ref[0, 3:4, :])
    h2_ref[...] = h2

    logits = jnp.dot(h2.astype(BF16), wr_ref[...], preferred_element_type=F32)
    logits = logits.T[0:ROUTER_ROWS, :] + br_ref[:, 0:1]
    tm = logits.shape[1]
    rows = [logits[r:r + 1, :] for r in range(N_GROUPS + N_EXPERTS)]
    _, gi = _first_argmax4(rows[0:N_GROUPS])
    sel = []
    for e in range(EXPERTS_PER_GROUP):
        v = rows[N_GROUPS + 3 * EXPERTS_PER_GROUP + e]
        for g in range(N_GROUPS - 2, -1, -1):
            v = jnp.where(gi == g, rows[N_GROUPS + g * EXPERTS_PER_GROUP + e], v)
        sel.append(v)
    _, i1 = _first_argmax4(sel)
    rest = [jnp.where(i1 == e, -jnp.inf, sel[e]) for e in range(EXPERTS_PER_GROUP)]
    _, i2 = _first_argmax4(rest)
    code = jnp.minimum(i1, i2) * EXPERTS_PER_GROUP + jnp.maximum(i1, i2)
    cig = jnp.where(code == 1, 0, jnp.where(code == 2, 1, jnp.where(
        code == 3, 2, jnp.where(code == 7, 3, jnp.where(code == 6, 4, 5)))))
    cls = gi * PAIRS_PER_GROUP + cig

    onehot = (lax.broadcasted_iota(jnp.int32, (ROUTER_ROWS, tm), 0) == cls
              ).astype(F32)
    src = lax.broadcasted_iota(jnp.int32, (tm, tm), 0)
    dst = lax.broadcasted_iota(jnp.int32, (tm, tm), 1)
    before = (src < dst).astype(F32).astype(BF16)
    prefix = jnp.dot(onehot.astype(BF16), before, preferred_element_type=F32)
    carry = counts_ref[:, 0:1]
    rank = jnp.sum(onehot * (prefix + carry), axis=0, keepdims=True)
    counts_ref[...] = counts_ref[...] + jnp.sum(onehot, axis=1, keepdims=True)
    route_ref[0:1, :] = cls
    route_ref[1:2, :] = rank.astype(jnp.int32)
    route_ref[2:8, :] = jnp.zeros((6, tm), jnp.int32)


def _merge(attn, pool, proj, x2d, ada3, wa_b, wp_b, wo_b, norm2_g, wr_lanes, br_col,
           w_e_gate, w_e_up, w_e_down, seq):
    t, d = x2d.shape
    n_step = t // TM_MERGE
    tiles_per_seq = seq // TM_MERGE
    flat = lambda w: w.reshape(-1, w.shape[-1])
    slab = lambda w: pl.BlockSpec((w.shape[0] // n_step, w.shape[1]),
                                  lambda i: (i, 0))
    cast = lambda w: jax.ShapeDtypeStruct(w.shape, BF16)
    eg, eu, ed = flat(w_e_gate), flat(w_e_up), flat(w_e_down)
    egu = jax.ShapeDtypeStruct((eg.shape[0], 2 * EXPERT_FF), BF16)
    const = lambda shape: pl.BlockSpec(shape, lambda i: (0,) * len(shape),
                                       pipeline_mode=pl.Buffered(1))
    return pl.pallas_call(
        _merge_kernel,
        out_shape=(jax.ShapeDtypeStruct((t, d), F32),
                   jax.ShapeDtypeStruct((t, d), F32),
                   jax.ShapeDtypeStruct((8, t), jnp.int32),
                   jax.ShapeDtypeStruct((ROUTER_ROWS, 128), F32),
                   egu, cast(ed)),
        grid=(n_step,),
        in_specs=[
            pl.BlockSpec((TM_MERGE, ATTN_WIDTH), lambda i: (i, 0)),
            pl.BlockSpec((TM_MERGE, POOL_WIDTH), lambda i: (i, 0)),
            pl.BlockSpec((TM_MERGE, d), lambda i: (i, 0)),
            pl.BlockSpec((TM_MERGE, d), lambda i: (i, 1)),
            pl.BlockSpec((TM_MERGE, d), lambda i: (i, 0)),
            pl.BlockSpec((1, N_ADA, d), lambda i: (i // tiles_per_seq, 0, 0)),
            const((ATTN_WIDTH, d)),
            const((POOL_WIDTH, d)),
            const((d, d)),
            const((1, d)),
            const((d, ROUTER_LANES)),
            const((ROUTER_ROWS, 128)),
            slab(eg), slab(eu), slab(ed),
        ],
        out_specs=(
            pl.BlockSpec((TM_MERGE, d), lambda i: (i, 0)),
            pl.BlockSpec((TM_MERGE, d), lambda i: (i, 0)),
            pl.BlockSpec((8, TM_MERGE), lambda i: (0, i)),
            pl.BlockSpec((ROUTER_ROWS, 128), lambda i: (0, 0)),
            slab(egu), slab(ed),
        ),
        compiler_params=pltpu.CompilerParams(
            dimension_semantics=("arbitrary",), vmem_limit_bytes=VMEM_LIMIT),
        name="merge",
    )(attn, pool, proj, proj, x2d, ada3, wa_b, wp_b, wo_b, norm2_g, wr_lanes, br_col,
      eg, eu, ed)


def _row_copy(src_ref, src_row, dst_ref, dst_row, sem):
    return pltpu.make_async_copy(src_ref.at[pl.ds(src_row, 1)],
                                 dst_ref.at[pl.ds(dst_row, 1)], sem)


def _sorted_row(cls_ref, rank_ref, base_ref, token):
    return base_ref[cls_ref[token]] + rank_ref[token]


def _dispatch_kernel(cls_ref, rank_ref, base_ref, h_ref, hs_ref, sem):
    i = pl.program_id(0)
    first = i * TM_DISPATCH

    for r in range(TM_DISPATCH):
        _row_copy(h_ref, r, hs_ref,
                  _sorted_row(cls_ref, rank_ref, base_ref, first + r), sem).start()
    pltpu.make_async_copy(h_ref, hs_ref.at[pl.ds(0, TM_DISPATCH)], sem).wait()


def _dispatch(cls, rank, base, h2, n_rows):
    t, d = h2.shape
    return pl.pallas_call(
        _dispatch_kernel,
        out_shape=jax.ShapeDtypeStruct((n_rows, d), F32),
        grid_spec=pltpu.PrefetchScalarGridSpec(
            num_scalar_prefetch=3,
            grid=(t // TM_DISPATCH,),
            in_specs=[pl.BlockSpec((TM_DISPATCH, d), lambda i, *_: (i, 0))],
            out_specs=pl.BlockSpec(memory_space=pl.ANY),
            scratch_shapes=[pltpu.SemaphoreType.DMA(())],
        ),
        compiler_params=pltpu.CompilerParams(
            dimension_semantics=("arbitrary",), vmem_limit_bytes=VMEM_LIMIT,
            has_side_effects=True),
        name="dispatch",
    )(cls, rank, base, h2)


def _expert_kernel(blk_ref, ea_ref, eb_ref, lane_ref, nvalid_ref, nused_ref,
                   hs_ref, wgua_ref, wda_ref, wgub_ref, wdb_ref,
                   wr_ref, br_ref, ys_ref):
    i = pl.program_id(0)

    @pl.when(i >= nused_ref[0])
    def _():
        ys_ref[...] = jnp.zeros_like(ys_ref)

    @pl.when(i < nused_ref[0])
    def _():
        tm = hs_ref.shape[0]
        rows = lax.broadcasted_iota(jnp.int32, (tm, 1), 0)
        h = jnp.where(rows < nvalid_ref[i], hs_ref[...], 0.0).astype(BF16)
        logits = jnp.dot(h, wr_ref[...], preferred_element_type=F32) + br_ref[...]
        lane = lax.broadcasted_iota(jnp.int32, logits.shape, 1)
        is_group = lane < N_GROUPS
        gmax = jnp.max(jnp.where(is_group, logits, -jnp.inf), axis=1, keepdims=True)
        gexp = jnp.where(is_group, jnp.exp(logits - gmax), 0.0)
        pick = lambda v, l: jnp.sum(jnp.where(lane == l, v, 0.0), axis=1,
                                    keepdims=True)
        p_group = pick(gexp, lane_ref[0, i]) / jnp.sum(gexp, axis=1, keepdims=True)
        la = pick(logits, lane_ref[1, i])
        lb = pick(logits, lane_ref[2, i])
        mx = jnp.maximum(la, lb)
        ea = jnp.exp(la - mx)
        eb = jnp.exp(lb - mx)
        inv = p_group / (ea + eb)

        def expert(wgu_ref, wd_ref, weight):
            gu = jnp.dot(h, wgu_ref[0], preferred_element_type=F32)
            gate, up = gu[:, 0:EXPERT_FF], gu[:, EXPERT_FF:2 * EXPERT_FF]
            act = (jax.nn.silu(gate) * up * weight).astype(BF16)
            return jnp.dot(act, wd_ref[0], preferred_element_type=F32)

        ys_ref[...] = (expert(wgua_ref, wda_ref, ea * inv)
                       + expert(wgub_ref, wdb_ref, eb * inv))


def _experts(tile_blk, tile_ea, tile_eb, tile_lanes, tile_nvalid, n_used,
             hs, wgu_b, wd_b, wr_lanes, br_lanes):
    p_rows, d = hs.shape
    n_tiles = p_rows // TM_EXPERT
    row_map = lambda i, blk, ea, eb, ln, nv, nu: (blk[i], 0)
    wa_map = lambda i, blk, ea, eb, ln, nv, nu: (ea[i], 0, 0)
    wb_map = lambda i, blk, ea, eb, ln, nv, nu: (eb[i], 0, 0)
    const = lambda i, blk, ea, eb, ln, nv, nu: (0, 0)
    return pl.pallas_call(
        _expert_kernel,
        out_shape=jax.ShapeDtypeStruct((p_rows, d), F32),
        grid_spec=pltpu.PrefetchScalarGridSpec(
            num_scalar_prefetch=6,
            grid=(n_tiles,),
            in_specs=[
                pl.BlockSpec((TM_EXPERT, d), row_map),
                pl.BlockSpec((1, d, 2 * EXPERT_FF), wa_map),
                pl.BlockSpec((1, EXPERT_FF, d), wa_map),
                pl.BlockSpec((1, d, 2 * EXPERT_FF), wb_map),
                pl.BlockSpec((1, EXPERT_FF, d), wb_map),
                pl.BlockSpec((d, ROUTER_LANES), const),
                pl.BlockSpec((1, ROUTER_LANES), const),
            ],
            out_specs=pl.BlockSpec((TM_EXPERT, d), row_map),
        ),
        compiler_params=pltpu.CompilerParams(
            dimension_semantics=("arbitrary",), vmem_limit_bytes=VMEM_LIMIT),
        name="experts",
    )(tile_blk, tile_ea, tile_eb, tile_lanes, tile_nvalid, n_used,
      hs, wgu_b, wd_b, wgu_b, wd_b, wr_lanes, br_lanes)


def _combine_kernel(cls_ref, rank_ref, base_ref, ys_ref, x2_ref, ada_ref, g_ref,
                    o_ref, ybuf, sems):
    i = pl.program_id(0)
    n = pl.num_programs(0)

    def gather(tile, slot):
        for r in range(TM_COMBINE):
            row = _sorted_row(cls_ref, rank_ref, base_ref, tile * TM_COMBINE + r)
            _row_copy(ys_ref, row, ybuf.at[slot], r, sems.at[slot]).start()

    @pl.when(i == 0)
    def _():
        gather(0, 0)

    @pl.when(i + 1 < n)
    def _():
        gather(i + 1, (i + 1) % 2)

    slot = i % 2
    pltpu.make_async_copy(ys_ref.at[pl.ds(0, TM_COMBINE)], ybuf.at[slot],
                          sems.at[slot]).wait()

    x = x2_ref[...] + ada_ref[0, 5:6, :] * ybuf[slot]
    ms = jnp.mean(x * x, axis=-1, keepdims=True)
    o_ref[...] = x * lax.rsqrt(ms + NORM_EPS) * g_ref[...]


def _combine(cls, rank, base, ys, x2, ada3, final_g, seq):
    t, d = x2.shape
    tiles_per_seq = seq // TM_COMBINE
    return pl.pallas_call(
        _combine_kernel,
        out_shape=jax.ShapeDtypeStruct((t, d), F32),
        grid_spec=pltpu.PrefetchScalarGridSpec(
            num_scalar_prefetch=3,
            grid=(t // TM_COMBINE,),
            in_specs=[
                pl.BlockSpec(memory_space=pl.ANY),
                pl.BlockSpec((TM_COMBINE, d), lambda i, *_: (i, 0)),
                pl.BlockSpec((1, N_ADA, d),
                             lambda i, *_: (i // tiles_per_seq, 0, 0)),
                pl.BlockSpec((1, d), lambda i, *_: (0, 0)),
            ],
            out_specs=pl.BlockSpec((TM_COMBINE, d), lambda i, *_: (i, 0)),
            scratch_shapes=[pltpu.VMEM((2, TM_COMBINE, d), F32),
                            pltpu.SemaphoreType.DMA((2,))],
        ),
        compiler_params=pltpu.CompilerParams(
            dimension_semantics=("arbitrary",), vmem_limit_bytes=VMEM_LIMIT),
        name="combine",
    )(cls, rank, base, ys, x2, ada3, final_g)


def _routing_tables(route, counts, n_tiles):
    cls = route[0]
    rank = route[1]
    cnt = counts[:N_CLASSES, 0].astype(jnp.int32)
    tiles_c = (cnt + TM_EXPERT - 1) // TM_EXPERT
    tile_end = jnp.cumsum(tiles_c)
    tile_start = tile_end - tiles_c
    n_used = tile_end[-1]
    base = (tile_start * TM_EXPERT).astype(jnp.int32)

    steps = jnp.arange(n_tiles, dtype=jnp.int32)
    k = jnp.minimum(steps, n_used - 1)
    c = jnp.sum((k[:, None] >= tile_end[None, :]).astype(jnp.int32), axis=1)
    grp = c // PAIRS_PER_GROUP
    cig = c % PAIRS_PER_GROUP
    slot_a = jnp.asarray(CLASS_SLOT_A, jnp.int32)[cig]
    slot_b = jnp.asarray(CLASS_SLOT_B, jnp.int32)[cig]
    ea = grp * EXPERTS_PER_GROUP + slot_a
    eb = grp * EXPERTS_PER_GROUP + slot_b
    lanes = jnp.stack([grp, N_GROUPS + ea, N_GROUPS + eb]).astype(jnp.int32)
    nvalid = jnp.clip(cnt[c] - (k - tile_start[c]) * TM_EXPERT, 0, TM_EXPERT)
    return (cls, rank, base, steps, ea.astype(jnp.int32),
            eb.astype(jnp.int32), lanes, nvalid.astype(jnp.int32),
            n_used.reshape(1).astype(jnp.int32))


def kernel(x, c, w_ada, b_ada, norm1_g, w_in, sinks, w_pool, pool_scale,
           w_attn_branch, w_pool_branch, w_out, norm2_g, w_router_group,
           b_router_group, w_router_expert, b_router_expert, w_e_gate, w_e_up,
           w_e_down, final_g):
    b, s, d = x.shape
    t = b * s
    assert w_ada.shape[0] == 1, "single-layer block"
    assert d == D_MODEL and s % TM_PROJ == 0
    l = 0
    n_tiles = t // TM_EXPERT + N_CLASSES
    x2d = x.reshape(t, d)
    ada3 = _ada(c, w_ada[l], b_ada[l]).reshape(b, N_ADA, d)

    proj, wa_b, wp_b, wo_b = _in_proj(
        x2d, ada3, norm1_g[l].reshape(1, d), w_in[l], w_attn_branch[l],
        w_pool_branch[l], w_out[l], s)
    attn, pool = _mixers(proj, sinks[l], w_pool[l].astype(BF16),
                         pool_scale[l].reshape(1, -1), s)

    w_r = jnp.concatenate([w_router_group[l], w_router_expert[l]], axis=1)
    b_r = jnp.concatenate([b_router_group[l], b_router_expert[l]])
    n_r = N_GROUPS + N_EXPERTS
    br_col = jnp.broadcast_to(
        jnp.pad(b_r, (0, ROUTER_ROWS - n_r))[:, None], (ROUTER_ROWS, 128))
    wr_lanes = jnp.pad(w_r, ((0, 0), (0, ROUTER_LANES - n_r))).astype(BF16)
    br_lanes = jnp.pad(b_r, (0, ROUTER_LANES - n_r)).reshape(1, ROUTER_LANES)

    x2, h2, route, counts, egu_b, ed_b = _merge(
        attn, pool, proj, x2d, ada3, wa_b, wp_b, wo_b, norm2_g[l].reshape(1, d),
        wr_lanes, br_col, w_e_gate[l], w_e_up[l], w_e_down[l], s)

    cls, rank, base, tile_blk, tile_ea, tile_eb, tile_lanes, tile_nvalid, n_used = (
        _routing_tables(route, counts, n_tiles))
    hs = _dispatch(cls, rank, base, h2, n_tiles * TM_EXPERT)
    ys = _experts(tile_blk, tile_ea, tile_eb, tile_lanes, tile_nvalid, n_used,
                  hs, egu_b.reshape(N_EXPERTS, d, 2 * EXPERT_FF),
                  ed_b.reshape(w_e_down[l].shape), wr_lanes, br_lanes)
    out = _combine(cls, rank, base, ys, x2, ada3, final_g.reshape(1, d), s)
    return out.reshape(b, s, d)
```

```python
import functools

import jax
import jax.numpy as jnp
from jax import lax
from jax.experimental import pallas as pl
from jax.experimental.pallas import tpu as pltpu

F32 = jnp.float32
BF16 = jnp.bfloat16

D_MODEL = 2048
HEAD_DIM = 64
ATTN_HEADS = 16
KV_HEADS = 4
Q_PER_KV = ATTN_HEADS // KV_HEADS
ATTN_WIDTH = ATTN_HEADS * HEAD_DIM
KV_WIDTH = KV_HEADS * HEAD_DIM
BLOCK = 128
POOL_WINDOWS = (2, 4, 8, 16)
POOL_WIDTH = 1024
POOL_GROUP_DIM = 256
QKVU_WIDTH = ATTN_WIDTH + 2 * KV_WIDTH + POOL_WIDTH
GATE_WIDTH = 2 * D_MODEL
PROJ_WIDTH = GATE_WIDTH + QKVU_WIDTH
Q_COL = GATE_WIDTH
K_COL = Q_COL + ATTN_WIDTH
V_COL = K_COL + KV_WIDTH
U_COL = V_COL + KV_WIDTH
N_GROUPS = 4
EXPERTS_PER_GROUP = 4
N_EXPERTS = 16
EXPERT_FF = 512
N_ADA = 6
NORM_EPS = 1e-6
MASK_VALUE = -1e30

PAIRS_PER_GROUP = 6
N_CLASSES = N_GROUPS * PAIRS_PER_GROUP
CLASS_SLOT_A = (0, 0, 0, 1, 1, 3)
CLASS_SLOT_B = (1, 2, 3, 3, 2, 2)
ROUTER_ROWS = 32
ROUTER_LANES = 128

TM_PROJ = 2048
TN_PROJ = 512
NORM_ROWS = 256
NORM_CHUNKS = TM_PROJ // NORM_ROWS
TM_MIX = 1024
TM_MERGE = 256
TM_DISPATCH = 512
TM_EXPERT = 256
TM_COMBINE = 512
TN_ADA = 1024

VMEM_LIMIT = 52 * 1024 * 1024


def _rms_modulate(x, g, scale, shift):
    ms = jnp.mean(x * x, axis=-1, keepdims=True)
    return (x * lax.rsqrt(ms + NORM_EPS)) * (g * (1.0 + scale)) + shift


def _ada_kernel(cb_ref, w_ref, b_ref, o_ref):
    nb = cb_ref.shape[0]
    d = w_ref.shape[0]
    nchunk = w_ref.shape[1] // 128

    def body(kb, accs):
        k0 = pl.multiple_of(kb * 8, 8)
        new = list(accs)
        cbs = [cb_ref[b, pl.ds(k0, 8), :] for b in range(nb)]
        for j in range(nchunk):
            w = w_ref[pl.ds(k0, 8), j * 128:(j + 1) * 128]
            for b in range(nb):
                new[b * nchunk + j] = new[b * nchunk + j] + w * cbs[b]
        return tuple(new)

    init = tuple(jnp.zeros((8, 128), F32) for _ in range(nb * nchunk))
    accs = lax.fori_loop(0, d // 8, body, init, unroll=2)
    for b in range(nb):
        for j in range(nchunk):
            o_ref[b:b + 1, j * 128:(j + 1) * 128] = (
                jnp.sum(accs[b * nchunk + j], axis=0, keepdims=True)
                + b_ref[:, j * 128:(j + 1) * 128])


def _ada(c, w_ada, b_ada):
    nb, d = c.shape
    n = w_ada.shape[1]
    cb = jnp.broadcast_to(c[:, :, None], (nb, d, 128))
    return pl.pallas_call(
        _ada_kernel,
        out_shape=jax.ShapeDtypeStruct((nb, n), F32),
        grid=(n // TN_ADA,),
        in_specs=[
            pl.BlockSpec((nb, d, 128), lambda j: (0, 0, 0)),
            pl.BlockSpec((d, TN_ADA), lambda j: (0, j)),
            pl.BlockSpec((1, TN_ADA), lambda j: (0, j)),
        ],
        out_specs=pl.BlockSpec((nb, TN_ADA), lambda j: (0, j)),
        compiler_params=pltpu.CompilerParams(
            dimension_semantics=("arbitrary",), vmem_limit_bytes=VMEM_LIMIT),
        name="ada",
    )(cb, w_ada, b_ada.reshape(1, n))


def _in_proj_kernel(x_ref, ada_ref, g_ref, w_ref, wa_ref, wp_ref, wo_ref,
                    proj_ref, wa_o, wp_o, wo_o, h_even, h_odd, *, n_row):
    i = pl.program_id(0)
    j = pl.program_id(1)

    def norm_chunk(h_ref):
        r0 = pl.multiple_of(jnp.minimum(j, NORM_CHUNKS - 1) * NORM_ROWS, NORM_ROWS)
        h = _rms_modulate(x_ref[...], g_ref[...],
                          ada_ref[0, 1:2, :], ada_ref[0, 0:1, :])
        h_ref[pl.ds(r0, NORM_ROWS), :] = h.astype(BF16)

    def step(h_write, h_read):
        norm_chunk(h_write)
        acc = jnp.dot(h_read[...], w_ref[...].astype(BF16),
                      preferred_element_type=F32)
        is_gate = j < GATE_WIDTH // TN_PROJ
        gate = 0.5 * jnp.tanh(0.5 * acc) + 0.5
        proj_ref[...] = jnp.where(is_gate, gate, acc).astype(BF16)

    @pl.when(jnp.logical_and(j == 0, i < n_row))
    def _():
        wa_o[...] = wa_ref[...].astype(BF16)
        wp_o[...] = wp_ref[...].astype(BF16)
        wo_o[...] = wo_ref[...].astype(BF16)

    @pl.when(jnp.logical_and(i == 0, j < NORM_CHUNKS))
    def _():
        norm_chunk(h_even)

    @pl.when(i % 2 == 1)
    def _():
        step(h_odd, h_even)

    @pl.when(jnp.logical_and(i > 0, i % 2 == 0))
    def _():
        step(h_even, h_odd)


def _in_proj(x2d, ada3, norm_g, w_in, wa, wp, wo, seq):
    t, d = x2d.shape
    n_all = PROJ_WIDTH // TN_PROJ
    n_gate = GATE_WIDTH // TN_PROJ
    n_row = t // TM_PROJ
    tiles_per_seq = seq // TM_PROJ
    assert NORM_CHUNKS <= n_all
    norm_tile = lambda i: jnp.minimum(i, n_row - 1)
    out_tile = lambda i: jnp.maximum(i - 1, 0)
    w_tile = lambda i, j: (
        0, (jnp.where(i == 0, 0, j) + (n_all - n_gate)) % n_all)
    slab = lambda w: pl.BlockSpec((w.shape[0] // n_row, w.shape[1]),
                                  lambda i, j: (norm_tile(i), 0))
    cast = lambda w: jax.ShapeDtypeStruct(w.shape, BF16)
    return pl.pallas_call(
        functools.partial(_in_proj_kernel, n_row=n_row),
        out_shape=(jax.ShapeDtypeStruct((t, PROJ_WIDTH), BF16),
                   cast(wa), cast(wp), cast(wo)),
        grid=(n_row + 1, n_all),
        in_specs=[
            pl.BlockSpec((NORM_ROWS, d), lambda i, j: (
                norm_tile(i) * NORM_CHUNKS + jnp.minimum(j, NORM_CHUNKS - 1), 0)),
            pl.BlockSpec((1, N_ADA, d),
                         lambda i, j: (norm_tile(i) // tiles_per_seq, 0, 0)),
            pl.BlockSpec((1, d), lambda i, j: (0, 0)),
            pl.BlockSpec((d, TN_PROJ), w_tile),
            slab(wa), slab(wp), slab(wo),
        ],
        out_specs=(
            pl.BlockSpec((TM_PROJ, TN_PROJ),
                         lambda i, j: (out_tile(i), jnp.where(i == 0, 0, j))),
            slab(wa), slab(wp), slab(wo),
        ),
        scratch_shapes=[pltpu.VMEM((TM_PROJ, d), BF16),
                        pltpu.VMEM((TM_PROJ, d), BF16)],
        compiler_params=pltpu.CompilerParams(
            dimension_semantics=("arbitrary", "arbitrary"),
            vmem_limit_bytes=VMEM_LIMIT),
        name="in_proj",
    )(x2d, ada3, norm_g, w_in, wa, wp, wo)


def _mix_kernel(fill_ref, q_ref, kc_ref, vc_ref, kp_ref, vp_ref,
                ulo_ref, uhi_ref, plo_ref, phi_ref, wp_ref, ps_ref,
                attn_ref, pool_ref, kfull, vfull, ufull, *, steps_per_seq):
    i = pl.program_id(0)
    first_step = (i % steps_per_seq) == 0
    kfull[0:BLOCK, :] = kp_ref[...]
    kfull[BLOCK:BLOCK + TM_MIX, :] = kc_ref[...]
    vfull[0:BLOCK, :] = vp_ref[...]
    vfull[BLOCK:BLOCK + TM_MIX, :] = vc_ref[...]
    half = POOL_WIDTH // 2
    zeros = jnp.zeros((BLOCK, half), BF16)
    ufull[0:BLOCK, 0:half] = jnp.where(first_step, zeros, plo_ref[...])
    ufull[0:BLOCK, half:POOL_WIDTH] = jnp.where(first_step, zeros, phi_ref[...])
    ufull[BLOCK:BLOCK + TM_MIX, 0:half] = ulo_ref[...]
    ufull[BLOCK:BLOCK + TM_MIX, half:POOL_WIDTH] = uhi_ref[...]
    pos0 = (i % steps_per_seq) * TM_MIX
    nq = Q_PER_KV * BLOCK
    prow = lax.broadcasted_iota(jnp.int32, (BLOCK, 2 * BLOCK), 0)
    pcol = lax.broadcasted_iota(jnp.int32, (BLOCK, 2 * BLOCK), 1)
    rel = prow + BLOCK - pcol
    bands = [((rel >= 0) & (rel < w)).astype(F32).astype(BF16)
             for w in POOL_WINDOWS]

    def sb_body(sb, carry):
        r0 = pl.multiple_of(sb * BLOCK, BLOCK)
        row = lax.broadcasted_iota(jnp.int32, (nq, 2 * BLOCK), 0) & (BLOCK - 1)
        col = lax.broadcasted_iota(jnp.int32, (nq, 2 * BLOCK), 1)
        dist = col - row
        lo = jnp.where(jnp.logical_and(first_step, sb == 0), BLOCK, 0)
        valid = (dist > 0) & (dist <= BLOCK) & (col >= lo)
        key0 = lax.broadcasted_iota(jnp.int32, (2 * BLOCK, HEAD_DIM), 0) == 0
        pos = pos0 + sb * BLOCK + lax.broadcasted_iota(jnp.int32, (BLOCK, 1), 0)
        for h in range(KV_HEADS):
            kb = kfull[pl.ds(r0, 2 * BLOCK), h * HEAD_DIM:(h + 1) * HEAD_DIM]
            vb = vfull[pl.ds(r0, 2 * BLOCK), h * HEAD_DIM:(h + 1) * HEAD_DIM]
            vb = jnp.where(key0, jnp.zeros_like(vb), vb)
            qg = jnp.concatenate(
                [q_ref[pl.ds(r0, BLOCK),
                       (Q_PER_KV * h + g) * HEAD_DIM:(Q_PER_KV * h + g + 1) * HEAD_DIM]
                 for g in range(Q_PER_KV)], axis=0)
            qg = qg * jnp.asarray(HEAD_DIM ** -0.5, BF16)
            logits = lax.dot_general(qg, kb, (((1,), (1,)), ((), ())),
                                     preferred_element_type=F32)
            logits = jnp.where(valid, logits, fill_ref[h])
            m = jnp.max(logits, axis=1, keepdims=True)
            p = jnp.exp(logits - m)
            denom = jnp.sum(p, axis=1, keepdims=True)
            o = jnp.dot(p.astype(BF16), vb, preferred_element_type=F32)
            o = (o * (1.0 / denom)).astype(BF16)
            attn_ref[pl.ds(r0, BLOCK),
                     h * Q_PER_KV * HEAD_DIM:(h + 1) * Q_PER_KV * HEAD_DIM] = (
                jnp.concatenate([o[g * BLOCK:(g + 1) * BLOCK, :]
                                 for g in range(Q_PER_KV)], axis=1))

            w = POOL_WINDOWS[h]
            cols = slice(h * POOL_GROUP_DIM, (h + 1) * POOL_GROUP_DIM)
            band = ufull[pl.ds(r0, 2 * BLOCK), cols]
            ssum = jnp.dot(bands[h], band, preferred_element_type=F32)
            xg = ufull[pl.ds(r0 + BLOCK, BLOCK), cols].astype(F32)
            count = jnp.minimum(pos + 1, w).astype(F32)
            pooled = ssum / count - xg
            mixed = jnp.dot(pooled.astype(BF16), wp_ref[h],
                            preferred_element_type=F32)
            pool_ref[pl.ds(r0, BLOCK), cols] = (
                mixed * ps_ref[:, cols]).astype(BF16)
        return carry

    lax.fori_loop(0, TM_MIX // BLOCK, sb_body, 0)


def _mixers(proj, sinks, w_pool_b, pool_scale, seq):
    t = proj.shape[0]
    assert len(POOL_WINDOWS) == KV_HEADS
    steps_per_seq = seq // TM_MIX
    sub = TM_MIX // BLOCK
    half = POOL_WIDTH // 2
    qcol = Q_COL // ATTN_WIDTH
    kcol = K_COL // KV_WIDTH
    ucol = U_COL // half
    prev = lambda i: jnp.maximum(i * sub - 1, 0)
    nq = Q_PER_KV * BLOCK
    sink_rows = jnp.repeat(sinks.astype(F32).reshape(KV_HEADS, Q_PER_KV), BLOCK,
                           axis=1)
    fill = jnp.full((KV_HEADS, nq, 2 * BLOCK), MASK_VALUE, F32)
    fill = fill.at[:, :, 0].set(sink_rows)
    return pl.pallas_call(
        functools.partial(_mix_kernel, steps_per_seq=steps_per_seq),
        out_shape=(jax.ShapeDtypeStruct((t, ATTN_WIDTH), BF16),
                   jax.ShapeDtypeStruct((t, POOL_WIDTH), BF16)),
        grid=(t // TM_MIX,),
        in_specs=[
            pl.BlockSpec((KV_HEADS, nq, 2 * BLOCK), lambda i: (0, 0, 0)),
            pl.BlockSpec((TM_MIX, ATTN_WIDTH), lambda i: (i, qcol)),
            pl.BlockSpec((TM_MIX, KV_WIDTH), lambda i: (i, kcol)),
            pl.BlockSpec((TM_MIX, KV_WIDTH), lambda i: (i, kcol + 1)),
            pl.BlockSpec((BLOCK, KV_WIDTH), lambda i: (prev(i), kcol)),
            pl.BlockSpec((BLOCK, KV_WIDTH), lambda i: (prev(i), kcol + 1)),
            pl.BlockSpec((TM_MIX, half), lambda i: (i, ucol)),
            pl.BlockSpec((TM_MIX, half), lambda i: (i, ucol + 1)),
            pl.BlockSpec((BLOCK, half), lambda i: (prev(i), ucol)),
            pl.BlockSpec((BLOCK, half), lambda i: (prev(i), ucol + 1)),
            pl.BlockSpec((len(POOL_WINDOWS), POOL_GROUP_DIM, POOL_GROUP_DIM),
                         lambda i: (0, 0, 0)),
            pl.BlockSpec((1, POOL_WIDTH), lambda i: (0, 0)),
        ],
        out_specs=(pl.BlockSpec((TM_MIX, ATTN_WIDTH), lambda i: (i, 0)),
                   pl.BlockSpec((TM_MIX, POOL_WIDTH), lambda i: (i, 0))),
        scratch_shapes=[pltpu.VMEM((BLOCK + TM_MIX, KV_WIDTH), BF16),
                        pltpu.VMEM((BLOCK + TM_MIX, KV_WIDTH), BF16),
                        pltpu.VMEM((BLOCK + TM_MIX, POOL_WIDTH), BF16)],
        compiler_params=pltpu.CompilerParams(
            dimension_semantics=("arbitrary",), vmem_limit_bytes=VMEM_LIMIT),
        name="mixers",
    )(fill, proj, proj, proj, proj, proj, proj, proj, proj, proj, w_pool_b,
      pool_scale)


def _first_argmax4(v):
    m = jnp.maximum(jnp.maximum(v[0], v[1]), jnp.maximum(v[2], v[3]))
    idx = jnp.where(v[0] == m, 0, jnp.where(v[1] == m, 1,
                                            jnp.where(v[2] == m, 2, 3)))
    return m, idx


def _merge_kernel(attn_ref, pool_ref, ga_ref, gp_ref, x_ref, ada_ref,
                  wa_ref, wp_ref, wo_ref, g2_ref, wr_ref, br_ref,
                  eg_ref, eu_ref, ed_ref,
                  x2_ref, h2_ref, route_ref, counts_ref, egu_o, ed_o):
    @pl.when(pl.program_id(0) == 0)
    def _():
        counts_ref[...] = jnp.zeros_like(counts_ref)

    egu_o[:, 0:EXPERT_FF] = eg_ref[...].astype(BF16)
    egu_o[:, EXPERT_FF:2 * EXPERT_FF] = eu_ref[...].astype(BF16)
    ed_o[...] = ed_ref[...].astype(BF16)
    a = jnp.dot(attn_ref[...], wa_ref[...], preferred_element_type=F32)
    p = jnp.dot(pool_ref[...], wp_ref[...], preferred_element_type=F32)
    merged = ga_ref[...].astype(F32) * a + gp_ref[...].astype(F32) * p
    out = jnp.dot(merged.astype(BF16), wo_ref[...], preferred_element_type=F32)
    x2 = x_ref[...] + ada_ref[0, 2:3, :] * out
    x2_ref[...] = x2
    h2 = _rms_modulate(x2, g2_ref[...], ada_ref[0, 4:5, :], ada_ref[0, 3:4, :])
    h2_ref[...] = h2

    logits = jnp.dot(h2.astype(BF16), wr_ref[...], preferred_element_type=F32)
    logits = logits.T[0:ROUTER_ROWS, :] + br_ref[:, 0:1]
    tm = logits.shape[1]
    rows = [logits[r:r + 1, :] for r in range(N_GROUPS + N_EXPERTS)]
    _, gi = _first_argmax4(rows[0:N_GROUPS])
    sel = []
    for e in range(EXPERTS_PER_GROUP):
        v = rows[N_GROUPS + 3 * EXPERTS_PER_GROUP + e]
        for g in range(N_GROUPS - 2, -1, -1):
            v = jnp.where(gi == g, rows[N_GROUPS + g * EXPERTS_PER_GROUP + e], v)
        sel.append(v)
    _, i1 = _first_argmax4(sel)
    rest = [jnp.where(i1 == e, -jnp.inf, sel[e]) for e in range(EXPERTS_PER_GROUP)]
    _, i2 = _first_argmax4(rest)
    code = jnp.minimum(i1, i2) * EXPERTS_PER_GROUP + jnp.maximum(i1, i2)
    cig = jnp.where(code == 1, 0, jnp.where(code == 2, 1, jnp.where(
        code == 3, 2, jnp.where(code == 7, 3, jnp.where(code == 6, 4, 5)))))
    cls = gi * PAIRS_PER_GROUP + cig

    onehot = (lax.broadcasted_iota(jnp.int32, (ROUTER_ROWS, tm), 0) == cls
              ).astype(F32)
    src = lax.broadcasted_iota(jnp.int32, (tm, tm), 0)
    dst = lax.broadcasted_iota(jnp.int32, (tm, tm), 1)
    before = (src < dst).astype(F32).astype(BF16)
    prefix = jnp.dot(onehot.astype(BF16), before, preferred_element_type=F32)
    carry = counts_ref[:, 0:1]
    rank = jnp.sum(onehot * (prefix + carry), axis=0, keepdims=True)
    counts_ref[...] = counts_ref[...] + jnp.sum(onehot, axis=1, keepdims=True)
    route_ref[0:1, :] = cls
    route_ref[1:2, :] = rank.astype(jnp.int32)
    route_ref[2:8, :] = jnp.zeros((6, tm), jnp.int32)


def _merge(attn, pool, proj, x2d, ada3, wa_b, wp_b, wo_b, norm2_g, wr_lanes, br_col,
           w_e_gate, w_e_up, w_e_down, seq):
    t, d = x2d.shape
    n_step = t // TM_MERGE
    tiles_per_seq = seq // TM_MERGE
    flat = lambda w: w.reshape(-1, w.shape[-1])
    slab = lambda w: pl.BlockSpec((w.shape[0] // n_step, w.shape[1]),
                                  lambda i: (i, 0))
    cast = lambda w: jax.ShapeDtypeStruct(w.shape, BF16)
    eg, eu, ed = flat(w_e_gate), flat(w_e_up), flat(w_e_down)
    egu = jax.ShapeDtypeStruct((eg.shape[0], 2 * EXPERT_FF), BF16)
    const = lambda shape: pl.BlockSpec(shape, lambda i: (0,) * len(shape),
                                       pipeline_mode=pl.Buffered(1))
    return pl.pallas_call(
        _merge_kernel,
        out_shape=(jax.ShapeDtypeStruct((t, d), F32),
                   jax.ShapeDtypeStruct((t, d), F32),
                   jax.ShapeDtypeStruct((8, t), jnp.int32),
                   jax.ShapeDtypeStruct((ROUTER_ROWS, 128), F32),
                   egu, cast(ed)),
        grid=(n_step,),
        in_specs=[
            pl.BlockSpec((TM_MERGE, ATTN_WIDTH), lambda i: (i, 0)),
            pl.BlockSpec((TM_MERGE, POOL_WIDTH), lambda i: (i, 0)),
            pl.BlockSpec((TM_MERGE, d), lambda i: (i, 0)),
            pl.BlockSpec((TM_MERGE, d), lambda i: (i, 1)),
            pl.BlockSpec((TM_MERGE, d), lambda i: (i, 0)),
            pl.BlockSpec((1, N_ADA, d), lambda i: (i // tiles_per_seq, 0, 0)),
            const((ATTN_WIDTH, d)),
            const((POOL_WIDTH, d)),
            const((d, d)),
            const((1, d)),
            const((d, ROUTER_LANES)),
            const((ROUTER_ROWS, 128)),
            slab(eg), slab(eu), slab(ed),
        ],
        out_specs=(
            pl.BlockSpec((TM_MERGE, d), lambda i: (i, 0)),
            pl.BlockSpec((TM_MERGE, d), lambda i: (i, 0)),
            pl.BlockSpec((8, TM_MERGE), lambda i: (0, i)),
            pl.BlockSpec((ROUTER_ROWS, 128), lambda i: (0, 0)),
            slab(egu), slab(ed),
        ),
        compiler_params=pltpu.CompilerParams(
            dimension_semantics=("arbitrary",), vmem_limit_bytes=VMEM_LIMIT),
        name="merge",
    )(attn, pool, proj, proj, x2d, ada3, wa_b, wp_b, wo_b, norm2_g, wr_lanes, br_col,
      eg, eu, ed)


def _row_copy(src_ref, src_row, dst_ref, dst_row, sem):
    return pltpu.make_async_copy(src_ref.at[pl.ds(src_row, 1)],
                                 dst_ref.at[pl.ds(dst_row, 1)], sem)


def _sorted_row(cls_ref, rank_ref, base_ref, token):
    return base_ref[cls_ref[token]] + rank_ref[token]


def _dispatch_kernel(cls_ref, rank_ref, base_ref, h_ref, hs_ref, sem):
    i = pl.program_id(0)
    first = i * TM_DISPATCH

    for r in range(TM_DISPATCH):
        _row_copy(h_ref, r, hs_ref,
                  _sorted_row(cls_ref, rank_ref, base_ref, first + r), sem).start()
    pltpu.make_async_copy(h_ref, hs_ref.at[pl.ds(0, TM_DISPATCH)], sem).wait()


def _dispatch(cls, rank, base, h2, n_rows):
    t, d = h2.shape
    return pl.pallas_call(
        _dispatch_kernel,
        out_shape=jax.ShapeDtypeStruct((n_rows, d), F32),
        grid_spec=pltpu.PrefetchScalarGridSpec(
            num_scalar_prefetch=3,
            grid=(t // TM_DISPATCH,),
            in_specs=[pl.BlockSpec((TM_DISPATCH, d), lambda i, *_: (i, 0))],
            out_specs=pl.BlockSpec(memory_space=pl.ANY),
            scratch_shapes=[pltpu.SemaphoreType.DMA(())],
        ),
        compiler_params=pltpu.CompilerParams(
            dimension_semantics=("arbitrary",), vmem_limit_bytes=VMEM_LIMIT,
            has_side_effects=True),
        name="dispatch",
    )(cls, rank, base, h2)


def _expert_kernel(blk_ref, ea_ref, eb_ref, lane_ref, nvalid_ref, nused_ref,
                   hs_ref, wgua_ref, wda_ref, wgub_ref, wdb_ref,
                   wr_ref, br_ref, ys_ref):
    i = pl.program_id(0)
    used = i < nused_ref[0]
    nvalid = nvalid_ref[i]
    half = TM_EXPERT // 2

    def tile_rows(tm):
        rows = lax.broadcasted_iota(jnp.int32, (tm, 1), 0)
        h = jnp.where(rows < nvalid, hs_ref[0:tm, :], 0.0).astype(BF16)
        logits = jnp.dot(h, wr_ref[...], preferred_element_type=F32) + br_ref[...]
        lane = lax.broadcasted_iota(jnp.int32, logits.shape, 1)
        is_group = lane < N_GROUPS
        gmax = jnp.max(jnp.where(is_group, logits, -jnp.inf), axis=1, keepdims=True)
        gexp = jnp.where(is_group, jnp.exp(logits - gmax), 0.0)
        pick = lambda v, l: jnp.sum(jnp.where(lane == l, v, 0.0), axis=1,
                                    keepdims=True)
        p_group = pick(gexp, lane_ref[0, i]) / jnp.sum(gexp, axis=1, keepdims=True)
        la = pick(logits, lane_ref[1, i])
        lb = pick(logits, lane_ref[2, i])
        mx = jnp.maximum(la, lb)
        ea = jnp.exp(la - mx)
        eb = jnp.exp(lb - mx)
        inv = p_group / (ea + eb)

        def expert(wgu_ref, wd_ref, weight):
            gu = jnp.dot(h, wgu_ref[0], preferred_element_type=F32)
            gate, up = gu[:, 0:EXPERT_FF], gu[:, EXPERT_FF:2 * EXPERT_FF]
            act = (jax.nn.silu(gate) * up * weight).astype(BF16)
            return jnp.dot(act, wd_ref[0], preferred_element_type=F32)

        ys_ref[0:tm, :] = (expert(wgua_ref, wda_ref, ea * inv)
                           + expert(wgub_ref, wdb_ref, eb * inv))

    @pl.when(jnp.logical_not(used))
    def _():
        ys_ref[...] = jnp.zeros_like(ys_ref)

    @pl.when(jnp.logical_and(used, nvalid > half))
    def _():
        tile_rows(TM_EXPERT)

    @pl.when(jnp.logical_and(used, nvalid <= half))
    def _():
        tile_rows(half)
        ys_ref[half:TM_EXPERT, :] = jnp.zeros((TM_EXPERT - half, ys_ref.shape[1]),
                                              ys_ref.dtype)


def _experts(tile_blk, tile_ea, tile_eb, tile_lanes, tile_nvalid, n_used,
             hs, wgu_b, wd_b, wr_lanes, br_lanes):
    p_rows, d = hs.shape
    n_tiles = p_rows // TM_EXPERT
    row_map = lambda i, blk, ea, eb, ln, nv, nu: (blk[i], 0)
    wa_map = lambda i, blk, ea, eb, ln, nv, nu: (ea[i], 0, 0)
    wb_map = lambda i, blk, ea, eb, ln, nv, nu: (eb[i], 0, 0)
    const = lambda i, blk, ea, eb, ln, nv, nu: (0, 0)
    return pl.pallas_call(
        _expert_kernel,
        out_shape=jax.ShapeDtypeStruct((p_rows, d), F32),
        grid_spec=pltpu.PrefetchScalarGridSpec(
            num_scalar_prefetch=6,
            grid=(n_tiles,),
            in_specs=[
                pl.BlockSpec((TM_EXPERT, d), row_map),
                pl.BlockSpec((1, d, 2 * EXPERT_FF), wa_map),
                pl.BlockSpec((1, EXPERT_FF, d), wa_map),
                pl.BlockSpec((1, d, 2 * EXPERT_FF), wb_map),
                pl.BlockSpec((1, EXPERT_FF, d), wb_map),
                pl.BlockSpec((d, ROUTER_LANES), const),
                pl.BlockSpec((1, ROUTER_LANES), const),
            ],
            out_specs=pl.BlockSpec((TM_EXPERT, d), row_map),
        ),
        compiler_params=pltpu.CompilerParams(
            dimension_semantics=("arbitrary",), vmem_limit_bytes=VMEM_LIMIT),
        name="experts",
    )(tile_blk, tile_ea, tile_eb, tile_lanes, tile_nvalid, n_used,
      hs, wgu_b, wd_b, wgu_b, wd_b, wr_lanes, br_lanes)


def _combine_kernel(cls_ref, rank_ref, base_ref, ys_ref, x2_ref, ada_ref, g_ref,
                    o_ref, ybuf, sems):
    i = pl.program_id(0)
    n = pl.num_programs(0)

    def gather(tile, slot):
        for r in range(TM_COMBINE):
            row = _sorted_row(cls_ref, rank_ref, base_ref, tile * TM_COMBINE + r)
            _row_copy(ys_ref, row, ybuf.at[slot], r, sems.at[slot]).start()

    @pl.when(i == 0)
    def _():
        gather(0, 0)

    @pl.when(i + 1 < n)
    def _():
        gather(i + 1, (i + 1) % 2)

    slot = i % 2
    pltpu.make_async_copy(ys_ref.at[pl.ds(0, TM_COMBINE)], ybuf.at[slot],
                          sems.at[slot]).wait()

    x = x2_ref[...] + ada_ref[0, 5:6, :] * ybuf[slot]
    ms = jnp.mean(x * x, axis=-1, keepdims=True)
    o_ref[...] = x * lax.rsqrt(ms + NORM_EPS) * g_ref[...]


def _combine(cls, rank, base, ys, x2, ada3, final_g, seq):
    t, d = x2.shape
    tiles_per_seq = seq // TM_COMBINE
    return pl.pallas_call(
        _combine_kernel,
        out_shape=jax.ShapeDtypeStruct((t, d), F32),
        grid_spec=pltpu.PrefetchScalarGridSpec(
            num_scalar_prefetch=3,
            grid=(t // TM_COMBINE,),
            in_specs=[
                pl.BlockSpec(memory_space=pl.ANY),
                pl.BlockSpec((TM_COMBINE, d), lambda i, *_: (i, 0)),
                pl.BlockSpec((1, N_ADA, d),
                             lambda i, *_: (i // tiles_per_seq, 0, 0)),
                pl.BlockSpec((1, d), lambda i, *_: (0, 0)),
            ],
            out_specs=pl.BlockSpec((TM_COMBINE, d), lambda i, *_: (i, 0)),
            scratch_shapes=[pltpu.VMEM((2, TM_COMBINE, d), F32),
                            pltpu.SemaphoreType.DMA((2,))],
        ),
        compiler_params=pltpu.CompilerParams(
            dimension_semantics=("arbitrary",), vmem_limit_bytes=VMEM_LIMIT),
        name="combine",
    )(cls, rank, base, ys, x2, ada3, final_g)


def _routing_tables(route, counts, n_tiles):
    cls = route[0]
    rank = route[1]
    cnt = counts[:N_CLASSES, 0].astype(jnp.int32)
    tiles_c = (cnt + TM_EXPERT - 1) // TM_EXPERT
    tile_end = jnp.cumsum(tiles_c)
    tile_start = tile_end - tiles_c
    n_used = tile_end[-1]
    base = (tile_start * TM_EXPERT).astype(jnp.int32)

    steps = jnp.arange(n_tiles, dtype=jnp.int32)
    k = jnp.minimum(steps, n_used - 1)
    c = jnp.sum((k[:, None] >= tile_end[None, :]).astype(jnp.int32), axis=1)
    grp = c // PAIRS_PER_GROUP
    cig = c % PAIRS_PER_GROUP
    slot_a = jnp.asarray(CLASS_SLOT_A, jnp.int32)[cig]
    slot_b = jnp.asarray(CLASS_SLOT_B, jnp.int32)[cig]
    ea = grp * EXPERTS_PER_GROUP + slot_a
    eb = grp * EXPERTS_PER_GROUP + slot_b
    lanes = jnp.stack([grp, N_GROUPS + ea, N_GROUPS + eb]).astype(jnp.int32)
    nvalid = jnp.clip(cnt[c] - (k - tile_start[c]) * TM_EXPERT, 0, TM_EXPERT)
    return (cls, rank, base, steps, ea.astype(jnp.int32),
            eb.astype(jnp.int32), lanes, nvalid.astype(jnp.int32),
            n_used.reshape(1).astype(jnp.int32))


def kernel(x, c, w_ada, b_ada, norm1_g, w_in, sinks, w_pool, pool_scale,
           w_attn_branch, w_pool_branch, w_out, norm2_g, w_router_group,
           b_router_group, w_router_expert, b_router_expert, w_e_gate, w_e_up,
           w_e_down, final_g):
    b, s, d = x.shape
    t = b * s
    assert w_ada.shape[0] == 1, "single-layer block"
    assert d == D_MODEL and s % TM_PROJ == 0
    l = 0
    n_tiles = t // TM_EXPERT + N_CLASSES
    x2d = x.reshape(t, d)
    ada3 = _ada(c, w_ada[l], b_ada[l]).reshape(b, N_ADA, d)

    proj, wa_b, wp_b, wo_b = _in_proj(
        x2d, ada3, norm1_g[l].reshape(1, d), w_in[l], w_attn_branch[l],
        w_pool_branch[l], w_out[l], s)
    attn, pool = _mixers(proj, sinks[l], w_pool[l].astype(BF16),
                         pool_scale[l].reshape(1, -1), s)

    w_r = jnp.concatenate([w_router_group[l], w_router_expert[l]], axis=1)
    b_r = jnp.concatenate([b_router_group[l], b_router_expert[l]])
    n_r = N_GROUPS + N_EXPERTS
    br_col = jnp.broadcast_to(
        jnp.pad(b_r, (0, ROUTER_ROWS - n_r))[:, None], (ROUTER_ROWS, 128))
    wr_lanes = jnp.pad(w_r, ((0, 0), (0, ROUTER_LANES - n_r))).astype(BF16)
    br_lanes = jnp.pad(b_r, (0, ROUTER_LANES - n_r)).reshape(1, ROUTER_LANES)

    x2, h2, route, counts, egu_b, ed_b = _merge(
        attn, pool, proj, x2d, ada3, wa_b, wp_b, wo_b, norm2_g[l].reshape(1, d),
        wr_lanes, br_col, w_e_gate[l], w_e_up[l], w_e_down[l], s)

    cls, rank, base, tile_blk, tile_ea, tile_eb, tile_lanes, tile_nvalid, n_used = (
        _routing_tables(route, counts, n_tiles))
    hs = _dispatch(cls, rank, base, h2, n_tiles * TM_EXPERT)
    ys = _experts(tile_blk, tile_ea, tile_eb, tile_lanes, tile_nvalid, n_used,
                  hs, egu_b.reshape(N_EXPERTS, d, 2 * EXPERT_FF),
                  ed_b.reshape(w_e_down[l].shape), wr_lanes, br_lanes)
    out = _combine(cls, rank, base, ys, x2, ada3, final_g.reshape(1, d), s)
    return out.reshape(b, s, d)
```

```python
import functools

import jax
import jax.numpy as jnp
from jax import lax
from jax.experimental import pallas as pl
from jax.experimental.pallas import tpu as pltpu

F32 = jnp.float32
BF16 = jnp.bfloat16

D_MODEL = 2048
HEAD_DIM = 64
ATTN_HEADS = 16
KV_HEADS = 4
Q_PER_KV = ATTN_HEADS // KV_HEADS
ATTN_WIDTH = ATTN_HEADS * HEAD_DIM
KV_WIDTH = KV_HEADS * HEAD_DIM
BLOCK = 128
POOL_WINDOWS = (2, 4, 8, 16)
POOL_WIDTH = 1024
POOL_GROUP_DIM = 256
QKVU_WIDTH = ATTN_WIDTH + 2 * KV_WIDTH + POOL_WIDTH
GATE_WIDTH = 2 * D_MODEL
PROJ_WIDTH = GATE_WIDTH + QKVU_WIDTH
Q_COL = GATE_WIDTH
K_COL = Q_COL + ATTN_WIDTH
V_COL = K_COL + KV_WIDTH
U_COL = V_COL + KV_WIDTH
N_GROUPS = 4
EXPERTS_PER_GROUP = 4
N_EXPERTS = 16
EXPERT_FF = 512
N_ADA = 6
NORM_EPS = 1e-6
MASK_VALUE = -1e30

PAIRS_PER_GROUP = 6
N_CLASSES = N_GROUPS * PAIRS_PER_GROUP
CLASS_SLOT_A = (0, 0, 0, 1, 1, 3)
CLASS_SLOT_B = (1, 2, 3, 3, 2, 2)
ROUTER_ROWS = 32
ROUTER_LANES = 128

TM_PROJ = 2048
TN_PROJ = 512
NORM_ROWS = 256
NORM_CHUNKS = TM_PROJ // NORM_ROWS
TM_MIX = 1024
TM_MERGE = 256
TM_DISPATCH = 512
TM_EXPERT = 256
TM_COMBINE = 512
TN_ADA = 1024

VMEM_LIMIT = 52 * 1024 * 1024


def _rms_modulate(x, g, scale, shift):
    ms = jnp.mean(x * x, axis=-1, keepdims=True)
    return (x * lax.rsqrt(ms + NORM_EPS)) * (g * (1.0 + scale)) + shift


def _ada_kernel(cb_ref, w_ref, b_ref, o_ref):
    nb = cb_ref.shape[0]
    d = w_ref.shape[0]
    nchunk = w_ref.shape[1] // 128

    def body(kb, accs):
        k0 = pl.multiple_of(kb * 8, 8)
        new = list(accs)
        cbs = [cb_ref[b, pl.ds(k0, 8), :] for b in range(nb)]
        for j in range(nchunk):
            w = w_ref[pl.ds(k0, 8), j * 128:(j + 1) * 128]
            for b in range(nb):
                new[b * nchunk + j] = new[b * nchunk + j] + w * cbs[b]
        return tuple(new)

    init = tuple(jnp.zeros((8, 128), F32) for _ in range(nb * nchunk))
    accs = lax.fori_loop(0, d // 8, body, init, unroll=2)
    for b in range(nb):
        for j in range(nchunk):
            o_ref[b:b + 1, j * 128:(j + 1) * 128] = (
                jnp.sum(accs[b * nchunk + j], axis=0, keepdims=True)
                + b_ref[:, j * 128:(j + 1) * 128])


def _ada(c, w_ada, b_ada):
    nb, d = c.shape
    n = w_ada.shape[1]
    cb = jnp.broadcast_to(c[:, :, None], (nb, d, 128))
    return pl.pallas_call(
        _ada_kernel,
        out_shape=jax.ShapeDtypeStruct((nb, n), F32),
        grid=(n // TN_ADA,),
        in_specs=[
            pl.BlockSpec((nb, d, 128), lambda j: (0, 0, 0)),
            pl.BlockSpec((d, TN_ADA), lambda j: (0, j)),
            pl.BlockSpec((1, TN_ADA), lambda j: (0, j)),
        ],
        out_specs=pl.BlockSpec((nb, TN_ADA), lambda j: (0, j)),
        compiler_params=pltpu.CompilerParams(
            dimension_semantics=("arbitrary",), vmem_limit_bytes=VMEM_LIMIT),
        name="ada",
    )(cb, w_ada, b_ada.reshape(1, n))


def _in_proj_kernel(x_ref, ada_ref, g_ref, w_ref, wa_ref, wp_ref, wo_ref,
                    proj_ref, wa_o, wp_o, wo_o, h_even, h_odd, *, n_row):
    i = pl.program_id(0)
    j = pl.program_id(1)

    def norm_chunk(h_ref):
        r0 = pl.multiple_of(jnp.minimum(j, NORM_CHUNKS - 1) * NORM_ROWS, NORM_ROWS)
        h = _rms_modulate(x_ref[...], g_ref[...],
                          ada_ref[0, 1:2, :], ada_ref[0, 0:1, :])
        h_ref[pl.ds(r0, NORM_ROWS), :] = h.astype(BF16)

    def step(h_write, h_read):
        norm_chunk(h_write)
        acc = jnp.dot(h_read[...], w_ref[...].astype(BF16),
                      preferred_element_type=F32)
        is_gate = j < GATE_WIDTH // TN_PROJ
        gate = 0.5 * jnp.tanh(0.5 * acc) + 0.5
        proj_ref[...] = jnp.where(is_gate, gate, acc).astype(BF16)

    @pl.when(jnp.logical_and(j == 0, i < n_row))
    def _():
        wa_o[...] = wa_ref[...].astype(BF16)
        wp_o[...] = wp_ref[...].astype(BF16)
        wo_o[...] = wo_ref[...].astype(BF16)

    @pl.when(jnp.logical_and(i == 0, j < NORM_CHUNKS))
    def _():
        norm_chunk(h_even)

    @pl.when(i % 2 == 1)
    def _():
        step(h_odd, h_even)

    @pl.when(jnp.logical_and(i > 0, i % 2 == 0))
    def _():
        step(h_even, h_odd)


def _in_proj(x2d, ada3, norm_g, w_in, wa, wp, wo, seq):
    t, d = x2d.shape
    n_all = PROJ_WIDTH // TN_PROJ
    n_gate = GATE_WIDTH // TN_PROJ
    n_row = t // TM_PROJ
    tiles_per_seq = seq // TM_PROJ
    assert NORM_CHUNKS <= n_all
    norm_tile = lambda i: jnp.minimum(i, n_row - 1)
    out_tile = lambda i: jnp.maximum(i - 1, 0)
    w_tile = lambda i, j: (
        0, (jnp.where(i == 0, 0, j) + (n_all - n_gate)) % n_all)
    slab = lambda w: pl.BlockSpec((w.shape[0] // n_row, w.shape[1]),
                                  lambda i, j: (norm_tile(i), 0))
    cast = lambda w: jax.ShapeDtypeStruct(w.shape, BF16)
    return pl.pallas_call(
        functools.partial(_in_proj_kernel, n_row=n_row),
        out_shape=(jax.ShapeDtypeStruct((t, PROJ_WIDTH), BF16),
                   cast(wa), cast(wp), cast(wo)),
        grid=(n_row + 1, n_all),
        in_specs=[
            pl.BlockSpec((NORM_ROWS, d), lambda i, j: (
                norm_tile(i) * NORM_CHUNKS + jnp.minimum(j, NORM_CHUNKS - 1), 0)),
            pl.BlockSpec((1, N_ADA, d),
                         lambda i, j: (norm_tile(i) // tiles_per_seq, 0, 0)),
            pl.BlockSpec((1, d), lambda i, j: (0, 0)),
            pl.BlockSpec((d, TN_PROJ), w_tile),
            slab(wa), slab(wp), slab(wo),
        ],
        out_specs=(
            pl.BlockSpec((TM_PROJ, TN_PROJ),
                         lambda i, j: (out_tile(i), jnp.where(i == 0, 0, j))),
            slab(wa), slab(wp), slab(wo),
        ),
        scratch_shapes=[pltpu.VMEM((TM_PROJ, d), BF16),
                        pltpu.VMEM((TM_PROJ, d), BF16)],
        compiler_params=pltpu.CompilerParams(
            dimension_semantics=("arbitrary", "arbitrary"),
            vmem_limit_bytes=VMEM_LIMIT),
        name="in_proj",
    )(x2d, ada3, norm_g, w_in, wa, wp, wo)


def _mix_kernel(fill_ref, q_ref, kc_ref, vc_ref, kp_ref, vp_ref,
                ulo_ref, uhi_ref, plo_ref, phi_ref, wp_ref, ps_ref,
                attn_ref, pool_ref, kfull, vfull, ufull, *, steps_per_seq):
    i = pl.program_id(0)
    first_step = (i % steps_per_seq) == 0
    kfull[0:BLOCK, :] = kp_ref[...]
    kfull[BLOCK:BLOCK + TM_MIX, :] = kc_ref[...]
    vfull[0:BLOCK, :] = vp_ref[...]
    vfull[BLOCK:BLOCK + TM_MIX, :] = vc_ref[...]
    half = POOL_WIDTH // 2
    zeros = jnp.zeros((BLOCK, half), BF16)
    ufull[0:BLOCK, 0:half] = jnp.where(first_step, zeros, plo_ref[...])
    ufull[0:BLOCK, half:POOL_WIDTH] = jnp.where(first_step, zeros, phi_ref[...])
    ufull[BLOCK:BLOCK + TM_MIX, 0:half] = ulo_ref[...]
    ufull[BLOCK:BLOCK + TM_MIX, half:POOL_WIDTH] = uhi_ref[...]
    pos0 = (i % steps_per_seq) * TM_MIX
    nq = Q_PER_KV * BLOCK
    prow = lax.broadcasted_iota(jnp.int32, (BLOCK, 2 * BLOCK), 0)
    pcol = lax.broadcasted_iota(jnp.int32, (BLOCK, 2 * BLOCK), 1)
    rel = prow + BLOCK - pcol
    bands = [((rel >= 0) & (rel < w)).astype(F32).astype(BF16)
             for w in POOL_WINDOWS]

    def sb_body(sb, carry):
        r0 = pl.multiple_of(sb * BLOCK, BLOCK)
        row = lax.broadcasted_iota(jnp.int32, (nq, 2 * BLOCK), 0) & (BLOCK - 1)
        col = lax.broadcasted_iota(jnp.int32, (nq, 2 * BLOCK), 1)
        dist = col - row
        lo = jnp.where(jnp.logical_and(first_step, sb == 0), BLOCK, 0)
        valid = (dist > 0) & (dist <= BLOCK) & (col >= lo)
        key0 = lax.broadcasted_iota(jnp.int32, (2 * BLOCK, HEAD_DIM), 0) == 0
        pos = pos0 + sb * BLOCK + lax.broadcasted_iota(jnp.int32, (BLOCK, 1), 0)
        for h in range(KV_HEADS):
            kb = kfull[pl.ds(r0, 2 * BLOCK), h * HEAD_DIM:(h + 1) * HEAD_DIM]
            vb = vfull[pl.ds(r0, 2 * BLOCK), h * HEAD_DIM:(h + 1) * HEAD_DIM]
            vb = jnp.where(key0, jnp.zeros_like(vb), vb)
            qg = jnp.concatenate(
                [q_ref[pl.ds(r0, BLOCK),
                       (Q_PER_KV * h + g) * HEAD_DIM:(Q_PER_KV * h + g + 1) * HEAD_DIM]
                 for g in range(Q_PER_KV)], axis=0)
            qg = qg * jnp.asarray(HEAD_DIM ** -0.5, BF16)
            logits = lax.dot_general(qg, kb, (((1,), (1,)), ((), ())),
                                     preferred_element_type=F32)
            logits = jnp.where(valid, logits, fill_ref[h])
            m = jnp.max(logits, axis=1, keepdims=True)
            p = jnp.exp(logits - m)
            denom = jnp.sum(p, axis=1, keepdims=True)
            o = jnp.dot(p.astype(BF16), vb, preferred_element_type=F32)
            o = (o * (1.0 / denom)).astype(BF16)
            attn_ref[pl.ds(r0, BLOCK),
                     h * Q_PER_KV * HEAD_DIM:(h + 1) * Q_PER_KV * HEAD_DIM] = (
                jnp.concatenate([o[g * BLOCK:(g + 1) * BLOCK, :]
                                 for g in range(Q_PER_KV)], axis=1))

            w = POOL_WINDOWS[h]
            cols = slice(h * POOL_GROUP_DIM, (h + 1) * POOL_GROUP_DIM)
            band = ufull[pl.ds(r0, 2 * BLOCK), cols]
            ssum = jnp.dot(bands[h], band, preferred_element_type=F32)
            xg = ufull[pl.ds(r0 + BLOCK, BLOCK), cols].astype(F32)
            count = jnp.minimum(pos + 1, w).astype(F32)
            pooled = ssum / count - xg
            mixed = jnp.dot(pooled.astype(BF16), wp_ref[h],
                            preferred_element_type=F32)
            pool_ref[pl.ds(r0, BLOCK), cols] = (
                mixed * ps_ref[:, cols]).astype(BF16)
        return carry

    lax.fori_loop(0, TM_MIX // BLOCK, sb_body, 0)


def _mixers(proj, sinks, w_pool_b, pool_scale, seq):
    t = proj.shape[0]
    assert len(POOL_WINDOWS) == KV_HEADS
    steps_per_seq = seq // TM_MIX
    sub = TM_MIX // BLOCK
    half = POOL_WIDTH // 2
    qcol = Q_COL // ATTN_WIDTH
    kcol = K_COL // KV_WIDTH
    ucol = U_COL // half
    prev = lambda i: jnp.maximum(i * sub - 1, 0)
    nq = Q_PER_KV * BLOCK
    sink_rows = jnp.repeat(sinks.astype(F32).reshape(KV_HEADS, Q_PER_KV), BLOCK,
                           axis=1)
    fill = jnp.full((KV_HEADS, nq, 2 * BLOCK), MASK_VALUE, F32)
    fill = fill.at[:, :, 0].set(sink_rows)
    return pl.pallas_call(
        functools.partial(_mix_kernel, steps_per_seq=steps_per_seq),
        out_shape=(jax.ShapeDtypeStruct((t, ATTN_WIDTH), BF16),
                   jax.ShapeDtypeStruct((t, POOL_WIDTH), BF16)),
        grid=(t // TM_MIX,),
        in_specs=[
            pl.BlockSpec((KV_HEADS, nq, 2 * BLOCK), lambda i: (0, 0, 0)),
            pl.BlockSpec((TM_MIX, ATTN_WIDTH), lambda i: (i, qcol)),
            pl.BlockSpec((TM_MIX, KV_WIDTH), lambda i: (i, kcol)),
            pl.BlockSpec((TM_MIX, KV_WIDTH), lambda i: (i, kcol + 1)),
            pl.BlockSpec((BLOCK, KV_WIDTH), lambda i: (prev(i), kcol)),
            pl.BlockSpec((BLOCK, KV_WIDTH), lambda i: (prev(i), kcol + 1)),
            pl.BlockSpec((TM_MIX, half), lambda i: (i, ucol)),
            pl.BlockSpec((TM_MIX, half), lambda i: (i, ucol + 1)),
            pl.BlockSpec((BLOCK, half), lambda i: (prev(i), ucol)),
            pl.BlockSpec((BLOCK, half), lambda i: (prev(i), ucol + 1)),
            pl.BlockSpec((len(POOL_WINDOWS), POOL_GROUP_DIM, POOL_GROUP_DIM),
                         lambda i: (0, 0, 0)),
            pl.BlockSpec((1, POOL_WIDTH), lambda i: (0, 0)),
        ],
        out_specs=(pl.BlockSpec((TM_MIX, ATTN_WIDTH), lambda i: (i, 0)),
                   pl.BlockSpec((TM_MIX, POOL_WIDTH), lambda i: (i, 0))),
        scratch_shapes=[pltpu.VMEM((BLOCK + TM_MIX, KV_WIDTH), BF16),
                        pltpu.VMEM((BLOCK + TM_MIX, KV_WIDTH), BF16),
                        pltpu.VMEM((BLOCK + TM_MIX, POOL_WIDTH), BF16)],
        compiler_params=pltpu.CompilerParams(
            dimension_semantics=("arbitrary",), vmem_limit_bytes=VMEM_LIMIT),
        name="mixers",
    )(fill, proj, proj, proj, proj, proj, proj, proj, proj, proj, w_pool_b,
      pool_scale)


def _first_argmax4(v):
    m = jnp.maximum(jnp.maximum(v[0], v[1]), jnp.maximum(v[2], v[3]))
    idx = jnp.where(v[0] == m, 0, jnp.where(v[1] == m, 1,
                                            jnp.where(v[2] == m, 2, 3)))
    return m, idx


def _merge_kernel(attn_ref, pool_ref, ga_ref, gp_ref, x_ref, ada_ref,
                  wa_ref, wp_ref, wo_ref, g2_ref, wr_ref, br_ref,
                  eg_ref, eu_ref, ed_ref,
                  x2_ref, h2_ref, route_ref, counts_ref, egu_o, ed_o):
    @pl.when(pl.program_id(0) == 0)
    def _():
        counts_ref[...] = jnp.zeros_like(counts_ref)

    egu_o[:, 0:EXPERT_FF] = eg_ref[...].astype(BF16)
    egu_o[:, EXPERT_FF:2 * EXPERT_FF] = eu_ref[...].astype(BF16)
    ed_o[...] = ed_ref[...].astype(BF16)
    a = jnp.dot(attn_ref[...], wa_ref[...], preferred_element_type=F32)
    p = jnp.dot(pool_ref[...], wp_ref[...], preferred_element_type=F32)
    merged = ga_ref[...].astype(F32) * a + gp_ref[...].astype(F32) * p
    out = jnp.dot(merged.astype(BF16), wo_ref[...], preferred_element_type=F32)
    x2 = x_ref[...] + ada_ref[0, 2:3, :] * out
    x2_ref[...] = x2
    h2 = _rms_modulate(x2, g2_ref[...], ada_ref[0, 4:5, :], ada_ref[0, 3:4, :])
    h2_ref[...] = h2

    logits = jnp.dot(h2.astype(BF16), wr_ref[...], preferred_element_type=F32)
    logits = logits.T[0:ROUTER_ROWS, :] + br_ref[:, 0:1]
    tm = logits.shape[1]
    rows = [logits[r:r + 1, :] for r in range(N_GROUPS + N_EXPERTS)]
    _, gi = _first_argmax4(rows[0:N_GROUPS])
    sel = []
    for e in range(EXPERTS_PER_GROUP):
        v = rows[N_GROUPS + 3 * EXPERTS_PER_GROUP + e]
        for g in range(N_GROUPS - 2, -1, -1):
            v = jnp.where(gi == g, rows[N_GROUPS + g * EXPERTS_PER_GROUP + e], v)
        sel.append(v)
    _, i1 = _first_argmax4(sel)
    rest = [jnp.where(i1 == e, -jnp.inf, sel[e]) for e in range(EXPERTS_PER_GROUP)]
    _, i2 = _first_argmax4(rest)
    code = jnp.minimum(i1, i2) * EXPERTS_PER_GROUP + jnp.maximum(i1, i2)
    cig = jnp.where(code == 1, 0, jnp.where(code == 2, 1, jnp.where(
        code == 3, 2, jnp.where(code == 7, 3, jnp.where(code == 6, 4, 5)))))
    cls = gi * PAIRS_PER_GROUP + cig

    onehot = (lax.broadcasted_iota(jnp.int32, (ROUTER_ROWS, tm), 0) == cls
              ).astype(F32)
    src = lax.broadcasted_iota(jnp.int32, (tm, tm), 0)
    dst = lax.broadcasted_iota(jnp.int32, (tm, tm), 1)
    before = (src < dst).astype(F32).astype(BF16)
    prefix = jnp.dot(onehot.astype(BF16), before, preferred_element_type=F32)
    carry = counts_ref[:, 0:1]
    rank = jnp.sum(onehot * (prefix + carry), axis=0, keepdims=True)
    counts_ref[...] = counts_ref[...] + jnp.sum(onehot, axis=1, keepdims=True)
    route_ref[0:1, :] = cls
    route_ref[1:2, :] = rank.astype(jnp.int32)
    route_ref[2:8, :] = jnp.zeros((6, tm), jnp.int32)


def _merge(attn, pool, proj, x2d, ada3, wa_b, wp_b, wo_b, norm2_g, wr_lanes, br_col,
           w_e_gate, w_e_up, w_e_down, seq):
    t, d = x2d.shape
    n_step = t // TM_MERGE
    tiles_per_seq = seq // TM_MERGE
    flat = lambda w: w.reshape(-1, w.shape[-1])
    slab = lambda w: pl.BlockSpec((w.shape[0] // n_step, w.shape[1]),
                                  lambda i: (i, 0))
    cast = lambda w: jax.ShapeDtypeStruct(w.shape, BF16)
    eg, eu, ed = flat(w_e_gate), flat(w_e_up), flat(w_e_down)
    egu = jax.ShapeDtypeStruct((eg.shape[0], 2 * EXPERT_FF), BF16)
    const = lambda shape: pl.BlockSpec(shape, lambda i: (0,) * len(shape),
                                       pipeline_mode=pl.Buffered(1))
    return pl.pallas_call(
        _merge_kernel,
        out_shape=(jax.ShapeDtypeStruct((t, d), F32),
                   jax.ShapeDtypeStruct((t, d), F32),
                   jax.ShapeDtypeStruct((8, t), jnp.int32),
                   jax.ShapeDtypeStruct((ROUTER_ROWS, 128), F32),
                   egu, cast(ed)),
        grid=(n_step,),
        in_specs=[
            pl.BlockSpec((TM_MERGE, ATTN_WIDTH), lambda i: (i, 0)),
            pl.BlockSpec((TM_MERGE, POOL_WIDTH), lambda i: (i, 0)),
            pl.BlockSpec((TM_MERGE, d), lambda i: (i, 0)),
            pl.BlockSpec((TM_MERGE, d), lambda i: (i, 1)),
            pl.BlockSpec((TM_MERGE, d), lambda i: (i, 0)),
            pl.BlockSpec((1, N_ADA, d), lambda i: (i // tiles_per_seq, 0, 0)),
            const((ATTN_WIDTH, d)),
            const((POOL_WIDTH, d)),
            const((d, d)),
            const((1, d)),
            const((d, ROUTER_LANES)),
            const((ROUTER_ROWS, 128)),
            slab(eg), slab(eu), slab(ed),
        ],
        out_specs=(
            pl.BlockSpec((TM_MERGE, d), lambda i: (i, 0)),
            pl.BlockSpec((TM_MERGE, d), lambda i: (i, 0)),
            pl.BlockSpec((8, TM_MERGE), lambda i: (0, i)),
            pl.BlockSpec((ROUTER_ROWS, 128), lambda i: (0, 0)),
            slab(egu), slab(ed),
        ),
        compiler_params=pltpu.CompilerParams(
            dimension_semantics=("arbitrary",), vmem_limit_bytes=VMEM_LIMIT),
        name="merge",
    )(attn, pool, proj, proj, x2d, ada3, wa_b, wp_b, wo_b, norm2_g, wr_lanes, br_col,
      eg, eu, ed)


def _row_copy(src_ref, src_row, dst_ref, dst_row, sem):
    return pltpu.make_async_copy(src_ref.at[pl.ds(src_row, 1)],
                                 dst_ref.at[pl.ds(dst_row, 1)], sem)


def _sorted_row(cls_ref, rank_ref, base_ref, token):
    return base_ref[cls_ref[token]] + rank_ref[token]


def _dispatch_kernel(cls_ref, rank_ref, base_ref, h_ref, hs_ref, sem):
    i = pl.program_id(0)
    first = i * TM_DISPATCH

    for r in range(TM_DISPATCH):
        _row_copy(h_ref, r, hs_ref,
                  _sorted_row(cls_ref, rank_ref, base_ref, first + r), sem).start()
    pltpu.make_async_copy(h_ref, hs_ref.at[pl.ds(0, TM_DISPATCH)], sem).wait()


def _dispatch(cls, rank, base, h2, n_rows):
    t, d = h2.shape
    return pl.pallas_call(
        _dispatch_kernel,
        out_shape=jax.ShapeDtypeStruct((n_rows, d), F32),
        grid_spec=pltpu.PrefetchScalarGridSpec(
            num_scalar_prefetch=3,
            grid=(t // TM_DISPATCH,),
            in_specs=[pl.BlockSpec((TM_DISPATCH, d), lambda i, *_: (i, 0))],
            out_specs=pl.BlockSpec(memory_space=pl.ANY),
            scratch_shapes=[pltpu.SemaphoreType.DMA(())],
        ),
        compiler_params=pltpu.CompilerParams(
            dimension_semantics=("arbitrary",), vmem_limit_bytes=VMEM_LIMIT,
            has_side_effects=True),
        name="dispatch",
    )(cls, rank, base, h2)


def _expert_kernel(blk_ref, ea_ref, eb_ref, lane_ref, lead_ref, nused_ref,
                   hs_ref, wgua_ref, wda_ref, wgub_ref, wdb_ref,
                   wr_ref, br_ref, ys_ref):
    i = pl.program_id(0)
    used = i < nused_ref[0]
    lead = lead_ref[i]
    half = TM_EXPERT // 2

    def tile_rows(r0):
        tm = TM_EXPERT - r0
        rows = r0 + lax.broadcasted_iota(jnp.int32, (tm, 1), 0)
        h = jnp.where(rows >= lead, hs_ref[r0:TM_EXPERT, :], 0.0).astype(BF16)
        logits = jnp.dot(h, wr_ref[...], preferred_element_type=F32) + br_ref[...]
        lane = lax.broadcasted_iota(jnp.int32, logits.shape, 1)
        is_group = lane < N_GROUPS
        gmax = jnp.max(jnp.where(is_group, logits, -jnp.inf), axis=1, keepdims=True)
        gexp = jnp.where(is_group, jnp.exp(logits - gmax), 0.0)
        pick = lambda v, l: jnp.sum(jnp.where(lane == l, v, 0.0), axis=1,
                                    keepdims=True)
        p_group = pick(gexp, lane_ref[0, i]) / jnp.sum(gexp, axis=1, keepdims=True)
        la = pick(logits, lane_ref[1, i])
        lb = pick(logits, lane_ref[2, i])
        mx = jnp.maximum(la, lb)
        ea = jnp.exp(la - mx)
        eb = jnp.exp(lb - mx)
        inv = p_group / (ea + eb)

        def expert(wgu_ref, wd_ref, weight):
            gu = jnp.dot(h, wgu_ref[0], preferred_element_type=F32)
            gate, up = gu[:, 0:EXPERT_FF], gu[:, EXPERT_FF:2 * EXPERT_FF]
            act = (jax.nn.silu(gate) * up * weight).astype(BF16)
            return jnp.dot(act, wd_ref[0], preferred_element_type=F32)

        ys_ref[r0:TM_EXPERT, :] = (expert(wgua_ref, wda_ref, ea * inv)
                                   + expert(wgub_ref, wdb_ref, eb * inv))

    @pl.when(jnp.logical_not(used))
    def _():
        ys_ref[...] = jnp.zeros_like(ys_ref)

    @pl.when(jnp.logical_and(used, lead < half))
    def _():
        tile_rows(0)

    @pl.when(jnp.logical_and(used, lead >= half))
    def _():
        ys_ref[0:half, :] = jnp.zeros((half, ys_ref.shape[1]), ys_ref.dtype)
        tile_rows(half)


def _experts(tile_blk, tile_ea, tile_eb, tile_lanes, tile_lead, n_used,
             hs, wgu_b, wd_b, wr_lanes, br_lanes):
    p_rows, d = hs.shape
    n_tiles = p_rows // TM_EXPERT
    row_map = lambda i, blk, ea, eb, ln, nv, nu: (blk[i], 0)
    wa_map = lambda i, blk, ea, eb, ln, nv, nu: (ea[i], 0, 0)
    wb_map = lambda i, blk, ea, eb, ln, nv, nu: (eb[i], 0, 0)
    const = lambda i, blk, ea, eb, ln, nv, nu: (0, 0)
    return pl.pallas_call(
        _expert_kernel,
        out_shape=jax.ShapeDtypeStruct((p_rows, d), F32),
        grid_spec=pltpu.PrefetchScalarGridSpec(
            num_scalar_prefetch=6,
            grid=(n_tiles,),
            in_specs=[
                pl.BlockSpec((TM_EXPERT, d), row_map),
                pl.BlockSpec((1, d, 2 * EXPERT_FF), wa_map),
                pl.BlockSpec((1, EXPERT_FF, d), wa_map),
                pl.BlockSpec((1, d, 2 * EXPERT_FF), wb_map),
                pl.BlockSpec((1, EXPERT_FF, d), wb_map),
                pl.BlockSpec((d, ROUTER_LANES), const),
                pl.BlockSpec((1, ROUTER_LANES), const),
            ],
            out_specs=pl.BlockSpec((TM_EXPERT, d), row_map),
        ),
        compiler_params=pltpu.CompilerParams(
            dimension_semantics=("arbitrary",), vmem_limit_bytes=VMEM_LIMIT),
        name="experts",
    )(tile_blk, tile_ea, tile_eb, tile_lanes, tile_lead, n_used,
      hs, wgu_b, wd_b, wgu_b, wd_b, wr_lanes, br_lanes)


def _combine_kernel(cls_ref, rank_ref, base_ref, ys_ref, x2_ref, ada_ref, g_ref,
                    o_ref, ybuf, sems):
    i = pl.program_id(0)
    n = pl.num_programs(0)

    def gather(tile, slot):
        for r in range(TM_COMBINE):
            row = _sorted_row(cls_ref, rank_ref, base_ref, tile * TM_COMBINE + r)
            _row_copy(ys_ref, row, ybuf.at[slot], r, sems.at[slot]).start()

    @pl.when(i == 0)
    def _():
        gather(0, 0)

    @pl.when(i + 1 < n)
    def _():
        gather(i + 1, (i + 1) % 2)

    slot = i % 2
    pltpu.make_async_copy(ys_ref.at[pl.ds(0, TM_COMBINE)], ybuf.at[slot],
                          sems.at[slot]).wait()

    x = x2_ref[...] + ada_ref[0, 5:6, :] * ybuf[slot]
    ms = jnp.mean(x * x, axis=-1, keepdims=True)
    o_ref[...] = x * lax.rsqrt(ms + NORM_EPS) * g_ref[...]


def _combine(cls, rank, base, ys, x2, ada3, final_g, seq):
    t, d = x2.shape
    tiles_per_seq = seq // TM_COMBINE
    return pl.pallas_call(
        _combine_kernel,
        out_shape=jax.ShapeDtypeStruct((t, d), F32),
        grid_spec=pltpu.PrefetchScalarGridSpec(
            num_scalar_prefetch=3,
            grid=(t // TM_COMBINE,),
            in_specs=[
                pl.BlockSpec(memory_space=pl.ANY),
                pl.BlockSpec((TM_COMBINE, d), lambda i, *_: (i, 0)),
                pl.BlockSpec((1, N_ADA, d),
                             lambda i, *_: (i // tiles_per_seq, 0, 0)),
                pl.BlockSpec((1, d), lambda i, *_: (0, 0)),
            ],
            out_specs=pl.BlockSpec((TM_COMBINE, d), lambda i, *_: (i, 0)),
            scratch_shapes=[pltpu.VMEM((2, TM_COMBINE, d), F32),
                            pltpu.SemaphoreType.DMA((2,))],
        ),
        compiler_params=pltpu.CompilerParams(
            dimension_semantics=("arbitrary",), vmem_limit_bytes=VMEM_LIMIT),
        name="combine",
    )(cls, rank, base, ys, x2, ada3, final_g)


def _routing_tables(route, counts, n_tiles):
    cls = route[0]
    rank = route[1]
    cnt = counts[:N_CLASSES, 0].astype(jnp.int32)
    tiles_c = (cnt + TM_EXPERT - 1) // TM_EXPERT
    tile_end = jnp.cumsum(tiles_c)
    tile_start = tile_end - tiles_c
    n_used = tile_end[-1]
    lead_c = tiles_c * TM_EXPERT - cnt
    base = (tile_start * TM_EXPERT + lead_c).astype(jnp.int32)

    steps = jnp.arange(n_tiles, dtype=jnp.int32)
    k = jnp.minimum(steps, n_used - 1)
    c = jnp.sum((k[:, None] >= tile_end[None, :]).astype(jnp.int32), axis=1)
    grp = c // PAIRS_PER_GROUP
    cig = c % PAIRS_PER_GROUP
    slot_a = jnp.asarray(CLASS_SLOT_A, jnp.int32)[cig]
    slot_b = jnp.asarray(CLASS_SLOT_B, jnp.int32)[cig]
    ea = grp * EXPERTS_PER_GROUP + slot_a
    eb = grp * EXPERTS_PER_GROUP + slot_b
    lanes = jnp.stack([grp, N_GROUPS + ea, N_GROUPS + eb]).astype(jnp.int32)
    lead = jnp.where(k == tile_start[c], lead_c[c], 0)
    return (cls, rank, base, steps, ea.astype(jnp.int32),
            eb.astype(jnp.int32), lanes, lead.astype(jnp.int32),
            n_used.reshape(1).astype(jnp.int32))


def kernel(x, c, w_ada, b_ada, norm1_g, w_in, sinks, w_pool, pool_scale,
           w_attn_branch, w_pool_branch, w_out, norm2_g, w_router_group,
           b_router_group, w_router_expert, b_router_expert, w_e_gate, w_e_up,
           w_e_down, final_g):
    b, s, d = x.shape
    t = b * s
    assert w_ada.shape[0] == 1, "single-layer block"
    assert d == D_MODEL and s % TM_PROJ == 0
    l = 0
    n_tiles = t // TM_EXPERT + N_CLASSES
    x2d = x.reshape(t, d)
    ada3 = _ada(c, w_ada[l], b_ada[l]).reshape(b, N_ADA, d)

    proj, wa_b, wp_b, wo_b = _in_proj(
        x2d, ada3, norm1_g[l].reshape(1, d), w_in[l], w_attn_branch[l],
        w_pool_branch[l], w_out[l], s)
    attn, pool = _mixers(proj, sinks[l], w_pool[l].astype(BF16),
                         pool_scale[l].reshape(1, -1), s)

    w_r = jnp.concatenate([w_router_group[l], w_router_expert[l]], axis=1)
    b_r = jnp.concatenate([b_router_group[l], b_router_expert[l]])
    n_r = N_GROUPS + N_EXPERTS
    br_col = jnp.broadcast_to(
        jnp.pad(b_r, (0, ROUTER_ROWS - n_r))[:, None], (ROUTER_ROWS, 128))
    wr_lanes = jnp.pad(w_r, ((0, 0), (0, ROUTER_LANES - n_r))).astype(BF16)
    br_lanes = jnp.pad(b_r, (0, ROUTER_LANES - n_r)).reshape(1, ROUTER_LANES)

    x2, h2, route, counts, egu_b, ed_b = _merge(
        attn, pool, proj, x2d, ada3, wa_b, wp_b, wo_b, norm2_g[l].reshape(1, d),
        wr_lanes, br_col, w_e_gate[l], w_e_up[l], w_e_down[l], s)

    cls, rank, base, tile_blk, tile_ea, tile_eb, tile_lanes, tile_lead, n_used = (
        _routing_tables(route, counts, n_tiles))
    hs = _dispatch(cls, rank, base, h2, n_tiles * TM_EXPERT)
    ys = _experts(tile_blk, tile_ea, tile_eb, tile_lanes, tile_lead, n_used,
                  hs, egu_b.reshape(N_EXPERTS, d, 2 * EXPERT_FF),
                  ed_b.reshape(w_e_down[l].shape), wr_lanes, br_lanes)
    out = _combine(cls, rank, base, ys, x2, ada3, final_g.reshape(1, d), s)
    return out.reshape(b, s, d)
```

```python
import functools

import jax
import jax.numpy as jnp
from jax import lax
from jax.experimental import pallas as pl
from jax.experimental.pallas import tpu as pltpu

F32 = jnp.float32
BF16 = jnp.bfloat16

D_MODEL = 2048
HEAD_DIM = 64
ATTN_HEADS = 16
KV_HEADS = 4
Q_PER_KV = ATTN_HEADS // KV_HEADS
ATTN_WIDTH = ATTN_HEADS * HEAD_DIM
KV_WIDTH = KV_HEADS * HEAD_DIM
BLOCK = 128
POOL_WINDOWS = (2, 4, 8, 16)
POOL_WIDTH = 1024
POOL_GROUP_DIM = 256
QKVU_WIDTH = ATTN_WIDTH + 2 * KV_WIDTH + POOL_WIDTH
GATE_WIDTH = 2 * D_MODEL
PROJ_WIDTH = GATE_WIDTH + QKVU_WIDTH
Q_COL = GATE_WIDTH
K_COL = Q_COL + ATTN_WIDTH
V_COL = K_COL + KV_WIDTH
U_COL = V_COL + KV_WIDTH
N_GROUPS = 4
EXPERTS_PER_GROUP = 4
N_EXPERTS = 16
EXPERT_FF = 512
N_ADA = 6
NORM_EPS = 1e-6
MASK_VALUE = -1e30

PAIRS_PER_GROUP = 6
N_CLASSES = N_GROUPS * PAIRS_PER_GROUP
CLASS_SLOT_A = (0, 0, 0, 1, 1, 3)
CLASS_SLOT_B = (1, 2, 3, 3, 2, 2)
ROUTER_ROWS = 32
ROUTER_LANES = 128

TM_PROJ = 2048
TN_PROJ = 512
NORM_ROWS = 256
NORM_CHUNKS = TM_PROJ // NORM_ROWS
TM_MIX = 1024
TM_MERGE = 256
TM_DISPATCH = 512
TM_EXPERT = 256
TM_COMBINE = 512
TN_ADA = 1024

VMEM_LIMIT = 52 * 1024 * 1024


def _rms_modulate(x, g, scale, shift):
    ms = jnp.mean(x * x, axis=-1, keepdims=True)
    return (x * lax.rsqrt(ms + NORM_EPS)) * (g * (1.0 + scale)) + shift


def _ada_kernel(cb_ref, w_ref, b_ref, o_ref):
    nb = cb_ref.shape[0]
    d = w_ref.shape[0]
    nchunk = w_ref.shape[1] // 128

    def body(kb, accs):
        k0 = pl.multiple_of(kb * 8, 8)
        new = list(accs)
        cbs = [cb_ref[b, pl.ds(k0, 8), :] for b in range(nb)]
        for j in range(nchunk):
            w = w_ref[pl.ds(k0, 8), j * 128:(j + 1) * 128]
            for b in range(nb):
                new[b * nchunk + j] = new[b * nchunk + j] + w * cbs[b]
        return tuple(new)

    init = tuple(jnp.zeros((8, 128), F32) for _ in range(nb * nchunk))
    accs = lax.fori_loop(0, d // 8, body, init, unroll=2)
    for b in range(nb):
        for j in range(nchunk):
            o_ref[b:b + 1, j * 128:(j + 1) * 128] = (
                jnp.sum(accs[b * nchunk + j], axis=0, keepdims=True)
                + b_ref[:, j * 128:(j + 1) * 128])


def _ada(c, w_ada, b_ada):
    nb, d = c.shape
    n = w_ada.shape[1]
    cb = jnp.broadcast_to(c[:, :, None], (nb, d, 128))
    return pl.pallas_call(
        _ada_kernel,
        out_shape=jax.ShapeDtypeStruct((nb, n), F32),
        grid=(n // TN_ADA,),
        in_specs=[
            pl.BlockSpec((nb, d, 128), lambda j: (0, 0, 0)),
            pl.BlockSpec((d, TN_ADA), lambda j: (0, j)),
            pl.BlockSpec((1, TN_ADA), lambda j: (0, j)),
        ],
        out_specs=pl.BlockSpec((nb, TN_ADA), lambda j: (0, j)),
        compiler_params=pltpu.CompilerParams(
            dimension_semantics=("arbitrary",), vmem_limit_bytes=VMEM_LIMIT),
        name="ada",
    )(cb, w_ada, b_ada.reshape(1, n))


def _in_proj_kernel(x_ref, ada_ref, g_ref, w_ref, wa_ref, wp_ref, wo_ref,
                    proj_ref, wa_o, wp_o, wo_o, h_even, h_odd, *, n_row):
    i = pl.program_id(0)
    j = pl.program_id(1)

    def norm_chunk(h_ref):
        r0 = pl.multiple_of(jnp.minimum(j, NORM_CHUNKS - 1) * NORM_ROWS, NORM_ROWS)
        h = _rms_modulate(x_ref[...], g_ref[...],
                          ada_ref[0, 1:2, :], ada_ref[0, 0:1, :])
        h_ref[pl.ds(r0, NORM_ROWS), :] = h.astype(BF16)

    def step(h_write, h_read):
        norm_chunk(h_write)
        acc = jnp.dot(h_read[...], w_ref[...].astype(BF16),
                      preferred_element_type=F32)
        is_gate = j < GATE_WIDTH // TN_PROJ
        gate = 0.5 * jnp.tanh(0.5 * acc) + 0.5
        proj_ref[...] = jnp.where(is_gate, gate, acc).astype(BF16)

    @pl.when(jnp.logical_and(j == 0, i < n_row))
    def _():
        wa_o[...] = wa_ref[...].astype(BF16)
        wp_o[...] = wp_ref[...].astype(BF16)
        wo_o[...] = wo_ref[...].astype(BF16)

    @pl.when(jnp.logical_and(i == 0, j < NORM_CHUNKS))
    def _():
        norm_chunk(h_even)

    @pl.when(i % 2 == 1)
    def _():
        step(h_odd, h_even)

    @pl.when(jnp.logical_and(i > 0, i % 2 == 0))
    def _():
        step(h_even, h_odd)


def _in_proj(x2d, ada3, norm_g, w_in, wa, wp, wo, seq):
    t, d = x2d.shape
    n_all = PROJ_WIDTH // TN_PROJ
    n_gate = GATE_WIDTH // TN_PROJ
    n_row = t // TM_PROJ
    tiles_per_seq = seq // TM_PROJ
    assert NORM_CHUNKS <= n_all
    norm_tile = lambda i: jnp.minimum(i, n_row - 1)
    out_tile = lambda i: jnp.maximum(i - 1, 0)
    w_tile = lambda i, j: (
        0, (jnp.where(i == 0, 0, j) + (n_all - n_gate)) % n_all)
    slab = lambda w: pl.BlockSpec((w.shape[0] // n_row, w.shape[1]),
                                  lambda i, j: (norm_tile(i), 0))
    cast = lambda w: jax.ShapeDtypeStruct(w.shape, BF16)
    return pl.pallas_call(
        functools.partial(_in_proj_kernel, n_row=n_row),
        out_shape=(jax.ShapeDtypeStruct((t, PROJ_WIDTH), BF16),
                   cast(wa), cast(wp), cast(wo)),
        grid=(n_row + 1, n_all),
        in_specs=[
            pl.BlockSpec((NORM_ROWS, d), lambda i, j: (
                norm_tile(i) * NORM_CHUNKS + jnp.minimum(j, NORM_CHUNKS - 1), 0)),
            pl.BlockSpec((1, N_ADA, d),
                         lambda i, j: (norm_tile(i) // tiles_per_seq, 0, 0)),
            pl.BlockSpec((1, d), lambda i, j: (0, 0)),
            pl.BlockSpec((d, TN_PROJ), w_tile),
            slab(wa), slab(wp), slab(wo),
        ],
        out_specs=(
            pl.BlockSpec((TM_PROJ, TN_PROJ),
                         lambda i, j: (out_tile(i), jnp.where(i == 0, 0, j))),
            slab(wa), slab(wp), slab(wo),
        ),
        scratch_shapes=[pltpu.VMEM((TM_PROJ, d), BF16),
                        pltpu.VMEM((TM_PROJ, d), BF16)],
        compiler_params=pltpu.CompilerParams(
            dimension_semantics=("arbitrary", "arbitrary"),
            vmem_limit_bytes=VMEM_LIMIT),
        name="in_proj",
    )(x2d, ada3, norm_g, w_in, wa, wp, wo)


def _mix_kernel(fill_ref, q_ref, kc_ref, vc_ref, kp_ref, vp_ref,
                ulo_ref, uhi_ref, plo_ref, phi_ref, wp_ref, ps_ref,
                attn_ref, pool_ref, kfull, vfull, ufull, *, steps_per_seq):
    i = pl.program_id(0)
    first_step = (i % steps_per_seq) == 0
    kfull[0:BLOCK, :] = kp_ref[...]
    kfull[BLOCK:BLOCK + TM_MIX, :] = kc_ref[...]
    vfull[0:BLOCK, :] = vp_ref[...]
    vfull[BLOCK:BLOCK + TM_MIX, :] = vc_ref[...]
    half = POOL_WIDTH // 2
    zeros = jnp.zeros((BLOCK, half), BF16)
    ufull[0:BLOCK, 0:half] = jnp.where(first_step, zeros, plo_ref[...])
    ufull[0:BLOCK, half:POOL_WIDTH] = jnp.where(first_step, zeros, phi_ref[...])
    ufull[BLOCK:BLOCK + TM_MIX, 0:half] = ulo_ref[...]
    ufull[BLOCK:BLOCK + TM_MIX, half:POOL_WIDTH] = uhi_ref[...]
    pos0 = (i % steps_per_seq) * TM_MIX
    nq = Q_PER_KV * BLOCK
    prow = lax.broadcasted_iota(jnp.int32, (BLOCK, 2 * BLOCK), 0)
    pcol = lax.broadcasted_iota(jnp.int32, (BLOCK, 2 * BLOCK), 1)
    rel = prow + BLOCK - pcol
    bands = [((rel >= 0) & (rel < w)).astype(F32).astype(BF16)
             for w in POOL_WINDOWS]

    def sb_body(sb, carry):
        r0 = pl.multiple_of(sb * BLOCK, BLOCK)
        row = lax.broadcasted_iota(jnp.int32, (nq, 2 * BLOCK), 0) & (BLOCK - 1)
        col = lax.broadcasted_iota(jnp.int32, (nq, 2 * BLOCK), 1)
        dist = col - row
        lo = jnp.where(jnp.logical_and(first_step, sb == 0), BLOCK, 0)
        valid = (dist > 0) & (dist <= BLOCK) & (col >= lo)
        key0 = lax.broadcasted_iota(jnp.int32, (2 * BLOCK, HEAD_DIM), 0) == 0
        pos = pos0 + sb * BLOCK + lax.broadcasted_iota(jnp.int32, (BLOCK, 1), 0)
        for h in range(KV_HEADS):
            kb = kfull[pl.ds(r0, 2 * BLOCK), h * HEAD_DIM:(h + 1) * HEAD_DIM]
            vb = vfull[pl.ds(r0, 2 * BLOCK), h * HEAD_DIM:(h + 1) * HEAD_DIM]
            vb = jnp.where(key0, jnp.zeros_like(vb), vb)
            qg = jnp.concatenate(
                [q_ref[pl.ds(r0, BLOCK),
                       (Q_PER_KV * h + g) * HEAD_DIM:(Q_PER_KV * h + g + 1) * HEAD_DIM]
                 for g in range(Q_PER_KV)], axis=0)
            qg = qg * jnp.asarray(HEAD_DIM ** -0.5, BF16)
            logits = lax.dot_general(qg, kb, (((1,), (1,)), ((), ())),
                                     preferred_element_type=F32)
            logits = jnp.where(valid, logits, fill_ref[h])
            m = jnp.max(logits, axis=1, keepdims=True)
            p = jnp.exp(logits - m)
            denom = jnp.sum(p, axis=1, keepdims=True)
            o = jnp.dot(p.astype(BF16), vb, preferred_element_type=F32)
            o = (o * (1.0 / denom)).astype(BF16)
            attn_ref[pl.ds(r0, BLOCK),
                     h * Q_PER_KV * HEAD_DIM:(h + 1) * Q_PER_KV * HEAD_DIM] = (
                jnp.concatenate([o[g * BLOCK:(g + 1) * BLOCK, :]
                                 for g in range(Q_PER_KV)], axis=1))

            w = POOL_WINDOWS[h]
            cols = slice(h * POOL_GROUP_DIM, (h + 1) * POOL_GROUP_DIM)
            band = ufull[pl.ds(r0, 2 * BLOCK), cols]
            ssum = jnp.dot(bands[h], band, preferred_element_type=F32)
            xg = ufull[pl.ds(r0 + BLOCK, BLOCK), cols].astype(F32)
            count = jnp.minimum(pos + 1, w).astype(F32)
            pooled = ssum / count - xg
            mixed = jnp.dot(pooled.astype(BF16), wp_ref[h],
                            preferred_element_type=F32)
            pool_ref[pl.ds(r0, BLOCK), cols] = (
                mixed * ps_ref[:, cols]).astype(BF16)
        return carry

    lax.fori_loop(0, TM_MIX // BLOCK, sb_body, 0)


def _mixers(proj, sinks, w_pool_b, pool_scale, seq):
    t = proj.shape[0]
    assert len(POOL_WINDOWS) == KV_HEADS
    steps_per_seq = seq // TM_MIX
    sub = TM_MIX // BLOCK
    half = POOL_WIDTH // 2
    qcol = Q_COL // ATTN_WIDTH
    kcol = K_COL // KV_WIDTH
    ucol = U_COL // half
    prev = lambda i: jnp.maximum(i * sub - 1, 0)
    nq = Q_PER_KV * BLOCK
    sink_rows = jnp.repeat(sinks.astype(F32).reshape(KV_HEADS, Q_PER_KV), BLOCK,
                           axis=1)
    fill = jnp.full((KV_HEADS, nq, 2 * BLOCK), MASK_VALUE, F32)
    fill = fill.at[:, :, 0].set(sink_rows)
    return pl.pallas_call(
        functools.partial(_mix_kernel, steps_per_seq=steps_per_seq),
        out_shape=(jax.ShapeDtypeStruct((t, ATTN_WIDTH), BF16),
                   jax.ShapeDtypeStruct((t, POOL_WIDTH), BF16)),
        grid=(t // TM_MIX,),
        in_specs=[
            pl.BlockSpec((KV_HEADS, nq, 2 * BLOCK), lambda i: (0, 0, 0)),
            pl.BlockSpec((TM_MIX, ATTN_WIDTH), lambda i: (i, qcol)),
            pl.BlockSpec((TM_MIX, KV_WIDTH), lambda i: (i, kcol)),
            pl.BlockSpec((TM_MIX, KV_WIDTH), lambda i: (i, kcol + 1)),
            pl.BlockSpec((BLOCK, KV_WIDTH), lambda i: (prev(i), kcol)),
            pl.BlockSpec((BLOCK, KV_WIDTH), lambda i: (prev(i), kcol + 1)),
            pl.BlockSpec((TM_MIX, half), lambda i: (i, ucol)),
            pl.BlockSpec((TM_MIX, half), lambda i: (i, ucol + 1)),
            pl.BlockSpec((BLOCK, half), lambda i: (prev(i), ucol)),
            pl.BlockSpec((BLOCK, half), lambda i: (prev(i), ucol + 1)),
            pl.BlockSpec((len(POOL_WINDOWS), POOL_GROUP_DIM, POOL_GROUP_DIM),
                         lambda i: (0, 0, 0)),
            pl.BlockSpec((1, POOL_WIDTH), lambda i: (0, 0)),
        ],
        out_specs=(pl.BlockSpec((TM_MIX, ATTN_WIDTH), lambda i: (i, 0)),
                   pl.BlockSpec((TM_MIX, POOL_WIDTH), lambda i: (i, 0))),
        scratch_shapes=[pltpu.VMEM((BLOCK + TM_MIX, KV_WIDTH), BF16),
                        pltpu.VMEM((BLOCK + TM_MIX, KV_WIDTH), BF16),
                        pltpu.VMEM((BLOCK + TM_MIX, POOL_WIDTH), BF16)],
        compiler_params=pltpu.CompilerParams(
            dimension_semantics=("arbitrary",), vmem_limit_bytes=VMEM_LIMIT),
        name="mixers",
    )(fill, proj, proj, proj, proj, proj, proj, proj, proj, proj, w_pool_b,
      pool_scale)


def _first_argmax4(v):
    m = jnp.maximum(jnp.maximum(v[0], v[1]), jnp.maximum(v[2], v[3]))
    idx = jnp.where(v[0] == m, 0, jnp.where(v[1] == m, 1,
                                            jnp.where(v[2] == m, 2, 3)))
    return m, idx


def _merge_kernel(attn_ref, pool_ref, ga_ref, gp_ref, x_ref, ada_ref,
                  wa_ref, wp_ref, wo_ref, g2_ref, wr_ref, br_ref,
                  eg_ref, eu_ref, ed_ref,
                  x2_ref, h2_ref, route_ref, counts_ref, egu_o, ed_o, lg_ref):
    i = pl.program_id(0)
    n_tile = pl.num_programs(0) - 1

    @pl.when(i == 0)
    def _():
        counts_ref[...] = jnp.zeros_like(counts_ref)
        lg_ref[...] = jnp.zeros_like(lg_ref)

    @pl.when(i < n_tile)
    def _():
        _route_tile(lg_ref[...], i > 0, route_ref, counts_ref)
        egu_o[:, 0:EXPERT_FF] = eg_ref[...].astype(BF16)
        egu_o[:, EXPERT_FF:2 * EXPERT_FF] = eu_ref[...].astype(BF16)
        ed_o[...] = ed_ref[...].astype(BF16)
        a = jnp.dot(attn_ref[...], wa_ref[...], preferred_element_type=F32)
        p = jnp.dot(pool_ref[...], wp_ref[...], preferred_element_type=F32)
        merged = ga_ref[...].astype(F32) * a + gp_ref[...].astype(F32) * p
        out = jnp.dot(merged.astype(BF16), wo_ref[...], preferred_element_type=F32)
        x2 = x_ref[...] + ada_ref[0, 2:3, :] * out
        x2_ref[...] = x2
        h2 = _rms_modulate(x2, g2_ref[...], ada_ref[0, 4:5, :], ada_ref[0, 3:4, :])
        h2_ref[...] = h2
        logits = jnp.dot(h2.astype(BF16), wr_ref[...], preferred_element_type=F32)
        lg_ref[...] = logits.T[0:ROUTER_ROWS, :] + br_ref[:, 0:1]

    @pl.when(i == n_tile)
    def _():
        _route_tile(lg_ref[...], True, route_ref, counts_ref)


def _route_tile(logits, live, route_ref, counts_ref):
    tm = logits.shape[1]
    rows = [logits[r:r + 1, :] for r in range(N_GROUPS + N_EXPERTS)]
    _, gi = _first_argmax4(rows[0:N_GROUPS])
    sel = []
    for e in range(EXPERTS_PER_GROUP):
        v = rows[N_GROUPS + 3 * EXPERTS_PER_GROUP + e]
        for g in range(N_GROUPS - 2, -1, -1):
            v = jnp.where(gi == g, rows[N_GROUPS + g * EXPERTS_PER_GROUP + e], v)
        sel.append(v)
    _, i1 = _first_argmax4(sel)
    rest = [jnp.where(i1 == e, -jnp.inf, sel[e]) for e in range(EXPERTS_PER_GROUP)]
    _, i2 = _first_argmax4(rest)
    code = jnp.minimum(i1, i2) * EXPERTS_PER_GROUP + jnp.maximum(i1, i2)
    cig = jnp.where(code == 1, 0, jnp.where(code == 2, 1, jnp.where(
        code == 3, 2, jnp.where(code == 7, 3, jnp.where(code == 6, 4, 5)))))
    cls = gi * PAIRS_PER_GROUP + cig

    onehot = (lax.broadcasted_iota(jnp.int32, (ROUTER_ROWS, tm), 0) == cls
              ).astype(F32)
    onehot = jnp.where(live, onehot, 0.0)
    src = lax.broadcasted_iota(jnp.int32, (tm, tm), 0)
    dst = lax.broadcasted_iota(jnp.int32, (tm, tm), 1)
    before = (src < dst).astype(F32).astype(BF16)
    prefix = jnp.dot(onehot.astype(BF16), before, preferred_element_type=F32)
    carry = counts_ref[:, 0:1]
    rank = jnp.sum(onehot * (prefix + carry), axis=0, keepdims=True)
    counts_ref[...] = counts_ref[...] + jnp.sum(onehot, axis=1, keepdims=True)
    route_ref[0:1, :] = cls
    route_ref[1:2, :] = rank.astype(jnp.int32)
    route_ref[2:8, :] = jnp.zeros((6, tm), jnp.int32)


def _merge(attn, pool, proj, x2d, ada3, wa_b, wp_b, wo_b, norm2_g, wr_lanes, br_col,
           w_e_gate, w_e_up, w_e_down, seq):
    t, d = x2d.shape
    n_step = t // TM_MERGE
    tiles_per_seq = seq // TM_MERGE
    cur = lambda i: jnp.minimum(i, n_step - 1)
    lag = lambda i: jnp.maximum(i - 1, 0)
    flat = lambda w: w.reshape(-1, w.shape[-1])
    slab = lambda w: pl.BlockSpec((w.shape[0] // n_step, w.shape[1]),
                                  lambda i: (cur(i), 0))
    cast = lambda w: jax.ShapeDtypeStruct(w.shape, BF16)
    eg, eu, ed = flat(w_e_gate), flat(w_e_up), flat(w_e_down)
    egu = jax.ShapeDtypeStruct((eg.shape[0], 2 * EXPERT_FF), BF16)
    const = lambda shape: pl.BlockSpec(shape, lambda i: (0,) * len(shape),
                                       pipeline_mode=pl.Buffered(1))
    return pl.pallas_call(
        _merge_kernel,
        out_shape=(jax.ShapeDtypeStruct((t, d), F32),
                   jax.ShapeDtypeStruct((t, d), F32),
                   jax.ShapeDtypeStruct((8, t), jnp.int32),
                   jax.ShapeDtypeStruct((ROUTER_ROWS, 128), F32),
                   egu, cast(ed)),
        grid=(n_step + 1,),
        in_specs=[
            pl.BlockSpec((TM_MERGE, ATTN_WIDTH), lambda i: (cur(i), 0)),
            pl.BlockSpec((TM_MERGE, POOL_WIDTH), lambda i: (cur(i), 0)),
            pl.BlockSpec((TM_MERGE, d), lambda i: (cur(i), 0)),
            pl.BlockSpec((TM_MERGE, d), lambda i: (cur(i), 1)),
            pl.BlockSpec((TM_MERGE, d), lambda i: (cur(i), 0)),
            pl.BlockSpec((1, N_ADA, d),
                         lambda i: (cur(i) // tiles_per_seq, 0, 0)),
            const((ATTN_WIDTH, d)),
            const((POOL_WIDTH, d)),
            const((d, d)),
            const((1, d)),
            const((d, ROUTER_LANES)),
            const((ROUTER_ROWS, 128)),
            slab(eg), slab(eu), slab(ed),
        ],
        out_specs=(
            pl.BlockSpec((TM_MERGE, d), lambda i: (cur(i), 0)),
            pl.BlockSpec((TM_MERGE, d), lambda i: (cur(i), 0)),
            pl.BlockSpec((8, TM_MERGE), lambda i: (0, lag(i))),
            pl.BlockSpec((ROUTER_ROWS, 128), lambda i: (0, 0)),
            slab(egu), slab(ed),
        ),
        scratch_shapes=[pltpu.VMEM((ROUTER_ROWS, TM_MERGE), F32)],
        compiler_params=pltpu.CompilerParams(
            dimension_semantics=("arbitrary",), vmem_limit_bytes=VMEM_LIMIT),
        name="merge",
    )(attn, pool, proj, proj, x2d, ada3, wa_b, wp_b, wo_b, norm2_g, wr_lanes, br_col,
      eg, eu, ed)


def _row_copy(src_ref, src_row, dst_ref, dst_row, sem):
    return pltpu.make_async_copy(src_ref.at[pl.ds(src_row, 1)],
                                 dst_ref.at[pl.ds(dst_row, 1)], sem)


def _sorted_row(cls_ref, rank_ref, base_ref, token):
    return base_ref[cls_ref[token]] + rank_ref[token]


def _dispatch_kernel(cls_ref, rank_ref, base_ref, h_ref, hs_ref, sem):
    i = pl.program_id(0)
    first = i * TM_DISPATCH

    for r in range(TM_DISPATCH):
        _row_copy(h_ref, r, hs_ref,
                  _sorted_row(cls_ref, rank_ref, base_ref, first + r), sem).start()
    pltpu.make_async_copy(h_ref, hs_ref.at[pl.ds(0, TM_DISPATCH)], sem).wait()


def _dispatch(cls, rank, base, h2, n_rows):
    t, d = h2.shape
    return pl.pallas_call(
        _dispatch_kernel,
        out_shape=jax.ShapeDtypeStruct((n_rows, d), F32),
        grid_spec=pltpu.PrefetchScalarGridSpec(
            num_scalar_prefetch=3,
            grid=(t // TM_DISPATCH,),
            in_specs=[pl.BlockSpec((TM_DISPATCH, d), lambda i, *_: (i, 0))],
            out_specs=pl.BlockSpec(memory_space=pl.ANY),
            scratch_shapes=[pltpu.SemaphoreType.DMA(())],
        ),
        compiler_params=pltpu.CompilerParams(
            dimension_semantics=("arbitrary",), vmem_limit_bytes=VMEM_LIMIT,
            has_side_effects=True),
        name="dispatch",
    )(cls, rank, base, h2)


def _expert_kernel(blk_ref, ea_ref, eb_ref, lane_ref, lead_ref, nused_ref,
                   hs_ref, wgua_ref, wda_ref, wgub_ref, wdb_ref,
                   wr_ref, br_ref, ys_ref):
    i = pl.program_id(0)
    used = i < nused_ref[0]
    lead = lead_ref[i]
    half = TM_EXPERT // 2

    def tile_rows(r0):
        tm = TM_EXPERT - r0
        rows = r0 + lax.broadcasted_iota(jnp.int32, (tm, 1), 0)
        h = jnp.where(rows >= lead, hs_ref[r0:TM_EXPERT, :], 0.0).astype(BF16)
        logits = jnp.dot(h, wr_ref[...], preferred_element_type=F32) + br_ref[...]
        lane = lax.broadcasted_iota(jnp.int32, logits.shape, 1)
        is_group = lane < N_GROUPS
        gmax = jnp.max(jnp.where(is_group, logits, -jnp.inf), axis=1, keepdims=True)
        gexp = jnp.where(is_group, jnp.exp(logits - gmax), 0.0)
        pick = lambda v, l: jnp.sum(jnp.where(lane == l, v, 0.0), axis=1,
                                    keepdims=True)
        p_group = pick(gexp, lane_ref[0, i]) / jnp.sum(gexp, axis=1, keepdims=True)
        la = pick(logits, lane_ref[1, i])
        lb = pick(logits, lane_ref[2, i])
        mx = jnp.maximum(la, lb)
        ea = jnp.exp(la - mx)
        eb = jnp.exp(lb - mx)
        inv = p_group / (ea + eb)

        def expert(wgu_ref, wd_ref, weight):
            gu = jnp.dot(h, wgu_ref[0], preferred_element_type=F32)
            gate, up = gu[:, 0:EXPERT_FF], gu[:, EXPERT_FF:2 * EXPERT_FF]
            act = (jax.nn.silu(gate) * up * weight).astype(BF16)
            return jnp.dot(act, wd_ref[0], preferred_element_type=F32)

        ys_ref[r0:TM_EXPERT, :] = (expert(wgua_ref, wda_ref, ea * inv)
                                   + expert(wgub_ref, wdb_ref, eb * inv))

    @pl.when(jnp.logical_not(used))
    def _():
        ys_ref[...] = jnp.zeros_like(ys_ref)

    @pl.when(jnp.logical_and(used, lead < half))
    def _():
        tile_rows(0)

    @pl.when(jnp.logical_and(used, lead >= half))
    def _():
        ys_ref[0:half, :] = jnp.zeros((half, ys_ref.shape[1]), ys_ref.dtype)
        tile_rows(half)


def _experts(tile_blk, tile_ea, tile_eb, tile_lanes, tile_lead, n_used,
             hs, wgu_b, wd_b, wr_lanes, br_lanes):
    p_rows, d = hs.shape
    n_tiles = p_rows // TM_EXPERT
    row_map = lambda i, blk, ea, eb, ln, nv, nu: (blk[i], 0)
    wa_map = lambda i, blk, ea, eb, ln, nv, nu: (ea[i], 0, 0)
    wb_map = lambda i, blk, ea, eb, ln, nv, nu: (eb[i], 0, 0)
    const = lambda i, blk, ea, eb, ln, nv, nu: (0, 0)
    return pl.pallas_call(
        _expert_kernel,
        out_shape=jax.ShapeDtypeStruct((p_rows, d), F32),
        grid_spec=pltpu.PrefetchScalarGridSpec(
            num_scalar_prefetch=6,
            grid=(n_tiles,),
            in_specs=[
                pl.BlockSpec((TM_EXPERT, d), row_map),
                pl.BlockSpec((1, d, 2 * EXPERT_FF), wa_map),
                pl.BlockSpec((1, EXPERT_FF, d), wa_map),
                pl.BlockSpec((1, d, 2 * EXPERT_FF), wb_map),
                pl.BlockSpec((1, EXPERT_FF, d), wb_map),
                pl.BlockSpec((d, ROUTER_LANES), const),
                pl.BlockSpec((1, ROUTER_LANES), const),
            ],
            out_specs=pl.BlockSpec((TM_EXPERT, d), row_map),
        ),
        compiler_params=pltpu.CompilerParams(
            dimension_semantics=("arbitrary",), vmem_limit_bytes=VMEM_LIMIT),
        name="experts",
    )(tile_blk, tile_ea, tile_eb, tile_lanes, tile_lead, n_used,
      hs, wgu_b, wd_b, wgu_b, wd_b, wr_lanes, br_lanes)


def _combine_kernel(cls_ref, rank_ref, base_ref, ys_ref, x2_ref, ada_ref, g_ref,
                    o_ref, ybuf, sems):
    i = pl.program_id(0)
    n = pl.num_programs(0)

    def gather(tile, slot):
        for r in range(TM_COMBINE):
            row = _sorted_row(cls_ref, rank_ref, base_ref, tile * TM_COMBINE + r)
            _row_copy(ys_ref, row, ybuf.at[slot], r, sems.at[slot]).start()

    @pl.when(i == 0)
    def _():
        gather(0, 0)

    @pl.when(i + 1 < n)
    def _():
        gather(i + 1, (i + 1) % 2)

    slot = i % 2
    pltpu.make_async_copy(ys_ref.at[pl.ds(0, TM_COMBINE)], ybuf.at[slot],
                          sems.at[slot]).wait()

    x = x2_ref[...] + ada_ref[0, 5:6, :] * ybuf[slot]
    ms = jnp.mean(x * x, axis=-1, keepdims=True)
    o_ref[...] = x * lax.rsqrt(ms + NORM_EPS) * g_ref[...]


def _combine(cls, rank, base, ys, x2, ada3, final_g, seq):
    t, d = x2.shape
    tiles_per_seq = seq // TM_COMBINE
    return pl.pallas_call(
        _combine_kernel,
        out_shape=jax.ShapeDtypeStruct((t, d), F32),
        grid_spec=pltpu.PrefetchScalarGridSpec(
            num_scalar_prefetch=3,
            grid=(t // TM_COMBINE,),
            in_specs=[
                pl.BlockSpec(memory_space=pl.ANY),
                pl.BlockSpec((TM_COMBINE, d), lambda i, *_: (i, 0)),
                pl.BlockSpec((1, N_ADA, d),
                             lambda i, *_: (i // tiles_per_seq, 0, 0)),
                pl.BlockSpec((1, d), lambda i, *_: (0, 0)),
            ],
            out_specs=pl.BlockSpec((TM_COMBINE, d), lambda i, *_: (i, 0)),
            scratch_shapes=[pltpu.VMEM((2, TM_COMBINE, d), F32),
                            pltpu.SemaphoreType.DMA((2,))],
        ),
        compiler_params=pltpu.CompilerParams(
            dimension_semantics=("arbitrary",), vmem_limit_bytes=VMEM_LIMIT),
        name="combine",
    )(cls, rank, base, ys, x2, ada3, final_g)


def _routing_tables(route, counts, n_tiles):
    cls = route[0]
    rank = route[1]
    cnt = counts[:N_CLASSES, 0].astype(jnp.int32)
    tiles_c = (cnt + TM_EXPERT - 1) // TM_EXPERT
    tile_end = jnp.cumsum(tiles_c)
    tile_start = tile_end - tiles_c
    n_used = tile_end[-1]
    lead_c = tiles_c * TM_EXPERT - cnt
    base = (tile_start * TM_EXPERT + lead_c).astype(jnp.int32)

    steps = jnp.arange(n_tiles, dtype=jnp.int32)
    k = jnp.minimum(steps, n_used - 1)
    c = jnp.sum((k[:, None] >= tile_end[None, :]).astype(jnp.int32), axis=1)
    grp = c // PAIRS_PER_GROUP
    cig = c % PAIRS_PER_GROUP
    slot_a = jnp.asarray(CLASS_SLOT_A, jnp.int32)[cig]
    slot_b = jnp.asarray(CLASS_SLOT_B, jnp.int32)[cig]
    ea = grp * EXPERTS_PER_GROUP + slot_a
    eb = grp * EXPERTS_PER_GROUP + slot_b
    lanes = jnp.stack([grp, N_GROUPS + ea, N_GROUPS + eb]).astype(jnp.int32)
    lead = jnp.where(k == tile_start[c], lead_c[c], 0)
    return (cls, rank, base, steps, ea.astype(jnp.int32),
            eb.astype(jnp.int32), lanes, lead.astype(jnp.int32),
            n_used.reshape(1).astype(jnp.int32))


def kernel(x, c, w_ada, b_ada, norm1_g, w_in, sinks, w_pool, pool_scale,
           w_attn_branch, w_pool_branch, w_out, norm2_g, w_router_group,
           b_router_group, w_router_expert, b_router_expert, w_e_gate, w_e_up,
           w_e_down, final_g):
    b, s, d = x.shape
    t = b * s
    assert w_ada.shape[0] == 1, "single-layer block"
    assert d == D_MODEL and s % TM_PROJ == 0
    l = 0
    n_tiles = t // TM_EXPERT + N_CLASSES
    x2d = x.reshape(t, d)
    ada3 = _ada(c, w_ada[l], b_ada[l]).reshape(b, N_ADA, d)

    proj, wa_b, wp_b, wo_b = _in_proj(
        x2d, ada3, norm1_g[l].reshape(1, d), w_in[l], w_attn_branch[l],
        w_pool_branch[l], w_out[l], s)
    attn, pool = _mixers(proj, sinks[l], w_pool[l].astype(BF16),
                         pool_scale[l].reshape(1, -1), s)

    w_r = jnp.concatenate([w_router_group[l], w_router_expert[l]], axis=1)
    b_r = jnp.concatenate([b_router_group[l], b_router_expert[l]])
    n_r = N_GROUPS + N_EXPERTS
    br_col = jnp.broadcast_to(
        jnp.pad(b_r, (0, ROUTER_ROWS - n_r))[:, None], (ROUTER_ROWS, 128))
    wr_lanes = jnp.pad(w_r, ((0, 0), (0, ROUTER_LANES - n_r))).astype(BF16)
    br_lanes = jnp.pad(b_r, (0, ROUTER_LANES - n_r)).reshape(1, ROUTER_LANES)

    x2, h2, route, counts, egu_b, ed_b = _merge(
        attn, pool, proj, x2d, ada3, wa_b, wp_b, wo_b, norm2_g[l].reshape(1, d),
        wr_lanes, br_col, w_e_gate[l], w_e_up[l], w_e_down[l], s)

    cls, rank, base, tile_blk, tile_ea, tile_eb, tile_lanes, tile_lead, n_used = (
        _routing_tables(route, counts, n_tiles))
    hs = _dispatch(cls, rank, base, h2, n_tiles * TM_EXPERT)
    ys = _experts(tile_blk, tile_ea, tile_eb, tile_lanes, tile_lead, n_used,
                  hs, egu_b.reshape(N_EXPERTS, d, 2 * EXPERT_FF),
                  ed_b.reshape(w_e_down[l].shape), wr_lanes, br_lanes)
    out = _combine(cls, rank, base, ys, x2, ada3, final_g.reshape(1, d), s)
    return out.reshape(b, s, d)
```

```python
import functools

import jax
import jax.numpy as jnp
from jax import lax
from jax.experimental import pallas as pl
from jax.experimental.pallas import tpu as pltpu

F32 = jnp.float32
BF16 = jnp.bfloat16

D_MODEL = 2048
HEAD_DIM = 64
ATTN_HEADS = 16
KV_HEADS = 4
Q_PER_KV = ATTN_HEADS // KV_HEADS
ATTN_WIDTH = ATTN_HEADS * HEAD_DIM
KV_WIDTH = KV_HEADS * HEAD_DIM
BLOCK = 128
POOL_WINDOWS = (2, 4, 8, 16)
POOL_WIDTH = 1024
POOL_GROUP_DIM = 256
QKVU_WIDTH = ATTN_WIDTH + 2 * KV_WIDTH + POOL_WIDTH
GATE_WIDTH = 2 * D_MODEL
PROJ_WIDTH = GATE_WIDTH + QKVU_WIDTH
Q_COL = GATE_WIDTH
K_COL = Q_COL + ATTN_WIDTH
V_COL = K_COL + KV_WIDTH
U_COL = V_COL + KV_WIDTH
N_GROUPS = 4
EXPERTS_PER_GROUP = 4
N_EXPERTS = 16
EXPERT_FF = 512
N_ADA = 6
NORM_EPS = 1e-6
MASK_VALUE = -1e30

PAIRS_PER_GROUP = 6
N_CLASSES = N_GROUPS * PAIRS_PER_GROUP
CLASS_SLOT_A = (0, 0, 0, 1, 1, 3)
CLASS_SLOT_B = (1, 2, 3, 3, 2, 2)
ROUTER_ROWS = 32
ROUTER_LANES = 128

TM_PROJ = 2048
TN_PROJ = 512
NORM_ROWS = 256
NORM_CHUNKS = TM_PROJ // NORM_ROWS
TM_MIX = 1024
TM_MERGE = 256
TM_DISPATCH = 512
TM_EXPERT = 256
TM_COMBINE = 512
TN_ADA = 1024

VMEM_LIMIT = 52 * 1024 * 1024


def _rms_modulate(x, g, scale, shift):
    ms = jnp.mean(x * x, axis=-1, keepdims=True)
    return (x * lax.rsqrt(ms + NORM_EPS)) * (g * (1.0 + scale)) + shift


def _ada_kernel(cb_ref, w_ref, b_ref, o_ref):
    nb = cb_ref.shape[0]
    d = w_ref.shape[0]
    nchunk = w_ref.shape[1] // 128

    def body(kb, accs):
        k0 = pl.multiple_of(kb * 8, 8)
        new = list(accs)
        cbs = [cb_ref[b, pl.ds(k0, 8), :] for b in range(nb)]
        for j in range(nchunk):
            w = w_ref[pl.ds(k0, 8), j * 128:(j + 1) * 128]
            for b in range(nb):
                new[b * nchunk + j] = new[b * nchunk + j] + w * cbs[b]
        return tuple(new)

    init = tuple(jnp.zeros((8, 128), F32) for _ in range(nb * nchunk))
    accs = lax.fori_loop(0, d // 8, body, init, unroll=2)
    for b in range(nb):
        for j in range(nchunk):
            o_ref[b:b + 1, j * 128:(j + 1) * 128] = (
                jnp.sum(accs[b * nchunk + j], axis=0, keepdims=True)
                + b_ref[:, j * 128:(j + 1) * 128])


def _ada(c, w_ada, b_ada):
    nb, d = c.shape
    n = w_ada.shape[1]
    cb = jnp.broadcast_to(c[:, :, None], (nb, d, 128))
    return pl.pallas_call(
        _ada_kernel,
        out_shape=jax.ShapeDtypeStruct((nb, n), F32),
        grid=(n // TN_ADA,),
        in_specs=[
            pl.BlockSpec((nb, d, 128), lambda j: (0, 0, 0)),
            pl.BlockSpec((d, TN_ADA), lambda j: (0, j)),
            pl.BlockSpec((1, TN_ADA), lambda j: (0, j)),
        ],
        out_specs=pl.BlockSpec((nb, TN_ADA), lambda j: (0, j)),
        compiler_params=pltpu.CompilerParams(
            dimension_semantics=("arbitrary",), vmem_limit_bytes=VMEM_LIMIT),
        name="ada",
    )(cb, w_ada, b_ada.reshape(1, n))


def _in_proj_kernel(x_ref, ada_ref, g_ref, w_ref, wa_ref, wp_ref, wo_ref,
                    proj_ref, wa_o, wp_o, wo_o, h_even, h_odd, *, n_row):
    i = pl.program_id(0)
    j = pl.program_id(1)

    def norm_chunk(h_ref):
        r0 = pl.multiple_of(jnp.minimum(j, NORM_CHUNKS - 1) * NORM_ROWS, NORM_ROWS)
        h = _rms_modulate(x_ref[...], g_ref[...],
                          ada_ref[0, 1:2, :], ada_ref[0, 0:1, :])
        h_ref[pl.ds(r0, NORM_ROWS), :] = h.astype(BF16)

    def step(h_write, h_read, gate_columns):
        norm_chunk(h_write)
        acc = jnp.dot(h_read[...], w_ref[...].astype(BF16),
                      preferred_element_type=F32)
        if gate_columns:
            acc = 0.5 * jnp.tanh(0.5 * acc) + 0.5
        proj_ref[...] = acc.astype(BF16)

    @pl.when(jnp.logical_and(j == 0, i < n_row))
    def _():
        wa_o[...] = wa_ref[...].astype(BF16)
        wp_o[...] = wp_ref[...].astype(BF16)
        wo_o[...] = wo_ref[...].astype(BF16)

    @pl.when(jnp.logical_and(i == 0, j < NORM_CHUNKS))
    def _():
        norm_chunk(h_even)

    is_gate = j < GATE_WIDTH // TN_PROJ
    odd = i % 2 == 1
    even = jnp.logical_and(i > 0, i % 2 == 0)
    for parity, h_write, h_read in ((odd, h_odd, h_even), (even, h_even, h_odd)):
        for gate_columns in (True, False):
            kind = is_gate if gate_columns else jnp.logical_not(is_gate)
            pl.when(jnp.logical_and(parity, kind))(
                functools.partial(step, h_write, h_read, gate_columns))


def _in_proj(x2d, ada3, norm_g, w_in, wa, wp, wo, seq):
    t, d = x2d.shape
    n_all = PROJ_WIDTH // TN_PROJ
    n_gate = GATE_WIDTH // TN_PROJ
    n_row = t // TM_PROJ
    tiles_per_seq = seq // TM_PROJ
    assert NORM_CHUNKS <= n_all
    norm_tile = lambda i: jnp.minimum(i, n_row - 1)
    out_tile = lambda i: jnp.maximum(i - 1, 0)
    w_tile = lambda i, j: (
        0, (jnp.where(i == 0, 0, j) + (n_all - n_gate)) % n_all)
    slab = lambda w: pl.BlockSpec((w.shape[0] // n_row, w.shape[1]),
                                  lambda i, j: (norm_tile(i), 0))
    cast = lambda w: jax.ShapeDtypeStruct(w.shape, BF16)
    return pl.pallas_call(
        functools.partial(_in_proj_kernel, n_row=n_row),
        out_shape=(jax.ShapeDtypeStruct((t, PROJ_WIDTH), BF16),
                   cast(wa), cast(wp), cast(wo)),
        grid=(n_row + 1, n_all),
        in_specs=[
            pl.BlockSpec((NORM_ROWS, d), lambda i, j: (
                norm_tile(i) * NORM_CHUNKS + jnp.minimum(j, NORM_CHUNKS - 1), 0)),
            pl.BlockSpec((1, N_ADA, d),
                         lambda i, j: (norm_tile(i) // tiles_per_seq, 0, 0)),
            pl.BlockSpec((1, d), lambda i, j: (0, 0)),
            pl.BlockSpec((d, TN_PROJ), w_tile),
            slab(wa), slab(wp), slab(wo),
        ],
        out_specs=(
            pl.BlockSpec((TM_PROJ, TN_PROJ),
                         lambda i, j: (out_tile(i), jnp.where(i == 0, 0, j))),
            slab(wa), slab(wp), slab(wo),
        ),
        scratch_shapes=[pltpu.VMEM((TM_PROJ, d), BF16),
                        pltpu.VMEM((TM_PROJ, d), BF16)],
        compiler_params=pltpu.CompilerParams(
            dimension_semantics=("arbitrary", "arbitrary"),
            vmem_limit_bytes=VMEM_LIMIT),
        name="in_proj",
    )(x2d, ada3, norm_g, w_in, wa, wp, wo)


def _mix_kernel(fill_ref, q_ref, kc_ref, vc_ref, kp_ref, vp_ref,
                ulo_ref, uhi_ref, plo_ref, phi_ref, wp_ref, ps_ref,
                attn_ref, pool_ref, kfull, vfull, ufull, *, steps_per_seq):
    i = pl.program_id(0)
    first_step = (i % steps_per_seq) == 0
    kfull[0:BLOCK, :] = kp_ref[...]
    kfull[BLOCK:BLOCK + TM_MIX, :] = kc_ref[...]
    vfull[0:BLOCK, :] = vp_ref[...]
    vfull[BLOCK:BLOCK + TM_MIX, :] = vc_ref[...]
    half = POOL_WIDTH // 2
    zeros = jnp.zeros((BLOCK, half), BF16)
    ufull[0:BLOCK, 0:half] = jnp.where(first_step, zeros, plo_ref[...])
    ufull[0:BLOCK, half:POOL_WIDTH] = jnp.where(first_step, zeros, phi_ref[...])
    ufull[BLOCK:BLOCK + TM_MIX, 0:half] = ulo_ref[...]
    ufull[BLOCK:BLOCK + TM_MIX, half:POOL_WIDTH] = uhi_ref[...]
    pos0 = (i % steps_per_seq) * TM_MIX
    nq = Q_PER_KV * BLOCK
    prow = lax.broadcasted_iota(jnp.int32, (BLOCK, 2 * BLOCK), 0)
    pcol = lax.broadcasted_iota(jnp.int32, (BLOCK, 2 * BLOCK), 1)
    rel = prow + BLOCK - pcol
    bands = [((rel >= 0) & (rel < w)).astype(F32).astype(BF16)
             for w in POOL_WINDOWS]

    def sb_body(sb, carry):
        r0 = pl.multiple_of(sb * BLOCK, BLOCK)
        row = lax.broadcasted_iota(jnp.int32, (nq, 2 * BLOCK), 0) & (BLOCK - 1)
        col = lax.broadcasted_iota(jnp.int32, (nq, 2 * BLOCK), 1)
        dist = col - row
        lo = jnp.where(jnp.logical_and(first_step, sb == 0), BLOCK, 0)
        valid = (dist > 0) & (dist <= BLOCK) & (col >= lo)
        key0 = lax.broadcasted_iota(jnp.int32, (2 * BLOCK, HEAD_DIM), 0) == 0
        pos = pos0 + sb * BLOCK + lax.broadcasted_iota(jnp.int32, (BLOCK, 1), 0)
        for h in range(KV_HEADS):
            kb = kfull[pl.ds(r0, 2 * BLOCK), h * HEAD_DIM:(h + 1) * HEAD_DIM]
            vb = vfull[pl.ds(r0, 2 * BLOCK), h * HEAD_DIM:(h + 1) * HEAD_DIM]
            vb = jnp.where(key0, jnp.zeros_like(vb), vb)
            qg = jnp.concatenate(
                [q_ref[pl.ds(r0, BLOCK),
                       (Q_PER_KV * h + g) * HEAD_DIM:(Q_PER_KV * h + g + 1) * HEAD_DIM]
                 for g in range(Q_PER_KV)], axis=0)
            qg = qg * jnp.asarray(HEAD_DIM ** -0.5, BF16)
            logits = lax.dot_general(qg, kb, (((1,), (1,)), ((), ())),
                                     preferred_element_type=F32)
            logits = jnp.where(valid, logits, fill_ref[h])
            m = jnp.max(logits, axis=1, keepdims=True)
            p = jnp.exp(logits - m)
            denom = jnp.sum(p, axis=1, keepdims=True)
            o = jnp.dot(p.astype(BF16), vb, preferred_element_type=F32)
            o = (o * (1.0 / denom)).astype(BF16)
            attn_ref[pl.ds(r0, BLOCK),
                     h * Q_PER_KV * HEAD_DIM:(h + 1) * Q_PER_KV * HEAD_DIM] = (
                jnp.concatenate([o[g * BLOCK:(g + 1) * BLOCK, :]
                                 for g in range(Q_PER_KV)], axis=1))

            w = POOL_WINDOWS[h]
            cols = slice(h * POOL_GROUP_DIM, (h + 1) * POOL_GROUP_DIM)
            band = ufull[pl.ds(r0, 2 * BLOCK), cols]
            ssum = jnp.dot(bands[h], band, preferred_element_type=F32)
            xg = ufull[pl.ds(r0 + BLOCK, BLOCK), cols].astype(F32)
            count = jnp.minimum(pos + 1, w).astype(F32)
            pooled = ssum / count - xg
            mixed = jnp.dot(pooled.astype(BF16), wp_ref[h],
                            preferred_element_type=F32)
            pool_ref[pl.ds(r0, BLOCK), cols] = (
                mixed * ps_ref[:, cols]).astype(BF16)
        return carry

    lax.fori_loop(0, TM_MIX // BLOCK, sb_body, 0)


def _mixers(proj, sinks, w_pool_b, pool_scale, seq):
    t = proj.shape[0]
    assert len(POOL_WINDOWS) == KV_HEADS
    steps_per_seq = seq // TM_MIX
    sub = TM_MIX // BLOCK
    half = POOL_WIDTH // 2
    qcol = Q_COL // ATTN_WIDTH
    kcol = K_COL // KV_WIDTH
    ucol = U_COL // half
    prev = lambda i: jnp.maximum(i * sub - 1, 0)
    nq = Q_PER_KV * BLOCK
    sink_rows = jnp.repeat(sinks.astype(F32).reshape(KV_HEADS, Q_PER_KV), BLOCK,
                           axis=1)
    fill = jnp.full((KV_HEADS, nq, 2 * BLOCK), MASK_VALUE, F32)
    fill = fill.at[:, :, 0].set(sink_rows)
    return pl.pallas_call(
        functools.partial(_mix_kernel, steps_per_seq=steps_per_seq),
        out_shape=(jax.ShapeDtypeStruct((t, ATTN_WIDTH), BF16),
                   jax.ShapeDtypeStruct((t, POOL_WIDTH), BF16)),
        grid=(t // TM_MIX,),
        in_specs=[
            pl.BlockSpec((KV_HEADS, nq, 2 * BLOCK), lambda i: (0, 0, 0)),
            pl.BlockSpec((TM_MIX, ATTN_WIDTH), lambda i: (i, qcol)),
            pl.BlockSpec((TM_MIX, KV_WIDTH), lambda i: (i, kcol)),
            pl.BlockSpec((TM_MIX, KV_WIDTH), lambda i: (i, kcol + 1)),
            pl.BlockSpec((BLOCK, KV_WIDTH), lambda i: (prev(i), kcol)),
            pl.BlockSpec((BLOCK, KV_WIDTH), lambda i: (prev(i), kcol + 1)),
            pl.BlockSpec((TM_MIX, half), lambda i: (i, ucol)),
            pl.BlockSpec((TM_MIX, half), lambda i: (i, ucol + 1)),
            pl.BlockSpec((BLOCK, half), lambda i: (prev(i), ucol)),
            pl.BlockSpec((BLOCK, half), lambda i: (prev(i), ucol + 1)),
            pl.BlockSpec((len(POOL_WINDOWS), POOL_GROUP_DIM, POOL_GROUP_DIM),
                         lambda i: (0, 0, 0)),
            pl.BlockSpec((1, POOL_WIDTH), lambda i: (0, 0)),
        ],
        out_specs=(pl.BlockSpec((TM_MIX, ATTN_WIDTH), lambda i: (i, 0)),
                   pl.BlockSpec((TM_MIX, POOL_WIDTH), lambda i: (i, 0))),
        scratch_shapes=[pltpu.VMEM((BLOCK + TM_MIX, KV_WIDTH), BF16),
                        pltpu.VMEM((BLOCK + TM_MIX, KV_WIDTH), BF16),
                        pltpu.VMEM((BLOCK + TM_MIX, POOL_WIDTH), BF16)],
        compiler_params=pltpu.CompilerParams(
            dimension_semantics=("arbitrary",), vmem_limit_bytes=VMEM_LIMIT),
        name="mixers",
    )(fill, proj, proj, proj, proj, proj, proj, proj, proj, proj, w_pool_b,
      pool_scale)


def _first_argmax4(v):
    m = jnp.maximum(jnp.maximum(v[0], v[1]), jnp.maximum(v[2], v[3]))
    idx = jnp.where(v[0] == m, 0, jnp.where(v[1] == m, 1,
                                            jnp.where(v[2] == m, 2, 3)))
    return m, idx


def _merge_kernel(attn_ref, pool_ref, ga_ref, gp_ref, x_ref, ada_ref,
                  wa_ref, wp_ref, wo_ref, g2_ref, wr_ref, br_ref,
                  eg_ref, eu_ref, ed_ref,
                  x2_ref, h2_ref, route_ref, counts_ref, egu_o, ed_o, lg_ref):
    i = pl.program_id(0)
    n_tile = pl.num_programs(0) - 1

    @pl.when(i == 0)
    def _():
        counts_ref[...] = jnp.zeros_like(counts_ref)
        lg_ref[...] = jnp.zeros_like(lg_ref)

    @pl.when(i < n_tile)
    def _():
        _route_tile(lg_ref[...], i > 0, route_ref, counts_ref)
        egu_o[:, 0:EXPERT_FF] = eg_ref[...].astype(BF16)
        egu_o[:, EXPERT_FF:2 * EXPERT_FF] = eu_ref[...].astype(BF16)
        ed_o[...] = ed_ref[...].astype(BF16)
        a = jnp.dot(attn_ref[...], wa_ref[...], preferred_element_type=F32)
        p = jnp.dot(pool_ref[...], wp_ref[...], preferred_element_type=F32)
        merged = ga_ref[...].astype(F32) * a + gp_ref[...].astype(F32) * p
        out = jnp.dot(merged.astype(BF16), wo_ref[...], preferred_element_type=F32)
        x2 = x_ref[...] + ada_ref[0, 2:3, :] * out
        x2_ref[...] = x2
        h2 = _rms_modulate(x2, g2_ref[...], ada_ref[0, 4:5, :], ada_ref[0, 3:4, :])
        h2_ref[...] = h2
        logits = jnp.dot(h2.astype(BF16), wr_ref[...], preferred_element_type=F32)
        lg_ref[...] = logits.T[0:ROUTER_ROWS, :] + br_ref[:, 0:1]

    @pl.when(i == n_tile)
    def _():
        _route_tile(lg_ref[...], True, route_ref, counts_ref)


def _route_tile(logits, live, route_ref, counts_ref):
    tm = logits.shape[1]
    rows = [logits[r:r + 1, :] for r in range(N_GROUPS + N_EXPERTS)]
    _, gi = _first_argmax4(rows[0:N_GROUPS])
    sel = []
    for e in range(EXPERTS_PER_GROUP):
        v = rows[N_GROUPS + 3 * EXPERTS_PER_GROUP + e]
        for g in range(N_GROUPS - 2, -1, -1):
            v = jnp.where(gi == g, rows[N_GROUPS + g * EXPERTS_PER_GROUP + e], v)
        sel.append(v)
    _, i1 = _first_argmax4(sel)
    rest = [jnp.where(i1 == e, -jnp.inf, sel[e]) for e in range(EXPERTS_PER_GROUP)]
    _, i2 = _first_argmax4(rest)
    code = jnp.minimum(i1, i2) * EXPERTS_PER_GROUP + jnp.maximum(i1, i2)
    cig = jnp.where(code == 1, 0, jnp.where(code == 2, 1, jnp.where(
        code == 3, 2, jnp.where(code == 7, 3, jnp.where(code == 6, 4, 5)))))
    cls = gi * PAIRS_PER_GROUP + cig

    onehot = (lax.broadcasted_iota(jnp.int32, (ROUTER_ROWS, tm), 0) == cls
              ).astype(F32)
    onehot = jnp.where(live, onehot, 0.0)
    src = lax.broadcasted_iota(jnp.int32, (tm, tm), 0)
    dst = lax.broadcasted_iota(jnp.int32, (tm, tm), 1)
    before = (src < dst).astype(F32).astype(BF16)
    prefix = jnp.dot(onehot.astype(BF16), before, preferred_element_type=F32)
    carry = counts_ref[:, 0:1]
    rank = jnp.sum(onehot * (prefix + carry), axis=0, keepdims=True)
    counts_ref[...] = counts_ref[...] + jnp.sum(onehot, axis=1, keepdims=True)
    route_ref[0:1, :] = cls
    route_ref[1:2, :] = rank.astype(jnp.int32)
    route_ref[2:8, :] = jnp.zeros((6, tm), jnp.int32)


def _merge(attn, pool, proj, x2d, ada3, wa_b, wp_b, wo_b, norm2_g, wr_lanes, br_col,
           w_e_gate, w_e_up, w_e_down, seq):
    t, d = x2d.shape
    n_step = t // TM_MERGE
    tiles_per_seq = seq // TM_MERGE
    cur = lambda i: jnp.minimum(i, n_step - 1)
    lag = lambda i: jnp.maximum(i - 1, 0)
    flat = lambda w: w.reshape(-1, w.shape[-1])
    slab = lambda w: pl.BlockSpec((w.shape[0] // n_step, w.shape[1]),
                                  lambda i: (cur(i), 0))
    cast = lambda w: jax.ShapeDtypeStruct(w.shape, BF16)
    eg, eu, ed = flat(w_e_gate), flat(w_e_up), flat(w_e_down)
    egu = jax.ShapeDtypeStruct((eg.shape[0], 2 * EXPERT_FF), BF16)
    const = lambda shape: pl.BlockSpec(shape, lambda i: (0,) * len(shape),
                                       pipeline_mode=pl.Buffered(1))
    return pl.pallas_call(
        _merge_kernel,
        out_shape=(jax.ShapeDtypeStruct((t, d), F32),
                   jax.ShapeDtypeStruct((t, d), F32),
                   jax.ShapeDtypeStruct((8, t), jnp.int32),
                   jax.ShapeDtypeStruct((ROUTER_ROWS, 128), F32),
                   egu, cast(ed)),
        grid=(n_step + 1,),
        in_specs=[
            pl.BlockSpec((TM_MERGE, ATTN_WIDTH), lambda i: (cur(i), 0)),
            pl.BlockSpec((TM_MERGE, POOL_WIDTH), lambda i: (cur(i), 0)),
            pl.BlockSpec((TM_MERGE, d), lambda i: (cur(i), 0)),
            pl.BlockSpec((TM_MERGE, d), lambda i: (cur(i), 1)),
            pl.BlockSpec((TM_MERGE, d), lambda i: (cur(i), 0)),
            pl.BlockSpec((1, N_ADA, d),
                         lambda i: (cur(i) // tiles_per_seq, 0, 0)),
            const((ATTN_WIDTH, d)),
            const((POOL_WIDTH, d)),
            const((d, d)),
            const((1, d)),
            const((d, ROUTER_LANES)),
            const((ROUTER_ROWS, 128)),
            slab(eg), slab(eu), slab(ed),
        ],
        out_specs=(
            pl.BlockSpec((TM_MERGE, d), lambda i: (cur(i), 0)),
            pl.BlockSpec((TM_MERGE, d), lambda i: (cur(i), 0)),
            pl.BlockSpec((8, TM_MERGE), lambda i: (0, lag(i))),
            pl.BlockSpec((ROUTER_ROWS, 128), lambda i: (0, 0)),
            slab(egu), slab(ed),
        ),
        scratch_shapes=[pltpu.VMEM((ROUTER_ROWS, TM_MERGE), F32)],
        compiler_params=pltpu.CompilerParams(
            dimension_semantics=("arbitrary",), vmem_limit_bytes=VMEM_LIMIT),
        name="merge",
    )(attn, pool, proj, proj, x2d, ada3, wa_b, wp_b, wo_b, norm2_g, wr_lanes, br_col,
      eg, eu, ed)


def _row_copy(src_ref, src_row, dst_ref, dst_row, sem):
    return pltpu.make_async_copy(src_ref.at[pl.ds(src_row, 1)],
                                 dst_ref.at[pl.ds(dst_row, 1)], sem)


def _sorted_row(cls_ref, rank_ref, base_ref, token):
    return base_ref[cls_ref[token]] + rank_ref[token]


def _dispatch_kernel(cls_ref, rank_ref, base_ref, h_ref, hs_ref, sem):
    i = pl.program_id(0)
    first = i * TM_DISPATCH

    for r in range(TM_DISPATCH):
        _row_copy(h_ref, r, hs_ref,
                  _sorted_row(cls_ref, rank_ref, base_ref, first + r), sem).start()
    pltpu.make_async_copy(h_ref, hs_ref.at[pl.ds(0, TM_DISPATCH)], sem).wait()


def _dispatch(cls, rank, base, h2, n_rows):
    t, d = h2.shape
    return pl.pallas_call(
        _dispatch_kernel,
        out_shape=jax.ShapeDtypeStruct((n_rows, d), F32),
        grid_spec=pltpu.PrefetchScalarGridSpec(
            num_scalar_prefetch=3,
            grid=(t // TM_DISPATCH,),
            in_specs=[pl.BlockSpec((TM_DISPATCH, d), lambda i, *_: (i, 0))],
            out_specs=pl.BlockSpec(memory_space=pl.ANY),
            scratch_shapes=[pltpu.SemaphoreType.DMA(())],
        ),
        compiler_params=pltpu.CompilerParams(
            dimension_semantics=("arbitrary",), vmem_limit_bytes=VMEM_LIMIT,
            has_side_effects=True),
        name="dispatch",
    )(cls, rank, base, h2)


def _expert_kernel(blk_ref, ea_ref, eb_ref, lane_ref, lead_ref, nused_ref,
                   hs_ref, wgua_ref, wda_ref, wgub_ref, wdb_ref,
                   wr_ref, br_ref, ys_ref):
    i = pl.program_id(0)
    used = i < nused_ref[0]
    lead = lead_ref[i]
    half = TM_EXPERT // 2

    def tile_rows(r0):
        tm = TM_EXPERT - r0
        rows = r0 + lax.broadcasted_iota(jnp.int32, (tm, 1), 0)
        h = jnp.where(rows >= lead, hs_ref[r0:TM_EXPERT, :], 0.0).astype(BF16)
        logits = jnp.dot(h, wr_ref[...], preferred_element_type=F32) + br_ref[...]
        lane = lax.broadcasted_iota(jnp.int32, logits.shape, 1)
        is_group = lane < N_GROUPS
        gmax = jnp.max(jnp.where(is_group, logits, -jnp.inf), axis=1, keepdims=True)
        gexp = jnp.where(is_group, jnp.exp(logits - gmax), 0.0)
        pick = lambda v, l: jnp.sum(jnp.where(lane == l, v, 0.0), axis=1,
                                    keepdims=True)
        p_group = pick(gexp, lane_ref[0, i]) / jnp.sum(gexp, axis=1, keepdims=True)
        la = pick(logits, lane_ref[1, i])
        lb = pick(logits, lane_ref[2, i])
        mx = jnp.maximum(la, lb)
        ea = jnp.exp(la - mx)
        eb = jnp.exp(lb - mx)
        inv = p_group / (ea + eb)

        def expert(wgu_ref, wd_ref, weight):
            gu = jnp.dot(h, wgu_ref[0], preferred_element_type=F32)
            gate, up = gu[:, 0:EXPERT_FF], gu[:, EXPERT_FF:2 * EXPERT_FF]
            act = (jax.nn.silu(gate) * up * weight).astype(BF16)
            return jnp.dot(act, wd_ref[0], preferred_element_type=F32)

        ys_ref[r0:TM_EXPERT, :] = (expert(wgua_ref, wda_ref, ea * inv)
                                   + expert(wgub_ref, wdb_ref, eb * inv))

    @pl.when(jnp.logical_not(used))
    def _():
        ys_ref[...] = jnp.zeros_like(ys_ref)

    @pl.when(jnp.logical_and(used, lead < half))
    def _():
        tile_rows(0)

    @pl.when(jnp.logical_and(used, lead >= half))
    def _():
        ys_ref[0:half, :] = jnp.zeros((half, ys_ref.shape[1]), ys_ref.dtype)
        tile_rows(half)


def _experts(tile_blk, tile_ea, tile_eb, tile_lanes, tile_lead, n_used,
             hs, wgu_b, wd_b, wr_lanes, br_lanes):
    p_rows, d = hs.shape
    n_tiles = p_rows // TM_EXPERT
    row_map = lambda i, blk, ea, eb, ln, nv, nu: (blk[i], 0)
    wa_map = lambda i, blk, ea, eb, ln, nv, nu: (ea[i], 0, 0)
    wb_map = lambda i, blk, ea, eb, ln, nv, nu: (eb[i], 0, 0)
    const = lambda i, blk, ea, eb, ln, nv, nu: (0, 0)
    return pl.pallas_call(
        _expert_kernel,
        out_shape=jax.ShapeDtypeStruct((p_rows, d), F32),
        grid_spec=pltpu.PrefetchScalarGridSpec(
            num_scalar_prefetch=6,
            grid=(n_tiles,),
            in_specs=[
                pl.BlockSpec((TM_EXPERT, d), row_map),
                pl.BlockSpec((1, d, 2 * EXPERT_FF), wa_map),
                pl.BlockSpec((1, EXPERT_FF, d), wa_map),
                pl.BlockSpec((1, d, 2 * EXPERT_FF), wb_map),
                pl.BlockSpec((1, EXPERT_FF, d), wb_map),
                pl.BlockSpec((d, ROUTER_LANES), const),
                pl.BlockSpec((1, ROUTER_LANES), const),
            ],
            out_specs=pl.BlockSpec((TM_EXPERT, d), row_map),
        ),
        compiler_params=pltpu.CompilerParams(
            dimension_semantics=("arbitrary",), vmem_limit_bytes=VMEM_LIMIT),
        name="experts",
    )(tile_blk, tile_ea, tile_eb, tile_lanes, tile_lead, n_used,
      hs, wgu_b, wd_b, wgu_b, wd_b, wr_lanes, br_lanes)


def _combine_kernel(cls_ref, rank_ref, base_ref, ys_ref, x2_ref, ada_ref, g_ref,
                    o_ref, ybuf, sems):
    i = pl.program_id(0)
    n = pl.num_programs(0)

    def gather(tile, slot):
        for r in range(TM_COMBINE):
            row = _sorted_row(cls_ref, rank_ref, base_ref, tile * TM_COMBINE + r)
            _row_copy(ys_ref, row, ybuf.at[slot], r, sems.at[slot]).start()

    @pl.when(i == 0)
    def _():
        gather(0, 0)

    @pl.when(i + 1 < n)
    def _():
        gather(i + 1, (i + 1) % 2)

    slot = i % 2
    pltpu.make_async_copy(ys_ref.at[pl.ds(0, TM_COMBINE)], ybuf.at[slot],
                          sems.at[slot]).wait()

    x = x2_ref[...] + ada_ref[0, 5:6, :] * ybuf[slot]
    ms = jnp.mean(x * x, axis=-1, keepdims=True)
    o_ref[...] = x * lax.rsqrt(ms + NORM_EPS) * g_ref[...]


def _combine(cls, rank, base, ys, x2, ada3, final_g, seq):
    t, d = x2.shape
    tiles_per_seq = seq // TM_COMBINE
    return pl.pallas_call(
        _combine_kernel,
        out_shape=jax.ShapeDtypeStruct((t, d), F32),
        grid_spec=pltpu.PrefetchScalarGridSpec(
            num_scalar_prefetch=3,
            grid=(t // TM_COMBINE,),
            in_specs=[
                pl.BlockSpec(memory_space=pl.ANY),
                pl.BlockSpec((TM_COMBINE, d), lambda i, *_: (i, 0)),
                pl.BlockSpec((1, N_ADA, d),
                             lambda i, *_: (i // tiles_per_seq, 0, 0)),
                pl.BlockSpec((1, d), lambda i, *_: (0, 0)),
            ],
            out_specs=pl.BlockSpec((TM_COMBINE, d), lambda i, *_: (i, 0)),
            scratch_shapes=[pltpu.VMEM((2, TM_COMBINE, d), F32),
                            pltpu.SemaphoreType.DMA((2,))],
        ),
        compiler_params=pltpu.CompilerParams(
            dimension_semantics=("arbitrary",), vmem_limit_bytes=VMEM_LIMIT),
        name="combine",
    )(cls, rank, base, ys, x2, ada3, final_g)


def _routing_tables(route, counts, n_tiles):
    cls = route[0]
    rank = route[1]
    cnt = counts[:N_CLASSES, 0].astype(jnp.int32)
    tiles_c = (cnt + TM_EXPERT - 1) // TM_EXPERT
    tile_end = jnp.cumsum(tiles_c)
    tile_start = tile_end - tiles_c
    n_used = tile_end[-1]
    lead_c = tiles_c * TM_EXPERT - cnt
    base = (tile_start * TM_EXPERT + lead_c).astype(jnp.int32)

    steps = jnp.arange(n_tiles, dtype=jnp.int32)
    k = jnp.minimum(steps, n_used - 1)
    c = jnp.sum((k[:, None] >= tile_end[None, :]).astype(jnp.int32), axis=1)
    grp = c // PAIRS_PER_GROUP
    cig = c % PAIRS_PER_GROUP
    slot_a = jnp.asarray(CLASS_SLOT_A, jnp.int32)[cig]
    slot_b = jnp.asarray(CLASS_SLOT_B, jnp.int32)[cig]
    ea = grp * EXPERTS_PER_GROUP + slot_a
    eb = grp * EXPERTS_PER_GROUP + slot_b
    lanes = jnp.stack([grp, N_GROUPS + ea, N_GROUPS + eb]).astype(jnp.int32)
    lead = jnp.where(k == tile_start[c], lead_c[c], 0)
    return (cls, rank, base, steps, ea.astype(jnp.int32),
            eb.astype(jnp.int32), lanes, lead.astype(jnp.int32),
            n_used.reshape(1).astype(jnp.int32))


def kernel(x, c, w_ada, b_ada, norm1_g, w_in, sinks, w_pool, pool_scale,
           w_attn_branch, w_pool_branch, w_out, norm2_g, w_router_group,
           b_router_group, w_router_expert, b_router_expert, w_e_gate, w_e_up,
           w_e_down, final_g):
    b, s, d = x.shape
    t = b * s
    assert w_ada.shape[0] == 1, "single-layer block"
    assert d == D_MODEL and s % TM_PROJ == 0
    l = 0
    n_tiles = t // TM_EXPERT + N_CLASSES
    x2d = x.reshape(t, d)
    ada3 = _ada(c, w_ada[l], b_ada[l]).reshape(b, N_ADA, d)

    proj, wa_b, wp_b, wo_b = _in_proj(
        x2d, ada3, norm1_g[l].reshape(1, d), w_in[l], w_attn_branch[l],
        w_pool_branch[l], w_out[l], s)
    attn, pool = _mixers(proj, sinks[l], w_pool[l].astype(BF16),
                         pool_scale[l].reshape(1, -1), s)

    w_r = jnp.concatenate([w_router_group[l], w_router_expert[l]], axis=1)
    b_r = jnp.concatenate([b_router_group[l], b_router_expert[l]])
    n_r = N_GROUPS + N_EXPERTS
    br_col = jnp.broadcast_to(
        jnp.pad(b_r, (0, ROUTER_ROWS - n_r))[:, None], (ROUTER_ROWS, 128))
    wr_lanes = jnp.pad(w_r, ((0, 0), (0, ROUTER_LANES - n_r))).astype(BF16)
    br_lanes = jnp.pad(b_r, (0, ROUTER_LANES - n_r)).reshape(1, ROUTER_LANES)

    x2, h2, route, counts, egu_b, ed_b = _merge(
        attn, pool, proj, x2d, ada3, wa_b, wp_b, wo_b, norm2_g[l].reshape(1, d),
        wr_lanes, br_col, w_e_gate[l], w_e_up[l], w_e_down[l], s)

    cls, rank, base, tile_blk, tile_ea, tile_eb, tile_lanes, tile_lead, n_used = (
        _routing_tables(route, counts, n_tiles))
    hs = _dispatch(cls, rank, base, h2, n_tiles * TM_EXPERT)
    ys = _experts(tile_blk, tile_ea, tile_eb, tile_lanes, tile_lead, n_used,
                  hs, egu_b.reshape(N_EXPERTS, d, 2 * EXPERT_FF),
                  ed_b.reshape(w_e_down[l].shape), wr_lanes, br_lanes)
    out = _combine(cls, rank, base, ys, x2, ada3, final_g.reshape(1, d), s)
    return out.reshape(b, s, d)
```

```python
import functools

import jax
import jax.numpy as jnp
from jax import lax
from jax.experimental import pallas as pl
from jax.experimental.pallas import tpu as pltpu

F32 = jnp.float32
BF16 = jnp.bfloat16

D_MODEL = 2048
HEAD_DIM = 64
ATTN_HEADS = 16
KV_HEADS = 4
Q_PER_KV = ATTN_HEADS // KV_HEADS
ATTN_WIDTH = ATTN_HEADS * HEAD_DIM
KV_WIDTH = KV_HEADS * HEAD_DIM
BLOCK = 128
POOL_WINDOWS = (2, 4, 8, 16)
POOL_WIDTH = 1024
POOL_GROUP_DIM = 256
QKVU_WIDTH = ATTN_WIDTH + 2 * KV_WIDTH + POOL_WIDTH
GATE_WIDTH = 2 * D_MODEL
PROJ_WIDTH = GATE_WIDTH + QKVU_WIDTH
Q_COL = GATE_WIDTH
K_COL = Q_COL + ATTN_WIDTH
V_COL = K_COL + KV_WIDTH
U_COL = V_COL + KV_WIDTH
N_GROUPS = 4
EXPERTS_PER_GROUP = 4
N_EXPERTS = 16
EXPERT_FF = 512
N_ADA = 6
NORM_EPS = 1e-6
MASK_VALUE = -1e30

PAIRS_PER_GROUP = 6
N_CLASSES = N_GROUPS * PAIRS_PER_GROUP
CLASS_SLOT_A = (0, 0, 0, 1, 1, 3)
CLASS_SLOT_B = (1, 2, 3, 3, 2, 2)
ROUTER_ROWS = 32
ROUTER_LANES = 128

TM_PROJ = 2048
TN_PROJ = 512
NORM_ROWS = 256
NORM_CHUNKS = TM_PROJ // NORM_ROWS
TM_MIX = 1024
TM_MERGE = 256
TM_DISPATCH = 512
TM_EXPERT = 256
TM_COMBINE = 512
TN_ADA = 1024

VMEM_LIMIT = 52 * 1024 * 1024


def _rms_modulate(x, g, scale, shift):
    ms = jnp.mean(x * x, axis=-1, keepdims=True)
    return (x * lax.rsqrt(ms + NORM_EPS)) * (g * (1.0 + scale)) + shift


def _ada_kernel(cb_ref, w_ref, b_ref, o_ref):
    nb = cb_ref.shape[0]
    d = w_ref.shape[0]
    nchunk = w_ref.shape[1] // 128

    def body(kb, accs):
        k0 = pl.multiple_of(kb * 8, 8)
        new = list(accs)
        cbs = [cb_ref[b, pl.ds(k0, 8), :] for b in range(nb)]
        for j in range(nchunk):
            w = w_ref[pl.ds(k0, 8), j * 128:(j + 1) * 128]
            for b in range(nb):
                new[b * nchunk + j] = new[b * nchunk + j] + w * cbs[b]
        return tuple(new)

    init = tuple(jnp.zeros((8, 128), F32) for _ in range(nb * nchunk))
    accs = lax.fori_loop(0, d // 8, body, init, unroll=2)
    for b in range(nb):
        for j in range(nchunk):
            o_ref[b:b + 1, j * 128:(j + 1) * 128] = (
                jnp.sum(accs[b * nchunk + j], axis=0, keepdims=True)
                + b_ref[:, j * 128:(j + 1) * 128])


def _ada(c, w_ada, b_ada):
    nb, d = c.shape
    n = w_ada.shape[1]
    cb = jnp.broadcast_to(c[:, :, None], (nb, d, 128))
    return pl.pallas_call(
        _ada_kernel,
        out_shape=jax.ShapeDtypeStruct((nb, n), F32),
        grid=(n // TN_ADA,),
        in_specs=[
            pl.BlockSpec((nb, d, 128), lambda j: (0, 0, 0)),
            pl.BlockSpec((d, TN_ADA), lambda j: (0, j)),
            pl.BlockSpec((1, TN_ADA), lambda j: (0, j)),
        ],
        out_specs=pl.BlockSpec((nb, TN_ADA), lambda j: (0, j)),
        compiler_params=pltpu.CompilerParams(
            dimension_semantics=("arbitrary",), vmem_limit_bytes=VMEM_LIMIT),
        name="ada",
    )(cb, w_ada, b_ada.reshape(1, n))


def _in_proj_kernel(x_ref, ada_ref, g_ref, w_ref, wa_ref, wp_ref, wo_ref,
                    proj_ref, wa_o, wp_o, wo_o, h_even, h_odd, *, n_row):
    i = pl.program_id(0)
    j = pl.program_id(1)

    def norm_chunk(h_ref):
        r0 = pl.multiple_of(j * NORM_ROWS, NORM_ROWS)
        h = _rms_modulate(x_ref[...], g_ref[...],
                          ada_ref[0, 1:2, :], ada_ref[0, 0:1, :])
        h_ref[pl.ds(r0, NORM_ROWS), :] = h.astype(BF16)

    def step(h_write, h_read, early):
        if early:
            norm_chunk(h_write)
        acc = jnp.dot(h_read[...], w_ref[...].astype(BF16),
                      preferred_element_type=F32)
        if early:
            acc = 0.5 * jnp.tanh(0.5 * acc) + 0.5
        proj_ref[...] = acc.astype(BF16)

    @pl.when(jnp.logical_and(j == 0, i < n_row))
    def _():
        wa_o[...] = wa_ref[...].astype(BF16)
        wp_o[...] = wp_ref[...].astype(BF16)
        wo_o[...] = wo_ref[...].astype(BF16)

    @pl.when(jnp.logical_and(i == 0, j < NORM_CHUNKS))
    def _():
        norm_chunk(h_even)

    is_early = j < NORM_CHUNKS
    odd = i % 2 == 1
    even = jnp.logical_and(i > 0, i % 2 == 0)
    for parity, h_write, h_read in ((odd, h_odd, h_even), (even, h_even, h_odd)):
        for early in (True, False):
            kind = is_early if early else jnp.logical_not(is_early)
            pl.when(jnp.logical_and(parity, kind))(
                functools.partial(step, h_write, h_read, early))


def _in_proj(x2d, ada3, norm_g, w_in, wa, wp, wo, seq):
    t, d = x2d.shape
    n_all = PROJ_WIDTH // TN_PROJ
    n_gate = GATE_WIDTH // TN_PROJ
    n_row = t // TM_PROJ
    tiles_per_seq = seq // TM_PROJ
    assert NORM_CHUNKS == n_gate
    norm_tile = lambda i: jnp.minimum(i, n_row - 1)
    out_tile = lambda i: jnp.maximum(i - 1, 0)
    w_tile = lambda i, j: (
        0, (jnp.where(i == 0, 0, j) + (n_all - n_gate)) % n_all)
    slab = lambda w: pl.BlockSpec((w.shape[0] // n_row, w.shape[1]),
                                  lambda i, j: (norm_tile(i), 0))
    cast = lambda w: jax.ShapeDtypeStruct(w.shape, BF16)
    return pl.pallas_call(
        functools.partial(_in_proj_kernel, n_row=n_row),
        out_shape=(jax.ShapeDtypeStruct((t, PROJ_WIDTH), BF16),
                   cast(wa), cast(wp), cast(wo)),
        grid=(n_row + 1, n_all),
        in_specs=[
            pl.BlockSpec((NORM_ROWS, d), lambda i, j: (
                norm_tile(i) * NORM_CHUNKS + jnp.minimum(j, NORM_CHUNKS - 1), 0)),
            pl.BlockSpec((1, N_ADA, d),
                         lambda i, j: (norm_tile(i) // tiles_per_seq, 0, 0)),
            pl.BlockSpec((1, d), lambda i, j: (0, 0)),
            pl.BlockSpec((d, TN_PROJ), w_tile),
            slab(wa), slab(wp), slab(wo),
        ],
        out_specs=(
            pl.BlockSpec((TM_PROJ, TN_PROJ),
                         lambda i, j: (out_tile(i), jnp.where(i == 0, 0, j))),
            slab(wa), slab(wp), slab(wo),
        ),
        scratch_shapes=[pltpu.VMEM((TM_PROJ, d), BF16),
                        pltpu.VMEM((TM_PROJ, d), BF16)],
        compiler_params=pltpu.CompilerParams(
            dimension_semantics=("arbitrary", "arbitrary"),
            vmem_limit_bytes=VMEM_LIMIT),
        name="in_proj",
    )(x2d, ada3, norm_g, w_in, wa, wp, wo)


def _mix_kernel(fill_ref, q_ref, kc_ref, vc_ref, kp_ref, vp_ref,
                ulo_ref, uhi_ref, plo_ref, phi_ref, wp_ref, ps_ref,
                attn_ref, pool_ref, kfull, vfull, ufull, *, steps_per_seq):
    i = pl.program_id(0)
    first_step = (i % steps_per_seq) == 0
    kfull[0:BLOCK, :] = kp_ref[...]
    kfull[BLOCK:BLOCK + TM_MIX, :] = kc_ref[...]
    vfull[0:BLOCK, :] = vp_ref[...]
    vfull[BLOCK:BLOCK + TM_MIX, :] = vc_ref[...]
    half = POOL_WIDTH // 2
    zeros = jnp.zeros((BLOCK, half), BF16)
    ufull[0:BLOCK, 0:half] = jnp.where(first_step, zeros, plo_ref[...])
    ufull[0:BLOCK, half:POOL_WIDTH] = jnp.where(first_step, zeros, phi_ref[...])
    ufull[BLOCK:BLOCK + TM_MIX, 0:half] = ulo_ref[...]
    ufull[BLOCK:BLOCK + TM_MIX, half:POOL_WIDTH] = uhi_ref[...]
    pos0 = (i % steps_per_seq) * TM_MIX
    nq = Q_PER_KV * BLOCK
    prow = lax.broadcasted_iota(jnp.int32, (BLOCK, 2 * BLOCK), 0)
    pcol = lax.broadcasted_iota(jnp.int32, (BLOCK, 2 * BLOCK), 1)
    rel = prow + BLOCK - pcol
    bands = [((rel >= 0) & (rel < w)).astype(F32).astype(BF16)
             for w in POOL_WINDOWS]

    def sb_body(sb, carry):
        r0 = pl.multiple_of(sb * BLOCK, BLOCK)
        row = lax.broadcasted_iota(jnp.int32, (nq, 2 * BLOCK), 0) & (BLOCK - 1)
        col = lax.broadcasted_iota(jnp.int32, (nq, 2 * BLOCK), 1)
        dist = col - row
        lo = jnp.where(jnp.logical_and(first_step, sb == 0), BLOCK, 0)
        valid = (dist > 0) & (dist <= BLOCK) & (col >= lo)
        key0 = lax.broadcasted_iota(jnp.int32, (2 * BLOCK, HEAD_DIM), 0) == 0
        pos = pos0 + sb * BLOCK + lax.broadcasted_iota(jnp.int32, (BLOCK, 1), 0)
        for h in range(KV_HEADS):
            kb = kfull[pl.ds(r0, 2 * BLOCK), h * HEAD_DIM:(h + 1) * HEAD_DIM]
            vb = vfull[pl.ds(r0, 2 * BLOCK), h * HEAD_DIM:(h + 1) * HEAD_DIM]
            vb = jnp.where(key0, jnp.zeros_like(vb), vb)
            qg = jnp.concatenate(
                [q_ref[pl.ds(r0, BLOCK),
                       (Q_PER_KV * h + g) * HEAD_DIM:(Q_PER_KV * h + g + 1) * HEAD_DIM]
                 for g in range(Q_PER_KV)], axis=0)
            qg = qg * jnp.asarray(HEAD_DIM ** -0.5, BF16)
            logits = lax.dot_general(qg, kb, (((1,), (1,)), ((), ())),
                                     preferred_element_type=F32)
            logits = jnp.where(valid, logits, fill_ref[h])
            m = jnp.max(logits, axis=1, keepdims=True)
            p = jnp.exp(logits - m)
            denom = jnp.sum(p, axis=1, keepdims=True)
            o = jnp.dot(p.astype(BF16), vb, preferred_element_type=F32)
            o = (o * (1.0 / denom)).astype(BF16)
            attn_ref[pl.ds(r0, BLOCK),
                     h * Q_PER_KV * HEAD_DIM:(h + 1) * Q_PER_KV * HEAD_DIM] = (
                jnp.concatenate([o[g * BLOCK:(g + 1) * BLOCK, :]
                                 for g in range(Q_PER_KV)], axis=1))

            w = POOL_WINDOWS[h]
            cols = slice(h * POOL_GROUP_DIM, (h + 1) * POOL_GROUP_DIM)
            band = ufull[pl.ds(r0, 2 * BLOCK), cols]
            ssum = jnp.dot(bands[h], band, preferred_element_type=F32)
            xg = ufull[pl.ds(r0 + BLOCK, BLOCK), cols].astype(F32)
            count = jnp.minimum(pos + 1, w).astype(F32)
            pooled = ssum / count - xg
            mixed = jnp.dot(pooled.astype(BF16), wp_ref[h],
                            preferred_element_type=F32)
            pool_ref[pl.ds(r0, BLOCK), cols] = (
                mixed * ps_ref[:, cols]).astype(BF16)
        return carry

    lax.fori_loop(0, TM_MIX // BLOCK, sb_body, 0)


def _mixers(proj, sinks, w_pool_b, pool_scale, seq):
    t = proj.shape[0]
    assert len(POOL_WINDOWS) == KV_HEADS
    steps_per_seq = seq // TM_MIX
    sub = TM_MIX // BLOCK
    half = POOL_WIDTH // 2
    qcol = Q_COL // ATTN_WIDTH
    kcol = K_COL // KV_WIDTH
    ucol = U_COL // half
    prev = lambda i: jnp.maximum(i * sub - 1, 0)
    nq = Q_PER_KV * BLOCK
    sink_rows = jnp.repeat(sinks.astype(F32).reshape(KV_HEADS, Q_PER_KV), BLOCK,
                           axis=1)
    fill = jnp.full((KV_HEADS, nq, 2 * BLOCK), MASK_VALUE, F32)
    fill = fill.at[:, :, 0].set(sink_rows)
    return pl.pallas_call(
        functools.partial(_mix_kernel, steps_per_seq=steps_per_seq),
        out_shape=(jax.ShapeDtypeStruct((t, ATTN_WIDTH), BF16),
                   jax.ShapeDtypeStruct((t, POOL_WIDTH), BF16)),
        grid=(t // TM_MIX,),
        in_specs=[
            pl.BlockSpec((KV_HEADS, nq, 2 * BLOCK), lambda i: (0, 0, 0)),
            pl.BlockSpec((TM_MIX, ATTN_WIDTH), lambda i: (i, qcol)),
            pl.BlockSpec((TM_MIX, KV_WIDTH), lambda i: (i, kcol)),
            pl.BlockSpec((TM_MIX, KV_WIDTH), lambda i: (i, kcol + 1)),
            pl.BlockSpec((BLOCK, KV_WIDTH), lambda i: (prev(i), kcol)),
            pl.BlockSpec((BLOCK, KV_WIDTH), lambda i: (prev(i), kcol + 1)),
            pl.BlockSpec((TM_MIX, half), lambda i: (i, ucol)),
            pl.BlockSpec((TM_MIX, half), lambda i: (i, ucol + 1)),
            pl.BlockSpec((BLOCK, half), lambda i: (prev(i), ucol)),
            pl.BlockSpec((BLOCK, half), lambda i: (prev(i), ucol + 1)),
            pl.BlockSpec((len(POOL_WINDOWS), POOL_GROUP_DIM, POOL_GROUP_DIM),
                         lambda i: (0, 0, 0)),
            pl.BlockSpec((1, POOL_WIDTH), lambda i: (0, 0)),
        ],
        out_specs=(pl.BlockSpec((TM_MIX, ATTN_WIDTH), lambda i: (i, 0)),
                   pl.BlockSpec((TM_MIX, POOL_WIDTH), lambda i: (i, 0))),
        scratch_shapes=[pltpu.VMEM((BLOCK + TM_MIX, KV_WIDTH), BF16),
                        pltpu.VMEM((BLOCK + TM_MIX, KV_WIDTH), BF16),
                        pltpu.VMEM((BLOCK + TM_MIX, POOL_WIDTH), BF16)],
        compiler_params=pltpu.CompilerParams(
            dimension_semantics=("arbitrary",), vmem_limit_bytes=VMEM_LIMIT),
        name="mixers",
    )(fill, proj, proj, proj, proj, proj, proj, proj, proj, proj, w_pool_b,
      pool_scale)


def _first_argmax4(v):
    m = jnp.maximum(jnp.maximum(v[0], v[1]), jnp.maximum(v[2], v[3]))
    idx = jnp.where(v[0] == m, 0, jnp.where(v[1] == m, 1,
                                            jnp.where(v[2] == m, 2, 3)))
    return m, idx


def _merge_kernel(attn_ref, pool_ref, ga_ref, gp_ref, x_ref, ada_ref,
                  wa_ref, wp_ref, wo_ref, g2_ref, wr_ref, br_ref,
                  eg_ref, eu_ref, ed_ref,
                  x2_ref, h2_ref, route_ref, counts_ref, egu_o, ed_o, lg_ref):
    i = pl.program_id(0)
    n_tile = pl.num_programs(0) - 1

    @pl.when(i == 0)
    def _():
        counts_ref[...] = jnp.zeros_like(counts_ref)
        lg_ref[...] = jnp.zeros_like(lg_ref)

    @pl.when(i < n_tile)
    def _():
        _route_tile(lg_ref[...], i > 0, route_ref, counts_ref)
        egu_o[:, 0:EXPERT_FF] = eg_ref[...].astype(BF16)
        egu_o[:, EXPERT_FF:2 * EXPERT_FF] = eu_ref[...].astype(BF16)
        ed_o[...] = ed_ref[...].astype(BF16)
        a = jnp.dot(attn_ref[...], wa_ref[...], preferred_element_type=F32)
        p = jnp.dot(pool_ref[...], wp_ref[...], preferred_element_type=F32)
        merged = ga_ref[...].astype(F32) * a + gp_ref[...].astype(F32) * p
        out = jnp.dot(merged.astype(BF16), wo_ref[...], preferred_element_type=F32)
        x2 = x_ref[...] + ada_ref[0, 2:3, :] * out
        x2_ref[...] = x2
        h2 = _rms_modulate(x2, g2_ref[...], ada_ref[0, 4:5, :], ada_ref[0, 3:4, :])
        h2_ref[...] = h2
        logits = jnp.dot(h2.astype(BF16), wr_ref[...], preferred_element_type=F32)
        lg_ref[...] = logits.T[0:ROUTER_ROWS, :] + br_ref[:, 0:1]

    @pl.when(i == n_tile)
    def _():
        _route_tile(lg_ref[...], True, route_ref, counts_ref)


def _route_tile(logits, live, route_ref, counts_ref):
    tm = logits.shape[1]
    rows = [logits[r:r + 1, :] for r in range(N_GROUPS + N_EXPERTS)]
    _, gi = _first_argmax4(rows[0:N_GROUPS])
    sel = []
    for e in range(EXPERTS_PER_GROUP):
        v = rows[N_GROUPS + 3 * EXPERTS_PER_GROUP + e]
        for g in range(N_GROUPS - 2, -1, -1):
            v = jnp.where(gi == g, rows[N_GROUPS + g * EXPERTS_PER_GROUP + e], v)
        sel.append(v)
    _, i1 = _first_argmax4(sel)
    rest = [jnp.where(i1 == e, -jnp.inf, sel[e]) for e in range(EXPERTS_PER_GROUP)]
    _, i2 = _first_argmax4(rest)
    code = jnp.minimum(i1, i2) * EXPERTS_PER_GROUP + jnp.maximum(i1, i2)
    cig = jnp.where(code == 1, 0, jnp.where(code == 2, 1, jnp.where(
        code == 3, 2, jnp.where(code == 7, 3, jnp.where(code == 6, 4, 5)))))
    cls = gi * PAIRS_PER_GROUP + cig

    onehot = (lax.broadcasted_iota(jnp.int32, (ROUTER_ROWS, tm), 0) == cls
              ).astype(F32)
    onehot = jnp.where(live, onehot, 0.0)
    src = lax.broadcasted_iota(jnp.int32, (tm, tm), 0)
    dst = lax.broadcasted_iota(jnp.int32, (tm, tm), 1)
    before = (src < dst).astype(F32).astype(BF16)
    prefix = jnp.dot(onehot.astype(BF16), before, preferred_element_type=F32)
    carry = counts_ref[:, 0:1]
    rank = jnp.sum(onehot * (prefix + carry), axis=0, keepdims=True)
    counts_ref[...] = counts_ref[...] + jnp.sum(onehot, axis=1, keepdims=True)
    route_ref[0:1, :] = cls
    route_ref[1:2, :] = rank.astype(jnp.int32)
    route_ref[2:8, :] = jnp.zeros((6, tm), jnp.int32)


def _merge(attn, pool, proj, x2d, ada3, wa_b, wp_b, wo_b, norm2_g, wr_lanes, br_col,
           w_e_gate, w_e_up, w_e_down, seq):
    t, d = x2d.shape
    n_step = t // TM_MERGE
    tiles_per_seq = seq // TM_MERGE
    cur = lambda i: jnp.minimum(i, n_step - 1)
    lag = lambda i: jnp.maximum(i - 1, 0)
    flat = lambda w: w.reshape(-1, w.shape[-1])
    slab = lambda w: pl.BlockSpec((w.shape[0] // n_step, w.shape[1]),
                                  lambda i: (cur(i), 0))
    cast = lambda w: jax.ShapeDtypeStruct(w.shape, BF16)
    eg, eu, ed = flat(w_e_gate), flat(w_e_up), flat(w_e_down)
    egu = jax.ShapeDtypeStruct((eg.shape[0], 2 * EXPERT_FF), BF16)
    const = lambda shape: pl.BlockSpec(shape, lambda i: (0,) * len(shape),
                                       pipeline_mode=pl.Buffered(1))
    return pl.pallas_call(
        _merge_kernel,
        out_shape=(jax.ShapeDtypeStruct((t, d), F32),
                   jax.ShapeDtypeStruct((t, d), F32),
                   jax.ShapeDtypeStruct((8, t), jnp.int32),
                   jax.ShapeDtypeStruct((ROUTER_ROWS, 128), F32),
                   egu, cast(ed)),
        grid=(n_step + 1,),
        in_specs=[
            pl.BlockSpec((TM_MERGE, ATTN_WIDTH), lambda i: (cur(i), 0)),
            pl.BlockSpec((TM_MERGE, POOL_WIDTH), lambda i: (cur(i), 0)),
            pl.BlockSpec((TM_MERGE, d), lambda i: (cur(i), 0)),
            pl.BlockSpec((TM_MERGE, d), lambda i: (cur(i), 1)),
            pl.BlockSpec((TM_MERGE, d), lambda i: (cur(i), 0)),
            pl.BlockSpec((1, N_ADA, d),
                         lambda i: (cur(i) // tiles_per_seq, 0, 0)),
            const((ATTN_WIDTH, d)),
            const((POOL_WIDTH, d)),
            const((d, d)),
            const((1, d)),
            const((d, ROUTER_LANES)),
            const((ROUTER_ROWS, 128)),
            slab(eg), slab(eu), slab(ed),
        ],
        out_specs=(
            pl.BlockSpec((TM_MERGE, d), lambda i: (cur(i), 0)),
            pl.BlockSpec((TM_MERGE, d), lambda i: (cur(i), 0)),
            pl.BlockSpec((8, TM_MERGE), lambda i: (0, lag(i))),
            pl.BlockSpec((ROUTER_ROWS, 128), lambda i: (0, 0)),
            slab(egu), slab(ed),
        ),
        scratch_shapes=[pltpu.VMEM((ROUTER_ROWS, TM_MERGE), F32)],
        compiler_params=pltpu.CompilerParams(
            dimension_semantics=("arbitrary",), vmem_limit_bytes=VMEM_LIMIT),
        name="merge",
    )(attn, pool, proj, proj, x2d, ada3, wa_b, wp_b, wo_b, norm2_g, wr_lanes, br_col,
      eg, eu, ed)


def _row_copy(src_ref, src_row, dst_ref, dst_row, sem):
    return pltpu.make_async_copy(src_ref.at[pl.ds(src_row, 1)],
                                 dst_ref.at[pl.ds(dst_row, 1)], sem)


def _sorted_row(cls_ref, rank_ref, base_ref, token):
    return base_ref[cls_ref[token]] + rank_ref[token]


def _dispatch_kernel(cls_ref, rank_ref, base_ref, h_ref, hs_ref, sem):
    i = pl.program_id(0)
    first = i * TM_DISPATCH

    for r in range(TM_DISPATCH):
        _row_copy(h_ref, r, hs_ref,
                  _sorted_row(cls_ref, rank_ref, base_ref, first + r), sem).start()
    pltpu.make_async_copy(h_ref, hs_ref.at[pl.ds(0, TM_DISPATCH)], sem).wait()


def _dispatch(cls, rank, base, h2, n_rows):
    t, d = h2.shape
    return pl.pallas_call(
        _dispatch_kernel,
        out_shape=jax.ShapeDtypeStruct((n_rows, d), F32),
        grid_spec=pltpu.PrefetchScalarGridSpec(
            num_scalar_prefetch=3,
            grid=(t // TM_DISPATCH,),
            in_specs=[pl.BlockSpec((TM_DISPATCH, d), lambda i, *_: (i, 0))],
            out_specs=pl.BlockSpec(memory_space=pl.ANY),
            scratch_shapes=[pltpu.SemaphoreType.DMA(())],
        ),
        compiler_params=pltpu.CompilerParams(
            dimension_semantics=("arbitrary",), vmem_limit_bytes=VMEM_LIMIT,
            has_side_effects=True),
        name="dispatch",
    )(cls, rank, base, h2)


def _expert_kernel(blk_ref, ea_ref, eb_ref, lane_ref, lead_ref, nused_ref,
                   hs_ref, wgua_ref, wda_ref, wgub_ref, wdb_ref,
                   wr_ref, br_ref, ys_ref):
    i = pl.program_id(0)
    used = i < nused_ref[0]
    lead = lead_ref[i]
    half = TM_EXPERT // 2

    def tile_rows(r0):
        tm = TM_EXPERT - r0
        rows = r0 + lax.broadcasted_iota(jnp.int32, (tm, 1), 0)
        h = jnp.where(rows >= lead, hs_ref[r0:TM_EXPERT, :], 0.0).astype(BF16)
        logits = jnp.dot(h, wr_ref[...], preferred_element_type=F32) + br_ref[...]
        lane = lax.broadcasted_iota(jnp.int32, logits.shape, 1)
        is_group = lane < N_GROUPS
        gmax = jnp.max(jnp.where(is_group, logits, -jnp.inf), axis=1, keepdims=True)
        gexp = jnp.where(is_group, jnp.exp(logits - gmax), 0.0)
        pick = lambda v, l: jnp.sum(jnp.where(lane == l, v, 0.0), axis=1,
                                    keepdims=True)
        p_group = pick(gexp, lane_ref[0, i]) / jnp.sum(gexp, axis=1, keepdims=True)
        la = pick(logits, lane_ref[1, i])
        lb = pick(logits, lane_ref[2, i])
        mx = jnp.maximum(la, lb)
        ea = jnp.exp(la - mx)
        eb = jnp.exp(lb - mx)
        inv = p_group / (ea + eb)

        def expert(wgu_ref, wd_ref, weight):
            gu = jnp.dot(h, wgu_ref[0], preferred_element_type=F32)
            gate, up = gu[:, 0:EXPERT_FF], gu[:, EXPERT_FF:2 * EXPERT_FF]
            act = (jax.nn.silu(gate) * up * weight).astype(BF16)
            return jnp.dot(act, wd_ref[0], preferred_element_type=F32)

        ys_ref[r0:TM_EXPERT, :] = (expert(wgua_ref, wda_ref, ea * inv)
                                   + expert(wgub_ref, wdb_ref, eb * inv))

    @pl.when(jnp.logical_not(used))
    def _():
        ys_ref[...] = jnp.zeros_like(ys_ref)

    @pl.when(jnp.logical_and(used, lead < half))
    def _():
        tile_rows(0)

    @pl.when(jnp.logical_and(used, lead >= half))
    def _():
        ys_ref[0:half, :] = jnp.zeros((half, ys_ref.shape[1]), ys_ref.dtype)
        tile_rows(half)


def _experts(tile_blk, tile_ea, tile_eb, tile_lanes, tile_lead, n_used,
             hs, wgu_b, wd_b, wr_lanes, br_lanes):
    p_rows, d = hs.shape
    n_tiles = p_rows // TM_EXPERT
    row_map = lambda i, blk, ea, eb, ln, nv, nu: (blk[i], 0)
    wa_map = lambda i, blk, ea, eb, ln, nv, nu: (ea[i], 0, 0)
    wb_map = lambda i, blk, ea, eb, ln, nv, nu: (eb[i], 0, 0)
    const = lambda i, blk, ea, eb, ln, nv, nu: (0, 0)
    return pl.pallas_call(
        _expert_kernel,
        out_shape=jax.ShapeDtypeStruct((p_rows, d), F32),
        grid_spec=pltpu.PrefetchScalarGridSpec(
            num_scalar_prefetch=6,
            grid=(n_tiles,),
            in_specs=[
                pl.BlockSpec((TM_EXPERT, d), row_map),
                pl.BlockSpec((1, d, 2 * EXPERT_FF), wa_map),
                pl.BlockSpec((1, EXPERT_FF, d), wa_map),
                pl.BlockSpec((1, d, 2 * EXPERT_FF), wb_map),
                pl.BlockSpec((1, EXPERT_FF, d), wb_map),
                pl.BlockSpec((d, ROUTER_LANES), const),
                pl.BlockSpec((1, ROUTER_LANES), const),
            ],
            out_specs=pl.BlockSpec((TM_EXPERT, d), row_map),
        ),
        compiler_params=pltpu.CompilerParams(
            dimension_semantics=("arbitrary",), vmem_limit_bytes=VMEM_LIMIT),
        name="experts",
    )(tile_blk, tile_ea, tile_eb, tile_lanes, tile_lead, n_used,
      hs, wgu_b, wd_b, wgu_b, wd_b, wr_lanes, br_lanes)


def _combine_kernel(cls_ref, rank_ref, base_ref, ys_ref, x2_ref, ada_ref, g_ref,
                    o_ref, ybuf, sems):
    i = pl.program_id(0)
    n = pl.num_programs(0)

    def gather(tile, slot):
        for r in range(TM_COMBINE):
            row = _sorted_row(cls_ref, rank_ref, base_ref, tile * TM_COMBINE + r)
            _row_copy(ys_ref, row, ybuf.at[slot], r, sems.at[slot]).start()

    @pl.when(i == 0)
    def _():
        gather(0, 0)

    @pl.when(i + 1 < n)
    def _():
        gather(i + 1, (i + 1) % 2)

    slot = i % 2
    pltpu.make_async_copy(ys_ref.at[pl.ds(0, TM_COMBINE)], ybuf.at[slot],
                          sems.at[slot]).wait()

    x = x2_ref[...] + ada_ref[0, 5:6, :] * ybuf[slot]
    ms = jnp.mean(x * x, axis=-1, keepdims=True)
    o_ref[...] = x * lax.rsqrt(ms + NORM_EPS) * g_ref[...]


def _combine(cls, rank, base, ys, x2, ada3, final_g, seq):
    t, d = x2.shape
    tiles_per_seq = seq // TM_COMBINE
    return pl.pallas_call(
        _combine_kernel,
        out_shape=jax.ShapeDtypeStruct((t, d), F32),
        grid_spec=pltpu.PrefetchScalarGridSpec(
            num_scalar_prefetch=3,
            grid=(t // TM_COMBINE,),
            in_specs=[
                pl.BlockSpec(memory_space=pl.ANY),
                pl.BlockSpec((TM_COMBINE, d), lambda i, *_: (i, 0)),
                pl.BlockSpec((1, N_ADA, d),
                             lambda i, *_: (i // tiles_per_seq, 0, 0)),
                pl.BlockSpec((1, d), lambda i, *_: (0, 0)),
            ],
            out_specs=pl.BlockSpec((TM_COMBINE, d), lambda i, *_: (i, 0)),
            scratch_shapes=[pltpu.VMEM((2, TM_COMBINE, d), F32),
                            pltpu.SemaphoreType.DMA((2,))],
        ),
        compiler_params=pltpu.CompilerParams(
            dimension_semantics=("arbitrary",), vmem_limit_bytes=VMEM_LIMIT),
        name="combine",
    )(cls, rank, base, ys, x2, ada3, final_g)


def _routing_tables(route, counts, n_tiles):
    cls = route[0]
    rank = route[1]
    cnt = counts[:N_CLASSES, 0].astype(jnp.int32)
    tiles_c = (cnt + TM_EXPERT - 1) // TM_EXPERT
    tile_end = jnp.cumsum(tiles_c)
    tile_start = tile_end - tiles_c
    n_used = tile_end[-1]
    lead_c = tiles_c * TM_EXPERT - cnt
    base = (tile_start * TM_EXPERT + lead_c).astype(jnp.int32)

    steps = jnp.arange(n_tiles, dtype=jnp.int32)
    k = jnp.minimum(steps, n_used - 1)
    c = jnp.sum((k[:, None] >= tile_end[None, :]).astype(jnp.int32), axis=1)
    grp = c // PAIRS_PER_GROUP
    cig = c % PAIRS_PER_GROUP
    slot_a = jnp.asarray(CLASS_SLOT_A, jnp.int32)[cig]
    slot_b = jnp.asarray(CLASS_SLOT_B, jnp.int32)[cig]
    ea = grp * EXPERTS_PER_GROUP + slot_a
    eb = grp * EXPERTS_PER_GROUP + slot_b
    lanes = jnp.stack([grp, N_GROUPS + ea, N_GROUPS + eb]).astype(jnp.int32)
    lead = jnp.where(k == tile_start[c], lead_c[c], 0)
    return (cls, rank, base, steps, ea.astype(jnp.int32),
            eb.astype(jnp.int32), lanes, lead.astype(jnp.int32),
            n_used.reshape(1).astype(jnp.int32))


def kernel(x, c, w_ada, b_ada, norm1_g, w_in, sinks, w_pool, pool_scale,
           w_attn_branch, w_pool_branch, w_out, norm2_g, w_router_group,
           b_router_group, w_router_expert, b_router_expert, w_e_gate, w_e_up,
           w_e_down, final_g):
    b, s, d = x.shape
    t = b * s
    assert w_ada.shape[0] == 1, "single-layer block"
    assert d == D_MODEL and s % TM_PROJ == 0
    l = 0
    n_tiles = t // TM_EXPERT + N_CLASSES
    x2d = x.reshape(t, d)
    ada3 = _ada(c, w_ada[l], b_ada[l]).reshape(b, N_ADA, d)

    proj, wa_b, wp_b, wo_b = _in_proj(
        x2d, ada3, norm1_g[l].reshape(1, d), w_in[l], w_attn_branch[l],
        w_pool_branch[l], w_out[l], s)
    attn, pool = _mixers(proj, sinks[l], w_pool[l].astype(BF16),
                         pool_scale[l].reshape(1, -1), s)

    w_r = jnp.concatenate([w_router_group[l], w_router_expert[l]], axis=1)
    b_r = jnp.concatenate([b_router_group[l], b_router_expert[l]])
    n_r = N_GROUPS + N_EXPERTS
    br_col = jnp.broadcast_to(
        jnp.pad(b_r, (0, ROUTER_ROWS - n_r))[:, None], (ROUTER_ROWS, 128))
    wr_lanes = jnp.pad(w_r, ((0, 0), (0, ROUTER_LANES - n_r))).astype(BF16)
    br_lanes = jnp.pad(b_r, (0, ROUTER_LANES - n_r)).reshape(1, ROUTER_LANES)

    x2, h2, route, counts, egu_b, ed_b = _merge(
        attn, pool, proj, x2d, ada3, wa_b, wp_b, wo_b, norm2_g[l].reshape(1, d),
        wr_lanes, br_col, w_e_gate[l], w_e_up[l], w_e_down[l], s)

    cls, rank, base, tile_blk, tile_ea, tile_eb, tile_lanes, tile_lead, n_used = (
        _routing_tables(route, counts, n_tiles))
    hs = _dispatch(cls, rank, base, h2, n_tiles * TM_EXPERT)
    ys = _experts(tile_blk, tile_ea, tile_eb, tile_lanes, tile_lead, n_used,
                  hs, egu_b.reshape(N_EXPERTS, d, 2 * EXPERT_FF),
                  ed_b.reshape(w_e_down[l].shape), wr_lanes, br_lanes)
    out = _combine(cls, rank, base, ys, x2, ada3, final_g.reshape(1, d), s)
    return out.reshape(b, s, d)
```

```python
import functools

import jax
import jax.numpy as jnp
from jax import lax
from jax.experimental import pallas as pl
from jax.experimental.pallas import tpu as pltpu

F32 = jnp.float32
BF16 = jnp.bfloat16

D_MODEL = 2048
HEAD_DIM = 64
ATTN_HEADS = 16
KV_HEADS = 4
Q_PER_KV = ATTN_HEADS // KV_HEADS
ATTN_WIDTH = ATTN_HEADS * HEAD_DIM
KV_WIDTH = KV_HEADS * HEAD_DIM
BLOCK = 128
POOL_WINDOWS = (2, 4, 8, 16)
POOL_WIDTH = 1024
POOL_GROUP_DIM = 256
QKVU_WIDTH = ATTN_WIDTH + 2 * KV_WIDTH + POOL_WIDTH
GATE_WIDTH = 2 * D_MODEL
PROJ_WIDTH = GATE_WIDTH + QKVU_WIDTH
Q_COL = GATE_WIDTH
K_COL = Q_COL + ATTN_WIDTH
V_COL = K_COL + KV_WIDTH
U_COL = V_COL + KV_WIDTH
N_GROUPS = 4
EXPERTS_PER_GROUP = 4
N_EXPERTS = 16
EXPERT_FF = 512
N_ADA = 6
NORM_EPS = 1e-6
MASK_VALUE = -1e30

PAIRS_PER_GROUP = 6
N_CLASSES = N_GROUPS * PAIRS_PER_GROUP
CLASS_SLOT_A = (0, 0, 0, 1, 1, 3)
CLASS_SLOT_B = (1, 2, 3, 3, 2, 2)
ROUTER_ROWS = 32
ROUTER_LANES = 128

TM_PROJ = 2048
TN_PROJ = 512
NORM_ROWS = 256
NORM_CHUNKS = TM_PROJ // NORM_ROWS
TM_MIX = 1024
TM_MERGE = 256
TM_DISPATCH = 512
TM_EXPERT = 256
TM_COMBINE = 512
TN_ADA = 1024

VMEM_LIMIT = 52 * 1024 * 1024


def _rms_modulate(x, g, scale, shift):
    ms = jnp.mean(x * x, axis=-1, keepdims=True)
    return (x * lax.rsqrt(ms + NORM_EPS)) * (g * (1.0 + scale)) + shift


def _ada_kernel(cb_ref, w_ref, b_ref, o_ref):
    nb = cb_ref.shape[0]
    d = w_ref.shape[0]
    nchunk = w_ref.shape[1] // 128

    def body(kb, accs):
        k0 = pl.multiple_of(kb * 8, 8)
        new = list(accs)
        cbs = [cb_ref[b, pl.ds(k0, 8), :] for b in range(nb)]
        for j in range(nchunk):
            w = w_ref[pl.ds(k0, 8), j * 128:(j + 1) * 128]
            for b in range(nb):
                new[b * nchunk + j] = new[b * nchunk + j] + w * cbs[b]
        return tuple(new)

    init = tuple(jnp.zeros((8, 128), F32) for _ in range(nb * nchunk))
    accs = lax.fori_loop(0, d // 8, body, init, unroll=2)
    for b in range(nb):
        for j in range(nchunk):
            o_ref[b:b + 1, j * 128:(j + 1) * 128] = (
                jnp.sum(accs[b * nchunk + j], axis=0, keepdims=True)
                + b_ref[:, j * 128:(j + 1) * 128])


def _ada(c, w_ada, b_ada):
    nb, d = c.shape
    n = w_ada.shape[1]
    cb = jnp.broadcast_to(c[:, :, None], (nb, d, 128))
    return pl.pallas_call(
        _ada_kernel,
        out_shape=jax.ShapeDtypeStruct((nb, n), F32),
        grid=(n // TN_ADA,),
        in_specs=[
            pl.BlockSpec((nb, d, 128), lambda j: (0, 0, 0)),
            pl.BlockSpec((d, TN_ADA), lambda j: (0, j)),
            pl.BlockSpec((1, TN_ADA), lambda j: (0, j)),
        ],
        out_specs=pl.BlockSpec((nb, TN_ADA), lambda j: (0, j)),
        compiler_params=pltpu.CompilerParams(
            dimension_semantics=("arbitrary",), vmem_limit_bytes=VMEM_LIMIT),
        name="ada",
    )(cb, w_ada, b_ada.reshape(1, n))


def _in_proj_kernel(x_ref, ada_ref, g_ref, w_ref, wa_ref, wp_ref, wo_ref,
                    proj_ref, wa_o, wp_o, wo_o, h_even, h_odd, *, n_row):
    i = pl.program_id(0)
    j = pl.program_id(1)

    def norm_chunk(h_ref):
        r0 = pl.multiple_of(j * NORM_ROWS, NORM_ROWS)
        h = _rms_modulate(x_ref[...], g_ref[...],
                          ada_ref[0, 1:2, :], ada_ref[0, 0:1, :])
        h_ref[pl.ds(r0, NORM_ROWS), :] = h.astype(BF16)

    def step(h_write, h_read, early):
        if early:
            norm_chunk(h_write)
        acc = jnp.dot(h_read[...], w_ref[...].astype(BF16),
                      preferred_element_type=F32)
        if early:
            acc = 0.5 * jnp.tanh(0.5 * acc) + 0.5
        proj_ref[...] = acc.astype(BF16)

    @pl.when(jnp.logical_and(j == 0, i < n_row))
    def _():
        wa_o[...] = wa_ref[...].astype(BF16)
        wp_o[...] = wp_ref[...].astype(BF16)
        wo_o[...] = wo_ref[...].astype(BF16)

    @pl.when(jnp.logical_and(i == 0, j < NORM_CHUNKS))
    def _():
        norm_chunk(h_even)

    is_early = j < NORM_CHUNKS
    odd = i % 2 == 1
    even = jnp.logical_and(i > 0, i % 2 == 0)
    for parity, h_write, h_read in ((odd, h_odd, h_even), (even, h_even, h_odd)):
        for early in (True, False):
            kind = is_early if early else jnp.logical_not(is_early)
            pl.when(jnp.logical_and(parity, kind))(
                functools.partial(step, h_write, h_read, early))


def _in_proj(x2d, ada3, norm_g, w_in, wa, wp, wo, seq):
    t, d = x2d.shape
    n_all = PROJ_WIDTH // TN_PROJ
    n_gate = GATE_WIDTH // TN_PROJ
    n_row = t // TM_PROJ
    tiles_per_seq = seq // TM_PROJ
    assert NORM_CHUNKS == n_gate
    norm_tile = lambda i: jnp.minimum(i, n_row - 1)
    out_tile = lambda i: jnp.maximum(i - 1, 0)
    w_tile = lambda i, j: (
        0, (jnp.where(i == 0, 0, j) + (n_all - n_gate)) % n_all)
    slab = lambda w: pl.BlockSpec((w.shape[0] // n_row, w.shape[1]),
                                  lambda i, j: (norm_tile(i), 0))
    cast = lambda w: jax.ShapeDtypeStruct(w.shape, BF16)
    return pl.pallas_call(
        functools.partial(_in_proj_kernel, n_row=n_row),
        out_shape=(jax.ShapeDtypeStruct((t, PROJ_WIDTH), BF16),
                   cast(wa), cast(wp), cast(wo)),
        grid=(n_row + 1, n_all),
        in_specs=[
            pl.BlockSpec((NORM_ROWS, d), lambda i, j: (
                norm_tile(i) * NORM_CHUNKS + jnp.minimum(j, NORM_CHUNKS - 1), 0)),
            pl.BlockSpec((1, N_ADA, d),
                         lambda i, j: (norm_tile(i) // tiles_per_seq, 0, 0)),
            pl.BlockSpec((1, d), lambda i, j: (0, 0)),
            pl.BlockSpec((d, TN_PROJ), w_tile),
            slab(wa), slab(wp), slab(wo),
        ],
        out_specs=(
            pl.BlockSpec((TM_PROJ, TN_PROJ),
                         lambda i, j: (out_tile(i), jnp.where(i == 0, 0, j))),
            slab(wa), slab(wp), slab(wo),
        ),
        scratch_shapes=[pltpu.VMEM((TM_PROJ, d), BF16),
                        pltpu.VMEM((TM_PROJ, d), BF16)],
        compiler_params=pltpu.CompilerParams(
            dimension_semantics=("arbitrary", "arbitrary"),
            vmem_limit_bytes=VMEM_LIMIT),
        name="in_proj",
    )(x2d, ada3, norm_g, w_in, wa, wp, wo)


def _mix_kernel(fill_ref, q_ref, kc_ref, vc_ref, kp_ref, vp_ref,
                ulo_ref, uhi_ref, plo_ref, phi_ref, wp_ref, ps_ref,
                attn_ref, pool_ref, kfull, vfull, ufull, *, steps_per_seq):
    i = pl.program_id(0)
    first_step = (i % steps_per_seq) == 0
    kfull[0:BLOCK, :] = kp_ref[...]
    kfull[BLOCK:BLOCK + TM_MIX, :] = kc_ref[...]
    vfull[0:BLOCK, :] = vp_ref[...]
    vfull[BLOCK:BLOCK + TM_MIX, :] = vc_ref[...]
    half = POOL_WIDTH // 2
    zeros = jnp.zeros((BLOCK, half), BF16)
    ufull[0:BLOCK, 0:half] = jnp.where(first_step, zeros, plo_ref[...])
    ufull[0:BLOCK, half:POOL_WIDTH] = jnp.where(first_step, zeros, phi_ref[...])
    ufull[BLOCK:BLOCK + TM_MIX, 0:half] = ulo_ref[...]
    ufull[BLOCK:BLOCK + TM_MIX, half:POOL_WIDTH] = uhi_ref[...]
    pos0 = (i % steps_per_seq) * TM_MIX
    nq = Q_PER_KV * BLOCK
    prow = lax.broadcasted_iota(jnp.int32, (BLOCK, 2 * BLOCK), 0)
    pcol = lax.broadcasted_iota(jnp.int32, (BLOCK, 2 * BLOCK), 1)
    rel = prow + BLOCK - pcol
    bands = [((rel >= 0) & (rel < w)).astype(F32).astype(BF16)
             for w in POOL_WINDOWS]

    def sb_body(sb, carry):
        r0 = pl.multiple_of(sb * BLOCK, BLOCK)
        row = lax.broadcasted_iota(jnp.int32, (nq, 2 * BLOCK), 0) & (BLOCK - 1)
        col = lax.broadcasted_iota(jnp.int32, (nq, 2 * BLOCK), 1)
        dist = col - row
        lo = jnp.where(jnp.logical_and(first_step, sb == 0), BLOCK, 0)
        valid = (dist > 0) & (dist <= BLOCK) & (col >= lo)
        key0 = lax.broadcasted_iota(jnp.int32, (2 * BLOCK, HEAD_DIM), 0) == 0
        pos = pos0 + sb * BLOCK + lax.broadcasted_iota(jnp.int32, (BLOCK, 1), 0)
        heads = range(KV_HEADS)
        pcols = [slice(h * POOL_GROUP_DIM, (h + 1) * POOL_GROUP_DIM) for h in heads]

        logits, ssums = [], []
        for h in heads:
            kb = kfull[pl.ds(r0, 2 * BLOCK), h * HEAD_DIM:(h + 1) * HEAD_DIM]
            qg = jnp.concatenate(
                [q_ref[pl.ds(r0, BLOCK),
                       (Q_PER_KV * h + g) * HEAD_DIM:(Q_PER_KV * h + g + 1) * HEAD_DIM]
                 for g in range(Q_PER_KV)], axis=0)
            qg = qg * jnp.asarray(HEAD_DIM ** -0.5, BF16)
            logits.append(lax.dot_general(qg, kb, (((1,), (1,)), ((), ())),
                                          preferred_element_type=F32))
            band = ufull[pl.ds(r0, 2 * BLOCK), pcols[h]]
            ssums.append(jnp.dot(bands[h], band, preferred_element_type=F32))

        probs, denoms, pooled = [], [], []
        for h in heads:
            lg = jnp.where(valid, logits[h], fill_ref[h])
            m = jnp.max(lg, axis=1, keepdims=True)
            p = jnp.exp(lg - m)
            denoms.append(jnp.sum(p, axis=1, keepdims=True))
            probs.append(p.astype(BF16))
            xg = ufull[pl.ds(r0 + BLOCK, BLOCK), pcols[h]].astype(F32)
            count = jnp.minimum(pos + 1, POOL_WINDOWS[h]).astype(F32)
            pooled.append((ssums[h] / count - xg).astype(BF16))

        for h in heads:
            vb = vfull[pl.ds(r0, 2 * BLOCK), h * HEAD_DIM:(h + 1) * HEAD_DIM]
            vb = jnp.where(key0, jnp.zeros_like(vb), vb)
            o = jnp.dot(probs[h], vb, preferred_element_type=F32)
            o = (o * (1.0 / denoms[h])).astype(BF16)
            attn_ref[pl.ds(r0, BLOCK),
                     h * Q_PER_KV * HEAD_DIM:(h + 1) * Q_PER_KV * HEAD_DIM] = (
                jnp.concatenate([o[g * BLOCK:(g + 1) * BLOCK, :]
                                 for g in range(Q_PER_KV)], axis=1))
            mixed = jnp.dot(pooled[h], wp_ref[h], preferred_element_type=F32)
            pool_ref[pl.ds(r0, BLOCK), pcols[h]] = (
                mixed * ps_ref[:, pcols[h]]).astype(BF16)
        return carry

    lax.fori_loop(0, TM_MIX // BLOCK, sb_body, 0)


def _mixers(proj, sinks, w_pool_b, pool_scale, seq):
    t = proj.shape[0]
    assert len(POOL_WINDOWS) == KV_HEADS
    steps_per_seq = seq // TM_MIX
    sub = TM_MIX // BLOCK
    half = POOL_WIDTH // 2
    qcol = Q_COL // ATTN_WIDTH
    kcol = K_COL // KV_WIDTH
    ucol = U_COL // half
    prev = lambda i: jnp.maximum(i * sub - 1, 0)
    nq = Q_PER_KV * BLOCK
    sink_rows = jnp.repeat(sinks.astype(F32).reshape(KV_HEADS, Q_PER_KV), BLOCK,
                           axis=1)
    fill = jnp.full((KV_HEADS, nq, 2 * BLOCK), MASK_VALUE, F32)
    fill = fill.at[:, :, 0].set(sink_rows)
    return pl.pallas_call(
        functools.partial(_mix_kernel, steps_per_seq=steps_per_seq),
        out_shape=(jax.ShapeDtypeStruct((t, ATTN_WIDTH), BF16),
                   jax.ShapeDtypeStruct((t, POOL_WIDTH), BF16)),
        grid=(t // TM_MIX,),
        in_specs=[
            pl.BlockSpec((KV_HEADS, nq, 2 * BLOCK), lambda i: (0, 0, 0)),
            pl.BlockSpec((TM_MIX, ATTN_WIDTH), lambda i: (i, qcol)),
            pl.BlockSpec((TM_MIX, KV_WIDTH), lambda i: (i, kcol)),
            pl.BlockSpec((TM_MIX, KV_WIDTH), lambda i: (i, kcol + 1)),
            pl.BlockSpec((BLOCK, KV_WIDTH), lambda i: (prev(i), kcol)),
            pl.BlockSpec((BLOCK, KV_WIDTH), lambda i: (prev(i), kcol + 1)),
            pl.BlockSpec((TM_MIX, half), lambda i: (i, ucol)),
            pl.BlockSpec((TM_MIX, half), lambda i: (i, ucol + 1)),
            pl.BlockSpec((BLOCK, half), lambda i: (prev(i), ucol)),
            pl.BlockSpec((BLOCK, half), lambda i: (prev(i), ucol + 1)),
            pl.BlockSpec((len(POOL_WINDOWS), POOL_GROUP_DIM, POOL_GROUP_DIM),
                         lambda i: (0, 0, 0)),
            pl.BlockSpec((1, POOL_WIDTH), lambda i: (0, 0)),
        ],
        out_specs=(pl.BlockSpec((TM_MIX, ATTN_WIDTH), lambda i: (i, 0)),
                   pl.BlockSpec((TM_MIX, POOL_WIDTH), lambda i: (i, 0))),
        scratch_shapes=[pltpu.VMEM((BLOCK + TM_MIX, KV_WIDTH), BF16),
                        pltpu.VMEM((BLOCK + TM_MIX, KV_WIDTH), BF16),
                        pltpu.VMEM((BLOCK + TM_MIX, POOL_WIDTH), BF16)],
        compiler_params=pltpu.CompilerParams(
            dimension_semantics=("arbitrary",), vmem_limit_bytes=VMEM_LIMIT),
        name="mixers",
    )(fill, proj, proj, proj, proj, proj, proj, proj, proj, proj, w_pool_b,
      pool_scale)


def _first_argmax4(v):
    m = jnp.maximum(jnp.maximum(v[0], v[1]), jnp.maximum(v[2], v[3]))
    idx = jnp.where(v[0] == m, 0, jnp.where(v[1] == m, 1,
                                            jnp.where(v[2] == m, 2, 3)))
    return m, idx


def _merge_kernel(attn_ref, pool_ref, ga_ref, gp_ref, x_ref, ada_ref,
                  wa_ref, wp_ref, wo_ref, g2_ref, wr_ref, br_ref,
                  eg_ref, eu_ref, ed_ref,
                  x2_ref, h2_ref, route_ref, counts_ref, egu_o, ed_o, lg_ref):
    i = pl.program_id(0)
    n_tile = pl.num_programs(0) - 1

    @pl.when(i == 0)
    def _():
        counts_ref[...] = jnp.zeros_like(counts_ref)
        lg_ref[...] = jnp.zeros_like(lg_ref)

    @pl.when(i < n_tile)
    def _():
        _route_tile(lg_ref[...], i > 0, route_ref, counts_ref)
        egu_o[:, 0:EXPERT_FF] = eg_ref[...].astype(BF16)
        egu_o[:, EXPERT_FF:2 * EXPERT_FF] = eu_ref[...].astype(BF16)
        ed_o[...] = ed_ref[...].astype(BF16)
        a = jnp.dot(attn_ref[...], wa_ref[...], preferred_element_type=F32)
        p = jnp.dot(pool_ref[...], wp_ref[...], preferred_element_type=F32)
        merged = ga_ref[...].astype(F32) * a + gp_ref[...].astype(F32) * p
        out = jnp.dot(merged.astype(BF16), wo_ref[...], preferred_element_type=F32)
        x2 = x_ref[...] + ada_ref[0, 2:3, :] * out
        x2_ref[...] = x2
        h2 = _rms_modulate(x2, g2_ref[...], ada_ref[0, 4:5, :], ada_ref[0, 3:4, :])
        h2_ref[...] = h2
        logits = jnp.dot(h2.astype(BF16), wr_ref[...], preferred_element_type=F32)
        lg_ref[...] = logits.T[0:ROUTER_ROWS, :] + br_ref[:, 0:1]

    @pl.when(i == n_tile)
    def _():
        _route_tile(lg_ref[...], True, route_ref, counts_ref)


def _route_tile(logits, live, route_ref, counts_ref):
    tm = logits.shape[1]
    rows = [logits[r:r + 1, :] for r in range(N_GROUPS + N_EXPERTS)]
    _, gi = _first_argmax4(rows[0:N_GROUPS])
    sel = []
    for e in range(EXPERTS_PER_GROUP):
        v = rows[N_GROUPS + 3 * EXPERTS_PER_GROUP + e]
        for g in range(N_GROUPS - 2, -1, -1):
            v = jnp.where(gi == g, rows[N_GROUPS + g * EXPERTS_PER_GROUP + e], v)
        sel.append(v)
    _, i1 = _first_argmax4(sel)
    rest = [jnp.where(i1 == e, -jnp.inf, sel[e]) for e in range(EXPERTS_PER_GROUP)]
    _, i2 = _first_argmax4(rest)
    code = jnp.minimum(i1, i2) * EXPERTS_PER_GROUP + jnp.maximum(i1, i2)
    cig = jnp.where(code == 1, 0, jnp.where(code == 2, 1, jnp.where(
        code == 3, 2, jnp.where(code == 7, 3, jnp.where(code == 6, 4, 5)))))
    cls = gi * PAIRS_PER_GROUP + cig

    onehot = (lax.broadcasted_iota(jnp.int32, (ROUTER_ROWS, tm), 0) == cls
              ).astype(F32)
    onehot = jnp.where(live, onehot, 0.0)
    src = lax.broadcasted_iota(jnp.int32, (tm, tm), 0)
    dst = lax.broadcasted_iota(jnp.int32, (tm, tm), 1)
    before = (src < dst).astype(F32).astype(BF16)
    prefix = jnp.dot(onehot.astype(BF16), before, preferred_element_type=F32)
    carry = counts_ref[:, 0:1]
    rank = jnp.sum(onehot * (prefix + carry), axis=0, keepdims=True)
    counts_ref[...] = counts_ref[...] + jnp.sum(onehot, axis=1, keepdims=True)
    route_ref[0:1, :] = cls
    route_ref[1:2, :] = rank.astype(jnp.int32)
    route_ref[2:8, :] = jnp.zeros((6, tm), jnp.int32)


def _merge(attn, pool, proj, x2d, ada3, wa_b, wp_b, wo_b, norm2_g, wr_lanes, br_col,
           w_e_gate, w_e_up, w_e_down, seq):
    t, d = x2d.shape
    n_step = t // TM_MERGE
    tiles_per_seq = seq // TM_MERGE
    cur = lambda i: jnp.minimum(i, n_step - 1)
    lag = lambda i: jnp.maximum(i - 1, 0)
    flat = lambda w: w.reshape(-1, w.shape[-1])
    slab = lambda w: pl.BlockSpec((w.shape[0] // n_step, w.shape[1]),
                                  lambda i: (cur(i), 0))
    cast = lambda w: jax.ShapeDtypeStruct(w.shape, BF16)
    eg, eu, ed = flat(w_e_gate), flat(w_e_up), flat(w_e_down)
    egu = jax.ShapeDtypeStruct((eg.shape[0], 2 * EXPERT_FF), BF16)
    const = lambda shape: pl.BlockSpec(shape, lambda i: (0,) * len(shape),
                                       pipeline_mode=pl.Buffered(1))
    return pl.pallas_call(
        _merge_kernel,
        out_shape=(jax.ShapeDtypeStruct((t, d), F32),
                   jax.ShapeDtypeStruct((t, d), F32),
                   jax.ShapeDtypeStruct((8, t), jnp.int32),
                   jax.ShapeDtypeStruct((ROUTER_ROWS, 128), F32),
                   egu, cast(ed)),
        grid=(n_step + 1,),
        in_specs=[
            pl.BlockSpec((TM_MERGE, ATTN_WIDTH), lambda i: (cur(i), 0)),
            pl.BlockSpec((TM_MERGE, POOL_WIDTH), lambda i: (cur(i), 0)),
            pl.BlockSpec((TM_MERGE, d), lambda i: (cur(i), 0)),
            pl.BlockSpec((TM_MERGE, d), lambda i: (cur(i), 1)),
            pl.BlockSpec((TM_MERGE, d), lambda i: (cur(i), 0)),
            pl.BlockSpec((1, N_ADA, d),
                         lambda i: (cur(i) // tiles_per_seq, 0, 0)),
            const((ATTN_WIDTH, d)),
            const((POOL_WIDTH, d)),
            const((d, d)),
            const((1, d)),
            const((d, ROUTER_LANES)),
            const((ROUTER_ROWS, 128)),
            slab(eg), slab(eu), slab(ed),
        ],
        out_specs=(
            pl.BlockSpec((TM_MERGE, d), lambda i: (cur(i), 0)),
            pl.BlockSpec((TM_MERGE, d), lambda i: (cur(i), 0)),
            pl.BlockSpec((8, TM_MERGE), lambda i: (0, lag(i))),
            pl.BlockSpec((ROUTER_ROWS, 128), lambda i: (0, 0)),
            slab(egu), slab(ed),
        ),
        scratch_shapes=[pltpu.VMEM((ROUTER_ROWS, TM_MERGE), F32)],
        compiler_params=pltpu.CompilerParams(
            dimension_semantics=("arbitrary",), vmem_limit_bytes=VMEM_LIMIT),
        name="merge",
    )(attn, pool, proj, proj, x2d, ada3, wa_b, wp_b, wo_b, norm2_g, wr_lanes, br_col,
      eg, eu, ed)


def _row_copy(src_ref, src_row, dst_ref, dst_row, sem):
    return pltpu.make_async_copy(src_ref.at[pl.ds(src_row, 1)],
                                 dst_ref.at[pl.ds(dst_row, 1)], sem)


def _sorted_row(cls_ref, rank_ref, base_ref, token):
    return base_ref[cls_ref[token]] + rank_ref[token]


def _dispatch_kernel(cls_ref, rank_ref, base_ref, h_ref, hs_ref, sem):
    i = pl.program_id(0)
    first = i * TM_DISPATCH

    for r in range(TM_DISPATCH):
        _row_copy(h_ref, r, hs_ref,
                  _sorted_row(cls_ref, rank_ref, base_ref, first + r), sem).start()
    pltpu.make_async_copy(h_ref, hs_ref.at[pl.ds(0, TM_DISPATCH)], sem).wait()


def _dispatch(cls, rank, base, h2, n_rows):
    t, d = h2.shape
    return pl.pallas_call(
        _dispatch_kernel,
        out_shape=jax.ShapeDtypeStruct((n_rows, d), F32),
        grid_spec=pltpu.PrefetchScalarGridSpec(
            num_scalar_prefetch=3,
            grid=(t // TM_DISPATCH,),
            in_specs=[pl.BlockSpec((TM_DISPATCH, d), lambda i, *_: (i, 0))],
            out_specs=pl.BlockSpec(memory_space=pl.ANY),
            scratch_shapes=[pltpu.SemaphoreType.DMA(())],
        ),
        compiler_params=pltpu.CompilerParams(
            dimension_semantics=("arbitrary",), vmem_limit_bytes=VMEM_LIMIT,
            has_side_effects=True),
        name="dispatch",
    )(cls, rank, base, h2)


def _expert_kernel(blk_ref, ea_ref, eb_ref, lane_ref, lead_ref, nused_ref,
                   hs_ref, wgua_ref, wda_ref, wgub_ref, wdb_ref,
                   wr_ref, br_ref, ys_ref):
    i = pl.program_id(0)
    used = i < nused_ref[0]
    lead = lead_ref[i]
    half = TM_EXPERT // 2

    def tile_rows(r0):
        tm = TM_EXPERT - r0
        rows = r0 + lax.broadcasted_iota(jnp.int32, (tm, 1), 0)
        h = jnp.where(rows >= lead, hs_ref[r0:TM_EXPERT, :], 0.0).astype(BF16)
        logits = jnp.dot(h, wr_ref[...], preferred_element_type=F32) + br_ref[...]
        lane = lax.broadcasted_iota(jnp.int32, logits.shape, 1)
        is_group = lane < N_GROUPS
        gmax = jnp.max(jnp.where(is_group, logits, -jnp.inf), axis=1, keepdims=True)
        gexp = jnp.where(is_group, jnp.exp(logits - gmax), 0.0)
        pick = lambda v, l: jnp.sum(jnp.where(lane == l, v, 0.0), axis=1,
                                    keepdims=True)
        p_group = pick(gexp, lane_ref[0, i]) / jnp.sum(gexp, axis=1, keepdims=True)
        la = pick(logits, lane_ref[1, i])
        lb = pick(logits, lane_ref[2, i])
        mx = jnp.maximum(la, lb)
        ea = jnp.exp(la - mx)
        eb = jnp.exp(lb - mx)
        inv = p_group / (ea + eb)

        def expert(wgu_ref, wd_ref, weight):
            gu = jnp.dot(h, wgu_ref[0], preferred_element_type=F32)
            gate, up = gu[:, 0:EXPERT_FF], gu[:, EXPERT_FF:2 * EXPERT_FF]
            act = (jax.nn.silu(gate) * up * weight).astype(BF16)
            return jnp.dot(act, wd_ref[0], preferred_element_type=F32)

        ys_ref[r0:TM_EXPERT, :] = (expert(wgua_ref, wda_ref, ea * inv)
                                   + expert(wgub_ref, wdb_ref, eb * inv))

    @pl.when(jnp.logical_not(used))
    def _():
        ys_ref[...] = jnp.zeros_like(ys_ref)

    @pl.when(jnp.logical_and(used, lead < half))
    def _():
        tile_rows(0)

    @pl.when(jnp.logical_and(used, lead >= half))
    def _():
        ys_ref[0:half, :] = jnp.zeros((half, ys_ref.shape[1]), ys_ref.dtype)
        tile_rows(half)


def _experts(tile_blk, tile_ea, tile_eb, tile_lanes, tile_lead, n_used,
             hs, wgu_b, wd_b, wr_lanes, br_lanes):
    p_rows, d = hs.shape
    n_tiles = p_rows // TM_EXPERT
    row_map = lambda i, blk, ea, eb, ln, nv, nu: (blk[i], 0)
    wa_map = lambda i, blk, ea, eb, ln, nv, nu: (ea[i], 0, 0)
    wb_map = lambda i, blk, ea, eb, ln, nv, nu: (eb[i], 0, 0)
    const = lambda i, blk, ea, eb, ln, nv, nu: (0, 0)
    return pl.pallas_call(
        _expert_kernel,
        out_shape=jax.ShapeDtypeStruct((p_rows, d), F32),
        grid_spec=pltpu.PrefetchScalarGridSpec(
            num_scalar_prefetch=6,
            grid=(n_tiles,),
            in_specs=[
                pl.BlockSpec((TM_EXPERT, d), row_map),
                pl.BlockSpec((1, d, 2 * EXPERT_FF), wa_map),
                pl.BlockSpec((1, EXPERT_FF, d), wa_map),
                pl.BlockSpec((1, d, 2 * EXPERT_FF), wb_map),
                pl.BlockSpec((1, EXPERT_FF, d), wb_map),
                pl.BlockSpec((d, ROUTER_LANES), const),
                pl.BlockSpec((1, ROUTER_LANES), const),
            ],
            out_specs=pl.BlockSpec((TM_EXPERT, d), row_map),
        ),
        compiler_params=pltpu.CompilerParams(
            dimension_semantics=("arbitrary",), vmem_limit_bytes=VMEM_LIMIT),
        name="experts",
    )(tile_blk, tile_ea, tile_eb, tile_lanes, tile_lead, n_used,
      hs, wgu_b, wd_b, wgu_b, wd_b, wr_lanes, br_lanes)


def _combine_kernel(cls_ref, rank_ref, base_ref, ys_ref, x2_ref, ada_ref, g_ref,
                    o_ref, ybuf, sems):
    i = pl.program_id(0)
    n = pl.num_programs(0)

    def gather(tile, slot):
        for r in range(TM_COMBINE):
            row = _sorted_row(cls_ref, rank_ref, base_ref, tile * TM_COMBINE + r)
            _row_copy(ys_ref, row, ybuf.at[slot], r, sems.at[slot]).start()

    @pl.when(i == 0)
    def _():
        gather(0, 0)

    @pl.when(i + 1 < n)
    def _():
        gather(i + 1, (i + 1) % 2)

    slot = i % 2
    pltpu.make_async_copy(ys_ref.at[pl.ds(0, TM_COMBINE)], ybuf.at[slot],
                          sems.at[slot]).wait()

    x = x2_ref[...] + ada_ref[0, 5:6, :] * ybuf[slot]
    ms = jnp.mean(x * x, axis=-1, keepdims=True)
    o_ref[...] = x * lax.rsqrt(ms + NORM_EPS) * g_ref[...]


def _combine(cls, rank, base, ys, x2, ada3, final_g, seq):
    t, d = x2.shape
    tiles_per_seq = seq // TM_COMBINE
    return pl.pallas_call(
        _combine_kernel,
        out_shape=jax.ShapeDtypeStruct((t, d), F32),
        grid_spec=pltpu.PrefetchScalarGridSpec(
            num_scalar_prefetch=3,
            grid=(t // TM_COMBINE,),
            in_specs=[
                pl.BlockSpec(memory_space=pl.ANY),
                pl.BlockSpec((TM_COMBINE, d), lambda i, *_: (i, 0)),
                pl.BlockSpec((1, N_ADA, d),
                             lambda i, *_: (i // tiles_per_seq, 0, 0)),
                pl.BlockSpec((1, d), lambda i, *_: (0, 0)),
            ],
            out_specs=pl.BlockSpec((TM_COMBINE, d), lambda i, *_: (i, 0)),
            scratch_shapes=[pltpu.VMEM((2, TM_COMBINE, d), F32),
                            pltpu.SemaphoreType.DMA((2,))],
        ),
        compiler_params=pltpu.CompilerParams(
            dimension_semantics=("arbitrary",), vmem_limit_bytes=VMEM_LIMIT),
        name="combine",
    )(cls, rank, base, ys, x2, ada3, final_g)


def _routing_tables(route, counts, n_tiles):
    cls = route[0]
    rank = route[1]
    cnt = counts[:N_CLASSES, 0].astype(jnp.int32)
    tiles_c = (cnt + TM_EXPERT - 1) // TM_EXPERT
    tile_end = jnp.cumsum(tiles_c)
    tile_start = tile_end - tiles_c
    n_used = tile_end[-1]
    lead_c = tiles_c * TM_EXPERT - cnt
    base = (tile_start * TM_EXPERT + lead_c).astype(jnp.int32)

    steps = jnp.arange(n_tiles, dtype=jnp.int32)
    k = jnp.minimum(steps, n_used - 1)
    c = jnp.sum((k[:, None] >= tile_end[None, :]).astype(jnp.int32), axis=1)
    grp = c // PAIRS_PER_GROUP
    cig = c % PAIRS_PER_GROUP
    slot_a = jnp.asarray(CLASS_SLOT_A, jnp.int32)[cig]
    slot_b = jnp.asarray(CLASS_SLOT_B, jnp.int32)[cig]
    ea = grp * EXPERTS_PER_GROUP + slot_a
    eb = grp * EXPERTS_PER_GROUP + slot_b
    lanes = jnp.stack([grp, N_GROUPS + ea, N_GROUPS + eb]).astype(jnp.int32)
    lead = jnp.where(k == tile_start[c], lead_c[c], 0)
    return (cls, rank, base, steps, ea.astype(jnp.int32),
            eb.astype(jnp.int32), lanes, lead.astype(jnp.int32),
            n_used.reshape(1).astype(jnp.int32))


def kernel(x, c, w_ada, b_ada, norm1_g, w_in, sinks, w_pool, pool_scale,
           w_attn_branch, w_pool_branch, w_out, norm2_g, w_router_group,
           b_router_group, w_router_expert, b_router_expert, w_e_gate, w_e_up,
           w_e_down, final_g):
    b, s, d = x.shape
    t = b * s
    assert w_ada.shape[0] == 1, "single-layer block"
    assert d == D_MODEL and s % TM_PROJ == 0
    l = 0
    n_tiles = t // TM_EXPERT + N_CLASSES
    x2d = x.reshape(t, d)
    ada3 = _ada(c, w_ada[l], b_ada[l]).reshape(b, N_ADA, d)

    proj, wa_b, wp_b, wo_b = _in_proj(
        x2d, ada3, norm1_g[l].reshape(1, d), w_in[l], w_attn_branch[l],
        w_pool_branch[l], w_out[l], s)
    attn, pool = _mixers(proj, sinks[l], w_pool[l].astype(BF16),
                         pool_scale[l].reshape(1, -1), s)

    w_r = jnp.concatenate([w_router_group[l], w_router_expert[l]], axis=1)
    b_r = jnp.concatenate([b_router_group[l], b_router_expert[l]])
    n_r = N_GROUPS + N_EXPERTS
    br_col = jnp.broadcast_to(
        jnp.pad(b_r, (0, ROUTER_ROWS - n_r))[:, None], (ROUTER_ROWS, 128))
    wr_lanes = jnp.pad(w_r, ((0, 0), (0, ROUTER_LANES - n_r))).astype(BF16)
    br_lanes = jnp.pad(b_r, (0, ROUTER_LANES - n_r)).reshape(1, ROUTER_LANES)

    x2, h2, route, counts, egu_b, ed_b = _merge(
        attn, pool, proj, x2d, ada3, wa_b, wp_b, wo_b, norm2_g[l].reshape(1, d),
        wr_lanes, br_col, w_e_gate[l], w_e_up[l], w_e_down[l], s)

    cls, rank, base, tile_blk, tile_ea, tile_eb, tile_lanes, tile_lead, n_used = (
        _routing_tables(route, counts, n_tiles))
    hs = _dispatch(cls, rank, base, h2, n_tiles * TM_EXPERT)
    ys = _experts(tile_blk, tile_ea, tile_eb, tile_lanes, tile_lead, n_used,
                  hs, egu_b.reshape(N_EXPERTS, d, 2 * EXPERT_FF),
                  ed_b.reshape(w_e_down[l].shape), wr_lanes, br_lanes)
    out = _combine(cls, rank, base, ys, x2, ada3, final_g.reshape(1, d), s)
    return out.reshape(b, s, d)
```

```python
import functools

import jax
import jax.numpy as jnp
from jax import lax
from jax.experimental import pallas as pl
from jax.experimental.pallas import tpu as pltpu

F32 = jnp.float32
BF16 = jnp.bfloat16

D_MODEL = 2048
HEAD_DIM = 64
ATTN_HEADS = 16
KV_HEADS = 4
Q_PER_KV = ATTN_HEADS // KV_HEADS
ATTN_WIDTH = ATTN_HEADS * HEAD_DIM
KV_WIDTH = KV_HEADS * HEAD_DIM
BLOCK = 128
POOL_WINDOWS = (2, 4, 8, 16)
POOL_WIDTH = 1024
POOL_GROUP_DIM = 256
QKVU_WIDTH = ATTN_WIDTH + 2 * KV_WIDTH + POOL_WIDTH
GATE_WIDTH = 2 * D_MODEL
PROJ_WIDTH = GATE_WIDTH + QKVU_WIDTH
Q_COL = GATE_WIDTH
K_COL = Q_COL + ATTN_WIDTH
V_COL = K_COL + KV_WIDTH
U_COL = V_COL + KV_WIDTH
N_GROUPS = 4
EXPERTS_PER_GROUP = 4
N_EXPERTS = 16
EXPERT_FF = 512
N_ADA = 6
NORM_EPS = 1e-6
MASK_VALUE = -1e30

PAIRS_PER_GROUP = 6
N_CLASSES = N_GROUPS * PAIRS_PER_GROUP
CLASS_SLOT_A = (0, 0, 0, 1, 1, 3)
CLASS_SLOT_B = (1, 2, 3, 3, 2, 2)
ROUTER_ROWS = 32
ROUTER_LANES = 128

TM_PROJ = 2048
TN_PROJ = 512
NORM_ROWS = 256
NORM_CHUNKS = TM_PROJ // NORM_ROWS
TM_MIX = 1024
TM_MERGE = 256
TM_DISPATCH = 512
TM_EXPERT = 256
TM_COMBINE = 512
TN_ADA = 1024

VMEM_LIMIT = 52 * 1024 * 1024


def _rms_modulate(x, g, scale, shift):
    ms = jnp.mean(x * x, axis=-1, keepdims=True)
    return (x * lax.rsqrt(ms + NORM_EPS)) * (g * (1.0 + scale)) + shift


def _ada_kernel(cb_ref, w_ref, b_ref, o_ref):
    nb = cb_ref.shape[0]
    d = w_ref.shape[0]
    nchunk = w_ref.shape[1] // 128

    def body(kb, accs):
        k0 = pl.multiple_of(kb * 8, 8)
        new = list(accs)
        cbs = [cb_ref[b, pl.ds(k0, 8), :] for b in range(nb)]
        for j in range(nchunk):
            w = w_ref[pl.ds(k0, 8), j * 128:(j + 1) * 128]
            for b in range(nb):
                new[b * nchunk + j] = new[b * nchunk + j] + w * cbs[b]
        return tuple(new)

    init = tuple(jnp.zeros((8, 128), F32) for _ in range(nb * nchunk))
    accs = lax.fori_loop(0, d // 8, body, init, unroll=2)
    for b in range(nb):
        for j in range(nchunk):
            o_ref[b:b + 1, j * 128:(j + 1) * 128] = (
                jnp.sum(accs[b * nchunk + j], axis=0, keepdims=True)
                + b_ref[:, j * 128:(j + 1) * 128])


def _ada(c, w_ada, b_ada):
    nb, d = c.shape
    n = w_ada.shape[1]
    cb = jnp.broadcast_to(c[:, :, None], (nb, d, 128))
    return pl.pallas_call(
        _ada_kernel,
        out_shape=jax.ShapeDtypeStruct((nb, n), F32),
        grid=(n // TN_ADA,),
        in_specs=[
            pl.BlockSpec((nb, d, 128), lambda j: (0, 0, 0)),
            pl.BlockSpec((d, TN_ADA), lambda j: (0, j)),
            pl.BlockSpec((1, TN_ADA), lambda j: (0, j)),
        ],
        out_specs=pl.BlockSpec((nb, TN_ADA), lambda j: (0, j)),
        compiler_params=pltpu.CompilerParams(
            dimension_semantics=("arbitrary",), vmem_limit_bytes=VMEM_LIMIT),
        name="ada",
    )(cb, w_ada, b_ada.reshape(1, n))


def _in_proj_kernel(x_ref, ada_ref, g_ref, w_ref, wa_ref, wp_ref, wo_ref,
                    proj_ref, wa_o, wp_o, wo_o, h_even, h_odd, *, n_row):
    i = pl.program_id(0)
    j = pl.program_id(1)

    def norm_chunk(h_ref):
        r0 = pl.multiple_of(j * NORM_ROWS, NORM_ROWS)
        h = _rms_modulate(x_ref[...], g_ref[...],
                          ada_ref[0, 1:2, :], ada_ref[0, 0:1, :])
        h_ref[pl.ds(r0, NORM_ROWS), :] = h.astype(BF16)

    def step(h_write, h_read, early):
        if early:
            norm_chunk(h_write)
        acc = jnp.dot(h_read[...], w_ref[...].astype(BF16),
                      preferred_element_type=F32)
        if early:
            acc = 0.5 * jnp.tanh(0.5 * acc) + 0.5
        proj_ref[...] = acc.astype(BF16)

    @pl.when(jnp.logical_and(j == 0, i < n_row))
    def _():
        wa_o[...] = wa_ref[...].astype(BF16)
        wp_o[...] = wp_ref[...].astype(BF16)
        wo_o[...] = wo_ref[...].astype(BF16)

    @pl.when(jnp.logical_and(i == 0, j < NORM_CHUNKS))
    def _():
        norm_chunk(h_even)

    is_early = j < NORM_CHUNKS
    odd = i % 2 == 1
    even = jnp.logical_and(i > 0, i % 2 == 0)
    for parity, h_write, h_read in ((odd, h_odd, h_even), (even, h_even, h_odd)):
        for early in (True, False):
            kind = is_early if early else jnp.logical_not(is_early)
            pl.when(jnp.logical_and(parity, kind))(
                functools.partial(step, h_write, h_read, early))


def _in_proj(x2d, ada3, norm_g, w_in, wa, wp, wo, seq):
    t, d = x2d.shape
    n_all = PROJ_WIDTH // TN_PROJ
    n_gate = GATE_WIDTH // TN_PROJ
    n_row = t // TM_PROJ
    tiles_per_seq = seq // TM_PROJ
    assert NORM_CHUNKS == n_gate
    norm_tile = lambda i: jnp.minimum(i, n_row - 1)
    out_tile = lambda i: jnp.maximum(i - 1, 0)
    w_tile = lambda i, j: (
        0, (jnp.where(i == 0, 0, j) + (n_all - n_gate)) % n_all)
    slab = lambda w: pl.BlockSpec((w.shape[0] // n_row, w.shape[1]),
                                  lambda i, j: (norm_tile(i), 0))
    cast = lambda w: jax.ShapeDtypeStruct(w.shape, BF16)
    return pl.pallas_call(
        functools.partial(_in_proj_kernel, n_row=n_row),
        out_shape=(jax.ShapeDtypeStruct((t, PROJ_WIDTH), BF16),
                   cast(wa), cast(wp), cast(wo)),
        grid=(n_row + 1, n_all),
        in_specs=[
            pl.BlockSpec((NORM_ROWS, d), lambda i, j: (
                norm_tile(i) * NORM_CHUNKS + jnp.minimum(j, NORM_CHUNKS - 1), 0)),
            pl.BlockSpec((1, N_ADA, d),
                         lambda i, j: (norm_tile(i) // tiles_per_seq, 0, 0)),
            pl.BlockSpec((1, d), lambda i, j: (0, 0)),
            pl.BlockSpec((d, TN_PROJ), w_tile),
            slab(wa), slab(wp), slab(wo),
        ],
        out_specs=(
            pl.BlockSpec((TM_PROJ, TN_PROJ),
                         lambda i, j: (out_tile(i), jnp.where(i == 0, 0, j))),
            slab(wa), slab(wp), slab(wo),
        ),
        scratch_shapes=[pltpu.VMEM((TM_PROJ, d), BF16),
                        pltpu.VMEM((TM_PROJ, d), BF16)],
        compiler_params=pltpu.CompilerParams(
            dimension_semantics=("arbitrary", "arbitrary"),
            vmem_limit_bytes=VMEM_LIMIT),
        name="in_proj",
    )(x2d, ada3, norm_g, w_in, wa, wp, wo)


def _mix_kernel(fill_ref, q_ref, kc_ref, vc_ref, kp_ref, vp_ref,
                ulo_ref, uhi_ref, plo_ref, phi_ref, wp_ref, ps_ref,
                attn_ref, pool_ref, kfull, vfull, ufull, *, steps_per_seq):
    i = pl.program_id(0)
    first_step = (i % steps_per_seq) == 0
    kfull[0:BLOCK, :] = kp_ref[...]
    kfull[BLOCK:BLOCK + TM_MIX, :] = kc_ref[...]
    vfull[0:BLOCK, :] = vp_ref[...]
    vfull[BLOCK:BLOCK + TM_MIX, :] = vc_ref[...]
    half = POOL_WIDTH // 2
    zeros = jnp.zeros((BLOCK, half), BF16)
    ufull[0:BLOCK, 0:half] = jnp.where(first_step, zeros, plo_ref[...])
    ufull[0:BLOCK, half:POOL_WIDTH] = jnp.where(first_step, zeros, phi_ref[...])
    ufull[BLOCK:BLOCK + TM_MIX, 0:half] = ulo_ref[...]
    ufull[BLOCK:BLOCK + TM_MIX, half:POOL_WIDTH] = uhi_ref[...]
    pos0 = (i % steps_per_seq) * TM_MIX
    nq = Q_PER_KV * BLOCK
    prow = lax.broadcasted_iota(jnp.int32, (BLOCK, 2 * BLOCK), 0)
    pcol = lax.broadcasted_iota(jnp.int32, (BLOCK, 2 * BLOCK), 1)
    rel = prow + BLOCK - pcol
    bands = [((rel >= 0) & (rel < w)).astype(F32).astype(BF16)
             for w in POOL_WINDOWS]

    def sb_body(sb, carry):
        r0 = pl.multiple_of(sb * BLOCK, BLOCK)
        row = lax.broadcasted_iota(jnp.int32, (nq, 2 * BLOCK), 0) & (BLOCK - 1)
        col = lax.broadcasted_iota(jnp.int32, (nq, 2 * BLOCK), 1)
        dist = col - row
        lo = jnp.where(jnp.logical_and(first_step, sb == 0), BLOCK, 0)
        valid = (dist > 0) & (dist <= BLOCK) & (col >= lo)
        key0 = lax.broadcasted_iota(jnp.int32, (2 * BLOCK, HEAD_DIM), 0) == 0
        pos = pos0 + sb * BLOCK + lax.broadcasted_iota(jnp.int32, (BLOCK, 1), 0)
        heads = range(KV_HEADS)
        pcols = [slice(h * POOL_GROUP_DIM, (h + 1) * POOL_GROUP_DIM) for h in heads]

        logits, ssums = [], []
        for h in heads:
            kb = kfull[pl.ds(r0, 2 * BLOCK), h * HEAD_DIM:(h + 1) * HEAD_DIM]
            qg = jnp.concatenate(
                [q_ref[pl.ds(r0, BLOCK),
                       (Q_PER_KV * h + g) * HEAD_DIM:(Q_PER_KV * h + g + 1) * HEAD_DIM]
                 for g in range(Q_PER_KV)], axis=0)
            qg = qg * jnp.asarray(HEAD_DIM ** -0.5, BF16)
            logits.append(lax.dot_general(qg, kb, (((1,), (1,)), ((), ())),
                                          preferred_element_type=F32))
            band = ufull[pl.ds(r0, 2 * BLOCK), pcols[h]]
            ssums.append(jnp.dot(bands[h], band, preferred_element_type=F32))

        probs, denoms, pooled = [], [], []
        for h in heads:
            lg = jnp.where(valid, logits[h], fill_ref[h])
            m = jnp.max(lg, axis=1, keepdims=True)
            p = jnp.exp(lg - m)
            denoms.append(jnp.sum(p, axis=1, keepdims=True))
            probs.append(p.astype(BF16))
            xg = ufull[pl.ds(r0 + BLOCK, BLOCK), pcols[h]].astype(F32)
            count = jnp.minimum(pos + 1, POOL_WINDOWS[h]).astype(F32)
            pooled.append((ssums[h] / count - xg).astype(BF16))

        for h in heads:
            vb = vfull[pl.ds(r0, 2 * BLOCK), h * HEAD_DIM:(h + 1) * HEAD_DIM]
            vb = jnp.where(key0, jnp.zeros_like(vb), vb)
            o = jnp.dot(probs[h], vb, preferred_element_type=F32)
            o = (o * (1.0 / denoms[h])).astype(BF16)
            attn_ref[pl.ds(r0, BLOCK),
                     h * Q_PER_KV * HEAD_DIM:(h + 1) * Q_PER_KV * HEAD_DIM] = (
                jnp.concatenate([o[g * BLOCK:(g + 1) * BLOCK, :]
                                 for g in range(Q_PER_KV)], axis=1))
            mixed = jnp.dot(pooled[h], wp_ref[h], preferred_element_type=F32)
            pool_ref[pl.ds(r0, BLOCK), pcols[h]] = (
                mixed * ps_ref[:, pcols[h]]).astype(BF16)
        return carry

    lax.fori_loop(0, TM_MIX // BLOCK, sb_body, 0)


def _mixers(proj, sinks, w_pool_b, pool_scale, seq):
    t = proj.shape[0]
    assert len(POOL_WINDOWS) == KV_HEADS
    steps_per_seq = seq // TM_MIX
    sub = TM_MIX // BLOCK
    half = POOL_WIDTH // 2
    qcol = Q_COL // ATTN_WIDTH
    kcol = K_COL // KV_WIDTH
    ucol = U_COL // half
    prev = lambda i: jnp.maximum(i * sub - 1, 0)
    nq = Q_PER_KV * BLOCK
    sink_rows = jnp.repeat(sinks.astype(F32).reshape(KV_HEADS, Q_PER_KV), BLOCK,
                           axis=1)
    fill = jnp.full((KV_HEADS, nq, 2 * BLOCK), MASK_VALUE, F32)
    fill = fill.at[:, :, 0].set(sink_rows)
    return pl.pallas_call(
        functools.partial(_mix_kernel, steps_per_seq=steps_per_seq),
        out_shape=(jax.ShapeDtypeStruct((t, ATTN_WIDTH), BF16),
                   jax.ShapeDtypeStruct((t, POOL_WIDTH), BF16)),
        grid=(t // TM_MIX,),
        in_specs=[
            pl.BlockSpec((KV_HEADS, nq, 2 * BLOCK), lambda i: (0, 0, 0)),
            pl.BlockSpec((TM_MIX, ATTN_WIDTH), lambda i: (i, qcol)),
            pl.BlockSpec((TM_MIX, KV_WIDTH), lambda i: (i, kcol)),
            pl.BlockSpec((TM_MIX, KV_WIDTH), lambda i: (i, kcol + 1)),
            pl.BlockSpec((BLOCK, KV_WIDTH), lambda i: (prev(i), kcol)),
            pl.BlockSpec((BLOCK, KV_WIDTH), lambda i: (prev(i), kcol + 1)),
            pl.BlockSpec((TM_MIX, half), lambda i: (i, ucol)),
            pl.BlockSpec((TM_MIX, half), lambda i: (i, ucol + 1)),
            pl.BlockSpec((BLOCK, half), lambda i: (prev(i), ucol)),
            pl.BlockSpec((BLOCK, half), lambda i: (prev(i), ucol + 1)),
            pl.BlockSpec((len(POOL_WINDOWS), POOL_GROUP_DIM, POOL_GROUP_DIM),
                         lambda i: (0, 0, 0)),
            pl.BlockSpec((1, POOL_WIDTH), lambda i: (0, 0)),
        ],
        out_specs=(pl.BlockSpec((TM_MIX, ATTN_WIDTH), lambda i: (i, 0)),
                   pl.BlockSpec((TM_MIX, POOL_WIDTH), lambda i: (i, 0))),
        scratch_shapes=[pltpu.VMEM((BLOCK + TM_MIX, KV_WIDTH), BF16),
                        pltpu.VMEM((BLOCK + TM_MIX, KV_WIDTH), BF16),
                        pltpu.VMEM((BLOCK + TM_MIX, POOL_WIDTH), BF16)],
        compiler_params=pltpu.CompilerParams(
            dimension_semantics=("arbitrary",), vmem_limit_bytes=VMEM_LIMIT),
        name="mixers",
    )(fill, proj, proj, proj, proj, proj, proj, proj, proj, proj, w_pool_b,
      pool_scale)


def _first_argmax4(v):
    m = jnp.maximum(jnp.maximum(v[0], v[1]), jnp.maximum(v[2], v[3]))
    idx = jnp.where(v[0] == m, 0, jnp.where(v[1] == m, 1,
                                            jnp.where(v[2] == m, 2, 3)))
    return m, idx


def _merge_kernel(attn_ref, pool_ref, ga_ref, gp_ref, x_ref, ada_ref,
                  wa_ref, wp_ref, wo_ref, g2_ref, wr_ref, br_ref,
                  eg_ref, eu_ref, ed_ref,
                  x2_ref, h2_ref, route_ref, counts_ref, egu_o, ed_o, lg_ref):
    i = pl.program_id(0)
    n_tile = pl.num_programs(0) - 1

    @pl.when(i == 0)
    def _():
        counts_ref[...] = jnp.zeros_like(counts_ref)
        lg_ref[...] = jnp.zeros_like(lg_ref)

    @pl.when(i < n_tile)
    def _():
        _route_tile(lg_ref[...], i > 0, route_ref, counts_ref)
        a = jnp.dot(attn_ref[...], wa_ref[...], preferred_element_type=F32)
        p = jnp.dot(pool_ref[...], wp_ref[...], preferred_element_type=F32)
        merged = ga_ref[...].astype(F32) * a + gp_ref[...].astype(F32) * p
        out = jnp.dot(merged.astype(BF16), wo_ref[...], preferred_element_type=F32)
        x2 = x_ref[...] + ada_ref[0, 2:3, :] * out
        x2_ref[...] = x2
        h2 = _rms_modulate(x2, g2_ref[...], ada_ref[0, 4:5, :], ada_ref[0, 3:4, :])
        h2_ref[...] = h2
        logits = jnp.dot(h2.astype(BF16), wr_ref[...], preferred_element_type=F32)
        lg_ref[...] = logits.T[0:ROUTER_ROWS, :] + br_ref[:, 0:1]
        egu_o[:, 0:EXPERT_FF] = eg_ref[...].astype(BF16)
        egu_o[:, EXPERT_FF:2 * EXPERT_FF] = eu_ref[...].astype(BF16)
        ed_o[...] = ed_ref[...].astype(BF16)

    @pl.when(i == n_tile)
    def _():
        _route_tile(lg_ref[...], True, route_ref, counts_ref)


def _route_tile(logits, live, route_ref, counts_ref):
    tm = logits.shape[1]
    rows = [logits[r:r + 1, :] for r in range(N_GROUPS + N_EXPERTS)]
    _, gi = _first_argmax4(rows[0:N_GROUPS])
    sel = []
    for e in range(EXPERTS_PER_GROUP):
        v = rows[N_GROUPS + 3 * EXPERTS_PER_GROUP + e]
        for g in range(N_GROUPS - 2, -1, -1):
            v = jnp.where(gi == g, rows[N_GROUPS + g * EXPERTS_PER_GROUP + e], v)
        sel.append(v)
    _, i1 = _first_argmax4(sel)
    rest = [jnp.where(i1 == e, -jnp.inf, sel[e]) for e in range(EXPERTS_PER_GROUP)]
    _, i2 = _first_argmax4(rest)
    code = jnp.minimum(i1, i2) * EXPERTS_PER_GROUP + jnp.maximum(i1, i2)
    cig = jnp.where(code == 1, 0, jnp.where(code == 2, 1, jnp.where(
        code == 3, 2, jnp.where(code == 7, 3, jnp.where(code == 6, 4, 5)))))
    cls = gi * PAIRS_PER_GROUP + cig

    onehot = (lax.broadcasted_iota(jnp.int32, (ROUTER_ROWS, tm), 0) == cls
              ).astype(F32)
    onehot = jnp.where(live, onehot, 0.0)
    src = lax.broadcasted_iota(jnp.int32, (tm, tm), 0)
    dst = lax.broadcasted_iota(jnp.int32, (tm, tm), 1)
    before = (src < dst).astype(F32).astype(BF16)
    prefix = jnp.dot(onehot.astype(BF16), before, preferred_element_type=F32)
    carry = counts_ref[:, 0:1]
    rank = jnp.sum(onehot * (prefix + carry), axis=0, keepdims=True)
    counts_ref[...] = counts_ref[...] + jnp.sum(onehot, axis=1, keepdims=True)
    route_ref[0:1, :] = cls
    route_ref[1:2, :] = rank.astype(jnp.int32)
    route_ref[2:8, :] = jnp.zeros((6, tm), jnp.int32)


def _merge(attn, pool, proj, x2d, ada3, wa_b, wp_b, wo_b, norm2_g, wr_lanes, br_col,
           w_e_gate, w_e_up, w_e_down, seq):
    t, d = x2d.shape
    n_step = t // TM_MERGE
    tiles_per_seq = seq // TM_MERGE
    cur = lambda i: jnp.minimum(i, n_step - 1)
    lag = lambda i: jnp.maximum(i - 1, 0)
    flat = lambda w: w.reshape(-1, w.shape[-1])
    slab = lambda w: pl.BlockSpec((w.shape[0] // n_step, w.shape[1]),
                                  lambda i: (cur(i), 0))
    cast = lambda w: jax.ShapeDtypeStruct(w.shape, BF16)
    eg, eu, ed = flat(w_e_gate), flat(w_e_up), flat(w_e_down)
    egu = jax.ShapeDtypeStruct((eg.shape[0], 2 * EXPERT_FF), BF16)
    const = lambda shape: pl.BlockSpec(shape, lambda i: (0,) * len(shape),
                                       pipeline_mode=pl.Buffered(1))
    return pl.pallas_call(
        _merge_kernel,
        out_shape=(jax.ShapeDtypeStruct((t, d), F32),
                   jax.ShapeDtypeStruct((t, d), F32),
                   jax.ShapeDtypeStruct((8, t), jnp.int32),
                   jax.ShapeDtypeStruct((ROUTER_ROWS, 128), F32),
                   egu, cast(ed)),
        grid=(n_step + 1,),
        in_specs=[
            pl.BlockSpec((TM_MERGE, ATTN_WIDTH), lambda i: (cur(i), 0)),
            pl.BlockSpec((TM_MERGE, POOL_WIDTH), lambda i: (cur(i), 0)),
            pl.BlockSpec((TM_MERGE, d), lambda i: (cur(i), 0)),
            pl.BlockSpec((TM_MERGE, d), lambda i: (cur(i), 1)),
            pl.BlockSpec((TM_MERGE, d), lambda i: (cur(i), 0)),
            pl.BlockSpec((1, N_ADA, d),
                         lambda i: (cur(i) // tiles_per_seq, 0, 0)),
            const((ATTN_WIDTH, d)),
            const((POOL_WIDTH, d)),
            const((d, d)),
            const((1, d)),
            const((d, ROUTER_LANES)),
            const((ROUTER_ROWS, 128)),
            slab(eg), slab(eu), slab(ed),
        ],
        out_specs=(
            pl.BlockSpec((TM_MERGE, d), lambda i: (cur(i), 0)),
            pl.BlockSpec((TM_MERGE, d), lambda i: (cur(i), 0)),
            pl.BlockSpec((8, TM_MERGE), lambda i: (0, lag(i))),
            pl.BlockSpec((ROUTER_ROWS, 128), lambda i: (0, 0)),
            slab(egu), slab(ed),
        ),
        scratch_shapes=[pltpu.VMEM((ROUTER_ROWS, TM_MERGE), F32)],
        compiler_params=pltpu.CompilerParams(
            dimension_semantics=("arbitrary",), vmem_limit_bytes=VMEM_LIMIT),
        name="merge",
    )(attn, pool, proj, proj, x2d, ada3, wa_b, wp_b, wo_b, norm2_g, wr_lanes, br_col,
      eg, eu, ed)


def _row_copy(src_ref, src_row, dst_ref, dst_row, sem):
    return pltpu.make_async_copy(src_ref.at[pl.ds(src_row, 1)],
                                 dst_ref.at[pl.ds(dst_row, 1)], sem)


def _sorted_row(cls_ref, rank_ref, base_ref, token):
    return base_ref[cls_ref[token]] + rank_ref[token]


def _dispatch_kernel(cls_ref, rank_ref, base_ref, h_ref, hs_ref, sem):
    i = pl.program_id(0)
    first = i * TM_DISPATCH

    for r in range(TM_DISPATCH):
        _row_copy(h_ref, r, hs_ref,
                  _sorted_row(cls_ref, rank_ref, base_ref, first + r), sem).start()
    pltpu.make_async_copy(h_ref, hs_ref.at[pl.ds(0, TM_DISPATCH)], sem).wait()


def _dispatch(cls, rank, base, h2, n_rows):
    t, d = h2.shape
    return pl.pallas_call(
        _dispatch_kernel,
        out_shape=jax.ShapeDtypeStruct((n_rows, d), F32),
        grid_spec=pltpu.PrefetchScalarGridSpec(
            num_scalar_prefetch=3,
            grid=(t // TM_DISPATCH,),
            in_specs=[pl.BlockSpec((TM_DISPATCH, d), lambda i, *_: (i, 0))],
            out_specs=pl.BlockSpec(memory_space=pl.ANY),
            scratch_shapes=[pltpu.SemaphoreType.DMA(())],
        ),
        compiler_params=pltpu.CompilerParams(
            dimension_semantics=("arbitrary",), vmem_limit_bytes=VMEM_LIMIT,
            has_side_effects=True),
        name="dispatch",
    )(cls, rank, base, h2)


def _expert_kernel(blk_ref, ea_ref, eb_ref, lane_ref, lead_ref, nused_ref,
                   hs_ref, wgua_ref, wda_ref, wgub_ref, wdb_ref,
                   wr_ref, br_ref, ys_ref):
    i = pl.program_id(0)
    used = i < nused_ref[0]
    lead = lead_ref[i]
    half = TM_EXPERT // 2

    def tile_rows(r0):
        tm = TM_EXPERT - r0
        rows = r0 + lax.broadcasted_iota(jnp.int32, (tm, 1), 0)
        h = jnp.where(rows >= lead, hs_ref[r0:TM_EXPERT, :], 0.0).astype(BF16)
        logits = jnp.dot(h, wr_ref[...], preferred_element_type=F32) + br_ref[...]
        lane = lax.broadcasted_iota(jnp.int32, logits.shape, 1)
        is_group = lane < N_GROUPS
        gmax = jnp.max(jnp.where(is_group, logits, -jnp.inf), axis=1, keepdims=True)
        gexp = jnp.where(is_group, jnp.exp(logits - gmax), 0.0)
        pick = lambda v, l: jnp.sum(jnp.where(lane == l, v, 0.0), axis=1,
                                    keepdims=True)
        p_group = pick(gexp, lane_ref[0, i]) / jnp.sum(gexp, axis=1, keepdims=True)
        la = pick(logits, lane_ref[1, i])
        lb = pick(logits, lane_ref[2, i])
        mx = jnp.maximum(la, lb)
        ea = jnp.exp(la - mx)
        eb = jnp.exp(lb - mx)
        inv = p_group / (ea + eb)

        def expert(wgu_ref, wd_ref, weight):
            gu = jnp.dot(h, wgu_ref[0], preferred_element_type=F32)
            gate, up = gu[:, 0:EXPERT_FF], gu[:, EXPERT_FF:2 * EXPERT_FF]
            act = (jax.nn.silu(gate) * up * weight).astype(BF16)
            return jnp.dot(act, wd_ref[0], preferred_element_type=F32)

        ys_ref[r0:TM_EXPERT, :] = (expert(wgua_ref, wda_ref, ea * inv)
                                   + expert(wgub_ref, wdb_ref, eb * inv))

    @pl.when(jnp.logical_not(used))
    def _():
        ys_ref[...] = jnp.zeros_like(ys_ref)

    @pl.when(jnp.logical_and(used, lead < half))
    def _():
        tile_rows(0)

    @pl.when(jnp.logical_and(used, lead >= half))
    def _():
        ys_ref[0:half, :] = jnp.zeros((half, ys_ref.shape[1]), ys_ref.dtype)
        tile_rows(half)


def _experts(tile_blk, tile_ea, tile_eb, tile_lanes, tile_lead, n_used,
             hs, wgu_b, wd_b, wr_lanes, br_lanes):
    p_rows, d = hs.shape
    n_tiles = p_rows // TM_EXPERT
    row_map = lambda i, blk, ea, eb, ln, nv, nu: (blk[i], 0)
    wa_map = lambda i, blk, ea, eb, ln, nv, nu: (ea[i], 0, 0)
    wb_map = lambda i, blk, ea, eb, ln, nv, nu: (eb[i], 0, 0)
    const = lambda i, blk, ea, eb, ln, nv, nu: (0, 0)
    return pl.pallas_call(
        _expert_kernel,
        out_shape=jax.ShapeDtypeStruct((p_rows, d), F32),
        grid_spec=pltpu.PrefetchScalarGridSpec(
            num_scalar_prefetch=6,
            grid=(n_tiles,),
            in_specs=[
                pl.BlockSpec((TM_EXPERT, d), row_map),
                pl.BlockSpec((1, d, 2 * EXPERT_FF), wa_map),
                pl.BlockSpec((1, EXPERT_FF, d), wa_map),
                pl.BlockSpec((1, d, 2 * EXPERT_FF), wb_map),
                pl.BlockSpec((1, EXPERT_FF, d), wb_map),
                pl.BlockSpec((d, ROUTER_LANES), const),
                pl.BlockSpec((1, ROUTER_LANES), const),
            ],
            out_specs=pl.BlockSpec((TM_EXPERT, d), row_map),
        ),
        compiler_params=pltpu.CompilerParams(
            dimension_semantics=("arbitrary",), vmem_limit_bytes=VMEM_LIMIT),
        name="experts",
    )(tile_blk, tile_ea, tile_eb, tile_lanes, tile_lead, n_used,
      hs, wgu_b, wd_b, wgu_b, wd_b, wr_lanes, br_lanes)


def _combine_kernel(cls_ref, rank_ref, base_ref, ys_ref, x2_ref, ada_ref, g_ref,
                    o_ref, ybuf, sems):
    i = pl.program_id(0)
    n = pl.num_programs(0)

    def gather(tile, slot):
        for r in range(TM_COMBINE):
            row = _sorted_row(cls_ref, rank_ref, base_ref, tile * TM_COMBINE + r)
            _row_copy(ys_ref, row, ybuf.at[slot], r, sems.at[slot]).start()

    @pl.when(i == 0)
    def _():
        gather(0, 0)

    @pl.when(i + 1 < n)
    def _():
        gather(i + 1, (i + 1) % 2)

    slot = i % 2
    pltpu.make_async_copy(ys_ref.at[pl.ds(0, TM_COMBINE)], ybuf.at[slot],
                          sems.at[slot]).wait()

    x = x2_ref[...] + ada_ref[0, 5:6, :] * ybuf[slot]
    ms = jnp.mean(x * x, axis=-1, keepdims=True)
    o_ref[...] = x * lax.rsqrt(ms + NORM_EPS) * g_ref[...]


def _combine(cls, rank, base, ys, x2, ada3, final_g, seq):
    t, d = x2.shape
    tiles_per_seq = seq // TM_COMBINE
    return pl.pallas_call(
        _combine_kernel,
        out_shape=jax.ShapeDtypeStruct((t, d), F32),
        grid_spec=pltpu.PrefetchScalarGridSpec(
            num_scalar_prefetch=3,
            grid=(t // TM_COMBINE,),
            in_specs=[
                pl.BlockSpec(memory_space=pl.ANY),
                pl.BlockSpec((TM_COMBINE, d), lambda i, *_: (i, 0)),
                pl.BlockSpec((1, N_ADA, d),
                             lambda i, *_: (i // tiles_per_seq, 0, 0)),
                pl.BlockSpec((1, d), lambda i, *_: (0, 0)),
            ],
            out_specs=pl.BlockSpec((TM_COMBINE, d), lambda i, *_: (i, 0)),
            scratch_shapes=[pltpu.VMEM((2, TM_COMBINE, d), F32),
                            pltpu.SemaphoreType.DMA((2,))],
        ),
        compiler_params=pltpu.CompilerParams(
            dimension_semantics=("arbitrary",), vmem_limit_bytes=VMEM_LIMIT),
        name="combine",
    )(cls, rank, base, ys, x2, ada3, final_g)


def _routing_tables(route, counts, n_tiles):
    cls = route[0]
    rank = route[1]
    cnt = counts[:N_CLASSES, 0].astype(jnp.int32)
    tiles_c = (cnt + TM_EXPERT - 1) // TM_EXPERT
    tile_end = jnp.cumsum(tiles_c)
    tile_start = tile_end - tiles_c
    n_used = tile_end[-1]
    lead_c = tiles_c * TM_EXPERT - cnt
    base = (tile_start * TM_EXPERT + lead_c).astype(jnp.int32)

    steps = jnp.arange(n_tiles, dtype=jnp.int32)
    k = jnp.minimum(steps, n_used - 1)
    c = jnp.sum((k[:, None] >= tile_end[None, :]).astype(jnp.int32), axis=1)
    grp = c // PAIRS_PER_GROUP
    cig = c % PAIRS_PER_GROUP
    slot_a = jnp.asarray(CLASS_SLOT_A, jnp.int32)[cig]
    slot_b = jnp.asarray(CLASS_SLOT_B, jnp.int32)[cig]
    ea = grp * EXPERTS_PER_GROUP + slot_a
    eb = grp * EXPERTS_PER_GROUP + slot_b
    lanes = jnp.stack([grp, N_GROUPS + ea, N_GROUPS + eb]).astype(jnp.int32)
    lead = jnp.where(k == tile_start[c], lead_c[c], 0)
    return (cls, rank, base, steps, ea.astype(jnp.int32),
            eb.astype(jnp.int32), lanes, lead.astype(jnp.int32),
            n_used.reshape(1).astype(jnp.int32))


def kernel(x, c, w_ada, b_ada, norm1_g, w_in, sinks, w_pool, pool_scale,
           w_attn_branch, w_pool_branch, w_out, norm2_g, w_router_group,
           b_router_group, w_router_expert, b_router_expert, w_e_gate, w_e_up,
           w_e_down, final_g):
    b, s, d = x.shape
    t = b * s
    assert w_ada.shape[0] == 1, "single-layer block"
    assert d == D_MODEL and s % TM_PROJ == 0
    l = 0
    n_tiles = t // TM_EXPERT + N_CLASSES
    x2d = x.reshape(t, d)
    ada3 = _ada(c, w_ada[l], b_ada[l]).reshape(b, N_ADA, d)

    proj, wa_b, wp_b, wo_b = _in_proj(
        x2d, ada3, norm1_g[l].reshape(1, d), w_in[l], w_attn_branch[l],
        w_pool_branch[l], w_out[l], s)
    attn, pool = _mixers(proj, sinks[l], w_pool[l].astype(BF16),
                         pool_scale[l].reshape(1, -1), s)

    w_r = jnp.concatenate([w_router_group[l], w_router_expert[l]], axis=1)
    b_r = jnp.concatenate([b_router_group[l], b_router_expert[l]])
    n_r = N_GROUPS + N_EXPERTS
    br_col = jnp.broadcast_to(
        jnp.pad(b_r, (0, ROUTER_ROWS - n_r))[:, None], (ROUTER_ROWS, 128))
    wr_lanes = jnp.pad(w_r, ((0, 0), (0, ROUTER_LANES - n_r))).astype(BF16)
    br_lanes = jnp.pad(b_r, (0, ROUTER_LANES - n_r)).reshape(1, ROUTER_LANES)

    x2, h2, route, counts, egu_b, ed_b = _merge(
        attn, pool, proj, x2d, ada3, wa_b, wp_b, wo_b, norm2_g[l].reshape(1, d),
        wr_lanes, br_col, w_e_gate[l], w_e_up[l], w_e_down[l], s)

    cls, rank, base, tile_blk, tile_ea, tile_eb, tile_lanes, tile_lead, n_used = (
        _routing_tables(route, counts, n_tiles))
    hs = _dispatch(cls, rank, base, h2, n_tiles * TM_EXPERT)
    ys = _experts(tile_blk, tile_ea, tile_eb, tile_lanes, tile_lead, n_used,
                  hs, egu_b.reshape(N_EXPERTS, d, 2 * EXPERT_FF),
                  ed_b.reshape(w_e_down[l].shape), wr_lanes, br_lanes)
    out = _combine(cls, rank, base, ys, x2, ada3, final_g.reshape(1, d), s)
    return out.reshape(b, s, d)
```

```python
import functools

import jax
import jax.numpy as jnp
from jax import lax
from jax.experimental import pallas as pl
from jax.experimental.pallas import tpu as pltpu

F32 = jnp.float32
BF16 = jnp.bfloat16

D_MODEL = 2048
HEAD_DIM = 64
ATTN_HEADS = 16
KV_HEADS = 4
Q_PER_KV = ATTN_HEADS // KV_HEADS
ATTN_WIDTH = ATTN_HEADS * HEAD_DIM
KV_WIDTH = KV_HEADS * HEAD_DIM
BLOCK = 128
POOL_WINDOWS = (2, 4, 8, 16)
POOL_WIDTH = 1024
POOL_GROUP_DIM = 256
QKVU_WIDTH = ATTN_WIDTH + 2 * KV_WIDTH + POOL_WIDTH
GATE_WIDTH = 2 * D_MODEL
PROJ_WIDTH = GATE_WIDTH + QKVU_WIDTH
Q_COL = GATE_WIDTH
K_COL = Q_COL + ATTN_WIDTH
V_COL = K_COL + KV_WIDTH
U_COL = V_COL + KV_WIDTH
N_GROUPS = 4
EXPERTS_PER_GROUP = 4
N_EXPERTS = 16
EXPERT_FF = 512
N_ADA = 6
NORM_EPS = 1e-6
MASK_VALUE = -1e30

PAIRS_PER_GROUP = 6
N_CLASSES = N_GROUPS * PAIRS_PER_GROUP
CLASS_SLOT_A = (0, 0, 0, 1, 1, 3)
CLASS_SLOT_B = (1, 2, 3, 3, 2, 2)
ROUTER_ROWS = 32
ROUTER_LANES = 128

TM_PROJ = 2048
TN_PROJ = 512
NORM_ROWS = 256
NORM_CHUNKS = TM_PROJ // NORM_ROWS
TM_MIX = 1024
TM_MERGE = 256
TM_DISPATCH = 512
TM_EXPERT = 256
TM_COMBINE = 512
TN_ADA = 1024

VMEM_LIMIT = 52 * 1024 * 1024


def _rms_modulate(x, g, scale, shift):
    ms = jnp.mean(x * x, axis=-1, keepdims=True)
    return (x * lax.rsqrt(ms + NORM_EPS)) * (g * (1.0 + scale)) + shift


def _ada_kernel(cb_ref, w_ref, b_ref, o_ref):
    nb = cb_ref.shape[0]
    d = w_ref.shape[0]
    nchunk = w_ref.shape[1] // 128

    def body(kb, accs):
        k0 = pl.multiple_of(kb * 8, 8)
        new = list(accs)
        cbs = [cb_ref[b, pl.ds(k0, 8), :] for b in range(nb)]
        for j in range(nchunk):
            w = w_ref[pl.ds(k0, 8), j * 128:(j + 1) * 128]
            for b in range(nb):
                new[b * nchunk + j] = new[b * nchunk + j] + w * cbs[b]
        return tuple(new)

    init = tuple(jnp.zeros((8, 128), F32) for _ in range(nb * nchunk))
    accs = lax.fori_loop(0, d // 8, body, init, unroll=2)
    for b in range(nb):
        for j in range(nchunk):
            o_ref[b:b + 1, j * 128:(j + 1) * 128] = (
                jnp.sum(accs[b * nchunk + j], axis=0, keepdims=True)
                + b_ref[:, j * 128:(j + 1) * 128])


def _ada(c, w_ada, b_ada):
    nb, d = c.shape
    n = w_ada.shape[1]
    cb = jnp.broadcast_to(c[:, :, None], (nb, d, 128))
    return pl.pallas_call(
        _ada_kernel,
        out_shape=jax.ShapeDtypeStruct((nb, n), F32),
        grid=(n // TN_ADA,),
        in_specs=[
            pl.BlockSpec((nb, d, 128), lambda j: (0, 0, 0)),
            pl.BlockSpec((d, TN_ADA), lambda j: (0, j)),
            pl.BlockSpec((1, TN_ADA), lambda j: (0, j)),
        ],
        out_specs=pl.BlockSpec((nb, TN_ADA), lambda j: (0, j)),
        compiler_params=pltpu.CompilerParams(
            dimension_semantics=("arbitrary",), vmem_limit_bytes=VMEM_LIMIT),
        name="ada",
    )(cb, w_ada, b_ada.reshape(1, n))


def _in_proj_kernel(x_ref, ada_ref, g_ref, w_ref, wa_ref, wp_ref, wo_ref,
                    proj_ref, wa_o, wp_o, wo_o, h_even, h_odd, *, n_row):
    i = pl.program_id(0)
    j = pl.program_id(1)

    def norm_chunk(h_ref):
        r0 = pl.multiple_of(j * NORM_ROWS, NORM_ROWS)
        h = _rms_modulate(x_ref[...], g_ref[...],
                          ada_ref[0, 1:2, :], ada_ref[0, 0:1, :])
        h_ref[pl.ds(r0, NORM_ROWS), :] = h.astype(BF16)

    def step(h_write, h_read, early):
        if early:
            norm_chunk(h_write)
        acc = jnp.dot(h_read[...], w_ref[...].astype(BF16),
                      preferred_element_type=F32)
        if early:
            acc = 0.5 * jnp.tanh(0.5 * acc) + 0.5
        proj_ref[...] = acc.astype(BF16)

    @pl.when(jnp.logical_and(j == 0, i < n_row))
    def _():
        wa_o[...] = wa_ref[...].astype(BF16)
        wp_o[...] = wp_ref[...].astype(BF16)
        wo_o[...] = wo_ref[...].astype(BF16)

    @pl.when(jnp.logical_and(i == 0, j < NORM_CHUNKS))
    def _():
        norm_chunk(h_even)

    is_early = j < NORM_CHUNKS
    odd = i % 2 == 1
    even = jnp.logical_and(i > 0, i % 2 == 0)
    for parity, h_write, h_read in ((odd, h_odd, h_even), (even, h_even, h_odd)):
        for early in (True, False):
            kind = is_early if early else jnp.logical_not(is_early)
            pl.when(jnp.logical_and(parity, kind))(
                functools.partial(step, h_write, h_read, early))


def _in_proj(x2d, ada3, norm_g, w_in, wa, wp, wo, seq):
    t, d = x2d.shape
    n_all = PROJ_WIDTH // TN_PROJ
    n_gate = GATE_WIDTH // TN_PROJ
    n_row = t // TM_PROJ
    tiles_per_seq = seq // TM_PROJ
    assert NORM_CHUNKS == n_gate
    norm_tile = lambda i: jnp.minimum(i, n_row - 1)
    out_tile = lambda i: jnp.maximum(i - 1, 0)
    w_tile = lambda i, j: (
        0, (jnp.where(i == 0, 0, j) + (n_all - n_gate)) % n_all)
    slab = lambda w: pl.BlockSpec((w.shape[0] // n_row, w.shape[1]),
                                  lambda i, j: (norm_tile(i), 0))
    cast = lambda w: jax.ShapeDtypeStruct(w.shape, BF16)
    return pl.pallas_call(
        functools.partial(_in_proj_kernel, n_row=n_row),
        out_shape=(jax.ShapeDtypeStruct((t, PROJ_WIDTH), BF16),
                   cast(wa), cast(wp), cast(wo)),
        grid=(n_row + 1, n_all),
        in_specs=[
            pl.BlockSpec((NORM_ROWS, d), lambda i, j: (
                norm_tile(i) * NORM_CHUNKS + jnp.minimum(j, NORM_CHUNKS - 1), 0)),
            pl.BlockSpec((1, N_ADA, d),
                         lambda i, j: (norm_tile(i) // tiles_per_seq, 0, 0)),
            pl.BlockSpec((1, d), lambda i, j: (0, 0)),
            pl.BlockSpec((d, TN_PROJ), w_tile),
            slab(wa), slab(wp), slab(wo),
        ],
        out_specs=(
            pl.BlockSpec((TM_PROJ, TN_PROJ),
                         lambda i, j: (out_tile(i), jnp.where(i == 0, 0, j))),
            slab(wa), slab(wp), slab(wo),
        ),
        scratch_shapes=[pltpu.VMEM((TM_PROJ, d), BF16),
                        pltpu.VMEM((TM_PROJ, d), BF16)],
        compiler_params=pltpu.CompilerParams(
            dimension_semantics=("arbitrary", "arbitrary"),
            vmem_limit_bytes=VMEM_LIMIT),
        name="in_proj",
    )(x2d, ada3, norm_g, w_in, wa, wp, wo)


def _mix_kernel(fill_ref, q_ref, kc_ref, vc_ref, kp_ref, vp_ref,
                ulo_ref, uhi_ref, plo_ref, phi_ref, wp_ref, ps_ref,
                attn_ref, pool_ref, kfull, vfull, ufull, *, steps_per_seq):
    i = pl.program_id(0)
    first_step = (i % steps_per_seq) == 0
    kfull[0:BLOCK, :] = kp_ref[...]
    kfull[BLOCK:BLOCK + TM_MIX, :] = kc_ref[...]
    vfull[0:BLOCK, :] = vp_ref[...]
    vfull[BLOCK:BLOCK + TM_MIX, :] = vc_ref[...]
    half = POOL_WIDTH // 2
    zeros = jnp.zeros((BLOCK, half), BF16)
    ufull[0:BLOCK, 0:half] = jnp.where(first_step, zeros, plo_ref[...])
    ufull[0:BLOCK, half:POOL_WIDTH] = jnp.where(first_step, zeros, phi_ref[...])
    ufull[BLOCK:BLOCK + TM_MIX, 0:half] = ulo_ref[...]
    ufull[BLOCK:BLOCK + TM_MIX, half:POOL_WIDTH] = uhi_ref[...]
    pos0 = (i % steps_per_seq) * TM_MIX
    nq = Q_PER_KV * BLOCK
    prow = lax.broadcasted_iota(jnp.int32, (BLOCK, 2 * BLOCK), 0)
    pcol = lax.broadcasted_iota(jnp.int32, (BLOCK, 2 * BLOCK), 1)
    rel = prow + BLOCK - pcol
    bands = [((rel >= 0) & (rel < w)).astype(F32).astype(BF16)
             for w in POOL_WINDOWS]

    def sb_body(sb, carry):
        r0 = pl.multiple_of(sb * BLOCK, BLOCK)
        row = lax.broadcasted_iota(jnp.int32, (nq, 2 * BLOCK), 0) & (BLOCK - 1)
        col = lax.broadcasted_iota(jnp.int32, (nq, 2 * BLOCK), 1)
        dist = col - row
        lo = jnp.where(jnp.logical_and(first_step, sb == 0), BLOCK, 0)
        valid = (dist > 0) & (dist <= BLOCK) & (col >= lo)
        key0 = lax.broadcasted_iota(jnp.int32, (2 * BLOCK, HEAD_DIM), 0) == 0
        pos = pos0 + sb * BLOCK + lax.broadcasted_iota(jnp.int32, (BLOCK, 1), 0)
        heads = range(KV_HEADS)
        pcols = [slice(h * POOL_GROUP_DIM, (h + 1) * POOL_GROUP_DIM) for h in heads]

        logits, ssums = [], []
        for h in heads:
            kb = kfull[pl.ds(r0, 2 * BLOCK), h * HEAD_DIM:(h + 1) * HEAD_DIM]
            qg = jnp.concatenate(
                [q_ref[pl.ds(r0, BLOCK),
                       (Q_PER_KV * h + g) * HEAD_DIM:(Q_PER_KV * h + g + 1) * HEAD_DIM]
                 for g in range(Q_PER_KV)], axis=0)
            qg = qg * jnp.asarray(HEAD_DIM ** -0.5, BF16)
            logits.append(lax.dot_general(qg, kb, (((1,), (1,)), ((), ())),
                                          preferred_element_type=F32))
            band = ufull[pl.ds(r0, 2 * BLOCK), pcols[h]]
            ssums.append(jnp.dot(bands[h], band, preferred_element_type=F32))

        probs, denoms, pooled = [], [], []
        for h in heads:
            lg = jnp.where(valid, logits[h], fill_ref[h])
            m = jnp.max(lg, axis=1, keepdims=True)
            p = jnp.exp(lg - m)
            denoms.append(jnp.sum(p, axis=1, keepdims=True))
            probs.append(p.astype(BF16))
            xg = ufull[pl.ds(r0 + BLOCK, BLOCK), pcols[h]].astype(F32)
            count = jnp.minimum(pos + 1, POOL_WINDOWS[h]).astype(F32)
            pooled.append((ssums[h] / count - xg).astype(BF16))

        for h in heads:
            vb = vfull[pl.ds(r0, 2 * BLOCK), h * HEAD_DIM:(h + 1) * HEAD_DIM]
            vb = jnp.where(key0, jnp.zeros_like(vb), vb)
            o = jnp.dot(probs[h], vb, preferred_element_type=F32)
            o = (o * (1.0 / denoms[h])).astype(BF16)
            attn_ref[pl.ds(r0, BLOCK),
                     h * Q_PER_KV * HEAD_DIM:(h + 1) * Q_PER_KV * HEAD_DIM] = (
                jnp.concatenate([o[g * BLOCK:(g + 1) * BLOCK, :]
                                 for g in range(Q_PER_KV)], axis=1))
            mixed = jnp.dot(pooled[h], wp_ref[h], preferred_element_type=F32)
            pool_ref[pl.ds(r0, BLOCK), pcols[h]] = (
                mixed * ps_ref[:, pcols[h]]).astype(BF16)
        return carry

    lax.fori_loop(0, TM_MIX // BLOCK, sb_body, 0)


def _mixers(proj, sinks, w_pool_b, pool_scale, seq):
    t = proj.shape[0]
    assert len(POOL_WINDOWS) == KV_HEADS
    steps_per_seq = seq // TM_MIX
    sub = TM_MIX // BLOCK
    half = POOL_WIDTH // 2
    qcol = Q_COL // ATTN_WIDTH
    kcol = K_COL // KV_WIDTH
    ucol = U_COL // half
    prev = lambda i: jnp.maximum(i * sub - 1, 0)
    nq = Q_PER_KV * BLOCK
    sink_rows = jnp.repeat(sinks.astype(F32).reshape(KV_HEADS, Q_PER_KV), BLOCK,
                           axis=1)
    fill = jnp.full((KV_HEADS, nq, 2 * BLOCK), MASK_VALUE, F32)
    fill = fill.at[:, :, 0].set(sink_rows)
    return pl.pallas_call(
        functools.partial(_mix_kernel, steps_per_seq=steps_per_seq),
        out_shape=(jax.ShapeDtypeStruct((t, ATTN_WIDTH), BF16),
                   jax.ShapeDtypeStruct((t, POOL_WIDTH), BF16)),
        grid=(t // TM_MIX,),
        in_specs=[
            pl.BlockSpec((KV_HEADS, nq, 2 * BLOCK), lambda i: (0, 0, 0)),
            pl.BlockSpec((TM_MIX, ATTN_WIDTH), lambda i: (i, qcol)),
            pl.BlockSpec((TM_MIX, KV_WIDTH), lambda i: (i, kcol)),
            pl.BlockSpec((TM_MIX, KV_WIDTH), lambda i: (i, kcol + 1)),
            pl.BlockSpec((BLOCK, KV_WIDTH), lambda i: (prev(i), kcol)),
            pl.BlockSpec((BLOCK, KV_WIDTH), lambda i: (prev(i), kcol + 1)),
            pl.BlockSpec((TM_MIX, half), lambda i: (i, ucol)),
            pl.BlockSpec((TM_MIX, half), lambda i: (i, ucol + 1)),
            pl.BlockSpec((BLOCK, half), lambda i: (prev(i), ucol)),
            pl.BlockSpec((BLOCK, half), lambda i: (prev(i), ucol + 1)),
            pl.BlockSpec((len(POOL_WINDOWS), POOL_GROUP_DIM, POOL_GROUP_DIM),
                         lambda i: (0, 0, 0)),
            pl.BlockSpec((1, POOL_WIDTH), lambda i: (0, 0)),
        ],
        out_specs=(pl.BlockSpec((TM_MIX, ATTN_WIDTH), lambda i: (i, 0)),
                   pl.BlockSpec((TM_MIX, POOL_WIDTH), lambda i: (i, 0))),
        scratch_shapes=[pltpu.VMEM((BLOCK + TM_MIX, KV_WIDTH), BF16),
                        pltpu.VMEM((BLOCK + TM_MIX, KV_WIDTH), BF16),
                        pltpu.VMEM((BLOCK + TM_MIX, POOL_WIDTH), BF16)],
        compiler_params=pltpu.CompilerParams(
            dimension_semantics=("arbitrary",), vmem_limit_bytes=VMEM_LIMIT),
        name="mixers",
    )(fill, proj, proj, proj, proj, proj, proj, proj, proj, proj, w_pool_b,
      pool_scale)


def _first_argmax4(v):
    m = jnp.maximum(jnp.maximum(v[0], v[1]), jnp.maximum(v[2], v[3]))
    idx = jnp.where(v[0] == m, 0, jnp.where(v[1] == m, 1,
                                            jnp.where(v[2] == m, 2, 3)))
    return m, idx


def _merge_kernel(attn_ref, pool_ref, ga_ref, gp_ref, x_ref, ada_ref,
                  wa_ref, wp_ref, wo_ref, g2_ref, wr_ref, br_ref,
                  eg_ref, eu_ref, ed_ref,
                  x2_ref, h2_ref, route_ref, counts_ref, egu_o, ed_o, lg_ref):
    i = pl.program_id(0)
    n_tile = pl.num_programs(0) - 1

    @pl.when(i == 0)
    def _():
        counts_ref[...] = jnp.zeros_like(counts_ref)
        lg_ref[...] = jnp.zeros_like(lg_ref)

    @pl.when(i < n_tile)
    def _():
        _route_tile(lg_ref[...], i > 0, route_ref, counts_ref)
        a = jnp.dot(attn_ref[...], wa_ref[...], preferred_element_type=F32)
        p = jnp.dot(pool_ref[...], wp_ref[...], preferred_element_type=F32)
        merged = ga_ref[...].astype(F32) * a + gp_ref[...].astype(F32) * p
        out = jnp.dot(merged.astype(BF16), wo_ref[...], preferred_element_type=F32)
        x2 = x_ref[...] + ada_ref[0, 2:3, :] * out
        x2_ref[...] = x2
        h2 = _rms_modulate(x2, g2_ref[...], ada_ref[0, 4:5, :], ada_ref[0, 3:4, :])
        h2_ref[...] = h2
        logits = jnp.dot(h2.astype(BF16), wr_ref[...], preferred_element_type=F32)
        lg_ref[...] = logits.T[0:ROUTER_ROWS, :] + br_ref[:, 0:1]
        egu_o[:, 0:EXPERT_FF] = eg_ref[...].astype(BF16)
        egu_o[:, EXPERT_FF:2 * EXPERT_FF] = eu_ref[...].astype(BF16)
        ed_o[...] = ed_ref[...].astype(BF16)

    @pl.when(i == n_tile)
    def _():
        _route_tile(lg_ref[...], True, route_ref, counts_ref)


def _route_tile(logits, live, route_ref, counts_ref):
    tm = logits.shape[1]
    rows = [logits[r:r + 1, :] for r in range(N_GROUPS + N_EXPERTS)]
    _, gi = _first_argmax4(rows[0:N_GROUPS])
    sel = []
    for e in range(EXPERTS_PER_GROUP):
        v = rows[N_GROUPS + 3 * EXPERTS_PER_GROUP + e]
        for g in range(N_GROUPS - 2, -1, -1):
            v = jnp.where(gi == g, rows[N_GROUPS + g * EXPERTS_PER_GROUP + e], v)
        sel.append(v)
    _, i1 = _first_argmax4(sel)
    rest = [jnp.where(i1 == e, -jnp.inf, sel[e]) for e in range(EXPERTS_PER_GROUP)]
    _, i2 = _first_argmax4(rest)
    code = jnp.minimum(i1, i2) * EXPERTS_PER_GROUP + jnp.maximum(i1, i2)
    cig = jnp.where(code == 1, 0, jnp.where(code == 2, 1, jnp.where(
        code == 3, 2, jnp.where(code == 7, 3, jnp.where(code == 6, 4, 5)))))
    cls = gi * PAIRS_PER_GROUP + cig

    onehot = (lax.broadcasted_iota(jnp.int32, (ROUTER_ROWS, tm), 0) == cls
              ).astype(F32)
    onehot = jnp.where(live, onehot, 0.0)
    src = lax.broadcasted_iota(jnp.int32, (tm, tm), 0)
    dst = lax.broadcasted_iota(jnp.int32, (tm, tm), 1)
    before = (src < dst).astype(F32).astype(BF16)
    prefix = jnp.dot(onehot.astype(BF16), before, preferred_element_type=F32)
    carry = counts_ref[:, 0:1]
    rank = jnp.sum(onehot * (prefix + carry), axis=0, keepdims=True)
    counts_ref[...] = counts_ref[...] + jnp.sum(onehot, axis=1, keepdims=True)
    route_ref[0:1, :] = cls
    route_ref[1:2, :] = rank.astype(jnp.int32)
    route_ref[2:8, :] = jnp.zeros((6, tm), jnp.int32)


def _merge(attn, pool, proj, x2d, ada3, wa_b, wp_b, wo_b, norm2_g, wr_lanes, br_col,
           w_e_gate, w_e_up, w_e_down, seq):
    t, d = x2d.shape
    n_step = t // TM_MERGE
    tiles_per_seq = seq // TM_MERGE
    cur = lambda i: jnp.minimum(i, n_step - 1)
    lag = lambda i: jnp.maximum(i - 1, 0)
    flat = lambda w: w.reshape(-1, w.shape[-1])
    slab = lambda w: pl.BlockSpec((w.shape[0] // n_step, w.shape[1]),
                                  lambda i: (cur(i), 0))
    cast = lambda w: jax.ShapeDtypeStruct(w.shape, BF16)
    eg, eu, ed = flat(w_e_gate), flat(w_e_up), flat(w_e_down)
    egu = jax.ShapeDtypeStruct((eg.shape[0], 2 * EXPERT_FF), BF16)
    const = lambda shape: pl.BlockSpec(shape, lambda i: (0,) * len(shape),
                                       pipeline_mode=pl.Buffered(1))
    return pl.pallas_call(
        _merge_kernel,
        out_shape=(jax.ShapeDtypeStruct((t, d), F32),
                   jax.ShapeDtypeStruct((t, d), F32),
                   jax.ShapeDtypeStruct((8, t), jnp.int32),
                   jax.ShapeDtypeStruct((ROUTER_ROWS, 128), F32),
                   egu, cast(ed)),
        grid=(n_step + 1,),
        in_specs=[
            pl.BlockSpec((TM_MERGE, ATTN_WIDTH), lambda i: (cur(i), 0)),
            pl.BlockSpec((TM_MERGE, POOL_WIDTH), lambda i: (cur(i), 0)),
            pl.BlockSpec((TM_MERGE, d), lambda i: (cur(i), 0)),
            pl.BlockSpec((TM_MERGE, d), lambda i: (cur(i), 1)),
            pl.BlockSpec((TM_MERGE, d), lambda i: (cur(i), 0)),
            pl.BlockSpec((1, N_ADA, d),
                         lambda i: (cur(i) // tiles_per_seq, 0, 0)),
            const((ATTN_WIDTH, d)),
            const((POOL_WIDTH, d)),
            const((d, d)),
            const((1, d)),
            const((d, ROUTER_LANES)),
            const((ROUTER_ROWS, 128)),
            slab(eg), slab(eu), slab(ed),
        ],
        out_specs=(
            pl.BlockSpec((TM_MERGE, d), lambda i: (cur(i), 0)),
            pl.BlockSpec((TM_MERGE, d), lambda i: (cur(i), 0)),
            pl.BlockSpec((8, TM_MERGE), lambda i: (0, lag(i))),
            pl.BlockSpec((ROUTER_ROWS, 128), lambda i: (0, 0)),
            slab(egu), slab(ed),
        ),
        scratch_shapes=[pltpu.VMEM((ROUTER_ROWS, TM_MERGE), F32)],
        compiler_params=pltpu.CompilerParams(
            dimension_semantics=("arbitrary",), vmem_limit_bytes=VMEM_LIMIT),
        name="merge",
    )(attn, pool, proj, proj, x2d, ada3, wa_b, wp_b, wo_b, norm2_g, wr_lanes, br_col,
      eg, eu, ed)


def _row_copy(src_ref, src_row, dst_ref, dst_row, sem):
    return pltpu.make_async_copy(src_ref.at[pl.ds(src_row, 1)],
                                 dst_ref.at[pl.ds(dst_row, 1)], sem)


def _sorted_row(cls_ref, rank_ref, base_ref, token):
    return base_ref[cls_ref[token]] + rank_ref[token]


def _dispatch_kernel(cls_ref, rank_ref, base_ref, h_ref, hs_ref, sem):
    i = pl.program_id(0)
    first = i * TM_DISPATCH

    for r in range(TM_DISPATCH):
        _row_copy(h_ref, r, hs_ref,
                  _sorted_row(cls_ref, rank_ref, base_ref, first + r), sem).start()
    pltpu.make_async_copy(h_ref, hs_ref.at[pl.ds(0, TM_DISPATCH)], sem).wait()


def _dispatch(cls, rank, base, h2, n_rows):
    t, d = h2.shape
    return pl.pallas_call(
        _dispatch_kernel,
        out_shape=jax.ShapeDtypeStruct((n_rows, d), F32),
        grid_spec=pltpu.PrefetchScalarGridSpec(
            num_scalar_prefetch=3,
            grid=(t // TM_DISPATCH,),
            in_specs=[pl.BlockSpec((TM_DISPATCH, d), lambda i, *_: (i, 0))],
            out_specs=pl.BlockSpec(memory_space=pl.ANY),
            scratch_shapes=[pltpu.SemaphoreType.DMA(())],
        ),
        compiler_params=pltpu.CompilerParams(
            dimension_semantics=("arbitrary",), vmem_limit_bytes=VMEM_LIMIT,
            has_side_effects=True),
        name="dispatch",
    )(cls, rank, base, h2)


def _expert_kernel(blk_ref, ea_ref, eb_ref, lane_ref, lead_ref, nused_ref,
                   hs_ref, wgua_ref, wda_ref, wgub_ref, wdb_ref,
                   wr_ref, br_ref, ys_ref):
    i = pl.program_id(0)
    used = i < nused_ref[0]
    lead = lead_ref[i]
    half = TM_EXPERT // 2

    def tile_rows(r0):
        tm = TM_EXPERT - r0
        rows = r0 + lax.broadcasted_iota(jnp.int32, (tm, 1), 0)
        h = jnp.where(rows >= lead, hs_ref[r0:TM_EXPERT, :], 0.0).astype(BF16)
        logits = jnp.dot(h, wr_ref[...], preferred_element_type=F32) + br_ref[...]
        lane = lax.broadcasted_iota(jnp.int32, logits.shape, 1)
        is_group = lane < N_GROUPS
        gmax = jnp.max(jnp.where(is_group, logits, -jnp.inf), axis=1, keepdims=True)
        gexp = jnp.where(is_group, jnp.exp(logits - gmax), 0.0)
        pick = lambda v, l: jnp.sum(jnp.where(lane == l, v, 0.0), axis=1,
                                    keepdims=True)
        p_group = pick(gexp, lane_ref[0, i]) / jnp.sum(gexp, axis=1, keepdims=True)
        la = pick(logits, lane_ref[1, i])
        lb = pick(logits, lane_ref[2, i])
        mx = jnp.maximum(la, lb)
        ea = jnp.exp(la - mx)
        eb = jnp.exp(lb - mx)
        inv = p_group / (ea + eb)

        def activation(wgu_ref, weight):
            gu = jnp.dot(h, wgu_ref[0], preferred_element_type=F32)
            gate, up = gu[:, 0:EXPERT_FF], gu[:, EXPERT_FF:2 * EXPERT_FF]
            return (jax.nn.silu(gate) * up * weight).astype(BF16)

        act = jnp.concatenate([activation(wgua_ref, ea * inv),
                               activation(wgub_ref, eb * inv)], axis=1)
        dh = ys_ref.shape[1] // 2
        for c in range(2):
            cols = slice(c * dh, (c + 1) * dh)
            wd = jnp.concatenate([wda_ref[0, :, cols], wdb_ref[0, :, cols]], axis=0)
            ys_ref[r0:TM_EXPERT, cols] = jnp.dot(act, wd,
                                                  preferred_element_type=F32)

    @pl.when(jnp.logical_not(used))
    def _():
        ys_ref[...] = jnp.zeros_like(ys_ref)

    @pl.when(jnp.logical_and(used, lead < half))
    def _():
        tile_rows(0)

    @pl.when(jnp.logical_and(used, lead >= half))
    def _():
        ys_ref[0:half, :] = jnp.zeros((half, ys_ref.shape[1]), ys_ref.dtype)
        tile_rows(half)


def _experts(tile_blk, tile_ea, tile_eb, tile_lanes, tile_lead, n_used,
             hs, wgu_b, wd_b, wr_lanes, br_lanes):
    p_rows, d = hs.shape
    n_tiles = p_rows // TM_EXPERT
    row_map = lambda i, blk, ea, eb, ln, nv, nu: (blk[i], 0)
    wa_map = lambda i, blk, ea, eb, ln, nv, nu: (ea[i], 0, 0)
    wb_map = lambda i, blk, ea, eb, ln, nv, nu: (eb[i], 0, 0)
    const = lambda i, blk, ea, eb, ln, nv, nu: (0, 0)
    return pl.pallas_call(
        _expert_kernel,
        out_shape=jax.ShapeDtypeStruct((p_rows, d), F32),
        grid_spec=pltpu.PrefetchScalarGridSpec(
            num_scalar_prefetch=6,
            grid=(n_tiles,),
            in_specs=[
                pl.BlockSpec((TM_EXPERT, d), row_map),
                pl.BlockSpec((1, d, 2 * EXPERT_FF), wa_map),
                pl.BlockSpec((1, EXPERT_FF, d), wa_map),
                pl.BlockSpec((1, d, 2 * EXPERT_FF), wb_map),
                pl.BlockSpec((1, EXPERT_FF, d), wb_map),
                pl.BlockSpec((d, ROUTER_LANES), const),
                pl.BlockSpec((1, ROUTER_LANES), const),
            ],
            out_specs=pl.BlockSpec((TM_EXPERT, d), row_map),
        ),
        compiler_params=pltpu.CompilerParams(
            dimension_semantics=("arbitrary",), vmem_limit_bytes=VMEM_LIMIT),
        name="experts",
    )(tile_blk, tile_ea, tile_eb, tile_lanes, tile_lead, n_used,
      hs, wgu_b, wd_b, wgu_b, wd_b, wr_lanes, br_lanes)


def _combine_kernel(cls_ref, rank_ref, base_ref, ys_ref, x2_ref, ada_ref, g_ref,
                    o_ref, ybuf, sems):
    i = pl.program_id(0)
    n = pl.num_programs(0)

    def gather(tile, slot):
        for r in range(TM_COMBINE):
            row = _sorted_row(cls_ref, rank_ref, base_ref, tile * TM_COMBINE + r)
            _row_copy(ys_ref, row, ybuf.at[slot], r, sems.at[slot]).start()

    @pl.when(i == 0)
    def _():
        gather(0, 0)

    @pl.when(i + 1 < n)
    def _():
        gather(i + 1, (i + 1) % 2)

    slot = i % 2
    pltpu.make_async_copy(ys_ref.at[pl.ds(0, TM_COMBINE)], ybuf.at[slot],
                          sems.at[slot]).wait()

    x = x2_ref[...] + ada_ref[0, 5:6, :] * ybuf[slot]
    ms = jnp.mean(x * x, axis=-1, keepdims=True)
    o_ref[...] = x * lax.rsqrt(ms + NORM_EPS) * g_ref[...]


def _combine(cls, rank, base, ys, x2, ada3, final_g, seq):
    t, d = x2.shape
    tiles_per_seq = seq // TM_COMBINE
    return pl.pallas_call(
        _combine_kernel,
        out_shape=jax.ShapeDtypeStruct((t, d), F32),
        grid_spec=pltpu.PrefetchScalarGridSpec(
            num_scalar_prefetch=3,
            grid=(t // TM_COMBINE,),
            in_specs=[
                pl.BlockSpec(memory_space=pl.ANY),
                pl.BlockSpec((TM_COMBINE, d), lambda i, *_: (i, 0)),
                pl.BlockSpec((1, N_ADA, d),
                             lambda i, *_: (i // tiles_per_seq, 0, 0)),
                pl.BlockSpec((1, d), lambda i, *_: (0, 0)),
            ],
            out_specs=pl.BlockSpec((TM_COMBINE, d), lambda i, *_: (i, 0)),
            scratch_shapes=[pltpu.VMEM((2, TM_COMBINE, d), F32),
                            pltpu.SemaphoreType.DMA((2,))],
        ),
        compiler_params=pltpu.CompilerParams(
            dimension_semantics=("arbitrary",), vmem_limit_bytes=VMEM_LIMIT),
        name="combine",
    )(cls, rank, base, ys, x2, ada3, final_g)


def _routing_tables(route, counts, n_tiles):
    cls = route[0]
    rank = route[1]
    cnt = counts[:N_CLASSES, 0].astype(jnp.int32)
    tiles_c = (cnt + TM_EXPERT - 1) // TM_EXPERT
    tile_end = jnp.cumsum(tiles_c)
    tile_start = tile_end - tiles_c
    n_used = tile_end[-1]
    lead_c = tiles_c * TM_EXPERT - cnt
    base = (tile_start * TM_EXPERT + lead_c).astype(jnp.int32)

    steps = jnp.arange(n_tiles, dtype=jnp.int32)
    k = jnp.minimum(steps, n_used - 1)
    c = jnp.sum((k[:, None] >= tile_end[None, :]).astype(jnp.int32), axis=1)
    grp = c // PAIRS_PER_GROUP
    cig = c % PAIRS_PER_GROUP
    slot_a = jnp.asarray(CLASS_SLOT_A, jnp.int32)[cig]
    slot_b = jnp.asarray(CLASS_SLOT_B, jnp.int32)[cig]
    ea = grp * EXPERTS_PER_GROUP + slot_a
    eb = grp * EXPERTS_PER_GROUP + slot_b
    lanes = jnp.stack([grp, N_GROUPS + ea, N_GROUPS + eb]).astype(jnp.int32)
    lead = jnp.where(k == tile_start[c], lead_c[c], 0)
    return (cls, rank, base, steps, ea.astype(jnp.int32),
            eb.astype(jnp.int32), lanes, lead.astype(jnp.int32),
            n_used.reshape(1).astype(jnp.int32))


def kernel(x, c, w_ada, b_ada, norm1_g, w_in, sinks, w_pool, pool_scale,
           w_attn_branch, w_pool_branch, w_out, norm2_g, w_router_group,
           b_router_group, w_router_expert, b_router_expert, w_e_gate, w_e_up,
           w_e_down, final_g):
    b, s, d = x.shape
    t = b * s
    assert w_ada.shape[0] == 1, "single-layer block"
    assert d == D_MODEL and s % TM_PROJ == 0
    l = 0
    n_tiles = t // TM_EXPERT + N_CLASSES
    x2d = x.reshape(t, d)
    ada3 = _ada(c, w_ada[l], b_ada[l]).reshape(b, N_ADA, d)

    proj, wa_b, wp_b, wo_b = _in_proj(
        x2d, ada3, norm1_g[l].reshape(1, d), w_in[l], w_attn_branch[l],
        w_pool_branch[l], w_out[l], s)
    attn, pool = _mixers(proj, sinks[l], w_pool[l].astype(BF16),
                         pool_scale[l].reshape(1, -1), s)

    w_r = jnp.concatenate([w_router_group[l], w_router_expert[l]], axis=1)
    b_r = jnp.concatenate([b_router_group[l], b_router_expert[l]])
    n_r = N_GROUPS + N_EXPERTS
    br_col = jnp.broadcast_to(
        jnp.pad(b_r, (0, ROUTER_ROWS - n_r))[:, None], (ROUTER_ROWS, 128))
    wr_lanes = jnp.pad(w_r, ((0, 0), (0, ROUTER_LANES - n_r))).astype(BF16)
    br_lanes = jnp.pad(b_r, (0, ROUTER_LANES - n_r)).reshape(1, ROUTER_LANES)

    x2, h2, route, counts, egu_b, ed_b = _merge(
        attn, pool, proj, x2d, ada3, wa_b, wp_b, wo_b, norm2_g[l].reshape(1, d),
        wr_lanes, br_col, w_e_gate[l], w_e_up[l], w_e_down[l], s)

    cls, rank, base, tile_blk, tile_ea, tile_eb, tile_lanes, tile_lead, n_used = (
        _routing_tables(route, counts, n_tiles))
    hs = _dispatch(cls, rank, base, h2, n_tiles * TM_EXPERT)
    ys = _experts(tile_blk, tile_ea, tile_eb, tile_lanes, tile_lead, n_used,
                  hs, egu_b.reshape(N_EXPERTS, d, 2 * EXPERT_FF),
                  ed_b.reshape(w_e_down[l].shape), wr_lanes, br_lanes)
    out = _combine(cls, rank, base, ys, x2, ada3, final_g.reshape(1, d), s)
    return out.reshape(b, s, d)
```
